```python
import math
import jax
import jax.numpy as jnp
from jax import lax
import numpy as np

D_MODEL = 2048
BATCH = 8
SEQ = 4096
DEPTH = 4

D_MIX = D_MODEL
S5_WIDTH = D_MIX // 4
S5_GROUP = 16
S5_GROUPS = S5_WIDTH // S5_GROUP
S5_STATE = 64
MLA_HEADS = 8
MLA_NOPE = 128
MLA_ROPE = 64
MLA_V = 128
MLA_Q_RANK = D_MODEL // 4
MLA_KV_RANK = D_MODEL // 8
MLA_WIDTH = MLA_HEADS * MLA_V
ROPE_THETA = 10000.0
Q_BLOCK = 128
MASK_VALUE = -1e30
HG_HEADS = 4
HG_DK = 128
HG_DV = (D_MIX - S5_WIDTH - MLA_WIDTH) // HG_HEADS
HG_WIDTH = HG_HEADS * HG_DV
HG_CHUNK = 64
D_FF = 5504
CONV_W = 3
EPS = 1e-6
IN_SIZES = (S5_WIDTH, MLA_Q_RANK, MLA_KV_RANK, MLA_ROPE,
            HG_HEADS * HG_DK, HG_HEADS * HG_DK, HG_WIDTH, HG_WIDTH)
IN_OFFSETS = tuple(int(v) for v in np.cumsum(IN_SIZES)[:-1])
D_IN = int(sum(IN_SIZES))

kernel_name = 'hybrid_s5_mla_hgrn2_block'


def rmsnorm(x, gain):
    xf = x.astype(jnp.float32)
    xf = xf * lax.rsqrt(jnp.mean(xf * xf, axis=-1, keepdims=True) + EPS)
    return (xf * gain.astype(jnp.float32)).astype(x.dtype)


def rope_tables(positions):
    inv_freq = 1.0 / (ROPE_THETA ** (jnp.arange(0, MLA_ROPE, 2, dtype=jnp.float32) / MLA_ROPE))
    ang = positions.astype(jnp.float32)[..., None] * inv_freq
    return jnp.cos(ang), jnp.sin(ang)


def apply_rope(x, cos, sin):
    half = x.shape[-1] // 2
    x1, x2 = x[..., :half], x[..., half:]
    cos = cos.astype(x.dtype)
    sin = sin.astype(x.dtype)
    return jnp.concatenate([x1 * cos - x2 * sin, x2 * cos + x1 * sin], axis=-1)


def s5_mixer(u, lam_re, lam_im, log_dt, b_re, b_im, c_re, c_im, d, w_glu):
    bsz, seq, _ = u.shape
    uf = u.astype(jnp.float32).reshape(bsz, seq, S5_GROUPS, S5_GROUP)
    lam = lax.complex(lam_re.astype(jnp.float32), lam_im.astype(jnp.float32))
    dt = jnp.exp(log_dt.astype(jnp.float32))[:, None]
    lam_bar = jnp.exp(lam * dt)
    b = lax.complex(b_re.astype(jnp.float32), b_im.astype(jnp.float32))
    b_bar = ((lam_bar - 1.0) / lam)[..., None] * b
    bu = jnp.einsum('gpc,bsgc->bsgp', b_bar, uf.astype(jnp.complex64))
    a = jnp.broadcast_to(lam_bar, bu.shape)

    def combine(left, right):
        a_l, b_l = left
        a_r, b_r = right
        return a_r * a_l, a_r * b_l + b_r

    _, h = lax.associative_scan(combine, (a, bu), axis=1)
    cm = lax.complex(c_re.astype(jnp.float32), c_im.astype(jnp.float32))
    y = (jnp.einsum('gcp,bsgp->bsgc', cm, h).real
         + d.astype(jnp.float32).reshape(S5_GROUPS, S5_GROUP) * uf)
    y = jax.nn.gelu(y.reshape(bsz, seq, S5_WIDTH))
    out = y * jax.nn.sigmoid(y @ w_glu.astype(jnp.float32))
    return out.astype(u.dtype)


def causal_attention_blocks(q_nope, q_rope, k_nope, k_rope, v):
    seq = q_nope.shape[1]
    scale = (MLA_NOPE + MLA_ROPE) ** -0.5
    outs = []
    for blk in range(seq // Q_BLOCK):
        q0, q1 = blk * Q_BLOCK, (blk + 1) * Q_BLOCK
        s = (jnp.einsum('bqhd,bkhd->bhqk', q_nope[:, q0:q1], k_nope[:, :q1])
             + jnp.einsum('bqhr,bkr->bhqk', q_rope[:, q0:q1], k_rope[:, :q1])).astype(jnp.float32) * scale
        causal = jnp.arange(q1)[None, :] <= jnp.arange(q0, q1)[:, None]
        p = jax.nn.softmax(jnp.where(causal, s, MASK_VALUE), axis=-1).astype(v.dtype)
        outs.append(jnp.einsum('bhqk,bkhd->bqhd', p, v[:, :q1]))
    return jnp.concatenate(outs, axis=1)


def mla_mixer(c_q, c_kv, k_rope_in, q_norm, w_uq, kv_norm, w_ukv, cos, sin):
    bsz, seq, _ = c_q.shape
    q = (rmsnorm(c_q, q_norm) @ w_uq).reshape(bsz, seq, MLA_HEADS, MLA_NOPE + MLA_ROPE)
    q_nope = q[..., :MLA_NOPE]
    q_rope = apply_rope(q[..., MLA_NOPE:], cos[:, :, None, :], sin[:, :, None, :])
    kv = (rmsnorm(c_kv, kv_norm) @ w_ukv).reshape(bsz, seq, MLA_HEADS, MLA_NOPE + MLA_V)
    k_nope, v = kv[..., :MLA_NOPE], kv[..., MLA_NOPE:]
    k_rope = apply_rope(k_rope_in, cos, sin)
    o = causal_attention_blocks(q_nope, q_rope, k_nope, k_rope, v)
    return o.reshape(bsz, seq, MLA_WIDTH)


def hgrn2_chunkwise(q, k, v, logf):
    bsz, seq, nh, dk = q.shape
    dv = v.shape[-1]
    n_chunks = seq // HG_CHUNK

    def to_chunks(t):
        return t.reshape(bsz, n_chunks, HG_CHUNK, nh, t.shape[-1]).transpose(1, 0, 3, 2, 4)

    mask = jnp.tril(jnp.ones((HG_CHUNK, HG_CHUNK), dtype=bool))[:, :, None]

    def step(state, inp):
        qc, kc, vc, gc = inp
        b = jnp.cumsum(gc, axis=2)
        o_inter = jnp.einsum('bhtk,bhkv->bhtv', qc * jnp.exp(b), state)
        diff = b[:, :, :, None, :] - b[:, :, None, :, :]
        decay = jnp.where(mask, jnp.exp(jnp.where(mask, diff, 0.0)), 0.0)
        attn = jnp.einsum('bhtk,bhsk,bhtsk->bhts', qc, kc, decay)
        o_intra = jnp.einsum('bhts,bhsv->bhtv', attn, vc)
        b_last = b[:, :, -1:, :]
        new_state = (jnp.exp(b_last[:, :, 0, :])[..., None] * state
                     + jnp.einsum('bhsk,bhsv->bhkv', kc * jnp.exp(b_last - b), vc))
        return new_state, o_inter + o_intra

    state0 = jnp.zeros((bsz, nh, dk, dv), jnp.float32)
    _, o = lax.scan(step, state0, (to_chunks(q), to_chunks(k), to_chunks(v), to_chunks(logf)))
    return o.transpose(1, 0, 3, 2, 4).reshape(bsz, seq, nh, dv)


def hgrn2_mixer(q_in, f_in, i_in, g_in, lb, out_norm):
    bsz, seq, _ = q_in.shape
    z = f_in.astype(jnp.float32)
    lb = lb.astype(jnp.float32)
    logf = jnp.log(lb + (1.0 - lb) * jax.nn.sigmoid(z))
    k = (1.0 - lb) * jax.nn.sigmoid(-z)
    q = jax.nn.silu(q_in.astype(jnp.float32))

    def heads(t):
        return t.reshape(bsz, seq, HG_HEADS, t.shape[-1] // HG_HEADS)

    o = hgrn2_chunkwise(heads(q), heads(k), heads(i_in.astype(jnp.float32)), heads(logf))
    o = rmsnorm(o, out_norm) * jax.nn.silu(heads(g_in.astype(jnp.float32)))
    return o.reshape(bsz, seq, HG_WIDTH).astype(q_in.dtype)


def causal_dwconv(u, w, b):
    seq = u.shape[1]
    taps = w.shape[0]
    up = jnp.pad(u, ((0, 0), (taps - 1, 0), (0, 0)))
    out = b
    for j in range(taps):
        out = out + up[:, j:j + seq] * w[j]
    return out


def conv_geglu_ffn(h, w_up, conv_w, conv_b, w_down):
    u = causal_dwconv(h @ w_up, conv_w, conv_b)
    gate, val = jnp.split(u, 2, axis=-1)
    return (jax.nn.gelu(gate, approximate=True) * val) @ w_down


def _fwd_setup_inputs(seed: int = 0) -> dict:
    key = jax.random.key(seed)
    k = jax.random.split(key, 32)
    L = DEPTH

    def nrm(i, shape, scale=1.0):
        return scale * jax.random.normal(k[i], shape, jnp.float32)

    def gain(i, shape):
        return 1.0 + nrm(i, shape, 0.1)

    x = nrm(0, (BATCH, SEQ, D_MODEL))
    c = nrm(1, (BATCH, D_MODEL))
    offsets = jax.random.randint(k[2], (BATCH, 1), 0, 1024, dtype=jnp.int32)
    positions = offsets + jnp.arange(SEQ, dtype=jnp.int32)[None, :]
    w_in = nrm(3, (L, D_MODEL, D_IN), D_MODEL ** -0.5)
    s5_lambda_re = -0.5 + nrm(4, (L, S5_GROUPS, S5_STATE), 0.01)
    s5_lambda_im = math.pi * jnp.arange(S5_STATE, dtype=jnp.float32) + nrm(5, (L, S5_GROUPS, S5_STATE), 0.01)
    s5_log_dt = jax.random.uniform(k[6], (L, S5_GROUPS), jnp.float32, math.log(1e-3), math.log(1e-1))
    s5_b_re = nrm(7, (L, S5_GROUPS, S5_STATE, S5_GROUP), (2 * S5_GROUP) ** -0.5)
    s5_b_im = nrm(8, (L, S5_GROUPS, S5_STATE, S5_GROUP), (2 * S5_GROUP) ** -0.5)
    s5_c_re = nrm(9, (L, S5_GROUPS, S5_GROUP, S5_STATE), (2 * S5_STATE) ** -0.5)
    s5_c_im = nrm(10, (L, S5_GROUPS, S5_GROUP, S5_STATE), (2 * S5_STATE) ** -0.5)
    s5_d = nrm(11, (L, S5_WIDTH))
    s5_w_glu = nrm(12, (L, S5_WIDTH, S5_WIDTH), S5_WIDTH ** -0.5)
    mla_q_norm = gain(13, (L, MLA_Q_RANK))
    mla_w_uq = nrm(14, (L, MLA_Q_RANK, MLA_HEADS * (MLA_NOPE + MLA_ROPE)), MLA_Q_RANK ** -0.5)
    mla_kv_norm = gain(15, (L, MLA_KV_RANK))
    mla_w_ukv = nrm(16, (L, MLA_KV_RANK, MLA_HEADS * (MLA_NOPE + MLA_V)), MLA_KV_RANK ** -0.5)
    hg_lb_logits = nrm(17, (L, HG_HEADS * HG_DK), 0.5)
    hg_out_norm = gain(18, (L, HG_DV))
    w_out = nrm(19, (L, D_MIX, D_MODEL), D_MIX ** -0.5)
    mix_pre_norm = gain(20, (L, D_MODEL))
    mix_post_norm = gain(21, (L, D_MODEL))
    ffn_pre_norm = gain(22, (L, D_MODEL))
    ffn_post_norm = gain(23, (L, D_MODEL))
    ffn_w_up = nrm(24, (L, D_MODEL, 2 * D_FF), D_MODEL ** -0.5)
    ffn_conv_w = nrm(25, (L, CONV_W, 2 * D_FF), CONV_W ** -0.5)
    ffn_conv_b = nrm(26, (L, 2 * D_FF), 0.02)
    ffn_w_down = nrm(27, (L, D_FF, D_MODEL), D_FF ** -0.5)
    w_ada = nrm(28, (L, D_MODEL, 6 * D_MODEL), 0.5 * D_MODEL ** -0.5)
    b_ada = nrm(29, (L, 6 * D_MODEL), 0.02)
    return {'x': x, 'c': c, 'positions': positions, 'w_in': w_in,
            's5_lambda_re': s5_lambda_re, 's5_lambda_im': s5_lambda_im, 's5_log_dt': s5_log_dt,
            's5_b_re': s5_b_re, 's5_b_im': s5_b_im, 's5_c_re': s5_c_re, 's5_c_im': s5_c_im,
            's5_d': s5_d, 's5_w_glu': s5_w_glu,
            'mla_q_norm': mla_q_norm, 'mla_w_uq': mla_w_uq, 'mla_kv_norm': mla_kv_norm, 'mla_w_ukv': mla_w_ukv,
            'hg_lb_logits': hg_lb_logits, 'hg_out_norm': hg_out_norm, 'w_out': w_out,
            'mix_pre_norm': mix_pre_norm, 'mix_post_norm': mix_post_norm,
            'ffn_pre_norm': ffn_pre_norm, 'ffn_post_norm': ffn_post_norm,
            'ffn_w_up': ffn_w_up, 'ffn_conv_w': ffn_conv_w, 'ffn_conv_b': ffn_conv_b, 'ffn_w_down': ffn_w_down,
            'w_ada': w_ada, 'b_ada': b_ada}


def _fwd_reference(x, c, positions, w_in, s5_lambda_re, s5_lambda_im, s5_log_dt, s5_b_re, s5_b_im,
              s5_c_re, s5_c_im, s5_d, s5_w_glu, mla_q_norm, mla_w_uq, mla_kv_norm, mla_w_ukv,
              hg_lb_logits, hg_out_norm, w_out, mix_pre_norm, mix_post_norm, ffn_pre_norm, ffn_post_norm,
              ffn_w_up, ffn_conv_w, ffn_conv_b, ffn_w_down, w_ada, b_ada):
    cos, sin = rope_tables(positions)
    probs = jax.nn.softmax(hg_lb_logits.astype(jnp.float32), axis=0)
    lower_bounds = jnp.cumsum(probs, axis=0) - probs[0:1]
    c_act = jax.nn.silu(c)
    for l in range(DEPTH):
        mod = c_act @ w_ada[l] + b_ada[l]
        sh1, sc1, g1, sh2, sc2, g2 = jnp.split(mod[:, None, :], 6, axis=-1)
        h = rmsnorm(x, mix_pre_norm[l]) * (1.0 + sc1) + sh1
        proj = h @ w_in[l]
        u_s5, c_q, c_kv, k_rope, hq, hf, hi, hg = jnp.split(proj, IN_OFFSETS, axis=-1)
        y_s5 = s5_mixer(u_s5, s5_lambda_re[l], s5_lambda_im[l], s5_log_dt[l], s5_b_re[l], s5_b_im[l],
                        s5_c_re[l], s5_c_im[l], s5_d[l], s5_w_glu[l])
        y_mla = mla_mixer(c_q, c_kv, k_rope, mla_q_norm[l], mla_w_uq[l], mla_kv_norm[l], mla_w_ukv[l], cos, sin)
        y_hg = hgrn2_mixer(hq, hf, hi, hg, lower_bounds[l], hg_out_norm[l])
        mixed = jnp.concatenate([y_s5, y_mla, y_hg], axis=-1) @ w_out[l]
        x = x + g1 * rmsnorm(mixed, mix_post_norm[l])
        h = rmsnorm(x, ffn_pre_norm[l]) * (1.0 + sc2) + sh2
        y = conv_geglu_ffn(h, ffn_w_up[l], ffn_conv_w[l], ffn_conv_b[l], ffn_w_down[l])
        x = x + g2 * rmsnorm(y, ffn_post_norm[l])
    return x


import jax as _jax
import jax.numpy as _jnp

TWIN_FORMAT = 'train_step'
FWD_PARAMS = ['x', 'c', 'positions', 'w_in', 's5_lambda_re', 's5_lambda_im', 's5_log_dt', 's5_b_re', 's5_b_im', 's5_c_re', 's5_c_im', 's5_d', 's5_w_glu', 'mla_q_norm', 'mla_w_uq', 'mla_kv_norm', 'mla_w_ukv', 'hg_lb_logits', 'hg_out_norm', 'w_out', 'mix_pre_norm', 'mix_post_norm', 'ffn_pre_norm', 'ffn_post_norm', 'ffn_w_up', 'ffn_conv_w', 'ffn_conv_b', 'ffn_w_down', 'w_ada', 'b_ada']
TWIN_WEIGHTS = ['w_in', 's5_lambda_re', 's5_lambda_im', 's5_log_dt', 's5_b_re', 's5_b_im', 's5_c_re', 's5_c_im', 's5_d', 's5_w_glu', 'mla_q_norm', 'mla_w_uq', 'mla_kv_norm', 'mla_w_ukv', 'hg_lb_logits', 'hg_out_norm', 'w_out', 'mix_pre_norm', 'mix_post_norm', 'ffn_pre_norm', 'ffn_post_norm', 'ffn_w_up', 'ffn_conv_w', 'ffn_conv_b', 'ffn_w_down', 'w_ada', 'b_ada']
TWIN_DIFF_INPUT = 'x'
TWIN_INPUTS = ['x', 'c', 'positions', 'w_in', 's5_lambda_re', 's5_lambda_im', 's5_log_dt', 's5_b_re', 's5_b_im', 's5_c_re', 's5_c_im', 's5_d', 's5_w_glu', 'mla_q_norm', 'mla_w_uq', 'mla_kv_norm', 'mla_w_ukv', 'hg_lb_logits', 'hg_out_norm', 'w_out', 'mix_pre_norm', 'mix_post_norm', 'ffn_pre_norm', 'ffn_post_norm', 'ffn_w_up', 'ffn_conv_w', 'ffn_conv_b', 'ffn_w_down', 'w_ada', 'b_ada', 'loss_target', 'm_w_in', 'm_s5_lambda_re', 'm_s5_lambda_im', 'm_s5_log_dt', 'm_s5_b_re', 'm_s5_b_im', 'm_s5_c_re', 'm_s5_c_im', 'm_s5_d', 'm_s5_w_glu', 'm_mla_q_norm', 'm_mla_w_uq', 'm_mla_kv_norm', 'm_mla_w_ukv', 'm_hg_lb_logits', 'm_hg_out_norm', 'm_w_out', 'm_mix_pre_norm', 'm_mix_post_norm', 'm_ffn_pre_norm', 'm_ffn_post_norm', 'm_ffn_w_up', 'm_ffn_conv_w', 'm_ffn_conv_b', 'm_ffn_w_down', 'm_w_ada', 'm_b_ada', 'v_w_in', 'v_s5_lambda_re', 'v_s5_lambda_im', 'v_s5_log_dt', 'v_s5_b_re', 'v_s5_b_im', 'v_s5_c_re', 'v_s5_c_im', 'v_s5_d', 'v_s5_w_glu', 'v_mla_q_norm', 'v_mla_w_uq', 'v_mla_kv_norm', 'v_mla_w_ukv', 'v_hg_lb_logits', 'v_hg_out_norm', 'v_w_out', 'v_mix_pre_norm', 'v_mix_post_norm', 'v_ffn_pre_norm', 'v_ffn_post_norm', 'v_ffn_w_up', 'v_ffn_conv_w', 'v_ffn_conv_b', 'v_ffn_w_down', 'v_w_ada', 'v_b_ada']
TWIN_OUTPUTS = ['loss', 'grad_x', 'grad_w_in', 'grad_s5_lambda_re', 'grad_s5_lambda_im', 'grad_s5_log_dt', 'grad_s5_b_re', 'grad_s5_b_im', 'grad_s5_c_re', 'grad_s5_c_im', 'grad_s5_d', 'grad_s5_w_glu', 'grad_mla_q_norm', 'grad_mla_w_uq', 'grad_mla_kv_norm', 'grad_mla_w_ukv', 'grad_hg_lb_logits', 'grad_hg_out_norm', 'grad_w_out', 'grad_mix_pre_norm', 'grad_mix_post_norm', 'grad_ffn_pre_norm', 'grad_ffn_post_norm', 'grad_ffn_w_up', 'grad_ffn_conv_w', 'grad_ffn_conv_b', 'grad_ffn_w_down', 'grad_w_ada', 'grad_b_ada', 'delta_w_in', 'delta_s5_lambda_re', 'delta_s5_lambda_im', 'delta_s5_log_dt', 'delta_s5_b_re', 'delta_s5_b_im', 'delta_s5_c_re', 'delta_s5_c_im', 'delta_s5_d', 'delta_s5_w_glu', 'delta_mla_q_norm', 'delta_mla_w_uq', 'delta_mla_kv_norm', 'delta_mla_w_ukv', 'delta_hg_lb_logits', 'delta_hg_out_norm', 'delta_w_out', 'delta_mix_pre_norm', 'delta_mix_post_norm', 'delta_ffn_pre_norm', 'delta_ffn_post_norm', 'delta_ffn_w_up', 'delta_ffn_conv_w', 'delta_ffn_conv_b', 'delta_ffn_w_down', 'delta_w_ada', 'delta_b_ada', 'new_m_w_in', 'new_m_s5_lambda_re', 'new_m_s5_lambda_im', 'new_m_s5_log_dt', 'new_m_s5_b_re', 'new_m_s5_b_im', 'new_m_s5_c_re', 'new_m_s5_c_im', 'new_m_s5_d', 'new_m_s5_w_glu', 'new_m_mla_q_norm', 'new_m_mla_w_uq', 'new_m_mla_kv_norm', 'new_m_mla_w_ukv', 'new_m_hg_lb_logits', 'new_m_hg_out_norm', 'new_m_w_out', 'new_m_mix_pre_norm', 'new_m_mix_post_norm', 'new_m_ffn_pre_norm', 'new_m_ffn_post_norm', 'new_m_ffn_w_up', 'new_m_ffn_conv_w', 'new_m_ffn_conv_b', 'new_m_ffn_w_down', 'new_m_w_ada', 'new_m_b_ada', 'new_v_w_in', 'new_v_s5_lambda_re', 'new_v_s5_lambda_im', 'new_v_s5_log_dt', 'new_v_s5_b_re', 'new_v_s5_b_im', 'new_v_s5_c_re', 'new_v_s5_c_im', 'new_v_s5_d', 'new_v_s5_w_glu', 'new_v_mla_q_norm', 'new_v_mla_w_uq', 'new_v_mla_kv_norm', 'new_v_mla_w_ukv', 'new_v_hg_lb_logits', 'new_v_hg_out_norm', 'new_v_w_out', 'new_v_mix_pre_norm', 'new_v_mix_post_norm', 'new_v_ffn_pre_norm', 'new_v_ffn_post_norm', 'new_v_ffn_w_up', 'new_v_ffn_conv_w', 'new_v_ffn_conv_b', 'new_v_ffn_w_down', 'new_v_w_ada', 'new_v_b_ada']
TWIN_LEAF_KINDS = {'loss': 'loss', 'grad_x': 'grad_x', 'grad_w_in': 'grad_w', 'grad_s5_lambda_re': 'grad_w', 'grad_s5_lambda_im': 'grad_w', 'grad_s5_log_dt': 'grad_w', 'grad_s5_b_re': 'grad_w', 'grad_s5_b_im': 'grad_w', 'grad_s5_c_re': 'grad_w', 'grad_s5_c_im': 'grad_w', 'grad_s5_d': 'grad_w', 'grad_s5_w_glu': 'grad_w', 'grad_mla_q_norm': 'grad_w', 'grad_mla_w_uq': 'grad_w', 'grad_mla_kv_norm': 'grad_w', 'grad_mla_w_ukv': 'grad_w', 'grad_hg_lb_logits': 'grad_w', 'grad_hg_out_norm': 'grad_w', 'grad_w_out': 'grad_w', 'grad_mix_pre_norm': 'grad_w', 'grad_mix_post_norm': 'grad_w', 'grad_ffn_pre_norm': 'grad_w', 'grad_ffn_post_norm': 'grad_w', 'grad_ffn_w_up': 'grad_w', 'grad_ffn_conv_w': 'grad_w', 'grad_ffn_conv_b': 'grad_w', 'grad_ffn_w_down': 'grad_w', 'grad_w_ada': 'grad_w', 'grad_b_ada': 'grad_w', 'delta_w_in': 'delta_w', 'delta_s5_lambda_re': 'delta_w', 'delta_s5_lambda_im': 'delta_w', 'delta_s5_log_dt': 'delta_w', 'delta_s5_b_re': 'delta_w', 'delta_s5_b_im': 'delta_w', 'delta_s5_c_re': 'delta_w', 'delta_s5_c_im': 'delta_w', 'delta_s5_d': 'delta_w', 'delta_s5_w_glu': 'delta_w', 'delta_mla_q_norm': 'delta_w', 'delta_mla_w_uq': 'delta_w', 'delta_mla_kv_norm': 'delta_w', 'delta_mla_w_ukv': 'delta_w', 'delta_hg_lb_logits': 'delta_w', 'delta_hg_out_norm': 'delta_w', 'delta_w_out': 'delta_w', 'delta_mix_pre_norm': 'delta_w', 'delta_mix_post_norm': 'delta_w', 'delta_ffn_pre_norm': 'delta_w', 'delta_ffn_post_norm': 'delta_w', 'delta_ffn_w_up': 'delta_w', 'delta_ffn_conv_w': 'delta_w', 'delta_ffn_conv_b': 'delta_w', 'delta_ffn_w_down': 'delta_w', 'delta_w_ada': 'delta_w', 'delta_b_ada': 'delta_w', 'new_m_w_in': 'new_m', 'new_m_s5_lambda_re': 'new_m', 'new_m_s5_lambda_im': 'new_m', 'new_m_s5_log_dt': 'new_m', 'new_m_s5_b_re': 'new_m', 'new_m_s5_b_im': 'new_m', 'new_m_s5_c_re': 'new_m', 'new_m_s5_c_im': 'new_m', 'new_m_s5_d': 'new_m', 'new_m_s5_w_glu': 'new_m', 'new_m_mla_q_norm': 'new_m', 'new_m_mla_w_uq': 'new_m', 'new_m_mla_kv_norm': 'new_m', 'new_m_mla_w_ukv': 'new_m', 'new_m_hg_lb_logits': 'new_m', 'new_m_hg_out_norm': 'new_m', 'new_m_w_out': 'new_m', 'new_m_mix_pre_norm': 'new_m', 'new_m_mix_post_norm': 'new_m', 'new_m_ffn_pre_norm': 'new_m', 'new_m_ffn_post_norm': 'new_m', 'new_m_ffn_w_up': 'new_m', 'new_m_ffn_conv_w': 'new_m', 'new_m_ffn_conv_b': 'new_m', 'new_m_ffn_w_down': 'new_m', 'new_m_w_ada': 'new_m', 'new_m_b_ada': 'new_m', 'new_v_w_in': 'new_v', 'new_v_s5_lambda_re': 'new_v', 'new_v_s5_lambda_im': 'new_v', 'new_v_s5_log_dt': 'new_v', 'new_v_s5_b_re': 'new_v', 'new_v_s5_b_im': 'new_v', 'new_v_s5_c_re': 'new_v', 'new_v_s5_c_im': 'new_v', 'new_v_s5_d': 'new_v', 'new_v_s5_w_glu': 'new_v', 'new_v_mla_q_norm': 'new_v', 'new_v_mla_w_uq': 'new_v', 'new_v_mla_kv_norm': 'new_v', 'new_v_mla_w_ukv': 'new_v', 'new_v_hg_lb_logits': 'new_v', 'new_v_hg_out_norm': 'new_v', 'new_v_w_out': 'new_v', 'new_v_mix_pre_norm': 'new_v', 'new_v_mix_post_norm': 'new_v', 'new_v_ffn_pre_norm': 'new_v', 'new_v_ffn_post_norm': 'new_v', 'new_v_ffn_w_up': 'new_v', 'new_v_ffn_conv_w': 'new_v', 'new_v_ffn_conv_b': 'new_v', 'new_v_ffn_w_down': 'new_v', 'new_v_w_ada': 'new_v', 'new_v_b_ada': 'new_v'}


def _forward(args):
    return _fwd_reference(*[args[k] for k in FWD_PARAMS])


def _output_shape():
    def fwd():
        inp = _fwd_setup_inputs(0)
        return _fwd_reference(*[inp[k] for k in FWD_PARAMS])
    out = _jax.eval_shape(fwd)
    return out.shape, out.dtype

N_MICROBATCH = 1
ADAM_LR = 0.001
ADAM_B1 = 0.9
ADAM_B2 = 0.999
ADAM_EPS = 1e-08
ADAM_WD = 0.01
ADAM_STEP = 10
PER_EXAMPLE_BATCH_AXIS = {'x': 0, 'c': 0, 'positions': 0, 'loss_target': 0}
SHARED_INPUTS = []
_WEIGHT_DTYPES = {'w_in': _jnp.float32, 's5_lambda_re': _jnp.float32, 's5_lambda_im': _jnp.float32, 's5_log_dt': _jnp.float32, 's5_b_re': _jnp.float32, 's5_b_im': _jnp.float32, 's5_c_re': _jnp.float32, 's5_c_im': _jnp.float32, 's5_d': _jnp.float32, 's5_w_glu': _jnp.float32, 'mla_q_norm': _jnp.float32, 'mla_w_uq': _jnp.float32, 'mla_kv_norm': _jnp.float32, 'mla_w_ukv': _jnp.float32, 'hg_lb_logits': _jnp.float32, 'hg_out_norm': _jnp.float32, 'w_out': _jnp.float32, 'mix_pre_norm': _jnp.float32, 'mix_post_norm': _jnp.float32, 'ffn_pre_norm': _jnp.float32, 'ffn_post_norm': _jnp.float32, 'ffn_w_up': _jnp.float32, 'ffn_conv_w': _jnp.float32, 'ffn_conv_b': _jnp.float32, 'ffn_w_down': _jnp.float32, 'w_ada': _jnp.float32, 'b_ada': _jnp.float32}
MOMENT_SCALE = {'w_in': 3.506292e-01, 's5_lambda_re': 1.320980e-02, 's5_lambda_im': 1.679275e-02, 's5_log_dt': 2.025834e+00, 's5_b_re': 1.061176e-02, 's5_b_im': 1.076227e-02, 's5_c_re': 2.187129e-02, 's5_c_im': 2.242110e-02, 's5_d': 3.811774e-01, 's5_w_glu': 6.107971e-02, 'mla_q_norm': 3.459790e-02, 'mla_w_uq': 1.998866e-02, 'mla_kv_norm': 1.245604e+00, 'mla_w_ukv': 4.279971e-01, 'hg_lb_logits': 6.608718e-03, 'hg_out_norm': 6.119752e-01, 'w_out': 4.954815e-01, 'mix_pre_norm': 2.139927e-01, 'mix_post_norm': 1.776629e+00, 'ffn_pre_norm': 1.056870e-01, 'ffn_post_norm': 1.632002e+00, 'ffn_w_up': 6.394879e-02, 'ffn_conv_w': 7.187851e-02, 'ffn_conv_b': 1.775267e-01, 'ffn_w_down': 1.322000e-01, 'w_ada': 8.312998e-01, 'b_ada': 1.641969e+00}


def _to_microbatches(a, axis):
    t = _jnp.moveaxis(a, axis, 0)
    t = t.reshape((N_MICROBATCH, t.shape[0] // N_MICROBATCH) + t.shape[1:])
    return _jnp.moveaxis(t, 1, axis + 1)


def setup_inputs(seed: int = 0) -> dict:
    inp = _fwd_setup_inputs(seed)
    key = _jax.random.fold_in(_jax.random.key(seed), 7919)
    shape, _ = _output_shape()
    out = dict(inp)
    out["loss_target"] = _jax.random.normal(_jax.random.fold_in(key, 0), shape, _jnp.float32)
    for i, name in enumerate(TWIN_WEIGHTS):
        w = inp[name].astype(_jnp.float32)
        if MOMENT_SCALE is None:
            s = _jnp.sqrt(_jnp.mean(_jnp.square(w)) + 1e-30)
        else:
            s = MOMENT_SCALE[name]
        km, kv = _jax.random.split(_jax.random.fold_in(key, i + 1))
        out[name] = w
        out["m_" + name] = s * _jax.random.normal(km, w.shape, _jnp.float32)
        out["v_" + name] = (s * s) * _jax.random.uniform(kv, w.shape, _jnp.float32, 0.5, 1.5)
    if N_MICROBATCH > 1:
        for name, axis in PER_EXAMPLE_BATCH_AXIS.items():
            out[name] = _to_microbatches(out[name], axis)
    return {'x': out['x'], 'c': out['c'], 'positions': out['positions'], 'w_in': out['w_in'], 's5_lambda_re': out['s5_lambda_re'], 's5_lambda_im': out['s5_lambda_im'], 's5_log_dt': out['s5_log_dt'], 's5_b_re': out['s5_b_re'], 's5_b_im': out['s5_b_im'], 's5_c_re': out['s5_c_re'], 's5_c_im': out['s5_c_im'], 's5_d': out['s5_d'], 's5_w_glu': out['s5_w_glu'], 'mla_q_norm': out['mla_q_norm'], 'mla_w_uq': out['mla_w_uq'], 'mla_kv_norm': out['mla_kv_norm'], 'mla_w_ukv': out['mla_w_ukv'], 'hg_lb_logits': out['hg_lb_logits'], 'hg_out_norm': out['hg_out_norm'], 'w_out': out['w_out'], 'mix_pre_norm': out['mix_pre_norm'], 'mix_post_norm': out['mix_post_norm'], 'ffn_pre_norm': out['ffn_pre_norm'], 'ffn_post_norm': out['ffn_post_norm'], 'ffn_w_up': out['ffn_w_up'], 'ffn_conv_w': out['ffn_conv_w'], 'ffn_conv_b': out['ffn_conv_b'], 'ffn_w_down': out['ffn_w_down'], 'w_ada': out['w_ada'], 'b_ada': out['b_ada'], 'loss_target': out['loss_target'], 'm_w_in': out['m_w_in'], 'm_s5_lambda_re': out['m_s5_lambda_re'], 'm_s5_lambda_im': out['m_s5_lambda_im'], 'm_s5_log_dt': out['m_s5_log_dt'], 'm_s5_b_re': out['m_s5_b_re'], 'm_s5_b_im': out['m_s5_b_im'], 'm_s5_c_re': out['m_s5_c_re'], 'm_s5_c_im': out['m_s5_c_im'], 'm_s5_d': out['m_s5_d'], 'm_s5_w_glu': out['m_s5_w_glu'], 'm_mla_q_norm': out['m_mla_q_norm'], 'm_mla_w_uq': out['m_mla_w_uq'], 'm_mla_kv_norm': out['m_mla_kv_norm'], 'm_mla_w_ukv': out['m_mla_w_ukv'], 'm_hg_lb_logits': out['m_hg_lb_logits'], 'm_hg_out_norm': out['m_hg_out_norm'], 'm_w_out': out['m_w_out'], 'm_mix_pre_norm': out['m_mix_pre_norm'], 'm_mix_post_norm': out['m_mix_post_norm'], 'm_ffn_pre_norm': out['m_ffn_pre_norm'], 'm_ffn_post_norm': out['m_ffn_post_norm'], 'm_ffn_w_up': out['m_ffn_w_up'], 'm_ffn_conv_w': out['m_ffn_conv_w'], 'm_ffn_conv_b': out['m_ffn_conv_b'], 'm_ffn_w_down': out['m_ffn_w_down'], 'm_w_ada': out['m_w_ada'], 'm_b_ada': out['m_b_ada'], 'v_w_in': out['v_w_in'], 'v_s5_lambda_re': out['v_s5_lambda_re'], 'v_s5_lambda_im': out['v_s5_lambda_im'], 'v_s5_log_dt': out['v_s5_log_dt'], 'v_s5_b_re': out['v_s5_b_re'], 'v_s5_b_im': out['v_s5_b_im'], 'v_s5_c_re': out['v_s5_c_re'], 'v_s5_c_im': out['v_s5_c_im'], 'v_s5_d': out['v_s5_d'], 'v_s5_w_glu': out['v_s5_w_glu'], 'v_mla_q_norm': out['v_mla_q_norm'], 'v_mla_w_uq': out['v_mla_w_uq'], 'v_mla_kv_norm': out['v_mla_kv_norm'], 'v_mla_w_ukv': out['v_mla_w_ukv'], 'v_hg_lb_logits': out['v_hg_lb_logits'], 'v_hg_out_norm': out['v_hg_out_norm'], 'v_w_out': out['v_w_out'], 'v_mix_pre_norm': out['v_mix_pre_norm'], 'v_mix_post_norm': out['v_mix_post_norm'], 'v_ffn_pre_norm': out['v_ffn_pre_norm'], 'v_ffn_post_norm': out['v_ffn_post_norm'], 'v_ffn_w_up': out['v_ffn_w_up'], 'v_ffn_conv_w': out['v_ffn_conv_w'], 'v_ffn_conv_b': out['v_ffn_conv_b'], 'v_ffn_w_down': out['v_ffn_w_down'], 'v_w_ada': out['v_w_ada'], 'v_b_ada': out['v_b_ada']}


def _loss(weights, diff, rest, loss_target):
    with _jax.named_scope("forward"):
        args = {**rest, TWIN_DIFF_INPUT: diff, **{k: w.astype(_WEIGHT_DTYPES[k]) for k, w in weights.items()}}
        y = _forward(args)
    with _jax.named_scope("loss_head"):
        err = _jnp.square(y.astype(_jnp.float32) - loss_target)
        return 0.5 * _jnp.sum(_jnp.mean(err, axis=-1)) if err.ndim else 0.5 * err


def _adamw(w, g, m, v):
    m = ADAM_B1 * m + (1.0 - ADAM_B1) * g
    v = ADAM_B2 * v + (1.0 - ADAM_B2) * _jnp.square(g)
    m_hat = m / (1.0 - ADAM_B1 ** ADAM_STEP)
    v_hat = v / (1.0 - ADAM_B2 ** ADAM_STEP)
    delta = -ADAM_LR * (m_hat / (_jnp.sqrt(v_hat) + ADAM_EPS) + ADAM_WD * w)
    return delta, m, v


def reference(x, c, positions, w_in, s5_lambda_re, s5_lambda_im, s5_log_dt, s5_b_re, s5_b_im, s5_c_re, s5_c_im, s5_d, s5_w_glu, mla_q_norm, mla_w_uq, mla_kv_norm, mla_w_ukv, hg_lb_logits, hg_out_norm, w_out, mix_pre_norm, mix_post_norm, ffn_pre_norm, ffn_post_norm, ffn_w_up, ffn_conv_w, ffn_conv_b, ffn_w_down, w_ada, b_ada, loss_target, m_w_in, m_s5_lambda_re, m_s5_lambda_im, m_s5_log_dt, m_s5_b_re, m_s5_b_im, m_s5_c_re, m_s5_c_im, m_s5_d, m_s5_w_glu, m_mla_q_norm, m_mla_w_uq, m_mla_kv_norm, m_mla_w_ukv, m_hg_lb_logits, m_hg_out_norm, m_w_out, m_mix_pre_norm, m_mix_post_norm, m_ffn_pre_norm, m_ffn_post_norm, m_ffn_w_up, m_ffn_conv_w, m_ffn_conv_b, m_ffn_w_down, m_w_ada, m_b_ada, v_w_in, v_s5_lambda_re, v_s5_lambda_im, v_s5_log_dt, v_s5_b_re, v_s5_b_im, v_s5_c_re, v_s5_c_im, v_s5_d, v_s5_w_glu, v_mla_q_norm, v_mla_w_uq, v_mla_kv_norm, v_mla_w_ukv, v_hg_lb_logits, v_hg_out_norm, v_w_out, v_mix_pre_norm, v_mix_post_norm, v_ffn_pre_norm, v_ffn_post_norm, v_ffn_w_up, v_ffn_conv_w, v_ffn_conv_b, v_ffn_w_down, v_w_ada, v_b_ada):
    given = dict(x=x, c=c, positions=positions, w_in=w_in, s5_lambda_re=s5_lambda_re, s5_lambda_im=s5_lambda_im, s5_log_dt=s5_log_dt, s5_b_re=s5_b_re, s5_b_im=s5_b_im, s5_c_re=s5_c_re, s5_c_im=s5_c_im, s5_d=s5_d, s5_w_glu=s5_w_glu, mla_q_norm=mla_q_norm, mla_w_uq=mla_w_uq, mla_kv_norm=mla_kv_norm, mla_w_ukv=mla_w_ukv, hg_lb_logits=hg_lb_logits, hg_out_norm=hg_out_norm, w_out=w_out, mix_pre_norm=mix_pre_norm, mix_post_norm=mix_post_norm, ffn_pre_norm=ffn_pre_norm, ffn_post_norm=ffn_post_norm, ffn_w_up=ffn_w_up, ffn_conv_w=ffn_conv_w, ffn_conv_b=ffn_conv_b, ffn_w_down=ffn_w_down, w_ada=w_ada, b_ada=b_ada, loss_target=loss_target, m_w_in=m_w_in, m_s5_lambda_re=m_s5_lambda_re, m_s5_lambda_im=m_s5_lambda_im, m_s5_log_dt=m_s5_log_dt, m_s5_b_re=m_s5_b_re, m_s5_b_im=m_s5_b_im, m_s5_c_re=m_s5_c_re, m_s5_c_im=m_s5_c_im, m_s5_d=m_s5_d, m_s5_w_glu=m_s5_w_glu, m_mla_q_norm=m_mla_q_norm, m_mla_w_uq=m_mla_w_uq, m_mla_kv_norm=m_mla_kv_norm, m_mla_w_ukv=m_mla_w_ukv, m_hg_lb_logits=m_hg_lb_logits, m_hg_out_norm=m_hg_out_norm, m_w_out=m_w_out, m_mix_pre_norm=m_mix_pre_norm, m_mix_post_norm=m_mix_post_norm, m_ffn_pre_norm=m_ffn_pre_norm, m_ffn_post_norm=m_ffn_post_norm, m_ffn_w_up=m_ffn_w_up, m_ffn_conv_w=m_ffn_conv_w, m_ffn_conv_b=m_ffn_conv_b, m_ffn_w_down=m_ffn_w_down, m_w_ada=m_w_ada, m_b_ada=m_b_ada, v_w_in=v_w_in, v_s5_lambda_re=v_s5_lambda_re, v_s5_lambda_im=v_s5_lambda_im, v_s5_log_dt=v_s5_log_dt, v_s5_b_re=v_s5_b_re, v_s5_b_im=v_s5_b_im, v_s5_c_re=v_s5_c_re, v_s5_c_im=v_s5_c_im, v_s5_d=v_s5_d, v_s5_w_glu=v_s5_w_glu, v_mla_q_norm=v_mla_q_norm, v_mla_w_uq=v_mla_w_uq, v_mla_kv_norm=v_mla_kv_norm, v_mla_w_ukv=v_mla_w_ukv, v_hg_lb_logits=v_hg_lb_logits, v_hg_out_norm=v_hg_out_norm, v_w_out=v_w_out, v_mix_pre_norm=v_mix_pre_norm, v_mix_post_norm=v_mix_post_norm, v_ffn_pre_norm=v_ffn_pre_norm, v_ffn_post_norm=v_ffn_post_norm, v_ffn_w_up=v_ffn_w_up, v_ffn_conv_w=v_ffn_conv_w, v_ffn_conv_b=v_ffn_conv_b, v_ffn_w_down=v_ffn_w_down, v_w_ada=v_w_ada, v_b_ada=v_b_ada)
    weights = {n: given[n] for n in TWIN_WEIGHTS}
    shared = {n: given[n] for n in SHARED_INPUTS}
    per_example = {n: given[n] for n in ['x', 'c', 'positions']}
    grad_fn = _jax.value_and_grad(_loss, argnums=(0, 1))

    def one_microbatch(ex, loss_target):
        ex = dict(ex)
        diff = ex.pop(TWIN_DIFF_INPUT)
        return grad_fn(weights, diff, {**shared, **ex}, loss_target)

    if N_MICROBATCH == 1:
        loss, (grad_w, grad_x) = one_microbatch(per_example, given["loss_target"])
    else:
        def body(carry, xs):
            loss_sum, grad_sum = carry
            l_k, (gw_k, gx_k) = one_microbatch(xs[0], xs[1])
            with _jax.named_scope("update"):
                return (loss_sum + l_k, _jax.tree.map(_jnp.add, grad_sum, gw_k)), gx_k

        init = (_jnp.zeros((), _jnp.float32), _jax.tree.map(_jnp.zeros_like, weights))
        (loss, grad_w), grad_x = _jax.lax.scan(body, init, (per_example, given["loss_target"]))
    with _jax.named_scope("update"):
        delta_w, new_m, new_v = {}, {}, {}
        for n in TWIN_WEIGHTS:
            delta_w[n], new_m[n], new_v[n] = _adamw(weights[n], grad_w[n], given["m_" + n], given["v_" + n])
    return (loss, grad_x, *[grad_w[n] for n in TWIN_WEIGHTS], *[delta_w[n] for n in TWIN_WEIGHTS],
            *[new_m[n] for n in TWIN_WEIGHTS], *[new_v[n] for n in TWIN_WEIGHTS])
```

```python
import functools
import math

import numpy as np
import jax
import jax.numpy as jnp
from jax import lax
from jax.experimental import pallas as pl
from jax.experimental.pallas import tpu as pltpu

F32 = jnp.float32
BF16 = jnp.bfloat16
N_DEV = 8
V7X_VMEM_LIMIT = 56 * 1024 * 1024
LANE = 128

D_MODEL = 2048
S5_W = 512
S5_G = 32
S5_C = 16
S5_P = 64
S5_N = S5_G * S5_P
S5_TL = 512
MLA_H = 8
MLA_NOPE = 128
MLA_ROPE = 64
MLA_V = 128
MLA_HW = 256
HG_H = 4
HG_D = 128
HG_CH = 16
D_FF = 5504
D_FFP = 5632
D_IN = 3392
D_INP = 3456
EPS = 1e-6
MASK_VALUE = -1e30
ROPE_THETA = 10000.0
ATT_SCALE = (MLA_NOPE + MLA_ROPE) ** -0.5

ADAM_LR = 0.001
ADAM_B1 = 0.9
ADAM_B2 = 0.999
ADAM_EPS = 1e-08
ADAM_WD = 0.01
ADAM_STEP = 10

_IN_PERM = np.concatenate([np.arange(0, 1024), np.arange(1344, 3392), np.arange(1024, 1344)])
_IN_INV = np.argsort(_IN_PERM)

_NN = (((1,), (0,)), ((), ()))
_NT = (((1,), (1,)), ((), ()))
_TN = (((0,), (0,)), ((), ()))


def _cp(*sem):
    return pltpu.CompilerParams(dimension_semantics=sem, vmem_limit_bytes=V7X_VMEM_LIMIT)


def _tile(n, cap, align=LANE):
    if n <= cap:
        return n
    t = (cap // align) * align
    while t >= align:
        if n % t == 0:
            return t
        t -= align
    raise ValueError(f"no tile for {n}")


def _bdot(a, b, dims):
    return lax.dot_general(a.astype(BF16), b.astype(BF16), dims, preferred_element_type=F32)


def _mm(a, b, mode, out_dtype=F32, a_col0=0, a_cols=None, name="mm"):
    if mode == "tn":
        K = a.shape[0]
        M = a_cols if a_cols is not None else a.shape[1]
        N = b.shape[1]
    else:
        M = a.shape[0]
        K = a_cols if a_cols is not None else a.shape[1]
        N = b.shape[0] if mode == "nt" else b.shape[1]
    tm = _tile(M, 512, 8 if M < LANE else LANE)
    tn = _tile(N, 512)
    tk = _tile(K, 1024, 8 if K < LANE else LANE)
    nk = K // tk
    if mode == "tn":
        assert a_col0 % tm == 0
        a_spec = pl.BlockSpec((tk, tm), lambda i, j, k: (k, i + a_col0 // tm))
        b_spec = pl.BlockSpec((tk, tn), lambda i, j, k: (k, j))
        dims = _TN
    else:
        assert a_col0 % tk == 0
        a_spec = pl.BlockSpec((tm, tk), lambda i, j, k: (i, k + a_col0 // tk))
        if mode == "nn":
            b_spec = pl.BlockSpec((tk, tn), lambda i, j, k: (k, j))
            dims = _NN
        else:
            b_spec = pl.BlockSpec((tn, tk), lambda i, j, k: (j, k))
            dims = _NT

    def body(a_ref, b_ref, o_ref, acc_ref):
        k = pl.program_id(2)

        @pl.when(k == 0)
        def _():
            acc_ref[...] = jnp.zeros_like(acc_ref)

        acc_ref[...] += _bdot(a_ref[...], b_ref[...], dims)

        @pl.when(k == nk - 1)
        def _():
            o_ref[...] = acc_ref[...].astype(o_ref.dtype)

    return pl.pallas_call(
        body,
        name=name,
        grid=(M // tm, N // tn, nk),
        in_specs=[a_spec, b_spec],
        out_specs=pl.BlockSpec((tm, tn), lambda i, j, k: (i, j)),
        out_shape=jax.ShapeDtypeStruct((M, N), out_dtype),
        scratch_shapes=[pltpu.VMEM((tm, tn), F32)],
        compiler_params=_cp("parallel", "parallel", "arbitrary"),
    )(a, b)


def _row_spec(tm, width, cb):
    return pl.BlockSpec((tm, width), lambda i: (i, cb))


def _rowwise(fn, rows, params, outs, name, tm=256):
    S = rows[0][0].shape[0]
    nr, npar = len(rows), len(params)

    def body(*refs):
        xs = [r[...].astype(F32) for r in refs[:nr]]
        ps = [p[...] for p in refs[nr:nr + npar]]
        res = fn(*xs, *ps)
        for o, r in zip(refs[nr + npar:], res):
            o[...] = r.astype(o.dtype)

    res = pl.pallas_call(
        body,
        name=name,
        grid=(S // tm,),
        in_specs=[_row_spec(tm, w, cb) for _, w, cb in rows]
        + [pl.BlockSpec(p.shape, lambda i: (0, 0)) for p in params],
        out_specs=[_row_spec(tm, w, 0) for w, _ in outs],
        out_shape=[jax.ShapeDtypeStruct((S, w), dt) for w, dt in outs],
        compiler_params=_cp("parallel"),
    )(*[r[0] for r in rows], *params)
    return list(res)


def _rowwise_vjp(fn, rows, params, cts, name, row_grads, add_rows=None, tm=256):
    S = rows[0][0].shape[0]
    add_rows = add_rows or {}
    nr, npar = len(rows), len(params)
    flat_cts = [c for group in cts for c in group]
    ncts = len(flat_cts)
    add_keys = sorted(add_rows)
    nadd = len(add_keys)
    grad_idx = [i for i in range(nr) if row_grads[i]]

    def body(*refs):
        i = pl.program_id(0)
        xs = [r[...].astype(F32) for r in refs[:nr]]
        ps = [p[...] for p in refs[nr:nr + npar]]
        ct_refs = refs[nr + npar:nr + npar + ncts]
        add_refs = refs[nr + npar + ncts:nr + npar + ncts + nadd]
        out_refs = refs[nr + npar + ncts + nadd:]
        ct_vals, pos = [], 0
        for group in cts:
            v = ct_refs[pos][...].astype(F32)
            for r in ct_refs[pos + 1:pos + len(group)]:
                v = v + r[...].astype(F32)
            pos += len(group)
            ct_vals.append(v)
        _, vjp = jax.vjp(lambda *a: tuple(fn(*a)), *xs, *ps)
        grads = vjp(tuple(ct_vals))
        for o, gi in zip(out_refs[:len(grad_idx)], grad_idx):
            g = grads[gi]
            if gi in add_rows:
                g = g + add_refs[add_keys.index(gi)][...].astype(F32)
            o[...] = g.astype(o.dtype)
        dprefs = out_refs[len(grad_idx):]

        @pl.when(i == 0)
        def _():
            for dp in dprefs:
                dp[...] = jnp.zeros_like(dp)

        for dp, g in zip(dprefs, grads[nr:]):
            dp[...] += g

    res = pl.pallas_call(
        body,
        name=name,
        grid=(S // tm,),
        in_specs=[_row_spec(tm, w, cb) for _, w, cb in rows]
        + [pl.BlockSpec(p.shape, lambda i: (0, 0)) for p in params]
        + [_row_spec(tm, w, cb) for _, w, cb in flat_cts]
        + [_row_spec(tm, add_rows[k][1], add_rows[k][2]) for k in add_keys],
        out_specs=[_row_spec(tm, rows[gi][1], 0) for gi in grad_idx]
        + [pl.BlockSpec(p.shape, lambda i: (0, 0)) for p in params],
        out_shape=[jax.ShapeDtypeStruct((S, rows[gi][1]), F32) for gi in grad_idx]
        + [jax.ShapeDtypeStruct(p.shape, F32) for p in params],
        compiler_params=_cp("arbitrary"),
    )(*[r[0] for r in rows], *params, *[c[0] for c in flat_cts], *[add_rows[k][0] for k in add_keys])
    res = list(res)
    return res[:len(grad_idx)], res[len(grad_idx):]


def _rms(x, gain):
    return x * lax.rsqrt(jnp.mean(x * x, axis=-1, keepdims=True) + EPS) * gain


def _f_pre(x, gain, sc, sh):
    return (_rms(x, gain) * (1.0 + sc) + sh,)


def _f_post(x, y, gain, g):
    return (x + g * _rms(y, gain),)


def _f_norm(x, gain):
    return (_rms(x, gain),)


def _f_s5a(yc, u, d):
    return (jax.nn.gelu(yc + d * u, approximate=True),)


def _f_s5b(g, z):
    return (g * jax.nn.sigmoid(z),)


def _loss_head(y, target, tm=256):
    S, D = y.shape

    def body(y_ref, t_ref, dy_ref, acc_ref):
        i = pl.program_id(0)
        e = y_ref[...] - t_ref[...]
        dy_ref[...] = e * (1.0 / D)

        @pl.when(i == 0)
        def _():
            acc_ref[...] = jnp.zeros_like(acc_ref)

        acc_ref[...] += jnp.sum(e * e, axis=0, keepdims=True)

    dy, acc = pl.pallas_call(
        body,
        name="loss_head",
        grid=(S // tm,),
        in_specs=[_row_spec(tm, D, 0), _row_spec(tm, D, 0)],
        out_specs=[_row_spec(tm, D, 0), pl.BlockSpec((1, D), lambda i: (0, 0))],
        out_shape=[jax.ShapeDtypeStruct((S, D), F32), jax.ShapeDtypeStruct((1, D), F32)],
        compiler_params=_cp("arbitrary"),
    )(y, target)
    return 0.5 * jnp.sum(acc) / D, dy


def _s5_tile_scan(xr, xi, tab_ref, reverse, row8):
    for k in (1, 2, 4):
        pr = tab_ref[pl.ds(k - 1, 1), 0:S5_TL] if not reverse else tab_ref[pl.ds(8 - k, 1), 0:S5_TL]
        pi = tab_ref[pl.ds(k - 1, 1), S5_TL:2 * S5_TL] if not reverse else tab_ref[pl.ds(8 - k, 1), S5_TL:2 * S5_TL]
        if not reverse:
            keep = row8 >= k
            sr = jnp.where(keep, pltpu.roll(xr, k, 0), 0.0)
            si = jnp.where(keep, pltpu.roll(xi, k, 0), 0.0)
        else:
            keep = row8 < 8 - k
            sr = jnp.where(keep, pltpu.roll(xr, 8 - k, 0), 0.0)
            si = jnp.where(keep, pltpu.roll(xi, 8 - k, 0), 0.0)
        xr, xi = xr + pr * sr - pi * si, xi + pr * si + pi * sr
    return xr, xi


def _s5_scan(bu, tab, reverse=False, h=None, bu_fwd=None, tr=512):
    S = bu.shape[0]
    tr = min(tr, S)
    nl = S5_N // S5_TL
    nrb = S // tr
    w = 2 * S5_TL
    nt = tr // 8
    rmap = (lambda j, i: (i, j)) if not reverse else (lambda j, i: (nrb - 1 - i, j))

    def body(*refs):
        if reverse:
            x_ref, tab_ref, h_ref, b_ref, o_ref, acc_ref, cr_ref, ci_ref = refs
        else:
            x_ref, tab_ref, o_ref, cr_ref, ci_ref = refs
        i = pl.program_id(1)
        row8 = lax.broadcasted_iota(jnp.int32, (8, S5_TL), 0)

        @pl.when(i == 0)
        def _():
            cr_ref[...] = jnp.zeros_like(cr_ref)
            ci_ref[...] = jnp.zeros_like(ci_ref)
            if reverse:
                acc_ref[...] = jnp.zeros_like(acc_ref)

        tr_all = tab_ref[:, 0:S5_TL]
        ti_all = tab_ref[:, S5_TL:w]

        def tile(t, carry):
            tt = (nt - 1 - t) if reverse else t
            r = pl.ds(pl.multiple_of(tt * 8, 8), 8)
            xr, xi = _s5_tile_scan(x_ref[r, 0:S5_TL], x_ref[r, S5_TL:w], tab_ref, reverse, row8)
            cr = jnp.broadcast_to(cr_ref[...], (8, S5_TL))
            ci = jnp.broadcast_to(ci_ref[...], (8, S5_TL))
            hr = xr + tr_all * cr - ti_all * ci
            hi = xi + tr_all * ci + ti_all * cr
            o_ref[r, 0:S5_TL] = hr
            o_ref[r, S5_TL:w] = hi
            edge = pl.ds(tt * 8, 1) if reverse else pl.ds(tt * 8 + 7, 1)
            cr_ref[...] = o_ref[edge, 0:S5_TL]
            ci_ref[...] = o_ref[edge, S5_TL:w]
            if reverse:
                dr = h_ref[r, 0:S5_TL] - b_ref[r, 0:S5_TL]
                di = h_ref[r, S5_TL:w] - b_ref[r, S5_TL:w]
                acc_ref[:, 0:S5_TL] += hr * dr + hi * di
                acc_ref[:, S5_TL:w] += hi * dr - hr * di
            return carry

        lax.fori_loop(0, nt, tile, 0)

    blk = pl.BlockSpec((tr, w), rmap)
    in_specs = [blk, pl.BlockSpec((8, w), lambda j, i: (0, j))]
    out_specs = [blk]
    out_shape = [jax.ShapeDtypeStruct((S, 2 * S5_N), F32)]
    args = [bu, tab]
    if reverse:
        in_specs += [blk, blk]
        args += [h, bu_fwd]
        out_specs.append(pl.BlockSpec((8, w), lambda j, i: (0, j)))
        out_shape.append(jax.ShapeDtypeStruct((8, 2 * S5_N), F32))
    res = pl.pallas_call(
        body,
        name="s5_scan_bwd" if reverse else "s5_scan_fwd",
        grid=(nl, nrb),
        in_specs=in_specs,
        out_specs=out_specs,
        out_shape=out_shape,
        scratch_shapes=[pltpu.VMEM((1, S5_TL), F32), pltpu.VMEM((1, S5_TL), F32)],
        compiler_params=_cp("parallel", "arbitrary"),
    )(*args)
    return res if reverse else res[0]


def _ri_cols(re, im):
    lead = re.shape[:-1]
    nl = S5_N // S5_TL
    z = jnp.stack([re.reshape(*lead, nl, S5_TL), im.reshape(*lead, nl, S5_TL)], axis=-2)
    return z.reshape(*lead, 2 * S5_N)


def _ri_split(z):
    lead = z.shape[:-1]
    nl = S5_N // S5_TL
    z = z.reshape(*lead, nl, 2, S5_TL)
    return z[..., 0, :].reshape(*lead, S5_N), z[..., 1, :].reshape(*lead, S5_N)


def _s5_prep(lre, lim, logdt, bre, bim, cre, cim):
    lam = lax.complex(lre, lim)
    dt = jnp.exp(logdt)[:, None]
    lam_bar = jnp.exp(lam * dt)
    b = lax.complex(bre, bim)
    b_bar = ((lam_bar - 1.0) / lam)[..., None] * b
    eye = jnp.eye(S5_G, dtype=F32)
    bd_re = jnp.einsum("gpc,gh->gchp", jnp.real(b_bar), eye).reshape(S5_W, S5_N)
    bd_im = jnp.einsum("gpc,gh->gchp", jnp.imag(b_bar), eye).reshape(S5_W, S5_N)
    bd = _ri_cols(bd_re, bd_im)
    cd_re = jnp.einsum("gcp,gh->gchp", cre, eye).reshape(S5_W, S5_N)
    cd_im = jnp.einsum("gcp,gh->gchp", -cim, eye).reshape(S5_W, S5_N)
    cdt = _ri_cols(cd_re, cd_im)
    return jnp.real(lam_bar).reshape(1, S5_N), jnp.imag(lam_bar).reshape(1, S5_N), bd, cdt


def _s5_tables(lre, lim, logdt):
    lam = lax.complex(lre, lim)
    dt = jnp.exp(logdt)[:, None]
    k = jnp.arange(1, 9, dtype=F32)[:, None, None]
    pw = jnp.exp((lam * dt)[None] * k).reshape(8, S5_N)
    fwd = _ri_cols(jnp.real(pw), jnp.imag(pw))
    rev = _ri_cols(jnp.real(pw)[::-1], -jnp.imag(pw)[::-1])
    return fwd, rev


def _rope_tables(positions):
    inv_freq = 1.0 / (ROPE_THETA ** (jnp.arange(0, MLA_ROPE, 2, dtype=F32) / MLA_ROPE))
    ang = positions.astype(F32)[:, None] * inv_freq
    cos, sin = jnp.cos(ang), jnp.sin(ang)
    z = jnp.zeros_like(cos)
    cs = jnp.concatenate([cos, cos, z, z], axis=-1)
    sn = jnp.concatenate([-sin, sin, z, z], axis=-1)
    return cs, sn


def _rope_fwd(qraw, kvraw, proj, cs, sn, tm=256):
    S = qraw.shape[0]
    HW = MLA_H * MLA_HW

    def rope(x, c, s):
        lane = lax.broadcasted_iota(jnp.int32, x.shape, 1)
        sw = jnp.where(lane < 32, pltpu.roll(x, 96, 1), jnp.where(lane < 64, pltpu.roll(x, 32, 1), 0.0))
        return x * c + sw * s

    def body(q_ref, kv_ref, kr_ref, cs_ref, sn_ref, qo_ref, ko_ref, vo_ref):
        c, s = cs_ref[...], sn_ref[...]
        kr = rope(kr_ref[...], c, s).astype(BF16)
        for h in range(MLA_H):
            o = h * MLA_HW
            qo_ref[:, o:o + 128] = q_ref[:, o:o + 128].astype(BF16)
            qo_ref[:, o + 128:o + 256] = rope(q_ref[:, o + 128:o + 256], c, s).astype(BF16)
            ko_ref[:, o:o + 128] = kv_ref[:, o:o + 128].astype(BF16)
            ko_ref[:, o + 128:o + 256] = kr
            vo_ref[:, h * 128:(h + 1) * 128] = kv_ref[:, o + 128:o + 256].astype(BF16)

    return pl.pallas_call(
        body,
        name="rope_fwd",
        grid=(S // tm,),
        in_specs=[_row_spec(tm, HW, 0), _row_spec(tm, HW, 0), _row_spec(tm, 128, (D_INP - 128) // 128),
                  _row_spec(tm, 128, 0), _row_spec(tm, 128, 0)],
        out_specs=[_row_spec(tm, HW, 0), _row_spec(tm, HW, 0), _row_spec(tm, MLA_H * MLA_V, 0)],
        out_shape=[jax.ShapeDtypeStruct((S, HW), BF16), jax.ShapeDtypeStruct((S, HW), BF16),
                   jax.ShapeDtypeStruct((S, MLA_H * MLA_V), BF16)],
        compiler_params=_cp("parallel"),
    )(qraw, kvraw, proj, cs, sn)


def _rope_bwd(dq, dk, dv, cs, sn, tm=256):
    S = dq.shape[0]
    HW = MLA_H * MLA_HW

    def rope_t(x, c, s):
        lane = lax.broadcasted_iota(jnp.int32, x.shape, 1)
        w = x * s
        sw = jnp.where(lane < 32, pltpu.roll(w, 96, 1), jnp.where(lane < 64, pltpu.roll(w, 32, 1), 0.0))
        return x * c + sw

    def body(dq_ref, dk_ref, dv_ref, cs_ref, sn_ref, qo_ref, kvo_ref, kro_ref):
        c, s = cs_ref[...], sn_ref[...]
        kr = jnp.zeros((tm, 128), F32)
        for h in range(MLA_H):
            o = h * MLA_HW
            qo_ref[:, o:o + 128] = dq_ref[:, o:o + 128]
            qo_ref[:, o + 128:o + 256] = rope_t(dq_ref[:, o + 128:o + 256], c, s)
            kvo_ref[:, o:o + 128] = dk_ref[:, o:o + 128]
            kvo_ref[:, o + 128:o + 256] = dv_ref[:, h * 128:(h + 1) * 128]
            kr = kr + dk_ref[:, o + 128:o + 256]
        kro_ref[...] = rope_t(kr, c, s)

    return pl.pallas_call(
        body,
        name="rope_bwd",
        grid=(S // tm,),
        in_specs=[_row_spec(tm, HW, 0), _row_spec(tm, HW, 0), _row_spec(tm, MLA_H * MLA_V, 0),
                  _row_spec(tm, 128, 0), _row_spec(tm, 128, 0)],
        out_specs=[_row_spec(tm, HW, 0), _row_spec(tm, HW, 0), _row_spec(tm, 128, 0)],
        out_shape=[jax.ShapeDtypeStruct((S, HW), F32), jax.ShapeDtypeStruct((S, HW), F32),
                   jax.ShapeDtypeStruct((S, 128), F32)],
        compiler_params=_cp("parallel"),
    )(dq, dk, dv, cs, sn)


ATT_T = 256


def _att_mask(i, j, t):
    rows = i * t + lax.broadcasted_iota(jnp.int32, (t, t), 0)
    cols = j * t + lax.broadcasted_iota(jnp.int32, (t, t), 1)
    return cols <= rows


def _flash_fwd(q, k, v):
    S = q.shape[0]
    t = min(ATT_T, S)
    nq = S // t

    def body(q_ref, k_ref, v_ref, o_ref, lse_ref):
        i = pl.program_id(1)
        qb = q_ref[...]

        def step(j, carry):
            m, l, acc = carry
            r = pl.ds(pl.multiple_of(j * t, t), t)
            s = _bdot(qb, k_ref[r, :], _NT) * ATT_SCALE
            s = jnp.where(_att_mask(i, j, t), s, MASK_VALUE)
            m_new = jnp.maximum(m, jnp.max(s, axis=-1, keepdims=True))
            alpha = jnp.exp(m - m_new)
            p = jnp.exp(s - m_new)
            l = alpha * l + jnp.sum(p, axis=-1, keepdims=True)
            acc = alpha * acc + _bdot(p, v_ref[r, :], _NN)
            return m_new, l, acc

        m0 = jnp.full((t, 1), MASK_VALUE, F32)
        m, l, acc = lax.fori_loop(0, i + 1, step, (m0, jnp.zeros((t, 1), F32), jnp.zeros((t, MLA_V), F32)))
        o_ref[...] = acc / l
        lse_ref[...] = jnp.broadcast_to(m + jnp.log(l), (t, 128))

    return pl.pallas_call(
        body,
        name="flash_fwd",
        grid=(MLA_H, nq),
        in_specs=[pl.BlockSpec((t, MLA_HW), lambda h, i: (i, h)),
                  pl.BlockSpec((S, MLA_HW), lambda h, i: (0, h)),
                  pl.BlockSpec((S, MLA_V), lambda h, i: (0, h))],
        out_specs=[pl.BlockSpec((t, MLA_V), lambda h, i: (i, h)), pl.BlockSpec((t, 128), lambda h, i: (i, h))],
        out_shape=[jax.ShapeDtypeStruct((S, MLA_H * MLA_V), F32), jax.ShapeDtypeStruct((S, MLA_H * 128), F32)],
        compiler_params=_cp("parallel", "parallel"),
    )(q, k, v)


def _flash_bwd_dq(q, k, v, o, lse, dcat):
    S = q.shape[0]
    t = min(ATT_T, S)
    nq = S // t
    do_cb = S5_W // MLA_V

    def body(q_ref, k_ref, v_ref, o_ref, lse_ref, do_ref, dq_ref):
        i = pl.program_id(1)
        qb = q_ref[...]
        do = do_ref[...]
        delta = jnp.sum(do * o_ref[...], axis=-1, keepdims=True)
        lse1 = jnp.max(lse_ref[...], axis=-1, keepdims=True)
        dob = do.astype(BF16)

        def step(j, dq):
            r = pl.ds(pl.multiple_of(j * t, t), t)
            kb = k_ref[r, :]
            s = _bdot(qb, kb, _NT) * ATT_SCALE
            p = jnp.where(_att_mask(i, j, t), jnp.exp(s - lse1), 0.0)
            dp = _bdot(dob, v_ref[r, :], _NT)
            ds = p * (dp - delta) * ATT_SCALE
            return dq + _bdot(ds, kb, _NN)

        dq_ref[...] = lax.fori_loop(0, i + 1, step, jnp.zeros((t, MLA_HW), F32))

    return pl.pallas_call(
        body,
        name="flash_bwd_dq",
        grid=(MLA_H, nq),
        in_specs=[pl.BlockSpec((t, MLA_HW), lambda h, i: (i, h)),
                  pl.BlockSpec((S, MLA_HW), lambda h, i: (0, h)),
                  pl.BlockSpec((S, MLA_V), lambda h, i: (0, h)),
                  pl.BlockSpec((t, MLA_V), lambda h, i: (i, h)),
                  pl.BlockSpec((t, 128), lambda h, i: (i, h)),
                  pl.BlockSpec((t, MLA_V), lambda h, i: (i, do_cb + h))],
        out_specs=pl.BlockSpec((t, MLA_HW), lambda h, i: (i, h)),
        out_shape=jax.ShapeDtypeStruct((S, MLA_H * MLA_HW), F32),
        compiler_params=_cp("parallel", "parallel"),
    )(q, k, v, o, lse, dcat)


def _flash_bwd_dkv(q, k, v, o, lse, dcat):
    S = q.shape[0]
    t = min(ATT_T, S)
    nq = S // t
    do_cb = S5_W // MLA_V

    def body(q_ref, k_ref, v_ref, o_ref, lse_ref, do_ref, dk_ref, dv_ref):
        j = pl.program_id(1)
        kb = k_ref[...]
        vb = v_ref[...]

        def step(i, carry):
            dk, dv = carry
            r = pl.ds(pl.multiple_of(i * t, t), t)
            qb = q_ref[r, :]
            do = do_ref[r, :]
            delta = jnp.sum(do * o_ref[r, :], axis=-1, keepdims=True)
            lse1 = jnp.max(lse_ref[r, :], axis=-1, keepdims=True)
            s = _bdot(qb, kb, _NT) * ATT_SCALE
            p = jnp.where(_att_mask(i, j, t), jnp.exp(s - lse1), 0.0)
            dob = do.astype(BF16)
            dv = dv + _bdot(p, dob, _TN)
            dp = _bdot(dob, vb, _NT)
            ds = p * (dp - delta) * ATT_SCALE
            dk = dk + _bdot(ds, qb, _TN)
            return dk, dv

        dk, dv = lax.fori_loop(j, nq, step, (jnp.zeros((t, MLA_HW), F32), jnp.zeros((t, MLA_V), F32)))
        dk_ref[...] = dk
        dv_ref[...] = dv

    return pl.pallas_call(
        body,
        name="flash_bwd_dkv",
        grid=(MLA_H, nq),
        in_specs=[pl.BlockSpec((S, MLA_HW), lambda h, j: (0, h)),
                  pl.BlockSpec((t, MLA_HW), lambda h, j: (j, h)),
                  pl.BlockSpec((t, MLA_V), lambda h, j: (j, h)),
                  pl.BlockSpec((S, MLA_V), lambda h, j: (0, h)),
                  pl.BlockSpec((S, 128), lambda h, j: (0, h)),
                  pl.BlockSpec((S, MLA_V), lambda h, j: (0, do_cb + h))],
        out_specs=[pl.BlockSpec((t, MLA_HW), lambda h, j: (j, h)), pl.BlockSpec((t, MLA_V), lambda h, j: (j, h))],
        out_shape=[jax.ShapeDtypeStruct((S, MLA_H * MLA_HW), F32), jax.ShapeDtypeStruct((S, MLA_H * MLA_V), F32)],
        compiler_params=_cp("parallel", "parallel"),
    )(q, k, v, o, lse, dcat)


def _split3(x):
    x1 = x.astype(BF16)
    r1 = x - x1.astype(F32)
    x2 = r1.astype(BF16)
    x3 = (r1 - x2.astype(F32)).astype(BF16)
    return x1, x2, x3


def _tri_matmul(x, upper):
    n = x.shape[0]
    r = lax.broadcasted_iota(jnp.int32, (n, n), 0)
    c = lax.broadcasted_iota(jnp.int32, (n, n), 1)
    tri = jnp.where((r <= c) if upper else (r >= c), 1.0, 0.0).astype(BF16)
    x1, x2, x3 = _split3(x)
    dot = lambda v: lax.dot_general(tri, v, _NN, preferred_element_type=F32)
    return dot(x1) + dot(x2) + dot(x3)


@jax.custom_vjp
def _cumsum_rows(x):
    return _tri_matmul(x, False)


def _cumsum_rows_fwd(x):
    return _tri_matmul(x, False), None


def _cumsum_rows_bwd(_, ct):
    return (_tri_matmul(ct, True),)


_cumsum_rows.defvjp(_cumsum_rows_fwd, _cumsum_rows_bwd)


def _hg_step(qin, fin, vin, gin, st, lb, on):
    n = qin.shape[0]
    sig = jax.nn.sigmoid(fin)
    g = jnp.log(lb + (1.0 - lb) * sig)
    k = (1.0 - lb) * jax.nn.sigmoid(-fin)
    q = qin * jax.nn.sigmoid(qin)
    b = _cumsum_rows(g)
    o = _bdot(q * jnp.exp(b), st, _NT)
    row = lax.broadcasted_iota(jnp.int32, (n, HG_D), 0)
    row1 = lax.broadcasted_iota(jnp.int32, (n, 1), 0)
    b_s = None
    for s in range(n):
        sel = row == s
        b_s = jnp.sum(jnp.where(sel, b, 0.0), axis=0, keepdims=True)
        k_s = jnp.sum(jnp.where(sel, k, 0.0), axis=0, keepdims=True)
        v_s = jnp.sum(jnp.where(sel, vin, 0.0), axis=0, keepdims=True)
        e = jnp.exp(jnp.minimum(b - b_s, 0.0))
        c = jnp.sum(q * e * k_s, axis=-1, keepdims=True)
        o = o + jnp.where(row1 >= s, c, 0.0) * v_s
    st_new = st * jnp.exp(b_s) + _bdot(vin, k * jnp.exp(b_s - b), _TN)
    y = _rms(o, on) * (gin * jax.nn.sigmoid(gin))
    return y, st_new


def _hg_specs(tb, nb, reverse):
    rm = (lambda i: nb - 1 - i) if reverse else (lambda i: i)
    base = 1024 // HG_D
    return [pl.BlockSpec((tb, HG_D), lambda h, i, o=o: (rm(i), base + o * HG_H + h)) for o in range(4)], rm


def _hg_fwd(proj, lb, on, tb=256):
    S = proj.shape[0]
    nb = S // tb
    nc = tb // HG_CH
    in_specs, rm = _hg_specs(tb, nb, False)

    def body(q_ref, f_ref, v_ref, g_ref, lb_ref, on_ref, y_ref, sts_ref, st_ref):
        i = pl.program_id(1)

        @pl.when(i == 0)
        def _():
            st_ref[...] = jnp.zeros_like(st_ref)

        def step(c, carry):
            r = pl.ds(pl.multiple_of(c * HG_CH, HG_CH), HG_CH)
            st = st_ref[...]
            sts_ref[0, c] = st
            y, st_new = _hg_step(q_ref[r, :], f_ref[r, :], v_ref[r, :], g_ref[r, :], st, lb_ref[...], on_ref[...])
            y_ref[r, :] = y
            st_ref[...] = st_new
            return carry

        lax.fori_loop(0, nc, step, 0)

    return pl.pallas_call(
        body,
        name="hgrn2_fwd",
        grid=(HG_H, nb),
        in_specs=in_specs + [pl.BlockSpec((1, HG_D), lambda h, i: (0, h)), pl.BlockSpec((1, HG_D), lambda h, i: (0, 0))],
        out_specs=[pl.BlockSpec((tb, HG_D), lambda h, i: (i, h)),
                   pl.BlockSpec((1, nc, HG_D, HG_D), lambda h, i: (h, i, 0, 0))],
        out_shape=[jax.ShapeDtypeStruct((S, HG_H * HG_D), F32),
                   jax.ShapeDtypeStruct((HG_H, S // HG_CH, HG_D, HG_D), F32)],
        scratch_shapes=[pltpu.VMEM((HG_D, HG_D), F32)],
        compiler_params=_cp("parallel", "arbitrary"),
    )(proj, proj, proj, proj, lb, on)


def _hg_bwd(proj, sts, lb, on, dcat, tb=256):
    S = proj.shape[0]
    nb = S // tb
    nc = tb // HG_CH
    in_specs, rm = _hg_specs(tb, nb, True)
    dy_cb = (S5_W + MLA_H * MLA_V) // HG_D

    def body(q_ref, f_ref, v_ref, g_ref, lb_ref, on_ref, sts_ref, dy_ref,
             dq_ref, df_ref, dv_ref, dg_ref, dlb_ref, don_ref, dst_ref):
        i = pl.program_id(1)

        @pl.when(i == 0)
        def _():
            dst_ref[...] = jnp.zeros_like(dst_ref)
            dlb_ref[...] = jnp.zeros_like(dlb_ref)
            don_ref[...] = jnp.zeros_like(don_ref)

        def step(cc, carry):
            c = nc - 1 - cc
            r = pl.ds(pl.multiple_of(c * HG_CH, HG_CH), HG_CH)
            _, vjp = jax.vjp(_hg_step, q_ref[r, :], f_ref[r, :], v_ref[r, :], g_ref[r, :], sts_ref[0, c],
                             lb_ref[...], on_ref[...])
            dq, df, dv, dg, dst, dlb, don = vjp((dy_ref[r, :], dst_ref[...]))
            dq_ref[r, :] = dq
            df_ref[r, :] = df
            dv_ref[r, :] = dv
            dg_ref[r, :] = dg
            dst_ref[...] = dst
            dlb_ref[...] += dlb
            don_ref[...] += don
            return carry

        lax.fori_loop(0, nc, step, 0)

    blk = pl.BlockSpec((tb, HG_D), lambda h, i: (rm(i), h))
    par = pl.BlockSpec((1, HG_D), lambda h, i: (0, h))
    return pl.pallas_call(
        body,
        name="hgrn2_bwd",
        grid=(HG_H, nb),
        in_specs=in_specs + [pl.BlockSpec((1, HG_D), lambda h, i: (0, h)), pl.BlockSpec((1, HG_D), lambda h, i: (0, 0)),
                             pl.BlockSpec((1, nc, HG_D, HG_D), lambda h, i: (h, rm(i), 0, 0)),
                             pl.BlockSpec((tb, HG_D), lambda h, i: (rm(i), dy_cb + h))],
        out_specs=[blk, blk, blk, blk, par, par],
        out_shape=[jax.ShapeDtypeStruct((S, HG_H * HG_D), F32)] * 4 + [jax.ShapeDtypeStruct((1, HG_H * HG_D), F32)] * 2,
        scratch_shapes=[pltpu.VMEM((HG_D, HG_D), F32)],
        compiler_params=_cp("parallel", "arbitrary"),
    )(proj, proj, proj, proj, lb, on, sts, dcat)


CONV_TC = 512
CONV_NC = D_FFP // CONV_TC


def _shift_down(cur, halo, k):
    tm = cur.shape[0]
    row = lax.broadcasted_iota(jnp.int32, cur.shape, 0)
    top = jnp.concatenate([pltpu.roll(halo, k, 0), jnp.zeros((tm - 8, cur.shape[1]), F32)], axis=0)
    return jnp.where(row < k, top, pltpu.roll(cur, k, 0))


def _shift_up(cur, halo, k):
    tm = cur.shape[0]
    row = lax.broadcasted_iota(jnp.int32, cur.shape, 0)
    bot = jnp.concatenate([jnp.zeros((tm - 8, cur.shape[1]), F32), pltpu.roll(halo, 8 - k, 0)], axis=0)
    return jnp.where(row >= tm - k, bot, pltpu.roll(cur, tm - k, 0))


def _conv3(cur, halo, w_ref, b_ref):
    return (b_ref[...] + _shift_down(cur, halo, 2) * w_ref[pl.ds(0, 1), :]
            + _shift_down(cur, halo, 1) * w_ref[pl.ds(1, 1), :] + cur * w_ref[pl.ds(2, 1), :])


def _conv_fwd(u0, cw, cb, tm=256):
    S = u0.shape[0]
    nc = CONV_NC
    m8 = tm // 8
    prev = lambda i: jnp.maximum(i * m8 - 1, 0)

    def body(g_ref, v_ref, pg_ref, pv_ref, wg_ref, wv_ref, bg_ref, bv_ref, a_ref):
        on = (pl.program_id(1) > 0).astype(F32)
        ug = _conv3(g_ref[...], pg_ref[...] * on, wg_ref, bg_ref)
        uv = _conv3(v_ref[...], pv_ref[...] * on, wv_ref, bv_ref)
        a_ref[...] = (jax.nn.gelu(ug, approximate=True) * uv).astype(a_ref.dtype)

    tc = CONV_TC
    return pl.pallas_call(
        body,
        name="conv_geglu_fwd",
        grid=(nc, S // tm),
        in_specs=[pl.BlockSpec((tm, tc), lambda j, i: (i, j)), pl.BlockSpec((tm, tc), lambda j, i: (i, j + nc)),
                  pl.BlockSpec((8, tc), lambda j, i: (prev(i), j)), pl.BlockSpec((8, tc), lambda j, i: (prev(i), j + nc)),
                  pl.BlockSpec((3, tc), lambda j, i: (0, j)), pl.BlockSpec((3, tc), lambda j, i: (0, j + nc)),
                  pl.BlockSpec((1, tc), lambda j, i: (0, j)), pl.BlockSpec((1, tc), lambda j, i: (0, j + nc))],
        out_specs=pl.BlockSpec((tm, tc), lambda j, i: (i, j)),
        out_shape=jax.ShapeDtypeStruct((S, D_FFP), BF16),
        compiler_params=_cp("parallel", "parallel"),
    )(u0, u0, u0, u0, cw, cw, cb, cb)


def _conv_bwd_elem(da, u0, cw, cb, tm=256):
    S = u0.shape[0]
    nc = CONV_NC
    m8 = tm // 8
    prev = lambda i: jnp.maximum(i * m8 - 1, 0)

    def body(g_ref, v_ref, pg_ref, pv_ref, wg_ref, wv_ref, bg_ref, bv_ref, da_ref, du_ref, dw_ref, db_ref):
        j = pl.program_id(0)
        i = pl.program_id(1)
        on = (i > 0).astype(F32)
        pg = pg_ref[...] * on
        pv = pv_ref[...] * on
        ug = _conv3(g_ref[...], pg, wg_ref, bg_ref)
        uv = _conv3(v_ref[...], pv, wv_ref, bv_ref)
        _, vjp = jax.vjp(lambda a, b: jax.nn.gelu(a, approximate=True) * b, ug, uv)
        dug, duv = vjp(da_ref[...])
        is_gate = j < nc
        du = jnp.where(is_gate, dug, duv)
        cur = jnp.where(is_gate, g_ref[...], v_ref[...])
        halo = jnp.where(is_gate, pg, pv)
        du_ref[...] = du

        @pl.when(i == 0)
        def _():
            dw_ref[...] = jnp.zeros_like(dw_ref)
            db_ref[...] = jnp.zeros_like(db_ref)

        dw_ref[pl.ds(0, 1), :] += jnp.sum(du * _shift_down(cur, halo, 2), axis=0, keepdims=True)
        dw_ref[pl.ds(1, 1), :] += jnp.sum(du * _shift_down(cur, halo, 1), axis=0, keepdims=True)
        dw_ref[pl.ds(2, 1), :] += jnp.sum(du * cur, axis=0, keepdims=True)
        db_ref[...] += jnp.sum(du, axis=0, keepdims=True)

    tc = CONV_TC
    jj = lambda j: j % nc
    return pl.pallas_call(
        body,
        name="conv_geglu_bwd",
        grid=(2 * nc, S // tm),
        in_specs=[pl.BlockSpec((tm, tc), lambda j, i: (i, jj(j))), pl.BlockSpec((tm, tc), lambda j, i: (i, jj(j) + nc)),
                  pl.BlockSpec((8, tc), lambda j, i: (prev(i), jj(j))),
                  pl.BlockSpec((8, tc), lambda j, i: (prev(i), jj(j) + nc)),
                  pl.BlockSpec((3, tc), lambda j, i: (0, jj(j))), pl.BlockSpec((3, tc), lambda j, i: (0, jj(j) + nc)),
                  pl.BlockSpec((1, tc), lambda j, i: (0, jj(j))), pl.BlockSpec((1, tc), lambda j, i: (0, jj(j) + nc)),
                  pl.BlockSpec((tm, tc), lambda j, i: (i, jj(j)))],
        out_specs=[pl.BlockSpec((tm, tc), lambda j, i: (i, j)), pl.BlockSpec((3, tc), lambda j, i: (0, j)),
                   pl.BlockSpec((1, tc), lambda j, i: (0, j))],
        out_shape=[jax.ShapeDtypeStruct((S, 2 * D_FFP), F32), jax.ShapeDtypeStruct((3, 2 * D_FFP), F32),
                   jax.ShapeDtypeStruct((1, 2 * D_FFP), F32)],
        compiler_params=_cp("parallel", "arbitrary"),
    )(u0, u0, u0, u0, cw, cw, cb, cb, da)


def _conv_bwd_input(du, cw, tm=256):
    S = du.shape[0]
    nrb = S // tm
    m8 = tm // 8
    nxt = lambda i: jnp.minimum((i + 1) * m8, S // 8 - 1)

    def body(d_ref, n_ref, w_ref, o_ref):
        on = (pl.program_id(1) < nrb - 1).astype(F32)
        cur = d_ref[...]
        halo = n_ref[...] * on
        o_ref[...] = (cur * w_ref[pl.ds(2, 1), :] + _shift_up(cur, halo, 1) * w_ref[pl.ds(1, 1), :]
                      + _shift_up(cur, halo, 2) * w_ref[pl.ds(0, 1), :])

    tc = CONV_TC
    return pl.pallas_call(
        body,
        name="conv_bwd_input",
        grid=(2 * CONV_NC, nrb),
        in_specs=[pl.BlockSpec((tm, tc), lambda j, i: (i, j)), pl.BlockSpec((8, tc), lambda j, i: (nxt(i), j)),
                  pl.BlockSpec((3, tc), lambda j, i: (0, j))],
        out_specs=pl.BlockSpec((tm, tc), lambda j, i: (i, j)),
        out_shape=jax.ShapeDtypeStruct((S, 2 * D_FFP), F32),
        compiler_params=_cp("parallel", "parallel"),
    )(du, du, cw)


def _exchange(arrs, scatter, name):
    n = len(arrs)

    def body(*refs):
        ins, outs = refs[:n], refs[n:2 * n]
        send, recv, loc = refs[2 * n:]
        x, y, c = lax.axis_index("x"), lax.axis_index("y"), lax.axis_index("c")
        me = 4 * x + 2 * y + c
        sends, recvs, locs = [], [], []
        for a in range(n):
            lc = pltpu.make_async_copy(ins[a].at[me] if scatter else ins[a], outs[a].at[me], loc.at[a])
            lc.start()
            locs.append(lc)
            for k in range(1, N_DEV):
                px = 1 - x if k & 4 else x
                py = 1 - y if k & 2 else y
                pc = 1 - c if k & 1 else c
                peer = 4 * px + 2 * py + pc
                src = ins[a].at[peer] if scatter else ins[a]
                cp = pltpu.make_async_remote_copy(
                    src_ref=src, dst_ref=outs[a].at[me], send_sem=send.at[a, k - 1], recv_sem=recv.at[a, k - 1],
                    device_id=(px, py, pc), device_id_type=pl.DeviceIdType.MESH)
                cp.start()
                sends.append(cp)
                recvs.append(pltpu.make_async_remote_copy(
                    src_ref=src, dst_ref=outs[a].at[peer], send_sem=send.at[a, k - 1], recv_sem=recv.at[a, k - 1],
                    device_id=(px, py, pc), device_id_type=pl.DeviceIdType.MESH))
        for cp in recvs:
            cp.wait_recv()
        for cp in sends:
            cp.wait_send()
        for lc in locs:
            lc.wait()

    hbm = pl.BlockSpec(memory_space=pltpu.HBM)
    out_shape = [jax.ShapeDtypeStruct(a.shape if scatter else (N_DEV,) + a.shape, a.dtype) for a in arrs]
    return pl.pallas_call(
        body,
        name=name,
        in_specs=[hbm] * n,
        out_specs=[hbm] * n,
        out_shape=out_shape,
        scratch_shapes=[pltpu.SemaphoreType.DMA((n, N_DEV - 1)), pltpu.SemaphoreType.DMA((n, N_DEV - 1)),
                        pltpu.SemaphoreType.DMA((n,))],
    )(*arrs)


def _adamw(recv, w, m, v, name="adamw"):
    L, n, R, C = recv.shape
    fits = [t for t in range(8, R + 1, 8) if R % t == 0 and t * C * 4 <= (1 << 19)]
    tr = max(fits) if fits else R

    def body(r_ref, w_ref, m_ref, v_ref, g_ref, d_ref, mo_ref, vo_ref):
        g = r_ref[0, 0].astype(F32)
        for d in range(1, n):
            g = g + r_ref[0, d].astype(F32)
        mm = ADAM_B1 * m_ref[0] + (1.0 - ADAM_B1) * g
        vv = ADAM_B2 * v_ref[0] + (1.0 - ADAM_B2) * (g * g)
        m_hat = mm / (1.0 - ADAM_B1 ** ADAM_STEP)
        v_hat = vv / (1.0 - ADAM_B2 ** ADAM_STEP)
        g_ref[0] = g
        d_ref[0] = -ADAM_LR * (m_hat / (jnp.sqrt(v_hat) + ADAM_EPS) + ADAM_WD * w_ref[0])
        mo_ref[0] = mm
        vo_ref[0] = vv

    blk = pl.BlockSpec((1, tr, C), lambda l, i: (l, i, 0))
    return pl.pallas_call(
        body,
        name=name,
        grid=(L, R // tr),
        in_specs=[pl.BlockSpec((1, n, tr, C), lambda l, i: (l, 0, i, 0)), blk, blk, blk],
        out_specs=[blk] * 4,
        out_shape=[jax.ShapeDtypeStruct((L, R, C), F32)] * 4,
        compiler_params=_cp("parallel", "parallel"),
    )(recv, w, m, v)


def _layer_fwd(x, mod, W, P):
    sh1, sc1, g1, sh2, sc2, g2 = mod
    D = D_MODEL
    R = {"x": x}
    (h1,) = _rowwise(_f_pre, [(x, D, 0)], [P["n1"], sc1, sh1], [(D, BF16)], "pre_norm")
    proj = _mm(h1, W["w_in"], "nn", name="mm_in")
    R["h1"], R["proj"] = h1, proj
    bu = _mm(proj, W["bd"], "nn", a_col0=0, a_cols=S5_W, name="mm_s5_b")
    hs = _s5_scan(bu, P["tab_fwd"])
    yc = _mm(hs, W["cdt"], "nt", name="mm_s5_c")
    (gg,) = _rowwise(_f_s5a, [(yc, S5_W, 0), (proj, S5_W, 0)], [P["s5_d"]], [(S5_W, F32)], "s5_gelu")
    z = _mm(gg, W["w_glu"], "nn", name="mm_glu")
    (ys5,) = _rowwise(_f_s5b, [(gg, S5_W, 0), (z, S5_W, 0)], [], [(S5_W, BF16)], "s5_glu")
    R.update(bu=bu, hs=hs, yc=yc, gg=gg, z=z)
    (qn,) = _rowwise(_f_norm, [(proj, 512, 1)], [P["q_norm"]], [(512, BF16)], "q_norm")
    (kvn,) = _rowwise(_f_norm, [(proj, 256, 12)], [P["kv_norm"]], [(256, BF16)], "kv_norm")
    qraw = _mm(qn, W["w_uq"], "nn", name="mm_uq")
    kvraw = _mm(kvn, W["w_ukv"], "nn", name="mm_ukv")
    q, k, v = _rope_fwd(qraw, kvraw, proj, P["cs"], P["sn"])
    o, lse = _flash_fwd(q, k, v)
    R.update(qn=qn, kvn=kvn, q=q, k=k, v=v, o=o, lse=lse)
    yhg, sts = _hg_fwd(proj, P["lb"], P["hg_on"])
    R["sts"] = sts
    cat = jnp.concatenate([ys5, o.astype(BF16), yhg.astype(BF16)], axis=-1)
    mixed = _mm(cat, W["w_out"], "nn", name="mm_out")
    (x2,) = _rowwise(_f_post, [(x, D, 0), (mixed, D, 0)], [P["n2"], g1], [(D, F32)], "post_norm")
    R.update(cat=cat, mixed=mixed, x2=x2)
    (h2,) = _rowwise(_f_pre, [(x2, D, 0)], [P["n3"], sc2, sh2], [(D, BF16)], "pre_norm")
    u0 = _mm(h2, W["w_up"], "nn", name="mm_up")
    a = _conv_fwd(u0, P["conv_w"], P["conv_b"])
    y = _mm(a, W["w_down"], "nn", name="mm_down")
    (x3,) = _rowwise(_f_post, [(x2, D, 0), (y, D, 0)], [P["n4"], g2], [(D, F32)], "post_norm")
    R.update(h2=h2, u0=u0, a=a, y=y)
    return x3, R


def _layer_bwd(dx3, mod, W, P, R):
    sh1, sc1, g1, sh2, sc2, g2 = mod
    D = D_MODEL
    G = {}
    (dx2a, dy), (dn4, dg2) = _rowwise_vjp(_f_post, [(R["x2"], D, 0), (R["y"], D, 0)], [P["n4"], g2],
                                          [[(dx3, D, 0)]], "post_norm_bwd", [True, True])
    da = _mm(dy, W["w_down"], "nt", name="mm_down_dx")
    G["w_down"] = _mm(R["a"], dy, "tn", out_dtype=BF16, name="mm_down_dw")
    du, dcw, dcb = _conv_bwd_elem(da, R["u0"], P["conv_w"], P["conv_b"])
    du0 = _conv_bwd_input(du, P["conv_w"])
    dh2 = _mm(du0, W["w_up"], "nt", name="mm_up_dx")
    G["w_up"] = _mm(R["h2"], du0, "tn", out_dtype=BF16, name="mm_up_dw")
    (dx2,), (dn3, dsc2, dsh2) = _rowwise_vjp(_f_pre, [(R["x2"], D, 0)], [P["n3"], sc2, sh2], [[(dh2, D, 0)]],
                                             "pre_norm_bwd", [True], add_rows={0: (dx2a, D, 0)})
    (dxa, dmixed), (dn2, dg1) = _rowwise_vjp(_f_post, [(R["x"], D, 0), (R["mixed"], D, 0)], [P["n2"], g1],
                                             [[(dx2, D, 0)]], "post_norm_bwd", [True, True])
    dcat = _mm(dmixed, W["w_out"], "nt", name="mm_out_dx")
    G["w_out"] = _mm(R["cat"], dmixed, "tn", out_dtype=BF16, name="mm_out_dw")
    (dga, dz), _ = _rowwise_vjp(_f_s5b, [(R["gg"], S5_W, 0), (R["z"], S5_W, 0)], [], [[(dcat, S5_W, 0)]],
                                "s5_glu_bwd", [True, True])
    dgb = _mm(dz, W["w_glu"], "nt", name="mm_glu_dx")
    G["w_glu"] = _mm(R["gg"], dz, "tn", out_dtype=BF16, name="mm_glu_dw")
    (dyc, dua), (dd,) = _rowwise_vjp(_f_s5a, [(R["yc"], S5_W, 0), (R["proj"], S5_W, 0)], [P["s5_d"]],
                                     [[(dga, S5_W, 0), (dgb, S5_W, 0)]], "s5_gelu_bwd", [True, True])
    dhs = _mm(dyc, W["cdt"], "nn", name="mm_s5_c_dx")
    dcdt = _mm(dyc, R["hs"], "tn", name="mm_s5_c_dw")
    gs, acc = _s5_scan(dhs, P["tab_rev"], reverse=True, h=R["hs"], bu_fwd=R["bu"])
    dub = _mm(gs, W["bd"], "nt", name="mm_s5_b_dx")
    dbd = _mm(R["proj"], gs, "tn", a_col0=0, a_cols=S5_W, name="mm_s5_b_dw")
    dq = _flash_bwd_dq(R["q"], R["k"], R["v"], R["o"], R["lse"], dcat)
    dk, dv = _flash_bwd_dkv(R["q"], R["k"], R["v"], R["o"], R["lse"], dcat)
    dqraw, dkvraw, dkr = _rope_bwd(dq, dk, dv, P["cs"], P["sn"])
    dqn = _mm(dqraw, W["w_uq"], "nt", name="mm_uq_dx")
    G["w_uq"] = _mm(R["qn"], dqraw, "tn", out_dtype=BF16, name="mm_uq_dw")
    dkvn = _mm(dkvraw, W["w_ukv"], "nt", name="mm_ukv_dx")
    G["w_ukv"] = _mm(R["kvn"], dkvraw, "tn", out_dtype=BF16, name="mm_ukv_dw")
    (dcq,), (dqnorm,) = _rowwise_vjp(_f_norm, [(R["proj"], 512, 1)], [P["q_norm"]], [[(dqn, 512, 0)]],
                                     "q_norm_bwd", [True])
    (dckv,), (dkvnorm,) = _rowwise_vjp(_f_norm, [(R["proj"], 256, 12)], [P["kv_norm"]], [[(dkvn, 256, 0)]],
                                       "kv_norm_bwd", [True])
    dhq, dhf, dhi, dhg, dlb, don = _hg_bwd(R["proj"], R["sts"], P["lb"], P["hg_on"], dcat)
    dproj = jnp.concatenate([dua + dub, dcq, dhq, dhf, dhi, dhg, dckv, dkr], axis=-1)
    dh1 = _mm(dproj, W["w_in"], "nt", name="mm_in_dx")
    G["w_in"] = _mm(R["h1"], dproj, "tn", out_dtype=BF16, name="mm_in_dw")
    (dx,), (dn1, dsc1, dsh1) = _rowwise_vjp(_f_pre, [(R["x"], D, 0)], [P["n1"], sc1, sh1], [[(dh1, D, 0)]],
                                            "pre_norm_bwd", [True], add_rows={0: (dxa, D, 0)})
    dmod = jnp.concatenate([dsh1, dsc1, dg1, dsh2, dsc2, dg2], axis=-1)
    small = dict(n1=dn1, n2=dn2, n3=dn3, n4=dn4, s5_d=dd, q_norm=dqnorm, kv_norm=dkvnorm,
                 lb=dlb, hg_on=jnp.sum(don.reshape(HG_H, HG_D), axis=0, keepdims=True),
                 conv_w=dcw, conv_b=dcb, bd=dbd, cdt=dcdt, acc=jnp.sum(acc, axis=0, keepdims=True))
    return dx, dmod, G, small


def _cols_from_shards(g):
    return jnp.transpose(g, (1, 0, 2)).reshape(g.shape[1], -1)


def _cols_to_shards(w):
    K = w.shape[0]
    return jnp.transpose(w.reshape(K, N_DEV, -1), (1, 0, 2))


def _pad_ff(w):
    pad = [(0, 0)] * (w.ndim - 1) + [(0, D_FFP - D_FF)]
    return jnp.concatenate([jnp.pad(w[..., :D_FF], pad), jnp.pad(w[..., D_FF:], pad)], axis=-1)


def _unpad_ff(w):
    return jnp.concatenate([w[..., :D_FF], w[..., D_FFP:D_FFP + D_FF]], axis=-1)


def _assemble(gathered):
    w_in = _cols_from_shards(gathered["w_in"])[:, _IN_PERM]
    w_in = jnp.pad(w_in, ((0, 0), (0, D_INP - D_IN)))
    w_uq = _cols_from_shards(gathered["w_uq"]).reshape(-1, MLA_H, MLA_NOPE + MLA_ROPE)
    w_uq = jnp.pad(w_uq, ((0, 0), (0, 0), (0, MLA_HW - MLA_NOPE - MLA_ROPE))).reshape(-1, MLA_H * MLA_HW)
    w_down = gathered["w_down"].reshape(D_FF, D_MODEL)
    return dict(
        w_in=w_in,
        w_glu=gathered["w_glu"].reshape(S5_W, S5_W),
        w_uq=w_uq,
        w_ukv=_cols_from_shards(gathered["w_ukv"]),
        w_out=gathered["w_out"].reshape(D_MODEL, D_MODEL),
        w_up=_pad_ff(_cols_from_shards(gathered["w_up"])),
        w_down=jnp.pad(w_down, ((0, D_FFP - D_FF), (0, 0))),
    )


def _grad_shards(G):
    w_in = G["w_in"][:, :D_IN][:, _IN_INV]
    w_uq = G["w_uq"].reshape(-1, MLA_H, MLA_HW)[:, :, :MLA_NOPE + MLA_ROPE].reshape(-1, MLA_H * (MLA_NOPE + MLA_ROPE))
    return dict(
        w_in=_cols_to_shards(w_in),
        w_glu=G["w_glu"].reshape(N_DEV, -1, S5_W),
        w_uq=_cols_to_shards(w_uq),
        w_ukv=_cols_to_shards(G["w_ukv"]),
        w_out=G["w_out"].reshape(N_DEV, -1, D_MODEL),
        w_up=_cols_to_shards(_unpad_ff(G["w_up"])),
        w_down=G["w_down"][:D_FF].reshape(N_DEV, -1, D_MODEL),
    )


_BIG = ("w_in", "w_glu", "w_uq", "w_ukv", "w_out", "w_up", "w_down")
_SMALL = ("s5_lambda_re", "s5_lambda_im", "s5_log_dt", "s5_b_re", "s5_b_im", "s5_c_re", "s5_c_im", "s5_d",
          "mla_q_norm", "mla_kv_norm", "hg_lb_logits", "hg_out_norm", "mix_pre_norm", "mix_post_norm",
          "ffn_pre_norm", "ffn_post_norm", "ffn_conv_w_full", "ffn_conv_b", "b_ada")
PACK_ROW = 1024


def _pack(parts):
    flat = jnp.concatenate([p.reshape(-1) for p in parts])
    n = flat.shape[0]
    pad = (-n) % (8 * PACK_ROW)
    return jnp.pad(flat, (0, pad)).reshape(-1, PACK_ROW)


def _unpack(packed, shapes):
    flat = packed.reshape(-1)
    out, pos = [], 0
    for s in shapes:
        n = int(np.prod(s))
        out.append(flat[pos:pos + n].reshape(s))
        pos += n
    return out


def _step(x, c, positions, loss_target, w, m, v):
    x = x[0]
    S = x.shape[0]
    L = w["w_in"].shape[0]
    D = D_MODEL
    me = 4 * lax.axis_index("x") + 2 * lax.axis_index("y") + lax.axis_index("c")

    (c_all,) = _exchange([c], False, "gather_c")
    c_all = c_all.reshape(N_DEV, D)
    (c_act,) = _rowwise(lambda a: (a * jax.nn.sigmoid(a),), [(c_all, D, 0)], [], [(D, F32)], "silu_c", tm=N_DEV)
    mod_part = jnp.stack([_mm(c_act, w["w_ada"][l], "nn", name="mm_ada") for l in range(L)])
    (mod_all,) = _exchange([mod_part], False, "gather_mod")
    mod_mine = lax.dynamic_index_in_dim(mod_all, me, axis=2, keepdims=False)
    mod_full = jnp.transpose(mod_mine, (1, 0, 2)).reshape(L, 6 * D) + w["b_ada"]
    mods = [[mod_full[l:l + 1, i * D:(i + 1) * D] for i in range(6)] for l in range(L)]

    cs, sn = _rope_tables(positions[0])
    lower, lower_vjp = jax.vjp(lambda lg: jnp.cumsum(jax.nn.softmax(lg, axis=0), axis=0)
                               - jax.nn.softmax(lg, axis=0)[0:1], w["hg_lb_logits"])
    conv_w_full = []

    Ws, Ps, preps = [], [], []
    for l in range(L):
        shards = [w[n][l].astype(BF16) for n in _BIG] + [w["ffn_conv_w"][l]]
        got = _exchange(shards, False, "gather_weights")
        gathered = dict(zip(_BIG, got[:-1]))
        Wl = _assemble(gathered)
        cw_full = _cols_from_shards(got[-1])
        conv_w_full.append(cw_full)
        s5_args = (w["s5_lambda_re"][l], w["s5_lambda_im"][l], w["s5_log_dt"][l], w["s5_b_re"][l], w["s5_b_im"][l],
                   w["s5_c_re"][l], w["s5_c_im"][l])
        (lbr, lbi, bd, cdt), prep_vjp = jax.vjp(_s5_prep, *s5_args)
        tab_fwd, tab_rev = _s5_tables(*s5_args[:3])
        Wl["bd"], Wl["cdt"] = bd.astype(BF16), cdt.astype(BF16)
        row = lambda a: a.reshape(1, -1)
        Ps.append(dict(
            n1=row(w["mix_pre_norm"][l]), n2=row(w["mix_post_norm"][l]), n3=row(w["ffn_pre_norm"][l]),
            n4=row(w["ffn_post_norm"][l]), s5_d=row(w["s5_d"][l]), q_norm=row(w["mla_q_norm"][l]),
            kv_norm=row(w["mla_kv_norm"][l]), lb=row(lower[l]), hg_on=row(w["hg_out_norm"][l]),
            conv_w=_pad_ff(cw_full), conv_b=_pad_ff(row(w["ffn_conv_b"][l])),
            tab_fwd=tab_fwd, tab_rev=tab_rev, cs=cs, sn=sn, lam_bar=(lbr, lbi)))
        Ws.append(Wl)
        preps.append(prep_vjp)

    Rs = []
    h = x
    for l in range(L):
        h, R = _layer_fwd(h, mods[l], Ws[l], Ps[l])
        Rs.append(R)
    loss_local, dh = _loss_head(h, loss_target[0])
    loss = lax.psum(loss_local, ("x", "y", "c"))

    big_recv = {n: [None] * L for n in _BIG}
    small_g = {n: [None] * L for n in _SMALL if n != "hg_lb_logits"}
    dlower = [None] * L
    for l in reversed(range(L)):
        dh, dmod, G, sm = _layer_bwd(dh, mods[l], Ws[l], Ps[l], Rs[l])
        Rs[l] = None
        shards = _grad_shards(G)
        got = _exchange([shards[n] for n in _BIG], True, "scatter_grads")
        for n, r in zip(_BIG, got):
            big_recv[n][l] = r
        lbr, lbi = Ps[l]["lam_bar"]
        ar, ai = _ri_split(sm["acc"])
        dl = lax.complex(ar, ai) / lax.complex(lbr, -lbi)
        d_s5 = preps[l]((jnp.real(dl), jnp.imag(dl), sm["bd"], sm["cdt"]))
        for n, g in zip(("s5_lambda_re", "s5_lambda_im", "s5_log_dt", "s5_b_re", "s5_b_im", "s5_c_re", "s5_c_im"), d_s5):
            small_g[n][l] = g
        small_g["s5_d"][l] = sm["s5_d"][0]
        small_g["mla_q_norm"][l] = sm["q_norm"][0]
        small_g["mla_kv_norm"][l] = sm["kv_norm"][0]
        small_g["hg_out_norm"][l] = sm["hg_on"][0]
        small_g["mix_pre_norm"][l] = sm["n1"][0]
        small_g["mix_post_norm"][l] = sm["n2"][0]
        small_g["ffn_pre_norm"][l] = sm["n3"][0]
        small_g["ffn_post_norm"][l] = sm["n4"][0]
        small_g["ffn_conv_w_full"][l] = _unpad_ff(sm["conv_w"])
        small_g["ffn_conv_b"][l] = _unpad_ff(sm["conv_b"])[0]
        small_g["b_ada"][l] = dmod[0]
        dlower[l] = sm["lb"][0]
    small_g = {n: jnp.stack(gl) for n, gl in small_g.items()}
    (small_g["hg_lb_logits"],) = lower_vjp(jnp.stack(dlower))

    small_w = {n: w[n] for n in _SMALL if n != "ffn_conv_w_full"}
    small_w["ffn_conv_w_full"] = jnp.stack(conv_w_full)
    shapes = [small_w[n].shape for n in _SMALL]
    zeros_cw = jnp.zeros_like(small_w["ffn_conv_w_full"])
    pk_g = _pack([small_g[n] for n in _SMALL])
    pk_w = _pack([small_w[n] for n in _SMALL])
    pk_m = _pack([zeros_cw if n == "ffn_conv_w_full" else m[n] for n in _SMALL])
    pk_v = _pack([zeros_cw + 1.0 if n == "ffn_conv_w_full" else v[n] for n in _SMALL])
    (pk_all,) = _exchange([pk_g], False, "gather_small")
    sg, sd, sm_, sv = _adamw(pk_all[None], pk_w[None], pk_m[None], pk_v[None], name="adamw_small")
    small_out = {}
    for key, arr in (("g", sg), ("d", sd), ("m", sm_), ("v", sv)):
        small_out[key] = dict(zip(_SMALL, _unpack(arr[0], shapes)))

    n_cw = w["ffn_conv_w"].shape[-1]
    g_cw = lax.dynamic_slice_in_dim(small_out["g"]["ffn_conv_w_full"], me * n_cw, n_cw, axis=2)
    cw_out = _adamw(g_cw[:, None], w["ffn_conv_w"], m["ffn_conv_w"], v["ffn_conv_w"], name="adamw_conv_w")

    n_ada = w["w_ada"].shape[-1]
    flat_all = pk_all.reshape(N_DEV, -1)
    off = sum(int(np.prod(s)) for s in shapes[:-1])
    dmod_all = flat_all[:, off:off + L * 6 * D].reshape(N_DEV, L, 6 * D)
    dmod_cols = lax.dynamic_slice_in_dim(dmod_all, me * n_ada, n_ada, axis=2)
    g_ada = jnp.stack([_mm(c_act, dmod_cols[:, l], "tn", name="mm_ada_dw") for l in range(L)])
    ada_out = _adamw(g_ada[:, None], w["w_ada"], m["w_ada"], v["w_ada"], name="adamw_ada")

    big_out = {}
    for n in _BIG:
        recv = jnp.stack(big_recv[n])
        big_out[n] = _adamw(recv, w[n], m[n], v[n], name="adamw_" + n)
    big_out["w_ada"] = ada_out
    big_out["ffn_conv_w"] = cw_out
    return loss, dh[None], big_out, small_out


_WEIGHTS = ("w_in", "s5_lambda_re", "s5_lambda_im", "s5_log_dt", "s5_b_re", "s5_b_im", "s5_c_re", "s5_c_im", "s5_d",
            "s5_w_glu", "mla_q_norm", "mla_w_uq", "mla_kv_norm", "mla_w_ukv", "hg_lb_logits", "hg_out_norm", "w_out",
            "mix_pre_norm", "mix_post_norm", "ffn_pre_norm", "ffn_post_norm", "ffn_w_up", "ffn_conv_w", "ffn_conv_b",
            "ffn_w_down", "w_ada", "b_ada")
_ALIAS = {"s5_w_glu": "w_glu", "mla_w_uq": "w_uq", "mla_w_ukv": "w_ukv", "ffn_w_up": "w_up", "ffn_w_down": "w_down"}


def kernel(x, c, positions, w_in, s5_lambda_re, s5_lambda_im, s5_log_dt, s5_b_re, s5_b_im, s5_c_re, s5_c_im, s5_d, s5_w_glu, mla_q_norm, mla_w_uq, mla_kv_norm, mla_w_ukv, hg_lb_logits, hg_out_norm, w_out, mix_pre_norm, mix_post_norm, ffn_pre_norm, ffn_post_norm, ffn_w_up, ffn_conv_w, ffn_conv_b, ffn_w_down, w_ada, b_ada, loss_target, m_w_in, m_s5_lambda_re, m_s5_lambda_im, m_s5_log_dt, m_s5_b_re, m_s5_b_im, m_s5_c_re, m_s5_c_im, m_s5_d, m_s5_w_glu, m_mla_q_norm, m_mla_w_uq, m_mla_kv_norm, m_mla_w_ukv, m_hg_lb_logits, m_hg_out_norm, m_w_out, m_mix_pre_norm, m_mix_post_norm, m_ffn_pre_norm, m_ffn_post_norm, m_ffn_w_up, m_ffn_conv_w, m_ffn_conv_b, m_ffn_w_down, m_w_ada, m_b_ada, v_w_in, v_s5_lambda_re, v_s5_lambda_im, v_s5_log_dt, v_s5_b_re, v_s5_b_im, v_s5_c_re, v_s5_c_im, v_s5_d, v_s5_w_glu, v_mla_q_norm, v_mla_w_uq, v_mla_kv_norm, v_mla_w_ukv, v_hg_lb_logits, v_hg_out_norm, v_w_out, v_mix_pre_norm, v_mix_post_norm, v_ffn_pre_norm, v_ffn_post_norm, v_ffn_w_up, v_ffn_conv_w, v_ffn_conv_b, v_ffn_w_down, v_w_ada, v_b_ada):
    args = locals()
    key = lambda n: _ALIAS.get(n, n)
    w = {key(n): args[n] for n in _WEIGHTS}
    m = {key(n): args["m_" + n] for n in _WEIGHTS}
    v = {key(n): args["v_" + n] for n in _WEIGHTS}
    loss, grad_x, big, small = _step(x, c, positions, loss_target, w, m, v)

    def pick(n, idx):
        k = key(n)
        if k in big:
            return big[k][idx].reshape(w[k].shape)
        return small["gdmv"[idx]][k]

    outs = [loss, grad_x]
    for idx in range(4):
        outs += [pick(n, idx) for n in _WEIGHTS]
    return tuple(outs)
```

```python
import functools
import math

import numpy as np
import jax
import jax.numpy as jnp
from jax import lax
from jax.experimental import pallas as pl
from jax.experimental.pallas import tpu as pltpu

F32 = jnp.float32
BF16 = jnp.bfloat16
N_DEV = 8
V7X_VMEM_LIMIT = 56 * 1024 * 1024
MM_VMEM_BUDGET = 28 * 1024 * 1024
LANE = 128

D_MODEL = 2048
S5_W = 512
S5_G = 32
S5_C = 16
S5_P = 64
S5_N = S5_G * S5_P
S5_TL = 512
MLA_H = 8
MLA_NOPE = 128
MLA_ROPE = 64
MLA_V = 128
MLA_HW = 256
HG_H = 4
HG_D = 128
HG_CH = 16
D_FF = 5504
D_FFP = 5632
D_IN = 3392
D_INP = 3456
EPS = 1e-6
MASK_VALUE = -1e30
ROPE_THETA = 10000.0
ATT_SCALE = (MLA_NOPE + MLA_ROPE) ** -0.5

ADAM_LR = 0.001
ADAM_B1 = 0.9
ADAM_B2 = 0.999
ADAM_EPS = 1e-08
ADAM_WD = 0.01
ADAM_STEP = 10

_IN_PERM = np.concatenate([np.arange(0, 1024), np.arange(1344, 3392), np.arange(1024, 1344)])
_IN_INV = np.argsort(_IN_PERM)

_NN = (((1,), (0,)), ((), ()))
_NT = (((1,), (1,)), ((), ()))
_TN = (((0,), (0,)), ((), ()))


def _cp(*sem):
    return pltpu.CompilerParams(dimension_semantics=sem, vmem_limit_bytes=V7X_VMEM_LIMIT)


def _tile(n, cap, align=LANE):
    if n <= cap:
        return n
    t = (cap // align) * align
    while t >= align:
        if n % t == 0:
            return t
        t -= align
    raise ValueError(f"no tile for {n}")


def _bdot(a, b, dims):
    return lax.dot_general(a.astype(BF16), b.astype(BF16), dims, preferred_element_type=F32)


def _mm(a, b, mode, out_dtype=F32, a_col0=0, a_cols=None, name="mm"):
    if mode == "tn":
        K = a.shape[0]
        M = a_cols if a_cols is not None else a.shape[1]
        N = b.shape[1]
    else:
        M = a.shape[0]
        K = a_cols if a_cols is not None else a.shape[1]
        N = b.shape[0] if mode == "nt" else b.shape[1]
    tm = _tile(M, 1024, 8 if M < LANE else LANE)
    tn = _tile(N, 1024)
    kal = 8 if K < LANE else LANE
    tk = _tile(K, 2048, kal)
    osz = jnp.dtype(out_dtype).itemsize

    def vmem(tk_):
        acc = 0 if tk_ == K else tm * tn * 4
        return 2 * tk_ * (tm * a.dtype.itemsize + tn * b.dtype.itemsize) + 2 * tm * tn * osz + acc

    while vmem(tk) > MM_VMEM_BUDGET and tk > kal:
        tk = _tile(K, tk - kal, kal)
    nk = K // tk
    if mode == "tn":
        assert a_col0 % tm == 0
        a_spec = pl.BlockSpec((tk, tm), lambda i, j, k: (k, i + a_col0 // tm))
        b_spec = pl.BlockSpec((tk, tn), lambda i, j, k: (k, j))
        dims = _TN
    else:
        assert a_col0 % tk == 0
        a_spec = pl.BlockSpec((tm, tk), lambda i, j, k: (i, k + a_col0 // tk))
        if mode == "nn":
            b_spec = pl.BlockSpec((tk, tn), lambda i, j, k: (k, j))
            dims = _NN
        else:
            b_spec = pl.BlockSpec((tn, tk), lambda i, j, k: (j, k))
            dims = _NT

    if nk == 1:
        def body(a_ref, b_ref, o_ref):
            o_ref[...] = _bdot(a_ref[...], b_ref[...], dims).astype(o_ref.dtype)

        scratch = []
    else:
        def body(a_ref, b_ref, o_ref, acc_ref):
            k = pl.program_id(2)

            @pl.when(k == 0)
            def _():
                acc_ref[...] = jnp.zeros_like(acc_ref)

            acc_ref[...] += _bdot(a_ref[...], b_ref[...], dims)

            @pl.when(k == nk - 1)
            def _():
                o_ref[...] = acc_ref[...].astype(o_ref.dtype)

        scratch = [pltpu.VMEM((tm, tn), F32)]

    return pl.pallas_call(
        body,
        name=name,
        grid=(M // tm, N // tn, nk),
        in_specs=[a_spec, b_spec],
        out_specs=pl.BlockSpec((tm, tn), lambda i, j, k: (i, j)),
        out_shape=jax.ShapeDtypeStruct((M, N), out_dtype),
        scratch_shapes=scratch,
        compiler_params=_cp("parallel", "parallel", "arbitrary"),
    )(a, b)


def _row_spec(tm, width, cb):
    return pl.BlockSpec((tm, width), lambda i: (i, cb))


def _rowwise(fn, rows, params, outs, name, tm=256):
    S = rows[0][0].shape[0]
    nr, npar = len(rows), len(params)

    def body(*refs):
        xs = [r[...].astype(F32) for r in refs[:nr]]
        ps = [p[...] for p in refs[nr:nr + npar]]
        res = fn(*xs, *ps)
        for o, r in zip(refs[nr + npar:], res):
            o[...] = r.astype(o.dtype)

    res = pl.pallas_call(
        body,
        name=name,
        grid=(S // tm,),
        in_specs=[_row_spec(tm, w, cb) for _, w, cb in rows]
        + [pl.BlockSpec(p.shape, lambda i: (0, 0)) for p in params],
        out_specs=[_row_spec(tm, w, 0) for w, _ in outs],
        out_shape=[jax.ShapeDtypeStruct((S, w), dt) for w, dt in outs],
        compiler_params=_cp("parallel"),
    )(*[r[0] for r in rows], *params)
    return list(res)


def _rowwise_vjp(fn, rows, params, cts, name, row_grads, add_rows=None, tm=256):
    S = rows[0][0].shape[0]
    add_rows = add_rows or {}
    nr, npar = len(rows), len(params)
    flat_cts = [c for group in cts for c in group]
    ncts = len(flat_cts)
    add_keys = sorted(add_rows)
    nadd = len(add_keys)
    grad_idx = [i for i in range(nr) if row_grads[i]]

    def body(*refs):
        i = pl.program_id(0)
        xs = [r[...].astype(F32) for r in refs[:nr]]
        ps = [p[...] for p in refs[nr:nr + npar]]
        ct_refs = refs[nr + npar:nr + npar + ncts]
        add_refs = refs[nr + npar + ncts:nr + npar + ncts + nadd]
        out_refs = refs[nr + npar + ncts + nadd:]
        ct_vals, pos = [], 0
        for group in cts:
            v = ct_refs[pos][...].astype(F32)
            for r in ct_refs[pos + 1:pos + len(group)]:
                v = v + r[...].astype(F32)
            pos += len(group)
            ct_vals.append(v)
        _, vjp = jax.vjp(lambda *a: tuple(fn(*a)), *xs, *ps)
        grads = vjp(tuple(ct_vals))
        for o, gi in zip(out_refs[:len(grad_idx)], grad_idx):
            g = grads[gi]
            if gi in add_rows:
                g = g + add_refs[add_keys.index(gi)][...].astype(F32)
            o[...] = g.astype(o.dtype)
        dprefs = out_refs[len(grad_idx):]

        @pl.when(i == 0)
        def _():
            for dp in dprefs:
                dp[...] = jnp.zeros_like(dp)

        for dp, g in zip(dprefs, grads[nr:]):
            dp[...] += g

    res = pl.pallas_call(
        body,
        name=name,
        grid=(S // tm,),
        in_specs=[_row_spec(tm, w, cb) for _, w, cb in rows]
        + [pl.BlockSpec(p.shape, lambda i: (0, 0)) for p in params]
        + [_row_spec(tm, w, cb) for _, w, cb in flat_cts]
        + [_row_spec(tm, add_rows[k][1], add_rows[k][2]) for k in add_keys],
        out_specs=[_row_spec(tm, rows[gi][1], 0) for gi in grad_idx]
        + [pl.BlockSpec(p.shape, lambda i: (0, 0)) for p in params],
        out_shape=[jax.ShapeDtypeStruct((S, rows[gi][1]), F32) for gi in grad_idx]
        + [jax.ShapeDtypeStruct(p.shape, F32) for p in params],
        compiler_params=_cp("arbitrary"),
    )(*[r[0] for r in rows], *params, *[c[0] for c in flat_cts], *[add_rows[k][0] for k in add_keys])
    res = list(res)
    return res[:len(grad_idx)], res[len(grad_idx):]


def _rms(x, gain):
    return x * lax.rsqrt(jnp.mean(x * x, axis=-1, keepdims=True) + EPS) * gain


def _f_pre(x, gain, sc, sh):
    return (_rms(x, gain) * (1.0 + sc) + sh,)


def _f_post(x, y, gain, g):
    return (x + g * _rms(y, gain),)


def _f_norm(x, gain):
    return (_rms(x, gain),)


def _f_s5a(yc, u, d):
    return (jax.nn.gelu(yc + d * u, approximate=True),)


def _f_s5b(g, z):
    return (g * jax.nn.sigmoid(z),)


def _loss_head(y, target, tm=256):
    S, D = y.shape

    def body(y_ref, t_ref, dy_ref, acc_ref):
        i = pl.program_id(0)
        e = y_ref[...] - t_ref[...]
        dy_ref[...] = e * (1.0 / D)

        @pl.when(i == 0)
        def _():
            acc_ref[...] = jnp.zeros_like(acc_ref)

        acc_ref[...] += jnp.sum(e * e, axis=0, keepdims=True)

    dy, acc = pl.pallas_call(
        body,
        name="loss_head",
        grid=(S // tm,),
        in_specs=[_row_spec(tm, D, 0), _row_spec(tm, D, 0)],
        out_specs=[_row_spec(tm, D, 0), pl.BlockSpec((1, D), lambda i: (0, 0))],
        out_shape=[jax.ShapeDtypeStruct((S, D), F32), jax.ShapeDtypeStruct((1, D), F32)],
        compiler_params=_cp("arbitrary"),
    )(y, target)
    return 0.5 * jnp.sum(acc) / D, dy


def _s5_tile_scan(xr, xi, tab_ref, reverse, row8):
    for k in (1, 2, 4):
        pr = tab_ref[pl.ds(k - 1, 1), 0:S5_TL] if not reverse else tab_ref[pl.ds(8 - k, 1), 0:S5_TL]
        pi = tab_ref[pl.ds(k - 1, 1), S5_TL:2 * S5_TL] if not reverse else tab_ref[pl.ds(8 - k, 1), S5_TL:2 * S5_TL]
        if not reverse:
            keep = row8 >= k
            sr = jnp.where(keep, pltpu.roll(xr, k, 0), 0.0)
            si = jnp.where(keep, pltpu.roll(xi, k, 0), 0.0)
        else:
            keep = row8 < 8 - k
            sr = jnp.where(keep, pltpu.roll(xr, 8 - k, 0), 0.0)
            si = jnp.where(keep, pltpu.roll(xi, 8 - k, 0), 0.0)
        xr, xi = xr + pr * sr - pi * si, xi + pr * si + pi * sr
    return xr, xi


def _s5_scan(bu, tab, reverse=False, h=None, bu_fwd=None, tr=512):
    S = bu.shape[0]
    tr = min(tr, S)
    nl = S5_N // S5_TL
    nrb = S // tr
    w = 2 * S5_TL
    nt = tr // 8
    rmap = (lambda j, i: (i, j)) if not reverse else (lambda j, i: (nrb - 1 - i, j))

    def body(*refs):
        if reverse:
            x_ref, tab_ref, h_ref, b_ref, o_ref, acc_ref, cr_ref, ci_ref = refs
        else:
            x_ref, tab_ref, o_ref, cr_ref, ci_ref = refs
        i = pl.program_id(1)
        row8 = lax.broadcasted_iota(jnp.int32, (8, S5_TL), 0)

        @pl.when(i == 0)
        def _():
            cr_ref[...] = jnp.zeros_like(cr_ref)
            ci_ref[...] = jnp.zeros_like(ci_ref)
            if reverse:
                acc_ref[...] = jnp.zeros_like(acc_ref)

        tr_all = tab_ref[:, 0:S5_TL]
        ti_all = tab_ref[:, S5_TL:w]

        def tile(t, carry):
            tt = (nt - 1 - t) if reverse else t
            r = pl.ds(pl.multiple_of(tt * 8, 8), 8)
            xr, xi = _s5_tile_scan(x_ref[r, 0:S5_TL], x_ref[r, S5_TL:w], tab_ref, reverse, row8)
            cr = jnp.broadcast_to(cr_ref[...], (8, S5_TL))
            ci = jnp.broadcast_to(ci_ref[...], (8, S5_TL))
            hr = xr + tr_all * cr - ti_all * ci
            hi = xi + tr_all * ci + ti_all * cr
            o_ref[r, 0:S5_TL] = hr
            o_ref[r, S5_TL:w] = hi
            edge = pl.ds(tt * 8, 1) if reverse else pl.ds(tt * 8 + 7, 1)
            cr_ref[...] = o_ref[edge, 0:S5_TL]
            ci_ref[...] = o_ref[edge, S5_TL:w]
            if reverse:
                dr = h_ref[r, 0:S5_TL] - b_ref[r, 0:S5_TL]
                di = h_ref[r, S5_TL:w] - b_ref[r, S5_TL:w]
                acc_ref[:, 0:S5_TL] += hr * dr + hi * di
                acc_ref[:, S5_TL:w] += hi * dr - hr * di
            return carry

        lax.fori_loop(0, nt, tile, 0)

    blk = pl.BlockSpec((tr, w), rmap)
    in_specs = [blk, pl.BlockSpec((8, w), lambda j, i: (0, j))]
    out_specs = [blk]
    out_shape = [jax.ShapeDtypeStruct((S, 2 * S5_N), F32)]
    args = [bu, tab]
    if reverse:
        in_specs += [blk, blk]
        args += [h, bu_fwd]
        out_specs.append(pl.BlockSpec((8, w), lambda j, i: (0, j)))
        out_shape.append(jax.ShapeDtypeStruct((8, 2 * S5_N), F32))
    res = pl.pallas_call(
        body,
        name="s5_scan_bwd" if reverse else "s5_scan_fwd",
        grid=(nl, nrb),
        in_specs=in_specs,
        out_specs=out_specs,
        out_shape=out_shape,
        scratch_shapes=[pltpu.VMEM((1, S5_TL), F32), pltpu.VMEM((1, S5_TL), F32)],
        compiler_params=_cp("parallel", "arbitrary"),
    )(*args)
    return res if reverse else res[0]


def _ri_cols(re, im):
    lead = re.shape[:-1]
    nl = S5_N // S5_TL
    z = jnp.stack([re.reshape(*lead, nl, S5_TL), im.reshape(*lead, nl, S5_TL)], axis=-2)
    return z.reshape(*lead, 2 * S5_N)


def _ri_split(z):
    lead = z.shape[:-1]
    nl = S5_N // S5_TL
    z = z.reshape(*lead, nl, 2, S5_TL)
    return z[..., 0, :].reshape(*lead, S5_N), z[..., 1, :].reshape(*lead, S5_N)


def _s5_prep(lre, lim, logdt, bre, bim, cre, cim):
    lam = lax.complex(lre, lim)
    dt = jnp.exp(logdt)[:, None]
    lam_bar = jnp.exp(lam * dt)
    b = lax.complex(bre, bim)
    b_bar = ((lam_bar - 1.0) / lam)[..., None] * b
    eye = jnp.eye(S5_G, dtype=F32)
    bd_re = jnp.einsum("gpc,gh->gchp", jnp.real(b_bar), eye).reshape(S5_W, S5_N)
    bd_im = jnp.einsum("gpc,gh->gchp", jnp.imag(b_bar), eye).reshape(S5_W, S5_N)
    bd = _ri_cols(bd_re, bd_im)
    cd_re = jnp.einsum("gcp,gh->gchp", cre, eye).reshape(S5_W, S5_N)
    cd_im = jnp.einsum("gcp,gh->gchp", -cim, eye).reshape(S5_W, S5_N)
    cdt = _ri_cols(cd_re, cd_im)
    return jnp.real(lam_bar).reshape(1, S5_N), jnp.imag(lam_bar).reshape(1, S5_N), bd, cdt


def _s5_tables(lre, lim, logdt):
    lam = lax.complex(lre, lim)
    dt = jnp.exp(logdt)[:, None]
    k = jnp.arange(1, 9, dtype=F32)[:, None, None]
    pw = jnp.exp((lam * dt)[None] * k).reshape(8, S5_N)
    fwd = _ri_cols(jnp.real(pw), jnp.imag(pw))
    rev = _ri_cols(jnp.real(pw)[::-1], -jnp.imag(pw)[::-1])
    return fwd, rev


def _rope_tables(positions):
    inv_freq = 1.0 / (ROPE_THETA ** (jnp.arange(0, MLA_ROPE, 2, dtype=F32) / MLA_ROPE))
    ang = positions.astype(F32)[:, None] * inv_freq
    cos, sin = jnp.cos(ang), jnp.sin(ang)
    z = jnp.zeros_like(cos)
    cs = jnp.concatenate([cos, cos, z, z], axis=-1)
    sn = jnp.concatenate([-sin, sin, z, z], axis=-1)
    return cs, sn


def _rope_fwd(qraw, kvraw, proj, cs, sn, tm=256):
    S = qraw.shape[0]
    HW = MLA_H * MLA_HW

    def rope(x, c, s):
        lane = lax.broadcasted_iota(jnp.int32, x.shape, 1)
        sw = jnp.where(lane < 32, pltpu.roll(x, 96, 1), jnp.where(lane < 64, pltpu.roll(x, 32, 1), 0.0))
        return x * c + sw * s

    def body(q_ref, kv_ref, kr_ref, cs_ref, sn_ref, qo_ref, ko_ref, vo_ref):
        c, s = cs_ref[...], sn_ref[...]
        kr = rope(kr_ref[...], c, s).astype(BF16)
        for h in range(MLA_H):
            o = h * MLA_HW
            qo_ref[:, o:o + 128] = q_ref[:, o:o + 128].astype(BF16)
            qo_ref[:, o + 128:o + 256] = rope(q_ref[:, o + 128:o + 256], c, s).astype(BF16)
            ko_ref[:, o:o + 128] = kv_ref[:, o:o + 128].astype(BF16)
            ko_ref[:, o + 128:o + 256] = kr
            vo_ref[:, h * 128:(h + 1) * 128] = kv_ref[:, o + 128:o + 256].astype(BF16)

    return pl.pallas_call(
        body,
        name="rope_fwd",
        grid=(S // tm,),
        in_specs=[_row_spec(tm, HW, 0), _row_spec(tm, HW, 0), _row_spec(tm, 128, (D_INP - 128) // 128),
                  _row_spec(tm, 128, 0), _row_spec(tm, 128, 0)],
        out_specs=[_row_spec(tm, HW, 0), _row_spec(tm, HW, 0), _row_spec(tm, MLA_H * MLA_V, 0)],
        out_shape=[jax.ShapeDtypeStruct((S, HW), BF16), jax.ShapeDtypeStruct((S, HW), BF16),
                   jax.ShapeDtypeStruct((S, MLA_H * MLA_V), BF16)],
        compiler_params=_cp("parallel"),
    )(qraw, kvraw, proj, cs, sn)


def _rope_bwd(dq, dk, dv, cs, sn, tm=256):
    S = dq.shape[0]
    HW = MLA_H * MLA_HW

    def rope_t(x, c, s):
        lane = lax.broadcasted_iota(jnp.int32, x.shape, 1)
        w = x * s
        sw = jnp.where(lane < 32, pltpu.roll(w, 96, 1), jnp.where(lane < 64, pltpu.roll(w, 32, 1), 0.0))
        return x * c + sw

    def body(dq_ref, dk_ref, dv_ref, cs_ref, sn_ref, qo_ref, kvo_ref, kro_ref):
        c, s = cs_ref[...], sn_ref[...]
        kr = jnp.zeros((tm, 128), F32)
        for h in range(MLA_H):
            o = h * MLA_HW
            qo_ref[:, o:o + 128] = dq_ref[:, o:o + 128]
            qo_ref[:, o + 128:o + 256] = rope_t(dq_ref[:, o + 128:o + 256], c, s)
            kvo_ref[:, o:o + 128] = dk_ref[:, o:o + 128]
            kvo_ref[:, o + 128:o + 256] = dv_ref[:, h * 128:(h + 1) * 128]
            kr = kr + dk_ref[:, o + 128:o + 256]
        kro_ref[...] = rope_t(kr, c, s)

    return pl.pallas_call(
        body,
        name="rope_bwd",
        grid=(S // tm,),
        in_specs=[_row_spec(tm, HW, 0), _row_spec(tm, HW, 0), _row_spec(tm, MLA_H * MLA_V, 0),
                  _row_spec(tm, 128, 0), _row_spec(tm, 128, 0)],
        out_specs=[_row_spec(tm, HW, 0), _row_spec(tm, HW, 0), _row_spec(tm, 128, 0)],
        out_shape=[jax.ShapeDtypeStruct((S, HW), F32), jax.ShapeDtypeStruct((S, HW), F32),
                   jax.ShapeDtypeStruct((S, 128), F32)],
        compiler_params=_cp("parallel"),
    )(dq, dk, dv, cs, sn)


ATT_T = 256


def _att_mask(i, j, t):
    rows = i * t + lax.broadcasted_iota(jnp.int32, (t, t), 0)
    cols = j * t + lax.broadcasted_iota(jnp.int32, (t, t), 1)
    return cols <= rows


def _flash_fwd(q, k, v, exch=None):
    S = q.shape[0]
    t = min(ATT_T, S)
    nq = S // t

    def body(q_ref, k_ref, v_ref, o_ref, lse_ref):
        i = pl.program_id(1)
        qb = q_ref[...]

        def step(j, carry):
            m, l, acc = carry
            r = pl.ds(pl.multiple_of(j * t, t), t)
            s = _bdot(qb, k_ref[r, :], _NT) * ATT_SCALE
            s = jnp.where(_att_mask(i, j, t), s, MASK_VALUE)
            m_new = jnp.maximum(m, jnp.max(s, axis=-1, keepdims=True))
            alpha = jnp.exp(m - m_new)
            p = jnp.exp(s - m_new)
            l = alpha * l + jnp.sum(p, axis=-1, keepdims=True)
            acc = alpha * acc + _bdot(p, v_ref[r, :], _NN)
            return m_new, l, acc

        m0 = jnp.full((t, 1), MASK_VALUE, F32)
        m, l, acc = lax.fori_loop(0, i + 1, step, (m0, jnp.zeros((t, 1), F32), jnp.zeros((t, MLA_V), F32)))
        o_ref[...] = acc / l
        lse_ref[...] = jnp.broadcast_to(m + jnp.log(l), (t, 128))

    return _call_hosting(
        body, "flash_fwd", (MLA_H, nq),
        [pl.BlockSpec((t, MLA_HW), lambda h, i: (i, h)),
         pl.BlockSpec((S, MLA_HW), lambda h, i: (0, h)),
         pl.BlockSpec((S, MLA_V), lambda h, i: (0, h))],
        [pl.BlockSpec((t, MLA_V), lambda h, i: (i, h)), pl.BlockSpec((t, 128), lambda h, i: (i, h))],
        [jax.ShapeDtypeStruct((S, MLA_H * MLA_V), F32), jax.ShapeDtypeStruct((S, MLA_H * 128), F32)],
        [q, k, v], exch)


def _flash_bwd_dq(q, k, v, o, lse, dcat):
    S = q.shape[0]
    t = min(ATT_T, S)
    nq = S // t
    do_cb = S5_W // MLA_V

    def body(q_ref, k_ref, v_ref, o_ref, lse_ref, do_ref, dq_ref):
        i = pl.program_id(1)
        qb = q_ref[...]
        do = do_ref[...]
        delta = jnp.sum(do * o_ref[...], axis=-1, keepdims=True)
        lse1 = jnp.max(lse_ref[...], axis=-1, keepdims=True)
        dob = do.astype(BF16)

        def step(j, dq):
            r = pl.ds(pl.multiple_of(j * t, t), t)
            kb = k_ref[r, :]
            s = _bdot(qb, kb, _NT) * ATT_SCALE
            p = jnp.where(_att_mask(i, j, t), jnp.exp(s - lse1), 0.0)
            dp = _bdot(dob, v_ref[r, :], _NT)
            ds = p * (dp - delta) * ATT_SCALE
            return dq + _bdot(ds, kb, _NN)

        dq_ref[...] = lax.fori_loop(0, i + 1, step, jnp.zeros((t, MLA_HW), F32))

    return pl.pallas_call(
        body,
        name="flash_bwd_dq",
        grid=(MLA_H, nq),
        in_specs=[pl.BlockSpec((t, MLA_HW), lambda h, i: (i, h)),
                  pl.BlockSpec((S, MLA_HW), lambda h, i: (0, h)),
                  pl.BlockSpec((S, MLA_V), lambda h, i: (0, h)),
                  pl.BlockSpec((t, MLA_V), lambda h, i: (i, h)),
                  pl.BlockSpec((t, 128), lambda h, i: (i, h)),
                  pl.BlockSpec((t, MLA_V), lambda h, i: (i, do_cb + h))],
        out_specs=pl.BlockSpec((t, MLA_HW), lambda h, i: (i, h)),
        out_shape=jax.ShapeDtypeStruct((S, MLA_H * MLA_HW), F32),
        compiler_params=_cp("parallel", "parallel"),
    )(q, k, v, o, lse, dcat)


def _flash_bwd_dkv(q, k, v, o, lse, dcat, exch=None):
    S = q.shape[0]
    t = min(ATT_T, S)
    nq = S // t
    do_cb = S5_W // MLA_V

    def body(q_ref, k_ref, v_ref, o_ref, lse_ref, do_ref, dk_ref, dv_ref):
        j = pl.program_id(1)
        kb = k_ref[...]
        vb = v_ref[...]

        def step(i, carry):
            dk, dv = carry
            r = pl.ds(pl.multiple_of(i * t, t), t)
            qb = q_ref[r, :]
            do = do_ref[r, :]
            delta = jnp.sum(do * o_ref[r, :], axis=-1, keepdims=True)
            lse1 = jnp.max(lse_ref[r, :], axis=-1, keepdims=True)
            s = _bdot(qb, kb, _NT) * ATT_SCALE
            p = jnp.where(_att_mask(i, j, t), jnp.exp(s - lse1), 0.0)
            dob = do.astype(BF16)
            dv = dv + _bdot(p, dob, _TN)
            dp = _bdot(dob, vb, _NT)
            ds = p * (dp - delta) * ATT_SCALE
            dk = dk + _bdot(ds, qb, _TN)
            return dk, dv

        dk, dv = lax.fori_loop(j, nq, step, (jnp.zeros((t, MLA_HW), F32), jnp.zeros((t, MLA_V), F32)))
        dk_ref[...] = dk
        dv_ref[...] = dv

    return _call_hosting(
        body, "flash_bwd_dkv", (MLA_H, nq),
        [pl.BlockSpec((S, MLA_HW), lambda h, j: (0, h)),
         pl.BlockSpec((t, MLA_HW), lambda h, j: (j, h)),
         pl.BlockSpec((t, MLA_V), lambda h, j: (j, h)),
         pl.BlockSpec((S, MLA_V), lambda h, j: (0, h)),
         pl.BlockSpec((S, 128), lambda h, j: (0, h)),
         pl.BlockSpec((S, MLA_V), lambda h, j: (0, do_cb + h))],
        [pl.BlockSpec((t, MLA_HW), lambda h, j: (j, h)), pl.BlockSpec((t, MLA_V), lambda h, j: (j, h))],
        [jax.ShapeDtypeStruct((S, MLA_H * MLA_HW), F32), jax.ShapeDtypeStruct((S, MLA_H * MLA_V), F32)],
        [q, k, v, o, lse, dcat], exch)


def _split3(x):
    x1 = x.astype(BF16)
    r1 = x - x1.astype(F32)
    x2 = r1.astype(BF16)
    x3 = (r1 - x2.astype(F32)).astype(BF16)
    return x1, x2, x3


def _tri_matmul(x, upper):
    n = x.shape[0]
    r = lax.broadcasted_iota(jnp.int32, (n, n), 0)
    c = lax.broadcasted_iota(jnp.int32, (n, n), 1)
    tri = jnp.where((r <= c) if upper else (r >= c), 1.0, 0.0).astype(BF16)
    x1, x2, x3 = _split3(x)
    dot = lambda v: lax.dot_general(tri, v, _NN, preferred_element_type=F32)
    return dot(x1) + dot(x2) + dot(x3)


@jax.custom_vjp
def _cumsum_rows(x):
    return _tri_matmul(x, False)


def _cumsum_rows_fwd(x):
    return _tri_matmul(x, False), None


def _cumsum_rows_bwd(_, ct):
    return (_tri_matmul(ct, True),)


_cumsum_rows.defvjp(_cumsum_rows_fwd, _cumsum_rows_bwd)


def _hg_step(qin, fin, vin, gin, st, lb, on):
    n = qin.shape[0]
    sig = jax.nn.sigmoid(fin)
    g = jnp.log(lb + (1.0 - lb) * sig)
    k = (1.0 - lb) * jax.nn.sigmoid(-fin)
    q = qin * jax.nn.sigmoid(qin)
    b = _cumsum_rows(g)
    o = _bdot(q * jnp.exp(b), st, _NT)
    row = lax.broadcasted_iota(jnp.int32, (n, HG_D), 0)
    row1 = lax.broadcasted_iota(jnp.int32, (n, 1), 0)
    b_s = None
    for s in range(n):
        sel = row == s
        b_s = jnp.sum(jnp.where(sel, b, 0.0), axis=0, keepdims=True)
        k_s = jnp.sum(jnp.where(sel, k, 0.0), axis=0, keepdims=True)
        v_s = jnp.sum(jnp.where(sel, vin, 0.0), axis=0, keepdims=True)
        e = jnp.exp(jnp.minimum(b - b_s, 0.0))
        c = jnp.sum(q * e * k_s, axis=-1, keepdims=True)
        o = o + jnp.where(row1 >= s, c, 0.0) * v_s
    st_new = st * jnp.exp(b_s) + _bdot(vin, k * jnp.exp(b_s - b), _TN)
    y = _rms(o, on) * (gin * jax.nn.sigmoid(gin))
    return y, st_new


def _hg_specs(tb, nb, reverse):
    rm = (lambda i: nb - 1 - i) if reverse else (lambda i: i)
    base = 1024 // HG_D
    return [pl.BlockSpec((tb, HG_D), lambda h, i, o=o: (rm(i), base + o * HG_H + h)) for o in range(4)], rm


def _hg_fwd(proj, lb, on, tb=256):
    S = proj.shape[0]
    nb = S // tb
    nc = tb // HG_CH
    in_specs, rm = _hg_specs(tb, nb, False)

    def body(q_ref, f_ref, v_ref, g_ref, lb_ref, on_ref, y_ref, sts_ref, st_ref):
        i = pl.program_id(1)

        @pl.when(i == 0)
        def _():
            st_ref[...] = jnp.zeros_like(st_ref)

        def step(c, carry):
            r = pl.ds(pl.multiple_of(c * HG_CH, HG_CH), HG_CH)
            st = st_ref[...]
            sts_ref[0, c] = st
            y, st_new = _hg_step(q_ref[r, :], f_ref[r, :], v_ref[r, :], g_ref[r, :], st, lb_ref[...], on_ref[...])
            y_ref[r, :] = y
            st_ref[...] = st_new
            return carry

        lax.fori_loop(0, nc, step, 0)

    return pl.pallas_call(
        body,
        name="hgrn2_fwd",
        grid=(HG_H, nb),
        in_specs=in_specs + [pl.BlockSpec((1, HG_D), lambda h, i: (0, h)), pl.BlockSpec((1, HG_D), lambda h, i: (0, 0))],
        out_specs=[pl.BlockSpec((tb, HG_D), lambda h, i: (i, h)),
                   pl.BlockSpec((1, nc, HG_D, HG_D), lambda h, i: (h, i, 0, 0))],
        out_shape=[jax.ShapeDtypeStruct((S, HG_H * HG_D), F32),
                   jax.ShapeDtypeStruct((HG_H, S // HG_CH, HG_D, HG_D), F32)],
        scratch_shapes=[pltpu.VMEM((HG_D, HG_D), F32)],
        compiler_params=_cp("parallel", "arbitrary"),
    )(proj, proj, proj, proj, lb, on)


def _hg_bwd(proj, sts, lb, on, dcat, tb=256):
    S = proj.shape[0]
    nb = S // tb
    nc = tb // HG_CH
    in_specs, rm = _hg_specs(tb, nb, True)
    dy_cb = (S5_W + MLA_H * MLA_V) // HG_D

    def body(q_ref, f_ref, v_ref, g_ref, lb_ref, on_ref, sts_ref, dy_ref,
             dq_ref, df_ref, dv_ref, dg_ref, dlb_ref, don_ref, dst_ref):
        i = pl.program_id(1)

        @pl.when(i == 0)
        def _():
            dst_ref[...] = jnp.zeros_like(dst_ref)
            dlb_ref[...] = jnp.zeros_like(dlb_ref)
            don_ref[...] = jnp.zeros_like(don_ref)

        def step(cc, carry):
            c = nc - 1 - cc
            r = pl.ds(pl.multiple_of(c * HG_CH, HG_CH), HG_CH)
            _, vjp = jax.vjp(_hg_step, q_ref[r, :], f_ref[r, :], v_ref[r, :], g_ref[r, :], sts_ref[0, c],
                             lb_ref[...], on_ref[...])
            dq, df, dv, dg, dst, dlb, don = vjp((dy_ref[r, :], dst_ref[...]))
            dq_ref[r, :] = dq
            df_ref[r, :] = df
            dv_ref[r, :] = dv
            dg_ref[r, :] = dg
            dst_ref[...] = dst
            dlb_ref[...] += dlb
            don_ref[...] += don
            return carry

        lax.fori_loop(0, nc, step, 0)

    blk = pl.BlockSpec((tb, HG_D), lambda h, i: (rm(i), h))
    par = pl.BlockSpec((1, HG_D), lambda h, i: (0, h))
    return pl.pallas_call(
        body,
        name="hgrn2_bwd",
        grid=(HG_H, nb),
        in_specs=in_specs + [pl.BlockSpec((1, HG_D), lambda h, i: (0, h)), pl.BlockSpec((1, HG_D), lambda h, i: (0, 0)),
                             pl.BlockSpec((1, nc, HG_D, HG_D), lambda h, i: (h, rm(i), 0, 0)),
                             pl.BlockSpec((tb, HG_D), lambda h, i: (rm(i), dy_cb + h))],
        out_specs=[blk, blk, blk, blk, par, par],
        out_shape=[jax.ShapeDtypeStruct((S, HG_H * HG_D), F32)] * 4 + [jax.ShapeDtypeStruct((1, HG_H * HG_D), F32)] * 2,
        scratch_shapes=[pltpu.VMEM((HG_D, HG_D), F32)],
        compiler_params=_cp("parallel", "arbitrary"),
    )(proj, proj, proj, proj, lb, on, sts, dcat)


CONV_TC = 512
CONV_NC = D_FFP // CONV_TC


def _shift_down(cur, halo, k):
    tm = cur.shape[0]
    row = lax.broadcasted_iota(jnp.int32, cur.shape, 0)
    top = jnp.concatenate([pltpu.roll(halo, k, 0), jnp.zeros((tm - 8, cur.shape[1]), F32)], axis=0)
    return jnp.where(row < k, top, pltpu.roll(cur, k, 0))


def _shift_up(cur, halo, k):
    tm = cur.shape[0]
    row = lax.broadcasted_iota(jnp.int32, cur.shape, 0)
    bot = jnp.concatenate([jnp.zeros((tm - 8, cur.shape[1]), F32), pltpu.roll(halo, 8 - k, 0)], axis=0)
    return jnp.where(row >= tm - k, bot, pltpu.roll(cur, tm - k, 0))


def _conv3(cur, halo, w_ref, b_ref):
    return (b_ref[...] + _shift_down(cur, halo, 2) * w_ref[pl.ds(0, 1), :]
            + _shift_down(cur, halo, 1) * w_ref[pl.ds(1, 1), :] + cur * w_ref[pl.ds(2, 1), :])


def _conv_fwd(u0, cw, cb, tm=256):
    S = u0.shape[0]
    nc = CONV_NC
    m8 = tm // 8
    prev = lambda i: jnp.maximum(i * m8 - 1, 0)

    def body(g_ref, v_ref, pg_ref, pv_ref, wg_ref, wv_ref, bg_ref, bv_ref, a_ref):
        on = (pl.program_id(1) > 0).astype(F32)
        ug = _conv3(g_ref[...], pg_ref[...] * on, wg_ref, bg_ref)
        uv = _conv3(v_ref[...], pv_ref[...] * on, wv_ref, bv_ref)
        a_ref[...] = (jax.nn.gelu(ug, approximate=True) * uv).astype(a_ref.dtype)

    tc = CONV_TC
    return pl.pallas_call(
        body,
        name="conv_geglu_fwd",
        grid=(nc, S // tm),
        in_specs=[pl.BlockSpec((tm, tc), lambda j, i: (i, j)), pl.BlockSpec((tm, tc), lambda j, i: (i, j + nc)),
                  pl.BlockSpec((8, tc), lambda j, i: (prev(i), j)), pl.BlockSpec((8, tc), lambda j, i: (prev(i), j + nc)),
                  pl.BlockSpec((3, tc), lambda j, i: (0, j)), pl.BlockSpec((3, tc), lambda j, i: (0, j + nc)),
                  pl.BlockSpec((1, tc), lambda j, i: (0, j)), pl.BlockSpec((1, tc), lambda j, i: (0, j + nc))],
        out_specs=pl.BlockSpec((tm, tc), lambda j, i: (i, j)),
        out_shape=jax.ShapeDtypeStruct((S, D_FFP), BF16),
        compiler_params=_cp("parallel", "parallel"),
    )(u0, u0, u0, u0, cw, cw, cb, cb)


def _conv_bwd_elem(da, u0, cw, cb, tm=256):
    S = u0.shape[0]
    nc = CONV_NC
    m8 = tm // 8
    prev = lambda i: jnp.maximum(i * m8 - 1, 0)

    def body(g_ref, v_ref, pg_ref, pv_ref, wg_ref, wv_ref, bg_ref, bv_ref, da_ref, du_ref, dw_ref, db_ref):
        j = pl.program_id(0)
        i = pl.program_id(1)
        on = (i > 0).astype(F32)
        pg = pg_ref[...] * on
        pv = pv_ref[...] * on
        ug = _conv3(g_ref[...], pg, wg_ref, bg_ref)
        uv = _conv3(v_ref[...], pv, wv_ref, bv_ref)
        _, vjp = jax.vjp(lambda a, b: jax.nn.gelu(a, approximate=True) * b, ug, uv)
        dug, duv = vjp(da_ref[...])
        is_gate = j < nc
        du = jnp.where(is_gate, dug, duv)
        cur = jnp.where(is_gate, g_ref[...], v_ref[...])
        halo = jnp.where(is_gate, pg, pv)
        du_ref[...] = du

        @pl.when(i == 0)
        def _():
            dw_ref[...] = jnp.zeros_like(dw_ref)
            db_ref[...] = jnp.zeros_like(db_ref)

        dw_ref[pl.ds(0, 1), :] += jnp.sum(du * _shift_down(cur, halo, 2), axis=0, keepdims=True)
        dw_ref[pl.ds(1, 1), :] += jnp.sum(du * _shift_down(cur, halo, 1), axis=0, keepdims=True)
        dw_ref[pl.ds(2, 1), :] += jnp.sum(du * cur, axis=0, keepdims=True)
        db_ref[...] += jnp.sum(du, axis=0, keepdims=True)

    tc = CONV_TC
    jj = lambda j: j % nc
    return pl.pallas_call(
        body,
        name="conv_geglu_bwd",
        grid=(2 * nc, S // tm),
        in_specs=[pl.BlockSpec((tm, tc), lambda j, i: (i, jj(j))), pl.BlockSpec((tm, tc), lambda j, i: (i, jj(j) + nc)),
                  pl.BlockSpec((8, tc), lambda j, i: (prev(i), jj(j))),
                  pl.BlockSpec((8, tc), lambda j, i: (prev(i), jj(j) + nc)),
                  pl.BlockSpec((3, tc), lambda j, i: (0, jj(j))), pl.BlockSpec((3, tc), lambda j, i: (0, jj(j) + nc)),
                  pl.BlockSpec((1, tc), lambda j, i: (0, jj(j))), pl.BlockSpec((1, tc), lambda j, i: (0, jj(j) + nc)),
                  pl.BlockSpec((tm, tc), lambda j, i: (i, jj(j)))],
        out_specs=[pl.BlockSpec((tm, tc), lambda j, i: (i, j)), pl.BlockSpec((3, tc), lambda j, i: (0, j)),
                   pl.BlockSpec((1, tc), lambda j, i: (0, j))],
        out_shape=[jax.ShapeDtypeStruct((S, 2 * D_FFP), F32), jax.ShapeDtypeStruct((3, 2 * D_FFP), F32),
                   jax.ShapeDtypeStruct((1, 2 * D_FFP), F32)],
        compiler_params=_cp("parallel", "arbitrary"),
    )(u0, u0, u0, u0, cw, cw, cb, cb, da)


def _conv_bwd_input(du, cw, tm=256):
    S = du.shape[0]
    nrb = S // tm
    m8 = tm // 8
    nxt = lambda i: jnp.minimum((i + 1) * m8, S // 8 - 1)

    def body(d_ref, n_ref, w_ref, o_ref):
        on = (pl.program_id(1) < nrb - 1).astype(F32)
        cur = d_ref[...]
        halo = n_ref[...] * on
        o_ref[...] = (cur * w_ref[pl.ds(2, 1), :] + _shift_up(cur, halo, 1) * w_ref[pl.ds(1, 1), :]
                      + _shift_up(cur, halo, 2) * w_ref[pl.ds(0, 1), :])

    tc = CONV_TC
    return pl.pallas_call(
        body,
        name="conv_bwd_input",
        grid=(2 * CONV_NC, nrb),
        in_specs=[pl.BlockSpec((tm, tc), lambda j, i: (i, j)), pl.BlockSpec((8, tc), lambda j, i: (nxt(i), j)),
                  pl.BlockSpec((3, tc), lambda j, i: (0, j))],
        out_specs=pl.BlockSpec((tm, tc), lambda j, i: (i, j)),
        out_shape=jax.ShapeDtypeStruct((S, 2 * D_FFP), F32),
        compiler_params=_cp("parallel", "parallel"),
    )(du, du, cw)


def _exchange(arrs, scatter, name):
    n = len(arrs)

    def body(*refs):
        args = (refs[:n], refs[n:2 * n], *refs[2 * n:], scatter)
        _exchange_start(*args)
        _exchange_wait(*args)

    hbm = pl.BlockSpec(memory_space=pltpu.HBM)
    out_shape, sems = _exchange_shapes(arrs, scatter)
    return pl.pallas_call(
        body,
        name=name,
        in_specs=[hbm] * n,
        out_specs=[hbm] * n,
        out_shape=out_shape,
        scratch_shapes=sems,
    )(*arrs)


def _exchange_copies(ins, outs, send, recv, loc, scatter):
    x, y, c = lax.axis_index("x"), lax.axis_index("y"), lax.axis_index("c")
    me = 4 * x + 2 * y + c
    sends, recvs, locs = [], [], []
    for a in range(len(ins)):
        locs.append(pltpu.make_async_copy(ins[a].at[me] if scatter else ins[a], outs[a].at[me], loc.at[a]))
        for k in range(1, N_DEV):
            px = 1 - x if k & 4 else x
            py = 1 - y if k & 2 else y
            pc = 1 - c if k & 1 else c
            peer = 4 * px + 2 * py + pc
            src = ins[a].at[peer] if scatter else ins[a]
            sems = dict(send_sem=send.at[a, k - 1], recv_sem=recv.at[a, k - 1], device_id=(px, py, pc),
                        device_id_type=pl.DeviceIdType.MESH)
            sends.append(pltpu.make_async_remote_copy(src_ref=src, dst_ref=outs[a].at[me], **sems))
            recvs.append(pltpu.make_async_remote_copy(src_ref=src, dst_ref=outs[a].at[peer], **sems))
    return locs, sends, recvs


def _exchange_start(*refs):
    locs, sends, _ = _exchange_copies(*refs)
    for cp in locs + sends:
        cp.start()


def _exchange_wait(*refs):
    locs, sends, recvs = _exchange_copies(*refs)
    for cp in recvs:
        cp.wait_recv()
    for cp in sends:
        cp.wait_send()
    for cp in locs:
        cp.wait()


def _exchange_shapes(arrs, scatter):
    n = len(arrs)
    out_shape = [jax.ShapeDtypeStruct(a.shape if scatter else (N_DEV,) + a.shape, a.dtype) for a in arrs]
    sems = [pltpu.SemaphoreType.DMA((n, N_DEV - 1)), pltpu.SemaphoreType.DMA((n, N_DEV - 1)),
            pltpu.SemaphoreType.DMA((n,))]
    return out_shape, sems


def _call_hosting(body, name, grid, in_specs, out_specs, out_shape, args, exch):
    if exch is None:
        res = pl.pallas_call(body, name=name, grid=grid, in_specs=in_specs, out_specs=out_specs, out_shape=out_shape,
                             compiler_params=_cp(*["parallel"] * len(grid)))(*args)
        return list(res), []
    arrs, scatter = exch
    n, n_in, n_out = len(arrs), len(in_specs), len(out_specs)
    hbm = pl.BlockSpec(memory_space=pltpu.HBM)
    x_shape, sems = _exchange_shapes(arrs, scatter)

    def hosting_body(*refs):
        cin, xin = refs[:n_in], refs[n_in:n_in + n]
        cout, xout = refs[n_in + n:n_in + n + n_out], refs[n_in + n + n_out:n_in + 2 * n + n_out]
        xsem = refs[n_in + 2 * n + n_out:]
        ids = [pl.program_id(d) for d in range(len(grid))]
        first = functools.reduce(jnp.logical_and, [i == 0 for i in ids])
        last = functools.reduce(jnp.logical_and, [i == g - 1 for i, g in zip(ids, grid)])

        @pl.when(first)
        def _():
            _exchange_start(xin, xout, *xsem, scatter)

        body(*cin, *cout)

        @pl.when(last)
        def _():
            _exchange_wait(xin, xout, *xsem, scatter)

    res = pl.pallas_call(
        hosting_body, name=name + "_x", grid=grid, in_specs=in_specs + [hbm] * n, out_specs=out_specs + [hbm] * n,
        out_shape=out_shape + x_shape, scratch_shapes=sems,
        compiler_params=_cp(*["arbitrary"] * len(grid)))(*args, *arrs)
    return list(res[:n_out]), list(res[n_out:])


def _adamw(recv, w, m, v, name="adamw"):
    L, n, R, C = recv.shape
    fits = [t for t in range(8, R + 1, 8) if R % t == 0 and t * C * 4 <= (1 << 19)]
    tr = max(fits) if fits else R

    def body(r_ref, w_ref, m_ref, v_ref, g_ref, d_ref, mo_ref, vo_ref):
        g = r_ref[0, 0].astype(F32)
        for d in range(1, n):
            g = g + r_ref[0, d].astype(F32)
        mm = ADAM_B1 * m_ref[0] + (1.0 - ADAM_B1) * g
        vv = ADAM_B2 * v_ref[0] + (1.0 - ADAM_B2) * (g * g)
        m_hat = mm / (1.0 - ADAM_B1 ** ADAM_STEP)
        v_hat = vv / (1.0 - ADAM_B2 ** ADAM_STEP)
        g_ref[0] = g
        d_ref[0] = -ADAM_LR * (m_hat / (jnp.sqrt(v_hat) + ADAM_EPS) + ADAM_WD * w_ref[0])
        mo_ref[0] = mm
        vo_ref[0] = vv

    blk = pl.BlockSpec((1, tr, C), lambda l, i: (l, i, 0))
    return pl.pallas_call(
        body,
        name=name,
        grid=(L, R // tr),
        in_specs=[pl.BlockSpec((1, n, tr, C), lambda l, i: (l, 0, i, 0)), blk, blk, blk],
        out_specs=[blk] * 4,
        out_shape=[jax.ShapeDtypeStruct((L, R, C), F32)] * 4,
        compiler_params=_cp("parallel", "parallel"),
    )(recv, w, m, v)


def _layer_fwd(x, mod, W, P, exch=None):
    sh1, sc1, g1, sh2, sc2, g2 = mod
    D = D_MODEL
    R = {"x": x}
    (h1,) = _rowwise(_f_pre, [(x, D, 0)], [P["n1"], sc1, sh1], [(D, BF16)], "pre_norm")
    proj = _mm(h1, W["w_in"], "nn", name="mm_in")
    R["h1"], R["proj"] = h1, proj
    bu = _mm(proj, W["bd"], "nn", a_col0=0, a_cols=S5_W, name="mm_s5_b")
    hs = _s5_scan(bu, P["tab_fwd"])
    yc = _mm(hs, W["cdt"], "nt", name="mm_s5_c")
    (gg,) = _rowwise(_f_s5a, [(yc, S5_W, 0), (proj, S5_W, 0)], [P["s5_d"]], [(S5_W, F32)], "s5_gelu")
    z = _mm(gg, W["w_glu"], "nn", name="mm_glu")
    (ys5,) = _rowwise(_f_s5b, [(gg, S5_W, 0), (z, S5_W, 0)], [], [(S5_W, BF16)], "s5_glu")
    R.update(bu=bu, hs=hs, yc=yc, gg=gg, z=z)
    (qn,) = _rowwise(_f_norm, [(proj, 512, 1)], [P["q_norm"]], [(512, BF16)], "q_norm")
    (kvn,) = _rowwise(_f_norm, [(proj, 256, 12)], [P["kv_norm"]], [(256, BF16)], "kv_norm")
    qraw = _mm(qn, W["w_uq"], "nn", name="mm_uq")
    kvraw = _mm(kvn, W["w_ukv"], "nn", name="mm_ukv")
    q, k, v = _rope_fwd(qraw, kvraw, proj, P["cs"], P["sn"])
    (o, lse), got = _flash_fwd(q, k, v, exch)
    R.update(qn=qn, kvn=kvn, q=q, k=k, v=v, o=o, lse=lse)
    yhg, sts = _hg_fwd(proj, P["lb"], P["hg_on"])
    R["sts"] = sts
    cat = jnp.concatenate([ys5, o.astype(BF16), yhg.astype(BF16)], axis=-1)
    mixed = _mm(cat, W["w_out"], "nn", name="mm_out")
    (x2,) = _rowwise(_f_post, [(x, D, 0), (mixed, D, 0)], [P["n2"], g1], [(D, F32)], "post_norm")
    R.update(cat=cat, mixed=mixed, x2=x2)
    (h2,) = _rowwise(_f_pre, [(x2, D, 0)], [P["n3"], sc2, sh2], [(D, BF16)], "pre_norm")
    u0 = _mm(h2, W["w_up"], "nn", name="mm_up")
    a = _conv_fwd(u0, P["conv_w"], P["conv_b"])
    y = _mm(a, W["w_down"], "nn", name="mm_down")
    (x3,) = _rowwise(_f_post, [(x2, D, 0), (y, D, 0)], [P["n4"], g2], [(D, F32)], "post_norm")
    R.update(h2=h2, u0=u0, a=a, y=y)
    return x3, R, got


def _layer_bwd(dx3, mod, W, P, R, exch=None):
    sh1, sc1, g1, sh2, sc2, g2 = mod
    D = D_MODEL
    G = {}
    (dx2a, dy), (dn4, dg2) = _rowwise_vjp(_f_post, [(R["x2"], D, 0), (R["y"], D, 0)], [P["n4"], g2],
                                          [[(dx3, D, 0)]], "post_norm_bwd", [True, True])
    da = _mm(dy, W["w_down"], "nt", name="mm_down_dx")
    G["w_down"] = _mm(R["a"], dy, "tn", out_dtype=BF16, name="mm_down_dw")
    du, dcw, dcb = _conv_bwd_elem(da, R["u0"], P["conv_w"], P["conv_b"])
    du0 = _conv_bwd_input(du, P["conv_w"])
    dh2 = _mm(du0, W["w_up"], "nt", name="mm_up_dx")
    G["w_up"] = _mm(R["h2"], du0, "tn", out_dtype=BF16, name="mm_up_dw")
    (dx2,), (dn3, dsc2, dsh2) = _rowwise_vjp(_f_pre, [(R["x2"], D, 0)], [P["n3"], sc2, sh2], [[(dh2, D, 0)]],
                                             "pre_norm_bwd", [True], add_rows={0: (dx2a, D, 0)})
    (dxa, dmixed), (dn2, dg1) = _rowwise_vjp(_f_post, [(R["x"], D, 0), (R["mixed"], D, 0)], [P["n2"], g1],
                                             [[(dx2, D, 0)]], "post_norm_bwd", [True, True])
    dcat = _mm(dmixed, W["w_out"], "nt", name="mm_out_dx")
    G["w_out"] = _mm(R["cat"], dmixed, "tn", out_dtype=BF16, name="mm_out_dw")
    (dga, dz), _ = _rowwise_vjp(_f_s5b, [(R["gg"], S5_W, 0), (R["z"], S5_W, 0)], [], [[(dcat, S5_W, 0)]],
                                "s5_glu_bwd", [True, True])
    dgb = _mm(dz, W["w_glu"], "nt", name="mm_glu_dx")
    G["w_glu"] = _mm(R["gg"], dz, "tn", out_dtype=BF16, name="mm_glu_dw")
    (dyc, dua), (dd,) = _rowwise_vjp(_f_s5a, [(R["yc"], S5_W, 0), (R["proj"], S5_W, 0)], [P["s5_d"]],
                                     [[(dga, S5_W, 0), (dgb, S5_W, 0)]], "s5_gelu_bwd", [True, True])
    dhs = _mm(dyc, W["cdt"], "nn", name="mm_s5_c_dx")
    dcdt = _mm(dyc, R["hs"], "tn", name="mm_s5_c_dw")
    gs, acc = _s5_scan(dhs, P["tab_rev"], reverse=True, h=R["hs"], bu_fwd=R["bu"])
    dub = _mm(gs, W["bd"], "nt", name="mm_s5_b_dx")
    dbd = _mm(R["proj"], gs, "tn", a_col0=0, a_cols=S5_W, name="mm_s5_b_dw")
    dq = _flash_bwd_dq(R["q"], R["k"], R["v"], R["o"], R["lse"], dcat)
    (dk, dv), got = _flash_bwd_dkv(R["q"], R["k"], R["v"], R["o"], R["lse"], dcat, exch)
    dqraw, dkvraw, dkr = _rope_bwd(dq, dk, dv, P["cs"], P["sn"])
    dqn = _mm(dqraw, W["w_uq"], "nt", name="mm_uq_dx")
    G["w_uq"] = _mm(R["qn"], dqraw, "tn", out_dtype=BF16, name="mm_uq_dw")
    dkvn = _mm(dkvraw, W["w_ukv"], "nt", name="mm_ukv_dx")
    G["w_ukv"] = _mm(R["kvn"], dkvraw, "tn", out_dtype=BF16, name="mm_ukv_dw")
    (dcq,), (dqnorm,) = _rowwise_vjp(_f_norm, [(R["proj"], 512, 1)], [P["q_norm"]], [[(dqn, 512, 0)]],
                                     "q_norm_bwd", [True])
    (dckv,), (dkvnorm,) = _rowwise_vjp(_f_norm, [(R["proj"], 256, 12)], [P["kv_norm"]], [[(dkvn, 256, 0)]],
                                       "kv_norm_bwd", [True])
    dhq, dhf, dhi, dhg, dlb, don = _hg_bwd(R["proj"], R["sts"], P["lb"], P["hg_on"], dcat)
    dproj = jnp.concatenate([dua + dub, dcq, dhq, dhf, dhi, dhg, dckv, dkr], axis=-1)
    dh1 = _mm(dproj, W["w_in"], "nt", name="mm_in_dx")
    G["w_in"] = _mm(R["h1"], dproj, "tn", out_dtype=BF16, name="mm_in_dw")
    (dx,), (dn1, dsc1, dsh1) = _rowwise_vjp(_f_pre, [(R["x"], D, 0)], [P["n1"], sc1, sh1], [[(dh1, D, 0)]],
                                            "pre_norm_bwd", [True], add_rows={0: (dxa, D, 0)})
    dmod = jnp.concatenate([dsh1, dsc1, dg1, dsh2, dsc2, dg2], axis=-1)
    small = dict(n1=dn1, n2=dn2, n3=dn3, n4=dn4, s5_d=dd, q_norm=dqnorm, kv_norm=dkvnorm,
                 lb=dlb, hg_on=jnp.sum(don.reshape(HG_H, HG_D), axis=0, keepdims=True),
                 conv_w=dcw, conv_b=dcb, bd=dbd, cdt=dcdt, acc=jnp.sum(acc, axis=0, keepdims=True))
    return dx, dmod, G, small, got


def _cols_from_shards(g):
    return jnp.transpose(g, (1, 0, 2)).reshape(g.shape[1], -1)


def _cols_to_shards(w):
    K = w.shape[0]
    return jnp.transpose(w.reshape(K, N_DEV, -1), (1, 0, 2))


def _pad_ff(w):
    pad = [(0, 0)] * (w.ndim - 1) + [(0, D_FFP - D_FF)]
    return jnp.concatenate([jnp.pad(w[..., :D_FF], pad), jnp.pad(w[..., D_FF:], pad)], axis=-1)


def _unpad_ff(w):
    return jnp.concatenate([w[..., :D_FF], w[..., D_FFP:D_FFP + D_FF]], axis=-1)


def _assemble(gathered):
    w_in = _cols_from_shards(gathered["w_in"])
    w_in = jnp.concatenate([w_in[:, 0:1024], w_in[:, 1344:D_IN], w_in[:, 1024:1344],
                            jnp.zeros((w_in.shape[0], D_INP - D_IN), w_in.dtype)], axis=1)
    w_uq = _cols_from_shards(gathered["w_uq"]).reshape(-1, MLA_H, MLA_NOPE + MLA_ROPE)
    w_uq = jnp.pad(w_uq, ((0, 0), (0, 0), (0, MLA_HW - MLA_NOPE - MLA_ROPE))).reshape(-1, MLA_H * MLA_HW)
    w_down = gathered["w_down"].reshape(D_FF, D_MODEL)
    return dict(
        w_in=w_in,
        w_glu=gathered["w_glu"].reshape(S5_W, S5_W),
        w_uq=w_uq,
        w_ukv=_cols_from_shards(gathered["w_ukv"]),
        w_out=gathered["w_out"].reshape(D_MODEL, D_MODEL),
        w_up=_pad_ff(_cols_from_shards(gathered["w_up"])),
        w_down=jnp.pad(w_down, ((0, D_FFP - D_FF), (0, 0))),
    )


def _grad_shards(G):
    w_in = jnp.concatenate([G["w_in"][:, 0:1024], G["w_in"][:, 3072:D_IN], G["w_in"][:, 1024:3072]], axis=1)
    w_uq = G["w_uq"].reshape(-1, MLA_H, MLA_HW)[:, :, :MLA_NOPE + MLA_ROPE].reshape(-1, MLA_H * (MLA_NOPE + MLA_ROPE))
    return dict(
        w_in=_cols_to_shards(w_in),
        w_glu=G["w_glu"].reshape(N_DEV, -1, S5_W),
        w_uq=_cols_to_shards(w_uq),
        w_ukv=_cols_to_shards(G["w_ukv"]),
        w_out=G["w_out"].reshape(N_DEV, -1, D_MODEL),
        w_up=_cols_to_shards(_unpad_ff(G["w_up"])),
        w_down=G["w_down"][:D_FF].reshape(N_DEV, -1, D_MODEL),
    )


_BIG = ("w_in", "w_glu", "w_uq", "w_ukv", "w_out", "w_up", "w_down")
_SMALL = ("s5_lambda_re", "s5_lambda_im", "s5_log_dt", "s5_b_re", "s5_b_im", "s5_c_re", "s5_c_im", "s5_d",
          "mla_q_norm", "mla_kv_norm", "hg_lb_logits", "hg_out_norm", "mix_pre_norm", "mix_post_norm",
          "ffn_pre_norm", "ffn_post_norm", "ffn_conv_w_full", "ffn_conv_b", "b_ada")
PACK_ROW = 1024


def _pack(parts):
    flat = jnp.concatenate([p.reshape(-1) for p in parts])
    n = flat.shape[0]
    pad = (-n) % (8 * PACK_ROW)
    return jnp.pad(flat, (0, pad)).reshape(-1, PACK_ROW)


def _unpack(packed, shapes):
    flat = packed.reshape(-1)
    out, pos = [], 0
    for s in shapes:
        n = int(np.prod(s))
        out.append(flat[pos:pos + n].reshape(s))
        pos += n
    return out


def _step(x, c, positions, loss_target, w, m, v):
    x = x[0]
    S = x.shape[0]
    L = w["w_in"].shape[0]
    D = D_MODEL
    me = 4 * lax.axis_index("x") + 2 * lax.axis_index("y") + lax.axis_index("c")

    (c_all,) = _exchange([c], False, "gather_c")
    c_all = c_all.reshape(N_DEV, D)
    (c_act,) = _rowwise(lambda a: (a * jax.nn.sigmoid(a),), [(c_all, D, 0)], [], [(D, F32)], "silu_c", tm=N_DEV)
    mod_part = jnp.stack([_mm(c_act, w["w_ada"][l], "nn", name="mm_ada") for l in range(L)])
    (mod_all,) = _exchange([mod_part], False, "gather_mod")
    mod_mine = lax.dynamic_index_in_dim(mod_all, me, axis=2, keepdims=False)
    mod_full = jnp.transpose(mod_mine, (1, 0, 2)).reshape(L, 6 * D) + w["b_ada"]
    mods = [[mod_full[l:l + 1, i * D:(i + 1) * D] for i in range(6)] for l in range(L)]

    cs, sn = _rope_tables(positions[0])
    lower, lower_vjp = jax.vjp(lambda lg: jnp.cumsum(jax.nn.softmax(lg, axis=0), axis=0)
                               - jax.nn.softmax(lg, axis=0)[0:1], w["hg_lb_logits"])
    conv_w_full = []

    def weight_shards(l):
        return [w[n][l].astype(BF16) for n in _BIG] + [w["ffn_conv_w"][l]]

    def layer_params(l, got):
        gathered = dict(zip(_BIG, got[:-1]))
        Wl = _assemble(gathered)
        cw_full = _cols_from_shards(got[-1])
        s5_args = (w["s5_lambda_re"][l], w["s5_lambda_im"][l], w["s5_log_dt"][l], w["s5_b_re"][l], w["s5_b_im"][l],
                   w["s5_c_re"][l], w["s5_c_im"][l])
        (lbr, lbi, bd, cdt), prep_vjp = jax.vjp(_s5_prep, *s5_args)
        tab_fwd, tab_rev = _s5_tables(*s5_args[:3])
        Wl["bd"], Wl["cdt"] = bd.astype(BF16), cdt.astype(BF16)
        row = lambda a: a.reshape(1, -1)
        Pl = dict(
            n1=row(w["mix_pre_norm"][l]), n2=row(w["mix_post_norm"][l]), n3=row(w["ffn_pre_norm"][l]),
            n4=row(w["ffn_post_norm"][l]), s5_d=row(w["s5_d"][l]), q_norm=row(w["mla_q_norm"][l]),
            kv_norm=row(w["mla_kv_norm"][l]), lb=row(lower[l]), hg_on=row(w["hg_out_norm"][l]),
            conv_w=_pad_ff(cw_full), conv_b=_pad_ff(row(w["ffn_conv_b"][l])),
            tab_fwd=tab_fwd, tab_rev=tab_rev, cs=cs, sn=sn, lam_bar=(lbr, lbi))
        return Wl, Pl, prep_vjp, cw_full

    Ws, Ps, preps, Rs = [], [], [], []
    h = x
    got = _exchange(weight_shards(0), False, "gather_weights")
    for l in range(L):
        Wl, Pl, prep_vjp, cw_full = layer_params(l, got)
        Ws.append(Wl)
        Ps.append(Pl)
        preps.append(prep_vjp)
        conv_w_full.append(cw_full)
        h, R, got = _layer_fwd(h, mods[l], Wl, Pl, (weight_shards(l + 1), False) if l + 1 < L else None)
        Rs.append(R)
    loss_local, dh = _loss_head(h, loss_target[0])
    loss = lax.psum(loss_local, ("x", "y", "c"))

    big_recv = {n: [None] * L for n in _BIG}
    small_g = {n: [None] * L for n in _SMALL if n != "hg_lb_logits"}
    dlower = [None] * L
    pending = None
    for l in reversed(range(L)):
        dh, dmod, G, sm, got = _layer_bwd(dh, mods[l], Ws[l], Ps[l], Rs[l], pending)
        Rs[l] = None
        for n, r in zip(_BIG, got):
            big_recv[n][l + 1] = r
        shards = _grad_shards(G)
        pending = ([shards[n] for n in _BIG], True)
        lbr, lbi = Ps[l]["lam_bar"]
        ar, ai = _ri_split(sm["acc"])
        dl = lax.complex(ar, ai) / lax.complex(lbr, -lbi)
        d_s5 = preps[l]((jnp.real(dl), jnp.imag(dl), sm["bd"], sm["cdt"]))
        for n, g in zip(("s5_lambda_re", "s5_lambda_im", "s5_log_dt", "s5_b_re", "s5_b_im", "s5_c_re", "s5_c_im"), d_s5):
            small_g[n][l] = g
        small_g["s5_d"][l] = sm["s5_d"][0]
        small_g["mla_q_norm"][l] = sm["q_norm"][0]
        small_g["mla_kv_norm"][l] = sm["kv_norm"][0]
        small_g["hg_out_norm"][l] = sm["hg_on"][0]
        small_g["mix_pre_norm"][l] = sm["n1"][0]
        small_g["mix_post_norm"][l] = sm["n2"][0]
        small_g["ffn_pre_norm"][l] = sm["n3"][0]
        small_g["ffn_post_norm"][l] = sm["n4"][0]
        small_g["ffn_conv_w_full"][l] = _unpad_ff(sm["conv_w"])
        small_g["ffn_conv_b"][l] = _unpad_ff(sm["conv_b"])[0]
        small_g["b_ada"][l] = dmod[0]
        dlower[l] = sm["lb"][0]
    for n, r in zip(_BIG, _exchange(pending[0], True, "scatter_grads")):
        big_recv[n][0] = r
    small_g = {n: jnp.stack(gl) for n, gl in small_g.items()}
    (small_g["hg_lb_logits"],) = lower_vjp(jnp.stack(dlower))

    small_w = {n: w[n] for n in _SMALL if n != "ffn_conv_w_full"}
    small_w["ffn_conv_w_full"] = jnp.stack(conv_w_full)
    shapes = [small_w[n].shape for n in _SMALL]
    zeros_cw = jnp.zeros_like(small_w["ffn_conv_w_full"])
    pk_g = _pack([small_g[n] for n in _SMALL])
    pk_w = _pack([small_w[n] for n in _SMALL])
    pk_m = _pack([zeros_cw if n == "ffn_conv_w_full" else m[n] for n in _SMALL])
    pk_v = _pack([zeros_cw + 1.0 if n == "ffn_conv_w_full" else v[n] for n in _SMALL])
    (pk_all,) = _exchange([pk_g], False, "gather_small")
    sg, sd, sm_, sv = _adamw(pk_all[None], pk_w[None], pk_m[None], pk_v[None], name="adamw_small")
    small_out = {}
    for key, arr in (("g", sg), ("d", sd), ("m", sm_), ("v", sv)):
        small_out[key] = dict(zip(_SMALL, _unpack(arr[0], shapes)))

    n_cw = w["ffn_conv_w"].shape[-1]
    g_cw = lax.dynamic_slice_in_dim(small_out["g"]["ffn_conv_w_full"], me * n_cw, n_cw, axis=2)
    cw_out = _adamw(g_cw[:, None], w["ffn_conv_w"], m["ffn_conv_w"], v["ffn_conv_w"], name="adamw_conv_w")

    n_ada = w["w_ada"].shape[-1]
    flat_all = pk_all.reshape(N_DEV, -1)
    off = sum(int(np.prod(s)) for s in shapes[:-1])
    dmod_all = flat_all[:, off:off + L * 6 * D].reshape(N_DEV, L, 6 * D)
    dmod_cols = lax.dynamic_slice_in_dim(dmod_all, me * n_ada, n_ada, axis=2)
    g_ada = jnp.stack([_mm(c_act, dmod_cols[:, l], "tn", name="mm_ada_dw") for l in range(L)])
    ada_out = _adamw(g_ada[:, None], w["w_ada"], m["w_ada"], v["w_ada"], name="adamw_ada")

    big_out = {}
    for n in _BIG:
        recv = jnp.stack(big_recv[n])
        big_out[n] = _adamw(recv, w[n], m[n], v[n], name="adamw_" + n)
    big_out["w_ada"] = ada_out
    big_out["ffn_conv_w"] = cw_out
    return loss, dh[None], big_out, small_out


_WEIGHTS = ("w_in", "s5_lambda_re", "s5_lambda_im", "s5_log_dt", "s5_b_re", "s5_b_im", "s5_c_re", "s5_c_im", "s5_d",
            "s5_w_glu", "mla_q_norm", "mla_w_uq", "mla_kv_norm", "mla_w_ukv", "hg_lb_logits", "hg_out_norm", "w_out",
            "mix_pre_norm", "mix_post_norm", "ffn_pre_norm", "ffn_post_norm", "ffn_w_up", "ffn_conv_w", "ffn_conv_b",
            "ffn_w_down", "w_ada", "b_ada")
_ALIAS = {"s5_w_glu": "w_glu", "mla_w_uq": "w_uq", "mla_w_ukv": "w_ukv", "ffn_w_up": "w_up", "ffn_w_down": "w_down"}


def kernel(x, c, positions, w_in, s5_lambda_re, s5_lambda_im, s5_log_dt, s5_b_re, s5_b_im, s5_c_re, s5_c_im, s5_d, s5_w_glu, mla_q_norm, mla_w_uq, mla_kv_norm, mla_w_ukv, hg_lb_logits, hg_out_norm, w_out, mix_pre_norm, mix_post_norm, ffn_pre_norm, ffn_post_norm, ffn_w_up, ffn_conv_w, ffn_conv_b, ffn_w_down, w_ada, b_ada, loss_target, m_w_in, m_s5_lambda_re, m_s5_lambda_im, m_s5_log_dt, m_s5_b_re, m_s5_b_im, m_s5_c_re, m_s5_c_im, m_s5_d, m_s5_w_glu, m_mla_q_norm, m_mla_w_uq, m_mla_kv_norm, m_mla_w_ukv, m_hg_lb_logits, m_hg_out_norm, m_w_out, m_mix_pre_norm, m_mix_post_norm, m_ffn_pre_norm, m_ffn_post_norm, m_ffn_w_up, m_ffn_conv_w, m_ffn_conv_b, m_ffn_w_down, m_w_ada, m_b_ada, v_w_in, v_s5_lambda_re, v_s5_lambda_im, v_s5_log_dt, v_s5_b_re, v_s5_b_im, v_s5_c_re, v_s5_c_im, v_s5_d, v_s5_w_glu, v_mla_q_norm, v_mla_w_uq, v_mla_kv_norm, v_mla_w_ukv, v_hg_lb_logits, v_hg_out_norm, v_w_out, v_mix_pre_norm, v_mix_post_norm, v_ffn_pre_norm, v_ffn_post_norm, v_ffn_w_up, v_ffn_conv_w, v_ffn_conv_b, v_ffn_w_down, v_w_ada, v_b_ada):
    args = locals()
    key = lambda n: _ALIAS.get(n, n)
    w = {key(n): args[n] for n in _WEIGHTS}
    m = {key(n): args["m_" + n] for n in _WEIGHTS}
    v = {key(n): args["v_" + n] for n in _WEIGHTS}
    loss, grad_x, big, small = _step(x, c, positions, loss_target, w, m, v)

    def pick(n, idx):
        k = key(n)
        if k in big:
            return big[k][idx].reshape(w[k].shape)
        return small["gdmv"[idx]][k]

    outs = [loss, grad_x]
    for idx in range(4):
        outs += [pick(n, idx) for n in _WEIGHTS]
    return tuple(outs)
```

```python
import functools
import math

import numpy as np
import jax
import jax.numpy as jnp
from jax import lax
from jax.experimental import pallas as pl
from jax.experimental.pallas import tpu as pltpu

F32 = jnp.float32
BF16 = jnp.bfloat16
N_DEV = 8
V7X_VMEM_LIMIT = 56 * 1024 * 1024
MM_VMEM_BUDGET = 28 * 1024 * 1024
LANE = 128

D_MODEL = 2048
S5_W = 512
S5_G = 32
S5_C = 16
S5_P = 64
S5_N = S5_G * S5_P
S5_TL = 512
MLA_H = 8
MLA_NOPE = 128
MLA_ROPE = 64
MLA_V = 128
MLA_HW = 256
HG_H = 4
HG_D = 128
HG_CH = 16
D_FF = 5504
D_FFP = 5632
D_IN = 3392
D_INP = 3456
EPS = 1e-6
MASK_VALUE = -1e30
ROPE_THETA = 10000.0
ATT_SCALE = (MLA_NOPE + MLA_ROPE) ** -0.5

ADAM_LR = 0.001
ADAM_B1 = 0.9
ADAM_B2 = 0.999
ADAM_EPS = 1e-08
ADAM_WD = 0.01
ADAM_STEP = 10

_IN_PERM = np.concatenate([np.arange(0, 1024), np.arange(1344, 3392), np.arange(1024, 1344)])
_IN_INV = np.argsort(_IN_PERM)

_NN = (((1,), (0,)), ((), ()))
_NT = (((1,), (1,)), ((), ()))
_TN = (((0,), (0,)), ((), ()))


def _cp(*sem):
    return pltpu.CompilerParams(dimension_semantics=sem, vmem_limit_bytes=V7X_VMEM_LIMIT)


def _tile(n, cap, align=LANE):
    if n <= cap:
        return n
    t = (cap // align) * align
    while t >= align:
        if n % t == 0:
            return t
        t -= align
    raise ValueError(f"no tile for {n}")


def _bdot(a, b, dims):
    return lax.dot_general(a.astype(BF16), b.astype(BF16), dims, preferred_element_type=F32)


def _mm(a, b, mode, out_dtype=F32, a_col0=0, a_cols=None, name="mm"):
    if mode == "tn":
        K = a.shape[0]
        M = a_cols if a_cols is not None else a.shape[1]
        N = b.shape[1]
    else:
        M = a.shape[0]
        K = a_cols if a_cols is not None else a.shape[1]
        N = b.shape[0] if mode == "nt" else b.shape[1]
    tm = _tile(M, 1024, 8 if M < LANE else LANE)
    tn = _tile(N, 1024)
    kal = 8 if K < LANE else LANE
    tk = _tile(K, 2048, kal)
    osz = jnp.dtype(out_dtype).itemsize

    def vmem(tk_):
        acc = 0 if tk_ == K else tm * tn * 4
        return 2 * tk_ * (tm * a.dtype.itemsize + tn * b.dtype.itemsize) + 2 * tm * tn * osz + acc

    while vmem(tk) > MM_VMEM_BUDGET and tk > kal:
        tk = _tile(K, tk - kal, kal)
    nk = K // tk
    if mode == "tn":
        assert a_col0 % tm == 0
        a_spec = pl.BlockSpec((tk, tm), lambda i, j, k: (k, i + a_col0 // tm))
        b_spec = pl.BlockSpec((tk, tn), lambda i, j, k: (k, j))
        dims = _TN
    else:
        assert a_col0 % tk == 0
        a_spec = pl.BlockSpec((tm, tk), lambda i, j, k: (i, k + a_col0 // tk))
        if mode == "nn":
            b_spec = pl.BlockSpec((tk, tn), lambda i, j, k: (k, j))
            dims = _NN
        else:
            b_spec = pl.BlockSpec((tn, tk), lambda i, j, k: (j, k))
            dims = _NT

    if nk == 1:
        def body(a_ref, b_ref, o_ref):
            o_ref[...] = _bdot(a_ref[...], b_ref[...], dims).astype(o_ref.dtype)

        scratch = []
    else:
        def body(a_ref, b_ref, o_ref, acc_ref):
            k = pl.program_id(2)

            @pl.when(k == 0)
            def _():
                acc_ref[...] = jnp.zeros_like(acc_ref)

            acc_ref[...] += _bdot(a_ref[...], b_ref[...], dims)

            @pl.when(k == nk - 1)
            def _():
                o_ref[...] = acc_ref[...].astype(o_ref.dtype)

        scratch = [pltpu.VMEM((tm, tn), F32)]

    return pl.pallas_call(
        body,
        name=name,
        grid=(M // tm, N // tn, nk),
        in_specs=[a_spec, b_spec],
        out_specs=pl.BlockSpec((tm, tn), lambda i, j, k: (i, j)),
        out_shape=jax.ShapeDtypeStruct((M, N), out_dtype),
        scratch_shapes=scratch,
        compiler_params=_cp("parallel", "parallel", "arbitrary"),
    )(a, b)


def _row_spec(tm, width, cb):
    return pl.BlockSpec((tm, width), lambda i: (i, cb))


def _rowwise(fn, rows, params, outs, name, tm=256):
    S = rows[0][0].shape[0]
    nr, npar = len(rows), len(params)

    def body(*refs):
        xs = [r[...].astype(F32) for r in refs[:nr]]
        ps = [p[...] for p in refs[nr:nr + npar]]
        res = fn(*xs, *ps)
        for o, r in zip(refs[nr + npar:], res):
            o[...] = r.astype(o.dtype)

    res = pl.pallas_call(
        body,
        name=name,
        grid=(S // tm,),
        in_specs=[_row_spec(tm, w, cb) for _, w, cb in rows]
        + [pl.BlockSpec(p.shape, lambda i: (0, 0)) for p in params],
        out_specs=[_row_spec(tm, w, 0) for w, _ in outs],
        out_shape=[jax.ShapeDtypeStruct((S, w), dt) for w, dt in outs],
        compiler_params=_cp("parallel"),
    )(*[r[0] for r in rows], *params)
    return list(res)


def _rowwise_vjp(fn, rows, params, cts, name, row_grads, add_rows=None, tm=256):
    S = rows[0][0].shape[0]
    add_rows = add_rows or {}
    nr, npar = len(rows), len(params)
    flat_cts = [c for group in cts for c in group]
    ncts = len(flat_cts)
    add_keys = sorted(add_rows)
    nadd = len(add_keys)
    grad_idx = [i for i in range(nr) if row_grads[i]]

    def body(*refs):
        i = pl.program_id(0)
        xs = [r[...].astype(F32) for r in refs[:nr]]
        ps = [p[...] for p in refs[nr:nr + npar]]
        ct_refs = refs[nr + npar:nr + npar + ncts]
        add_refs = refs[nr + npar + ncts:nr + npar + ncts + nadd]
        out_refs = refs[nr + npar + ncts + nadd:]
        ct_vals, pos = [], 0
        for group in cts:
            v = ct_refs[pos][...].astype(F32)
            for r in ct_refs[pos + 1:pos + len(group)]:
                v = v + r[...].astype(F32)
            pos += len(group)
            ct_vals.append(v)
        _, vjp = jax.vjp(lambda *a: tuple(fn(*a)), *xs, *ps)
        grads = vjp(tuple(ct_vals))
        for o, gi in zip(out_refs[:len(grad_idx)], grad_idx):
            g = grads[gi]
            if gi in add_rows:
                g = g + add_refs[add_keys.index(gi)][...].astype(F32)
            o[...] = g.astype(o.dtype)
        dprefs = out_refs[len(grad_idx):]

        @pl.when(i == 0)
        def _():
            for dp in dprefs:
                dp[...] = jnp.zeros_like(dp)

        for dp, g in zip(dprefs, grads[nr:]):
            dp[...] += g

    res = pl.pallas_call(
        body,
        name=name,
        grid=(S // tm,),
        in_specs=[_row_spec(tm, w, cb) for _, w, cb in rows]
        + [pl.BlockSpec(p.shape, lambda i: (0, 0)) for p in params]
        + [_row_spec(tm, w, cb) for _, w, cb in flat_cts]
        + [_row_spec(tm, add_rows[k][1], add_rows[k][2]) for k in add_keys],
        out_specs=[_row_spec(tm, rows[gi][1], 0) for gi in grad_idx]
        + [pl.BlockSpec(p.shape, lambda i: (0, 0)) for p in params],
        out_shape=[jax.ShapeDtypeStruct((S, rows[gi][1]), F32) for gi in grad_idx]
        + [jax.ShapeDtypeStruct(p.shape, F32) for p in params],
        compiler_params=_cp("arbitrary"),
    )(*[r[0] for r in rows], *params, *[c[0] for c in flat_cts], *[add_rows[k][0] for k in add_keys])
    res = list(res)
    return res[:len(grad_idx)], res[len(grad_idx):]


def _rms(x, gain):
    return x * lax.rsqrt(jnp.mean(x * x, axis=-1, keepdims=True) + EPS) * gain


def _f_pre(x, gain, sc, sh):
    return (_rms(x, gain) * (1.0 + sc) + sh,)


def _f_post(x, y, gain, g):
    return (x + g * _rms(y, gain),)


def _f_norm(x, gain):
    return (_rms(x, gain),)


def _f_s5a(yc, u, d):
    return (jax.nn.gelu(yc + d * u, approximate=True),)


def _f_s5b(g, z):
    return (g * jax.nn.sigmoid(z),)


def _loss_head(y, target, tm=256):
    S, D = y.shape

    def body(y_ref, t_ref, dy_ref, acc_ref):
        i = pl.program_id(0)
        e = y_ref[...] - t_ref[...]
        dy_ref[...] = e * (1.0 / D)

        @pl.when(i == 0)
        def _():
            acc_ref[...] = jnp.zeros_like(acc_ref)

        acc_ref[...] += jnp.sum(e * e, axis=0, keepdims=True)

    dy, acc = pl.pallas_call(
        body,
        name="loss_head",
        grid=(S // tm,),
        in_specs=[_row_spec(tm, D, 0), _row_spec(tm, D, 0)],
        out_specs=[_row_spec(tm, D, 0), pl.BlockSpec((1, D), lambda i: (0, 0))],
        out_shape=[jax.ShapeDtypeStruct((S, D), F32), jax.ShapeDtypeStruct((1, D), F32)],
        compiler_params=_cp("arbitrary"),
    )(y, target)
    return 0.5 * jnp.sum(acc) / D, dy


def _s5_tile_scan(xr, xi, tab_ref, reverse, row8):
    for k in (1, 2, 4):
        pr = tab_ref[pl.ds(k - 1, 1), 0:S5_TL] if not reverse else tab_ref[pl.ds(8 - k, 1), 0:S5_TL]
        pi = tab_ref[pl.ds(k - 1, 1), S5_TL:2 * S5_TL] if not reverse else tab_ref[pl.ds(8 - k, 1), S5_TL:2 * S5_TL]
        if not reverse:
            keep = row8 >= k
            sr = jnp.where(keep, pltpu.roll(xr, k, 0), 0.0)
            si = jnp.where(keep, pltpu.roll(xi, k, 0), 0.0)
        else:
            keep = row8 < 8 - k
            sr = jnp.where(keep, pltpu.roll(xr, 8 - k, 0), 0.0)
            si = jnp.where(keep, pltpu.roll(xi, 8 - k, 0), 0.0)
        xr, xi = xr + pr * sr - pi * si, xi + pr * si + pi * sr
    return xr, xi


def _s5_scan(bu, tab, reverse=False, h=None, bu_fwd=None, tr=512):
    S = bu.shape[0]
    tr = min(tr, S)
    nl = S5_N // S5_TL
    nrb = S // tr
    w = 2 * S5_TL
    nt = tr // 8
    rmap = (lambda j, i: (i, j)) if not reverse else (lambda j, i: (nrb - 1 - i, j))

    def body(*refs):
        if reverse:
            x_ref, tab_ref, h_ref, b_ref, o_ref, acc_ref, cr_ref, ci_ref = refs
        else:
            x_ref, tab_ref, o_ref, cr_ref, ci_ref = refs
        i = pl.program_id(1)
        row8 = lax.broadcasted_iota(jnp.int32, (8, S5_TL), 0)

        @pl.when(i == 0)
        def _():
            cr_ref[...] = jnp.zeros_like(cr_ref)
            ci_ref[...] = jnp.zeros_like(ci_ref)
            if reverse:
                acc_ref[...] = jnp.zeros_like(acc_ref)

        tr_all = tab_ref[:, 0:S5_TL]
        ti_all = tab_ref[:, S5_TL:w]

        def tile(t, carry):
            tt = (nt - 1 - t) if reverse else t
            r = pl.ds(pl.multiple_of(tt * 8, 8), 8)
            xr, xi = _s5_tile_scan(x_ref[r, 0:S5_TL], x_ref[r, S5_TL:w], tab_ref, reverse, row8)
            cr = jnp.broadcast_to(cr_ref[...], (8, S5_TL))
            ci = jnp.broadcast_to(ci_ref[...], (8, S5_TL))
            hr = xr + tr_all * cr - ti_all * ci
            hi = xi + tr_all * ci + ti_all * cr
            o_ref[r, 0:S5_TL] = hr
            o_ref[r, S5_TL:w] = hi
            edge = pl.ds(tt * 8, 1) if reverse else pl.ds(tt * 8 + 7, 1)
            cr_ref[...] = o_ref[edge, 0:S5_TL]
            ci_ref[...] = o_ref[edge, S5_TL:w]
            if reverse:
                dr = h_ref[r, 0:S5_TL] - b_ref[r, 0:S5_TL]
                di = h_ref[r, S5_TL:w] - b_ref[r, S5_TL:w]
                acc_ref[:, 0:S5_TL] += hr * dr + hi * di
                acc_ref[:, S5_TL:w] += hi * dr - hr * di
            return carry

        lax.fori_loop(0, nt, tile, 0)

    blk = pl.BlockSpec((tr, w), rmap)
    in_specs = [blk, pl.BlockSpec((8, w), lambda j, i: (0, j))]
    out_specs = [blk]
    out_shape = [jax.ShapeDtypeStruct((S, 2 * S5_N), F32)]
    args = [bu, tab]
    if reverse:
        in_specs += [blk, blk]
        args += [h, bu_fwd]
        out_specs.append(pl.BlockSpec((8, w), lambda j, i: (0, j)))
        out_shape.append(jax.ShapeDtypeStruct((8, 2 * S5_N), F32))
    res = pl.pallas_call(
        body,
        name="s5_scan_bwd" if reverse else "s5_scan_fwd",
        grid=(nl, nrb),
        in_specs=in_specs,
        out_specs=out_specs,
        out_shape=out_shape,
        scratch_shapes=[pltpu.VMEM((1, S5_TL), F32), pltpu.VMEM((1, S5_TL), F32)],
        compiler_params=_cp("parallel", "arbitrary"),
    )(*args)
    return res if reverse else res[0]


def _ri_cols(re, im):
    lead = re.shape[:-1]
    nl = S5_N // S5_TL
    z = jnp.stack([re.reshape(*lead, nl, S5_TL), im.reshape(*lead, nl, S5_TL)], axis=-2)
    return z.reshape(*lead, 2 * S5_N)


def _ri_split(z):
    lead = z.shape[:-1]
    nl = S5_N // S5_TL
    z = z.reshape(*lead, nl, 2, S5_TL)
    return z[..., 0, :].reshape(*lead, S5_N), z[..., 1, :].reshape(*lead, S5_N)


def _s5_prep(lre, lim, logdt, bre, bim, cre, cim):
    lam = lax.complex(lre, lim)
    dt = jnp.exp(logdt)[:, None]
    lam_bar = jnp.exp(lam * dt)
    b = lax.complex(bre, bim)
    b_bar = ((lam_bar - 1.0) / lam)[..., None] * b
    eye = jnp.eye(S5_G, dtype=F32)
    bd_re = jnp.einsum("gpc,gh->gchp", jnp.real(b_bar), eye).reshape(S5_W, S5_N)
    bd_im = jnp.einsum("gpc,gh->gchp", jnp.imag(b_bar), eye).reshape(S5_W, S5_N)
    bd = _ri_cols(bd_re, bd_im)
    cd_re = jnp.einsum("gcp,gh->gchp", cre, eye).reshape(S5_W, S5_N)
    cd_im = jnp.einsum("gcp,gh->gchp", -cim, eye).reshape(S5_W, S5_N)
    cdt = _ri_cols(cd_re, cd_im)
    return jnp.real(lam_bar).reshape(1, S5_N), jnp.imag(lam_bar).reshape(1, S5_N), bd, cdt


def _s5_tables(lre, lim, logdt):
    lam = lax.complex(lre, lim)
    dt = jnp.exp(logdt)[:, None]
    k = jnp.arange(1, 9, dtype=F32)[:, None, None]
    pw = jnp.exp((lam * dt)[None] * k).reshape(8, S5_N)
    fwd = _ri_cols(jnp.real(pw), jnp.imag(pw))
    rev = _ri_cols(jnp.real(pw)[::-1], -jnp.imag(pw)[::-1])
    return fwd, rev


def _rope_tables(positions):
    inv_freq = 1.0 / (ROPE_THETA ** (jnp.arange(0, MLA_ROPE, 2, dtype=F32) / MLA_ROPE))
    ang = positions.astype(F32)[:, None] * inv_freq
    cos, sin = jnp.cos(ang), jnp.sin(ang)
    z = jnp.zeros_like(cos)
    cs = jnp.concatenate([cos, cos, z, z], axis=-1)
    sn = jnp.concatenate([-sin, sin, z, z], axis=-1)
    return cs, sn


def _rope_fwd(qraw, kvraw, proj, cs, sn, tm=256):
    S = qraw.shape[0]
    HW = MLA_H * MLA_HW

    def rope(x, c, s):
        lane = lax.broadcasted_iota(jnp.int32, x.shape, 1)
        sw = jnp.where(lane < 32, pltpu.roll(x, 96, 1), jnp.where(lane < 64, pltpu.roll(x, 32, 1), 0.0))
        return x * c + sw * s

    def body(q_ref, kv_ref, kr_ref, cs_ref, sn_ref, qo_ref, ko_ref, vo_ref):
        c, s = cs_ref[...], sn_ref[...]
        kr = rope(kr_ref[...], c, s).astype(BF16)
        for h in range(MLA_H):
            o = h * MLA_HW
            qo_ref[:, o:o + 128] = q_ref[:, o:o + 128].astype(BF16)
            qo_ref[:, o + 128:o + 256] = rope(q_ref[:, o + 128:o + 256], c, s).astype(BF16)
            ko_ref[:, o:o + 128] = kv_ref[:, o:o + 128].astype(BF16)
            ko_ref[:, o + 128:o + 256] = kr
            vo_ref[:, h * 128:(h + 1) * 128] = kv_ref[:, o + 128:o + 256].astype(BF16)

    return pl.pallas_call(
        body,
        name="rope_fwd",
        grid=(S // tm,),
        in_specs=[_row_spec(tm, HW, 0), _row_spec(tm, HW, 0), _row_spec(tm, 128, (D_INP - 128) // 128),
                  _row_spec(tm, 128, 0), _row_spec(tm, 128, 0)],
        out_specs=[_row_spec(tm, HW, 0), _row_spec(tm, HW, 0), _row_spec(tm, MLA_H * MLA_V, 0)],
        out_shape=[jax.ShapeDtypeStruct((S, HW), BF16), jax.ShapeDtypeStruct((S, HW), BF16),
                   jax.ShapeDtypeStruct((S, MLA_H * MLA_V), BF16)],
        compiler_params=_cp("parallel"),
    )(qraw, kvraw, proj, cs, sn)


def _rope_bwd(dq, dk, dv, cs, sn, tm=256):
    S = dq.shape[0]
    HW = MLA_H * MLA_HW

    def rope_t(x, c, s):
        lane = lax.broadcasted_iota(jnp.int32, x.shape, 1)
        w = x * s
        sw = jnp.where(lane < 32, pltpu.roll(w, 96, 1), jnp.where(lane < 64, pltpu.roll(w, 32, 1), 0.0))
        return x * c + sw

    def body(dq_ref, dk_ref, dv_ref, cs_ref, sn_ref, qo_ref, kvo_ref, kro_ref):
        c, s = cs_ref[...], sn_ref[...]
        kr = jnp.zeros((tm, 128), F32)
        for h in range(MLA_H):
            o = h * MLA_HW
            qo_ref[:, o:o + 128] = dq_ref[:, o:o + 128]
            qo_ref[:, o + 128:o + 256] = rope_t(dq_ref[:, o + 128:o + 256], c, s)
            kvo_ref[:, o:o + 128] = dk_ref[:, o:o + 128]
            kvo_ref[:, o + 128:o + 256] = dv_ref[:, h * 128:(h + 1) * 128]
            kr = kr + dk_ref[:, o + 128:o + 256]
        kro_ref[...] = rope_t(kr, c, s)

    return pl.pallas_call(
        body,
        name="rope_bwd",
        grid=(S // tm,),
        in_specs=[_row_spec(tm, HW, 0), _row_spec(tm, HW, 0), _row_spec(tm, MLA_H * MLA_V, 0),
                  _row_spec(tm, 128, 0), _row_spec(tm, 128, 0)],
        out_specs=[_row_spec(tm, HW, 0), _row_spec(tm, HW, 0), _row_spec(tm, 128, 0)],
        out_shape=[jax.ShapeDtypeStruct((S, HW), F32), jax.ShapeDtypeStruct((S, HW), F32),
                   jax.ShapeDtypeStruct((S, 128), F32)],
        compiler_params=_cp("parallel"),
    )(dq, dk, dv, cs, sn)


ATT_T = 512


def _diag_mask(t):
    return lax.broadcasted_iota(jnp.int32, (t, t), 1) <= lax.broadcasted_iota(jnp.int32, (t, t), 0)


def _flash_fwd(q, k, v, exch=None):
    S = q.shape[0]
    t = min(ATT_T, S)
    nq = S // t

    def body(q_ref, k_ref, v_ref, o_ref, lse_ref):
        i = pl.program_id(1)
        qb = q_ref[...]

        def step(j, carry, masked):
            m, l, acc = carry
            r = pl.ds(pl.multiple_of(j * t, t), t)
            s = _bdot(qb, k_ref[r, :], _NT) * ATT_SCALE
            if masked:
                s = jnp.where(_diag_mask(t), s, MASK_VALUE)
            m_new = jnp.maximum(m, jnp.max(s, axis=-1, keepdims=True))
            alpha = jnp.exp(m - m_new)
            p = jnp.exp(s - m_new)
            l = alpha * l + jnp.sum(p, axis=-1, keepdims=True)
            acc = alpha * acc + _bdot(p, v_ref[r, :], _NN)
            return m_new, l, acc

        m0 = jnp.full((t, 1), MASK_VALUE, F32)
        init = (m0, jnp.zeros((t, 1), F32), jnp.zeros((t, MLA_V), F32))
        m, l, acc = step(i, lax.fori_loop(0, i, lambda j, c: step(j, c, False), init), True)
        o_ref[...] = acc / l
        lse_ref[...] = jnp.broadcast_to(m + jnp.log(l), (t, 128))

    return _call_hosting(
        body, "flash_fwd", (MLA_H, nq),
        [pl.BlockSpec((t, MLA_HW), lambda h, i: (i, h)),
         pl.BlockSpec((S, MLA_HW), lambda h, i: (0, h)),
         pl.BlockSpec((S, MLA_V), lambda h, i: (0, h))],
        [pl.BlockSpec((t, MLA_V), lambda h, i: (i, h)), pl.BlockSpec((t, 128), lambda h, i: (i, h))],
        [jax.ShapeDtypeStruct((S, MLA_H * MLA_V), F32), jax.ShapeDtypeStruct((S, MLA_H * 128), F32)],
        [q, k, v], exch)


def _flash_bwd_dq(q, k, v, o, lse, dcat):
    S = q.shape[0]
    t = min(ATT_T, S)
    nq = S // t
    do_cb = S5_W // MLA_V

    def body(q_ref, k_ref, v_ref, o_ref, lse_ref, do_ref, dq_ref):
        i = pl.program_id(1)
        qb = q_ref[...]
        do = do_ref[...]
        delta = jnp.sum(do * o_ref[...], axis=-1, keepdims=True)
        lse1 = jnp.max(lse_ref[...], axis=-1, keepdims=True)
        dob = do.astype(BF16)

        def step(j, dq, masked):
            r = pl.ds(pl.multiple_of(j * t, t), t)
            kb = k_ref[r, :]
            s = _bdot(qb, kb, _NT) * ATT_SCALE
            p = jnp.exp(s - lse1)
            if masked:
                p = jnp.where(_diag_mask(t), p, 0.0)
            dp = _bdot(dob, v_ref[r, :], _NT)
            ds = p * (dp - delta) * ATT_SCALE
            return dq + _bdot(ds, kb, _NN)

        dq = lax.fori_loop(0, i, lambda j, c: step(j, c, False), jnp.zeros((t, MLA_HW), F32))
        dq_ref[...] = step(i, dq, True)

    return pl.pallas_call(
        body,
        name="flash_bwd_dq",
        grid=(MLA_H, nq),
        in_specs=[pl.BlockSpec((t, MLA_HW), lambda h, i: (i, h)),
                  pl.BlockSpec((S, MLA_HW), lambda h, i: (0, h)),
                  pl.BlockSpec((S, MLA_V), lambda h, i: (0, h)),
                  pl.BlockSpec((t, MLA_V), lambda h, i: (i, h)),
                  pl.BlockSpec((t, 128), lambda h, i: (i, h)),
                  pl.BlockSpec((t, MLA_V), lambda h, i: (i, do_cb + h))],
        out_specs=pl.BlockSpec((t, MLA_HW), lambda h, i: (i, h)),
        out_shape=jax.ShapeDtypeStruct((S, MLA_H * MLA_HW), F32),
        compiler_params=_cp("parallel", "parallel"),
    )(q, k, v, o, lse, dcat)


def _flash_bwd_dkv(q, k, v, o, lse, dcat, exch=None):
    S = q.shape[0]
    t = min(ATT_T, S)
    nq = S // t
    do_cb = S5_W // MLA_V

    def body(q_ref, k_ref, v_ref, o_ref, lse_ref, do_ref, dk_ref, dv_ref):
        j = pl.program_id(1)
        kb = k_ref[...]
        vb = v_ref[...]

        def step(i, carry, masked):
            dk, dv = carry
            r = pl.ds(pl.multiple_of(i * t, t), t)
            qb = q_ref[r, :]
            do = do_ref[r, :]
            delta = jnp.sum(do * o_ref[r, :], axis=-1, keepdims=True)
            lse1 = jnp.max(lse_ref[r, :], axis=-1, keepdims=True)
            s = _bdot(qb, kb, _NT) * ATT_SCALE
            p = jnp.exp(s - lse1)
            if masked:
                p = jnp.where(_diag_mask(t), p, 0.0)
            dob = do.astype(BF16)
            dv = dv + _bdot(p, dob, _TN)
            dp = _bdot(dob, vb, _NT)
            ds = p * (dp - delta) * ATT_SCALE
            dk = dk + _bdot(ds, qb, _TN)
            return dk, dv

        first = step(j, (jnp.zeros((t, MLA_HW), F32), jnp.zeros((t, MLA_V), F32)), True)
        dk, dv = lax.fori_loop(j + 1, nq, lambda i, c: step(i, c, False), first)
        dk_ref[...] = dk
        dv_ref[...] = dv

    return _call_hosting(
        body, "flash_bwd_dkv", (MLA_H, nq),
        [pl.BlockSpec((S, MLA_HW), lambda h, j: (0, h)),
         pl.BlockSpec((t, MLA_HW), lambda h, j: (j, h)),
         pl.BlockSpec((t, MLA_V), lambda h, j: (j, h)),
         pl.BlockSpec((S, MLA_V), lambda h, j: (0, h)),
         pl.BlockSpec((S, 128), lambda h, j: (0, h)),
         pl.BlockSpec((S, MLA_V), lambda h, j: (0, do_cb + h))],
        [pl.BlockSpec((t, MLA_HW), lambda h, j: (j, h)), pl.BlockSpec((t, MLA_V), lambda h, j: (j, h))],
        [jax.ShapeDtypeStruct((S, MLA_H * MLA_HW), F32), jax.ShapeDtypeStruct((S, MLA_H * MLA_V), F32)],
        [q, k, v, o, lse, dcat], exch)


def _split3(x):
    x1 = x.astype(BF16)
    r1 = x - x1.astype(F32)
    x2 = r1.astype(BF16)
    x3 = (r1 - x2.astype(F32)).astype(BF16)
    return x1, x2, x3


def _tri_matmul(x, upper):
    n = x.shape[0]
    r = lax.broadcasted_iota(jnp.int32, (n, n), 0)
    c = lax.broadcasted_iota(jnp.int32, (n, n), 1)
    tri = jnp.where((r <= c) if upper else (r >= c), 1.0, 0.0).astype(BF16)
    x1, x2, x3 = _split3(x)
    dot = lambda v: lax.dot_general(tri, v, _NN, preferred_element_type=F32)
    return dot(x1) + dot(x2) + dot(x3)


@jax.custom_vjp
def _cumsum_rows(x):
    return _tri_matmul(x, False)


def _cumsum_rows_fwd(x):
    return _tri_matmul(x, False), None


def _cumsum_rows_bwd(_, ct):
    return (_tri_matmul(ct, True),)


_cumsum_rows.defvjp(_cumsum_rows_fwd, _cumsum_rows_bwd)


def _hg_step(qin, fin, vin, gin, st, lb, on):
    n = qin.shape[0]
    sig = jax.nn.sigmoid(fin)
    g = jnp.log(lb + (1.0 - lb) * sig)
    k = (1.0 - lb) * jax.nn.sigmoid(-fin)
    q = qin * jax.nn.sigmoid(qin)
    b = _cumsum_rows(g)
    o = _bdot(q * jnp.exp(b), st, _NT)
    row = lax.broadcasted_iota(jnp.int32, (n, HG_D), 0)
    row1 = lax.broadcasted_iota(jnp.int32, (n, 1), 0)
    b_s = None
    for s in range(n):
        sel = row == s
        b_s = jnp.sum(jnp.where(sel, b, 0.0), axis=0, keepdims=True)
        k_s = jnp.sum(jnp.where(sel, k, 0.0), axis=0, keepdims=True)
        v_s = jnp.sum(jnp.where(sel, vin, 0.0), axis=0, keepdims=True)
        e = jnp.exp(jnp.minimum(b - b_s, 0.0))
        c = jnp.sum(q * e * k_s, axis=-1, keepdims=True)
        o = o + jnp.where(row1 >= s, c, 0.0) * v_s
    st_new = st * jnp.exp(b_s) + _bdot(vin, k * jnp.exp(b_s - b), _TN)
    y = _rms(o, on) * (gin * jax.nn.sigmoid(gin))
    return y, st_new


def _hg_specs(tb, nb, reverse):
    rm = (lambda i: nb - 1 - i) if reverse else (lambda i: i)
    base = 1024 // HG_D
    return [pl.BlockSpec((tb, HG_D), lambda h, i, o=o: (rm(i), base + o * HG_H + h)) for o in range(4)], rm


def _hg_fwd(proj, lb, on, exch=None, tb=256):
    S = proj.shape[0]
    nb = S // tb
    nc = tb // HG_CH
    in_specs, rm = _hg_specs(tb, nb, False)

    def body(q_ref, f_ref, v_ref, g_ref, lb_ref, on_ref, y_ref, sts_ref, st_ref):
        i = pl.program_id(1)

        @pl.when(i == 0)
        def _():
            st_ref[...] = jnp.zeros_like(st_ref)

        def step(c, carry):
            r = pl.ds(pl.multiple_of(c * HG_CH, HG_CH), HG_CH)
            st = st_ref[...]
            sts_ref[0, c] = st
            y, st_new = _hg_step(q_ref[r, :], f_ref[r, :], v_ref[r, :], g_ref[r, :], st, lb_ref[...], on_ref[...])
            y_ref[r, :] = y
            st_ref[...] = st_new
            return carry

        lax.fori_loop(0, nc, step, 0)

    return _call_hosting(
        body, "hgrn2_fwd", (HG_H, nb),
        in_specs + [pl.BlockSpec((1, HG_D), lambda h, i: (0, h)), pl.BlockSpec((1, HG_D), lambda h, i: (0, 0))],
        [pl.BlockSpec((tb, HG_D), lambda h, i: (i, h)), pl.BlockSpec((1, nc, HG_D, HG_D), lambda h, i: (h, i, 0, 0))],
        [jax.ShapeDtypeStruct((S, HG_H * HG_D), F32), jax.ShapeDtypeStruct((HG_H, S // HG_CH, HG_D, HG_D), F32)],
        [proj, proj, proj, proj, lb, on], exch,
        scratch=[pltpu.VMEM((HG_D, HG_D), F32)], sem=["parallel", "arbitrary"])


def _hg_bwd(proj, sts, lb, on, dcat, exch=None, tb=256):
    S = proj.shape[0]
    nb = S // tb
    nc = tb // HG_CH
    in_specs, rm = _hg_specs(tb, nb, True)
    dy_cb = (S5_W + MLA_H * MLA_V) // HG_D

    def body(q_ref, f_ref, v_ref, g_ref, lb_ref, on_ref, sts_ref, dy_ref,
             dq_ref, df_ref, dv_ref, dg_ref, dlb_ref, don_ref, dst_ref):
        i = pl.program_id(1)

        @pl.when(i == 0)
        def _():
            dst_ref[...] = jnp.zeros_like(dst_ref)
            dlb_ref[...] = jnp.zeros_like(dlb_ref)
            don_ref[...] = jnp.zeros_like(don_ref)

        def step(cc, carry):
            c = nc - 1 - cc
            r = pl.ds(pl.multiple_of(c * HG_CH, HG_CH), HG_CH)
            _, vjp = jax.vjp(_hg_step, q_ref[r, :], f_ref[r, :], v_ref[r, :], g_ref[r, :], sts_ref[0, c],
                             lb_ref[...], on_ref[...])
            dq, df, dv, dg, dst, dlb, don = vjp((dy_ref[r, :], dst_ref[...]))
            dq_ref[r, :] = dq
            df_ref[r, :] = df
            dv_ref[r, :] = dv
            dg_ref[r, :] = dg
            dst_ref[...] = dst
            dlb_ref[...] += dlb
            don_ref[...] += don
            return carry

        lax.fori_loop(0, nc, step, 0)

    blk = pl.BlockSpec((tb, HG_D), lambda h, i: (rm(i), h))
    par = pl.BlockSpec((1, HG_D), lambda h, i: (0, h))
    return _call_hosting(
        body, "hgrn2_bwd", (HG_H, nb),
        in_specs + [pl.BlockSpec((1, HG_D), lambda h, i: (0, h)), pl.BlockSpec((1, HG_D), lambda h, i: (0, 0)),
                    pl.BlockSpec((1, nc, HG_D, HG_D), lambda h, i: (h, rm(i), 0, 0)),
                    pl.BlockSpec((tb, HG_D), lambda h, i: (rm(i), dy_cb + h))],
        [blk, blk, blk, blk, par, par],
        [jax.ShapeDtypeStruct((S, HG_H * HG_D), F32)] * 4 + [jax.ShapeDtypeStruct((1, HG_H * HG_D), F32)] * 2,
        [proj, proj, proj, proj, lb, on, sts, dcat], exch,
        scratch=[pltpu.VMEM((HG_D, HG_D), F32)], sem=["parallel", "arbitrary"])


CONV_TC = 512
CONV_NC = D_FFP // CONV_TC


def _shift_down(cur, halo, k):
    tm = cur.shape[0]
    row = lax.broadcasted_iota(jnp.int32, cur.shape, 0)
    top = jnp.concatenate([pltpu.roll(halo, k, 0), jnp.zeros((tm - 8, cur.shape[1]), F32)], axis=0)
    return jnp.where(row < k, top, pltpu.roll(cur, k, 0))


def _shift_up(cur, halo, k):
    tm = cur.shape[0]
    row = lax.broadcasted_iota(jnp.int32, cur.shape, 0)
    bot = jnp.concatenate([jnp.zeros((tm - 8, cur.shape[1]), F32), pltpu.roll(halo, 8 - k, 0)], axis=0)
    return jnp.where(row >= tm - k, bot, pltpu.roll(cur, tm - k, 0))


def _conv3(cur, halo, w_ref, b_ref):
    return (b_ref[...] + _shift_down(cur, halo, 2) * w_ref[pl.ds(0, 1), :]
            + _shift_down(cur, halo, 1) * w_ref[pl.ds(1, 1), :] + cur * w_ref[pl.ds(2, 1), :])


def _conv_fwd(u0, cw, cb, tm=256):
    S = u0.shape[0]
    nc = CONV_NC
    m8 = tm // 8
    prev = lambda i: jnp.maximum(i * m8 - 1, 0)

    def body(u_ref, p_ref, w_ref, b_ref, a_ref):
        on = (pl.program_id(1) > 0).astype(F32)
        u = _conv3(u_ref[...], p_ref[...] * on, w_ref, b_ref)
        a_ref[...] = (jax.nn.gelu(u[:, :CONV_TC], approximate=True) * u[:, CONV_TC:]).astype(a_ref.dtype)

    tc = CONV_TC
    return pl.pallas_call(
        body,
        name="conv_geglu_fwd",
        grid=(nc, S // tm),
        in_specs=[pl.BlockSpec((tm, 2 * tc), lambda j, i: (i, j)), pl.BlockSpec((8, 2 * tc), lambda j, i: (prev(i), j)),
                  pl.BlockSpec((3, 2 * tc), lambda j, i: (0, j)), pl.BlockSpec((1, 2 * tc), lambda j, i: (0, j))],
        out_specs=pl.BlockSpec((tm, tc), lambda j, i: (i, j)),
        out_shape=jax.ShapeDtypeStruct((S, D_FFP), BF16),
        compiler_params=_cp("parallel", "parallel"),
    )(u0, u0, cw, cb)


def _conv_bwd_elem(da, u0, cw, cb, tm=256):
    S = u0.shape[0]
    nc = CONV_NC
    m8 = tm // 8
    prev = lambda i: jnp.maximum(i * m8 - 1, 0)

    def body(u_ref, p_ref, w_ref, b_ref, da_ref, du_ref, dw_ref, db_ref):
        i = pl.program_id(1)
        cur = u_ref[...]
        halo = p_ref[...] * (i > 0).astype(F32)
        u = _conv3(cur, halo, w_ref, b_ref)
        _, vjp = jax.vjp(lambda a, b: jax.nn.gelu(a, approximate=True) * b, u[:, :CONV_TC], u[:, CONV_TC:])
        dug, duv = vjp(da_ref[...])
        du = jnp.concatenate([dug, duv], axis=1)
        du_ref[...] = du

        @pl.when(i == 0)
        def _():
            dw_ref[...] = jnp.zeros_like(dw_ref)
            db_ref[...] = jnp.zeros_like(db_ref)

        dw_ref[pl.ds(0, 1), :] += jnp.sum(du * _shift_down(cur, halo, 2), axis=0, keepdims=True)
        dw_ref[pl.ds(1, 1), :] += jnp.sum(du * _shift_down(cur, halo, 1), axis=0, keepdims=True)
        dw_ref[pl.ds(2, 1), :] += jnp.sum(du * cur, axis=0, keepdims=True)
        db_ref[...] += jnp.sum(du, axis=0, keepdims=True)

    tc = CONV_TC
    return pl.pallas_call(
        body,
        name="conv_geglu_bwd",
        grid=(nc, S // tm),
        in_specs=[pl.BlockSpec((tm, 2 * tc), lambda j, i: (i, j)), pl.BlockSpec((8, 2 * tc), lambda j, i: (prev(i), j)),
                  pl.BlockSpec((3, 2 * tc), lambda j, i: (0, j)), pl.BlockSpec((1, 2 * tc), lambda j, i: (0, j)),
                  pl.BlockSpec((tm, tc), lambda j, i: (i, j))],
        out_specs=[pl.BlockSpec((tm, 2 * tc), lambda j, i: (i, j)), pl.BlockSpec((3, 2 * tc), lambda j, i: (0, j)),
                   pl.BlockSpec((1, 2 * tc), lambda j, i: (0, j))],
        out_shape=[jax.ShapeDtypeStruct((S, 2 * D_FFP), F32), jax.ShapeDtypeStruct((3, 2 * D_FFP), F32),
                   jax.ShapeDtypeStruct((1, 2 * D_FFP), F32)],
        compiler_params=_cp("parallel", "arbitrary"),
    )(u0, u0, cw, cb, da)


def _conv_bwd_input(du, cw, tm=256):
    S = du.shape[0]
    nrb = S // tm
    m8 = tm // 8
    nxt = lambda i: jnp.minimum((i + 1) * m8, S // 8 - 1)

    def body(d_ref, n_ref, w_ref, o_ref):
        on = (pl.program_id(1) < nrb - 1).astype(F32)
        cur = d_ref[...]
        halo = n_ref[...] * on
        o_ref[...] = (cur * w_ref[pl.ds(2, 1), :] + _shift_up(cur, halo, 1) * w_ref[pl.ds(1, 1), :]
                      + _shift_up(cur, halo, 2) * w_ref[pl.ds(0, 1), :])

    tc = 2 * CONV_TC
    return pl.pallas_call(
        body,
        name="conv_bwd_input",
        grid=(CONV_NC, nrb),
        in_specs=[pl.BlockSpec((tm, tc), lambda j, i: (i, j)), pl.BlockSpec((8, tc), lambda j, i: (nxt(i), j)),
                  pl.BlockSpec((3, tc), lambda j, i: (0, j))],
        out_specs=pl.BlockSpec((tm, tc), lambda j, i: (i, j)),
        out_shape=jax.ShapeDtypeStruct((S, 2 * D_FFP), F32),
        compiler_params=_cp("parallel", "parallel"),
    )(du, du, cw)


def _exchange(arrs, scatter, name):
    n = len(arrs)

    def body(*refs):
        args = (refs[:n], refs[n:2 * n], *refs[2 * n:], scatter)
        _exchange_start(*args)
        _exchange_wait(*args)

    hbm = pl.BlockSpec(memory_space=pltpu.HBM)
    out_shape, sems = _exchange_shapes(arrs, scatter)
    return pl.pallas_call(
        body,
        name=name,
        in_specs=[hbm] * n,
        out_specs=[hbm] * n,
        out_shape=out_shape,
        scratch_shapes=sems,
    )(*arrs)


def _exchange_copies(ins, outs, send, recv, loc, scatter):
    x, y, c = lax.axis_index("x"), lax.axis_index("y"), lax.axis_index("c")
    me = 4 * x + 2 * y + c
    sends, recvs, locs = [], [], []
    for a in range(len(ins)):
        locs.append(pltpu.make_async_copy(ins[a].at[me] if scatter else ins[a], outs[a].at[me], loc.at[a]))
        for k in range(1, N_DEV):
            px = 1 - x if k & 4 else x
            py = 1 - y if k & 2 else y
            pc = 1 - c if k & 1 else c
            peer = 4 * px + 2 * py + pc
            src = ins[a].at[peer] if scatter else ins[a]
            sems = dict(send_sem=send.at[a, k - 1], recv_sem=recv.at[a, k - 1], device_id=(px, py, pc),
                        device_id_type=pl.DeviceIdType.MESH)
            sends.append(pltpu.make_async_remote_copy(src_ref=src, dst_ref=outs[a].at[me], **sems))
            recvs.append(pltpu.make_async_remote_copy(src_ref=src, dst_ref=outs[a].at[peer], **sems))
    return locs, sends, recvs


def _exchange_start(*refs):
    locs, sends, _ = _exchange_copies(*refs)
    for cp in locs + sends:
        cp.start()


def _exchange_wait(*refs):
    locs, sends, recvs = _exchange_copies(*refs)
    for cp in recvs:
        cp.wait_recv()
    for cp in sends:
        cp.wait_send()
    for cp in locs:
        cp.wait()


def _exchange_shapes(arrs, scatter):
    n = len(arrs)
    out_shape = [jax.ShapeDtypeStruct(a.shape if scatter else (N_DEV,) + a.shape, a.dtype) for a in arrs]
    sems = [pltpu.SemaphoreType.DMA((n, N_DEV - 1)), pltpu.SemaphoreType.DMA((n, N_DEV - 1)),
            pltpu.SemaphoreType.DMA((n,))]
    return out_shape, sems


def _call_hosting(body, name, grid, in_specs, out_specs, out_shape, args, exch, scratch=(), sem=None):
    scratch = list(scratch)
    if exch is None:
        res = pl.pallas_call(body, name=name, grid=grid, in_specs=in_specs, out_specs=out_specs, out_shape=out_shape,
                             scratch_shapes=scratch,
                             compiler_params=_cp(*(sem or ["parallel"] * len(grid))))(*args)
        return list(res), []
    arrs, scatter = exch
    n, n_in, n_out, n_scr = len(arrs), len(in_specs), len(out_specs), len(scratch)
    hbm = pl.BlockSpec(memory_space=pltpu.HBM)
    x_shape, sems = _exchange_shapes(arrs, scatter)

    def hosting_body(*refs):
        cin, xin = refs[:n_in], refs[n_in:n_in + n]
        cout, xout = refs[n_in + n:n_in + n + n_out], refs[n_in + n + n_out:n_in + 2 * n + n_out]
        cscr = refs[n_in + 2 * n + n_out:n_in + 2 * n + n_out + n_scr]
        xsem = refs[n_in + 2 * n + n_out + n_scr:]
        ids = [pl.program_id(d) for d in range(len(grid))]
        first = functools.reduce(jnp.logical_and, [i == 0 for i in ids])
        last = functools.reduce(jnp.logical_and, [i == g - 1 for i, g in zip(ids, grid)])

        @pl.when(first)
        def _():
            _exchange_start(xin, xout, *xsem, scatter)

        body(*cin, *cout, *cscr)

        @pl.when(last)
        def _():
            _exchange_wait(xin, xout, *xsem, scatter)

    res = pl.pallas_call(
        hosting_body, name=name + "_x", grid=grid, in_specs=in_specs + [hbm] * n, out_specs=out_specs + [hbm] * n,
        out_shape=out_shape + x_shape, scratch_shapes=scratch + sems,
        compiler_params=_cp(*["arbitrary"] * len(grid)))(*args, *arrs)
    return list(res[:n_out]), list(res[n_out:])


def _adamw(recv, w, m, v, name="adamw"):
    L, n, R, C = recv.shape
    fits = [t for t in range(8, R + 1, 8) if R % t == 0 and t * C * 4 <= (1 << 19)]
    tr = max(fits) if fits else R

    def body(r_ref, w_ref, m_ref, v_ref, g_ref, d_ref, mo_ref, vo_ref):
        g = r_ref[0, 0].astype(F32)
        for d in range(1, n):
            g = g + r_ref[0, d].astype(F32)
        mm = ADAM_B1 * m_ref[0] + (1.0 - ADAM_B1) * g
        vv = ADAM_B2 * v_ref[0] + (1.0 - ADAM_B2) * (g * g)
        m_hat = mm / (1.0 - ADAM_B1 ** ADAM_STEP)
        v_hat = vv / (1.0 - ADAM_B2 ** ADAM_STEP)
        g_ref[0] = g
        d_ref[0] = -ADAM_LR * (m_hat / (jnp.sqrt(v_hat) + ADAM_EPS) + ADAM_WD * w_ref[0])
        mo_ref[0] = mm
        vo_ref[0] = vv

    blk = pl.BlockSpec((1, tr, C), lambda l, i: (l, i, 0))
    return pl.pallas_call(
        body,
        name=name,
        grid=(L, R // tr),
        in_specs=[pl.BlockSpec((1, n, tr, C), lambda l, i: (l, 0, i, 0)), blk, blk, blk],
        out_specs=[blk] * 4,
        out_shape=[jax.ShapeDtypeStruct((L, R, C), F32)] * 4,
        compiler_params=_cp("parallel", "parallel"),
    )(recv, w, m, v)


def _layer_fwd(x, mod, W, P, exch=None):
    sh1, sc1, g1, sh2, sc2, g2 = mod
    D = D_MODEL
    R = {"x": x}
    (h1,) = _rowwise(_f_pre, [(x, D, 0)], [P["n1"], sc1, sh1], [(D, BF16)], "pre_norm")
    proj = _mm(h1, W["w_in"], "nn", name="mm_in")
    R["h1"], R["proj"] = h1, proj
    bu = _mm(proj, W["bd"], "nn", a_col0=0, a_cols=S5_W, name="mm_s5_b")
    hs = _s5_scan(bu, P["tab_fwd"])
    yc = _mm(hs, W["cdt"], "nt", name="mm_s5_c")
    (gg,) = _rowwise(_f_s5a, [(yc, S5_W, 0), (proj, S5_W, 0)], [P["s5_d"]], [(S5_W, F32)], "s5_gelu")
    z = _mm(gg, W["w_glu"], "nn", name="mm_glu")
    (ys5,) = _rowwise(_f_s5b, [(gg, S5_W, 0), (z, S5_W, 0)], [], [(S5_W, BF16)], "s5_glu")
    R.update(bu=bu, hs=hs, yc=yc, gg=gg, z=z)
    (qn,) = _rowwise(_f_norm, [(proj, 512, 1)], [P["q_norm"]], [(512, BF16)], "q_norm")
    (kvn,) = _rowwise(_f_norm, [(proj, 256, 12)], [P["kv_norm"]], [(256, BF16)], "kv_norm")
    qraw = _mm(qn, W["w_uq"], "nn", name="mm_uq")
    kvraw = _mm(kvn, W["w_ukv"], "nn", name="mm_ukv")
    q, k, v = _rope_fwd(qraw, kvraw, proj, P["cs"], P["sn"])
    exch = exch or (None, None)
    (o, lse), got_a = _flash_fwd(q, k, v, exch[0])
    R.update(qn=qn, kvn=kvn, q=q, k=k, v=v, o=o, lse=lse)
    (yhg, sts), got_b = _hg_fwd(proj, P["lb"], P["hg_on"], exch[1])
    R["sts"] = sts
    cat = jnp.concatenate([ys5, o.astype(BF16), yhg.astype(BF16)], axis=-1)
    mixed = _mm(cat, W["w_out"], "nn", name="mm_out")
    (x2,) = _rowwise(_f_post, [(x, D, 0), (mixed, D, 0)], [P["n2"], g1], [(D, F32)], "post_norm")
    R.update(cat=cat, mixed=mixed, x2=x2)
    (h2,) = _rowwise(_f_pre, [(x2, D, 0)], [P["n3"], sc2, sh2], [(D, BF16)], "pre_norm")
    u0 = _mm(h2, W["w_up"], "nn", name="mm_up")
    a = _conv_fwd(u0, P["conv_w"], P["conv_b"])
    y = _mm(a, W["w_down"], "nn", name="mm_down")
    (x3,) = _rowwise(_f_post, [(x2, D, 0), (y, D, 0)], [P["n4"], g2], [(D, F32)], "post_norm")
    R.update(h2=h2, u0=u0, a=a, y=y)
    return x3, R, (got_a, got_b)


def _layer_bwd(dx3, mod, W, P, R, exch=None):
    sh1, sc1, g1, sh2, sc2, g2 = mod
    D = D_MODEL
    G = {}
    (dx2a, dy), (dn4, dg2) = _rowwise_vjp(_f_post, [(R["x2"], D, 0), (R["y"], D, 0)], [P["n4"], g2],
                                          [[(dx3, D, 0)]], "post_norm_bwd", [True, True])
    da = _mm(dy, W["w_down"], "nt", name="mm_down_dx")
    G["w_down"] = _mm(R["a"], dy, "tn", out_dtype=BF16, name="mm_down_dw")
    du, dcw, dcb = _conv_bwd_elem(da, R["u0"], P["conv_w"], P["conv_b"])
    du0 = _conv_bwd_input(du, P["conv_w"])
    dh2 = _mm(du0, W["w_up"], "nt", name="mm_up_dx")
    G["w_up"] = _mm(R["h2"], du0, "tn", out_dtype=BF16, name="mm_up_dw")
    (dx2,), (dn3, dsc2, dsh2) = _rowwise_vjp(_f_pre, [(R["x2"], D, 0)], [P["n3"], sc2, sh2], [[(dh2, D, 0)]],
                                             "pre_norm_bwd", [True], add_rows={0: (dx2a, D, 0)})
    (dxa, dmixed), (dn2, dg1) = _rowwise_vjp(_f_post, [(R["x"], D, 0), (R["mixed"], D, 0)], [P["n2"], g1],
                                             [[(dx2, D, 0)]], "post_norm_bwd", [True, True])
    dcat = _mm(dmixed, W["w_out"], "nt", name="mm_out_dx")
    G["w_out"] = _mm(R["cat"], dmixed, "tn", out_dtype=BF16, name="mm_out_dw")
    (dga, dz), _ = _rowwise_vjp(_f_s5b, [(R["gg"], S5_W, 0), (R["z"], S5_W, 0)], [], [[(dcat, S5_W, 0)]],
                                "s5_glu_bwd", [True, True])
    dgb = _mm(dz, W["w_glu"], "nt", name="mm_glu_dx")
    G["w_glu"] = _mm(R["gg"], dz, "tn", out_dtype=BF16, name="mm_glu_dw")
    (dyc, dua), (dd,) = _rowwise_vjp(_f_s5a, [(R["yc"], S5_W, 0), (R["proj"], S5_W, 0)], [P["s5_d"]],
                                     [[(dga, S5_W, 0), (dgb, S5_W, 0)]], "s5_gelu_bwd", [True, True])
    dhs = _mm(dyc, W["cdt"], "nn", name="mm_s5_c_dx")
    dcdt = _mm(dyc, R["hs"], "tn", name="mm_s5_c_dw")
    gs, acc = _s5_scan(dhs, P["tab_rev"], reverse=True, h=R["hs"], bu_fwd=R["bu"])
    dub = _mm(gs, W["bd"], "nt", name="mm_s5_b_dx")
    dbd = _mm(R["proj"], gs, "tn", a_col0=0, a_cols=S5_W, name="mm_s5_b_dw")
    dq = _flash_bwd_dq(R["q"], R["k"], R["v"], R["o"], R["lse"], dcat)
    exch = exch or (None, None)
    (dk, dv), got_a = _flash_bwd_dkv(R["q"], R["k"], R["v"], R["o"], R["lse"], dcat, exch[0])
    dqraw, dkvraw, dkr = _rope_bwd(dq, dk, dv, P["cs"], P["sn"])
    dqn = _mm(dqraw, W["w_uq"], "nt", name="mm_uq_dx")
    G["w_uq"] = _mm(R["qn"], dqraw, "tn", out_dtype=BF16, name="mm_uq_dw")
    dkvn = _mm(dkvraw, W["w_ukv"], "nt", name="mm_ukv_dx")
    G["w_ukv"] = _mm(R["kvn"], dkvraw, "tn", out_dtype=BF16, name="mm_ukv_dw")
    (dcq,), (dqnorm,) = _rowwise_vjp(_f_norm, [(R["proj"], 512, 1)], [P["q_norm"]], [[(dqn, 512, 0)]],
                                     "q_norm_bwd", [True])
    (dckv,), (dkvnorm,) = _rowwise_vjp(_f_norm, [(R["proj"], 256, 12)], [P["kv_norm"]], [[(dkvn, 256, 0)]],
                                       "kv_norm_bwd", [True])
    (dhq, dhf, dhi, dhg, dlb, don), got_b = _hg_bwd(R["proj"], R["sts"], P["lb"], P["hg_on"], dcat, exch[1])
    dproj = jnp.concatenate([dua + dub, dcq, dhq, dhf, dhi, dhg, dckv, dkr], axis=-1)
    dh1 = _mm(dproj, W["w_in"], "nt", name="mm_in_dx")
    G["w_in"] = _mm(R["h1"], dproj, "tn", out_dtype=BF16, name="mm_in_dw")
    (dx,), (dn1, dsc1, dsh1) = _rowwise_vjp(_f_pre, [(R["x"], D, 0)], [P["n1"], sc1, sh1], [[(dh1, D, 0)]],
                                            "pre_norm_bwd", [True], add_rows={0: (dxa, D, 0)})
    dmod = jnp.concatenate([dsh1, dsc1, dg1, dsh2, dsc2, dg2], axis=-1)
    small = dict(n1=dn1, n2=dn2, n3=dn3, n4=dn4, s5_d=dd, q_norm=dqnorm, kv_norm=dkvnorm,
                 lb=dlb, hg_on=jnp.sum(don.reshape(HG_H, HG_D), axis=0, keepdims=True),
                 conv_w=dcw, conv_b=dcb, bd=dbd, cdt=dcdt, acc=jnp.sum(acc, axis=0, keepdims=True))
    return dx, dmod, G, small, (got_a, got_b)


def _cols_from_shards(g):
    return jnp.transpose(g, (1, 0, 2)).reshape(g.shape[1], -1)


def _cols_to_shards(w):
    K = w.shape[0]
    return jnp.transpose(w.reshape(K, N_DEV, -1), (1, 0, 2))


def _pad_ff(w):
    lead = w.shape[:-1]
    pad = [(0, 0)] * (w.ndim - 1) + [(0, D_FFP - D_FF)]
    halves = jnp.stack([jnp.pad(w[..., :D_FF], pad), jnp.pad(w[..., D_FF:], pad)], axis=-2)
    halves = halves.reshape(*lead, 2, CONV_NC, CONV_TC)
    return jnp.swapaxes(halves, -3, -2).reshape(*lead, 2 * D_FFP)


def _unpad_ff(w):
    lead = w.shape[:-1]
    halves = jnp.swapaxes(w.reshape(*lead, CONV_NC, 2, CONV_TC), -3, -2).reshape(*lead, 2, D_FFP)
    return jnp.concatenate([halves[..., 0, :D_FF], halves[..., 1, :D_FF]], axis=-1)


def _assemble(gathered):
    w_in = _cols_from_shards(gathered["w_in"])
    w_in = jnp.concatenate([w_in[:, 0:1024], w_in[:, 1344:D_IN], w_in[:, 1024:1344],
                            jnp.zeros((w_in.shape[0], D_INP - D_IN), w_in.dtype)], axis=1)
    w_uq = _cols_from_shards(gathered["w_uq"]).reshape(-1, MLA_H, MLA_NOPE + MLA_ROPE)
    w_uq = jnp.pad(w_uq, ((0, 0), (0, 0), (0, MLA_HW - MLA_NOPE - MLA_ROPE))).reshape(-1, MLA_H * MLA_HW)
    w_down = gathered["w_down"].reshape(D_FF, D_MODEL)
    return dict(
        w_in=w_in,
        w_glu=gathered["w_glu"].reshape(S5_W, S5_W),
        w_uq=w_uq,
        w_ukv=_cols_from_shards(gathered["w_ukv"]),
        w_out=gathered["w_out"].reshape(D_MODEL, D_MODEL),
        w_up=_pad_ff(_cols_from_shards(gathered["w_up"])),
        w_down=jnp.pad(w_down, ((0, D_FFP - D_FF), (0, 0))),
    )


def _grad_shards(G):
    w_in = jnp.concatenate([G["w_in"][:, 0:1024], G["w_in"][:, 3072:D_IN], G["w_in"][:, 1024:3072]], axis=1)
    w_uq = G["w_uq"].reshape(-1, MLA_H, MLA_HW)[:, :, :MLA_NOPE + MLA_ROPE].reshape(-1, MLA_H * (MLA_NOPE + MLA_ROPE))
    return dict(
        w_in=_cols_to_shards(w_in),
        w_glu=G["w_glu"].reshape(N_DEV, -1, S5_W),
        w_uq=_cols_to_shards(w_uq),
        w_ukv=_cols_to_shards(G["w_ukv"]),
        w_out=G["w_out"].reshape(N_DEV, -1, D_MODEL),
        w_up=_cols_to_shards(_unpad_ff(G["w_up"])),
        w_down=G["w_down"][:D_FF].reshape(N_DEV, -1, D_MODEL),
    )


_BIG = ("w_in", "w_glu", "w_uq", "w_ukv", "w_out", "w_up", "w_down")
_HOST_A = ("w_in", "w_glu", "w_uq", "w_ukv", "w_out", "ffn_conv_w")
_HOST_B = ("w_up", "w_down")
_GRAD_A = ("w_up",)
_GRAD_B = ("w_in", "w_glu", "w_uq", "w_ukv", "w_out", "w_down")
_SMALL = ("s5_lambda_re", "s5_lambda_im", "s5_log_dt", "s5_b_re", "s5_b_im", "s5_c_re", "s5_c_im", "s5_d",
          "mla_q_norm", "mla_kv_norm", "hg_lb_logits", "hg_out_norm", "mix_pre_norm", "mix_post_norm",
          "ffn_pre_norm", "ffn_post_norm", "ffn_conv_w_full", "ffn_conv_b", "b_ada")
PACK_ROW = 1024


def _pack(parts):
    flat = jnp.concatenate([p.reshape(-1) for p in parts])
    n = flat.shape[0]
    pad = (-n) % (8 * PACK_ROW)
    return jnp.pad(flat, (0, pad)).reshape(-1, PACK_ROW)


def _unpack(packed, shapes):
    flat = packed.reshape(-1)
    out, pos = [], 0
    for s in shapes:
        n = int(np.prod(s))
        out.append(flat[pos:pos + n].reshape(s))
        pos += n
    return out


def _step(x, c, positions, loss_target, w, m, v):
    x = x[0]
    S = x.shape[0]
    L = w["w_in"].shape[0]
    D = D_MODEL
    me = 4 * lax.axis_index("x") + 2 * lax.axis_index("y") + lax.axis_index("c")

    (c_all,) = _exchange([c], False, "gather_c")
    c_all = c_all.reshape(N_DEV, D)
    (c_act,) = _rowwise(lambda a: (a * jax.nn.sigmoid(a),), [(c_all, D, 0)], [], [(D, F32)], "silu_c", tm=N_DEV)
    mod_part = jnp.stack([_mm(c_act, w["w_ada"][l], "nn", name="mm_ada") for l in range(L)])
    (mod_all,) = _exchange([mod_part], False, "gather_mod")
    mod_mine = lax.dynamic_index_in_dim(mod_all, me, axis=2, keepdims=False)
    mod_full = jnp.transpose(mod_mine, (1, 0, 2)).reshape(L, 6 * D) + w["b_ada"]
    mods = [[mod_full[l:l + 1, i * D:(i + 1) * D] for i in range(6)] for l in range(L)]

    cs, sn = _rope_tables(positions[0])
    lower, lower_vjp = jax.vjp(lambda lg: jnp.cumsum(jax.nn.softmax(lg, axis=0), axis=0)
                               - jax.nn.softmax(lg, axis=0)[0:1], w["hg_lb_logits"])
    conv_w_full = []

    def weight_shards(l, names):
        return [w[n][l] if n == "ffn_conv_w" else w[n][l].astype(BF16) for n in names]

    def layer_params(l, gathered):
        Wl = _assemble(gathered)
        cw_full = _cols_from_shards(gathered["ffn_conv_w"])
        s5_args = (w["s5_lambda_re"][l], w["s5_lambda_im"][l], w["s5_log_dt"][l], w["s5_b_re"][l], w["s5_b_im"][l],
                   w["s5_c_re"][l], w["s5_c_im"][l])
        (lbr, lbi, bd, cdt), prep_vjp = jax.vjp(_s5_prep, *s5_args)
        tab_fwd, tab_rev = _s5_tables(*s5_args[:3])
        Wl["bd"], Wl["cdt"] = bd.astype(BF16), cdt.astype(BF16)
        row = lambda a: a.reshape(1, -1)
        Pl = dict(
            n1=row(w["mix_pre_norm"][l]), n2=row(w["mix_post_norm"][l]), n3=row(w["ffn_pre_norm"][l]),
            n4=row(w["ffn_post_norm"][l]), s5_d=row(w["s5_d"][l]), q_norm=row(w["mla_q_norm"][l]),
            kv_norm=row(w["mla_kv_norm"][l]), lb=row(lower[l]), hg_on=row(w["hg_out_norm"][l]),
            conv_w=_pad_ff(cw_full), conv_b=_pad_ff(row(w["ffn_conv_b"][l])),
            tab_fwd=tab_fwd, tab_rev=tab_rev, cs=cs, sn=sn, lam_bar=(lbr, lbi))
        return Wl, Pl, prep_vjp, cw_full

    Ws, Ps, preps, Rs = [], [], [], []
    h = x
    gathered = dict(zip(_HOST_A + _HOST_B, _exchange(weight_shards(0, _HOST_A + _HOST_B), False, "gather_weights")))
    for l in range(L):
        Wl, Pl, prep_vjp, cw_full = layer_params(l, gathered)
        Ws.append(Wl)
        Ps.append(Pl)
        preps.append(prep_vjp)
        conv_w_full.append(cw_full)
        nxt = ((weight_shards(l + 1, _HOST_A), False), (weight_shards(l + 1, _HOST_B), False)) if l + 1 < L else None
        h, R, (got_a, got_b) = _layer_fwd(h, mods[l], Wl, Pl, nxt)
        gathered = dict(zip(_HOST_A + _HOST_B, got_a + got_b))
        Rs.append(R)
    loss_local, dh = _loss_head(h, loss_target[0])
    loss = lax.psum(loss_local, ("x", "y", "c"))

    big_recv = {n: [None] * L for n in _BIG}
    small_g = {n: [None] * L for n in _SMALL if n != "hg_lb_logits"}
    dlower = [None] * L
    pending = None
    for l in reversed(range(L)):
        dh, dmod, G, sm, (got_a, got_b) = _layer_bwd(dh, mods[l], Ws[l], Ps[l], Rs[l], pending)
        Rs[l] = None
        for n, r in zip(_GRAD_A + _GRAD_B, got_a + got_b):
            big_recv[n][l + 1] = r
        shards = _grad_shards(G)
        pending = (([shards[n] for n in _GRAD_A], True), ([shards[n] for n in _GRAD_B], True))
        lbr, lbi = Ps[l]["lam_bar"]
        ar, ai = _ri_split(sm["acc"])
        dl = lax.complex(ar, ai) / lax.complex(lbr, -lbi)
        d_s5 = preps[l]((jnp.real(dl), jnp.imag(dl), sm["bd"], sm["cdt"]))
        for n, g in zip(("s5_lambda_re", "s5_lambda_im", "s5_log_dt", "s5_b_re", "s5_b_im", "s5_c_re", "s5_c_im"), d_s5):
            small_g[n][l] = g
        small_g["s5_d"][l] = sm["s5_d"][0]
        small_g["mla_q_norm"][l] = sm["q_norm"][0]
        small_g["mla_kv_norm"][l] = sm["kv_norm"][0]
        small_g["hg_out_norm"][l] = sm["hg_on"][0]
        small_g["mix_pre_norm"][l] = sm["n1"][0]
        small_g["mix_post_norm"][l] = sm["n2"][0]
        small_g["ffn_pre_norm"][l] = sm["n3"][0]
        small_g["ffn_post_norm"][l] = sm["n4"][0]
        small_g["ffn_conv_w_full"][l] = _unpad_ff(sm["conv_w"])
        small_g["ffn_conv_b"][l] = _unpad_ff(sm["conv_b"])[0]
        small_g["b_ada"][l] = dmod[0]
        dlower[l] = sm["lb"][0]
    for n, r in zip(_GRAD_A + _GRAD_B, _exchange(pending[0][0] + pending[1][0], True, "scatter_grads")):
        big_recv[n][0] = r
    small_g = {n: jnp.stack(gl) for n, gl in small_g.items()}
    (small_g["hg_lb_logits"],) = lower_vjp(jnp.stack(dlower))

    small_w = {n: w[n] for n in _SMALL if n != "ffn_conv_w_full"}
    small_w["ffn_conv_w_full"] = jnp.stack(conv_w_full)
    shapes = [small_w[n].shape for n in _SMALL]
    zeros_cw = jnp.zeros_like(small_w["ffn_conv_w_full"])
    pk_g = _pack([small_g[n] for n in _SMALL])
    pk_w = _pack([small_w[n] for n in _SMALL])
    pk_m = _pack([zeros_cw if n == "ffn_conv_w_full" else m[n] for n in _SMALL])
    pk_v = _pack([zeros_cw + 1.0 if n == "ffn_conv_w_full" else v[n] for n in _SMALL])
    (pk_all,) = _exchange([pk_g], False, "gather_small")
    sg, sd, sm_, sv = _adamw(pk_all[None], pk_w[None], pk_m[None], pk_v[None], name="adamw_small")
    small_out = {}
    for key, arr in (("g", sg), ("d", sd), ("m", sm_), ("v", sv)):
        small_out[key] = dict(zip(_SMALL, _unpack(arr[0], shapes)))

    n_cw = w["ffn_conv_w"].shape[-1]
    g_cw = lax.dynamic_slice_in_dim(small_out["g"]["ffn_conv_w_full"], me * n_cw, n_cw, axis=2)
    cw_out = _adamw(g_cw[:, None], w["ffn_conv_w"], m["ffn_conv_w"], v["ffn_conv_w"], name="adamw_conv_w")

    n_ada = w["w_ada"].shape[-1]
    flat_all = pk_all.reshape(N_DEV, -1)
    off = sum(int(np.prod(s)) for s in shapes[:-1])
    dmod_all = flat_all[:, off:off + L * 6 * D].reshape(N_DEV, L, 6 * D)
    dmod_cols = lax.dynamic_slice_in_dim(dmod_all, me * n_ada, n_ada, axis=2)
    g_ada = jnp.stack([_mm(c_act, dmod_cols[:, l], "tn", name="mm_ada_dw") for l in range(L)])
    ada_out = _adamw(g_ada[:, None], w["w_ada"], m["w_ada"], v["w_ada"], name="adamw_ada")

    big_out = {}
    for n in _BIG:
        recv = jnp.stack(big_recv[n])
        big_out[n] = _adamw(recv, w[n], m[n], v[n], name="adamw_" + n)
    big_out["w_ada"] = ada_out
    big_out["ffn_conv_w"] = cw_out
    return loss, dh[None], big_out, small_out


_WEIGHTS = ("w_in", "s5_lambda_re", "s5_lambda_im", "s5_log_dt", "s5_b_re", "s5_b_im", "s5_c_re", "s5_c_im", "s5_d",
            "s5_w_glu", "mla_q_norm", "mla_w_uq", "mla_kv_norm", "mla_w_ukv", "hg_lb_logits", "hg_out_norm", "w_out",
            "mix_pre_norm", "mix_post_norm", "ffn_pre_norm", "ffn_post_norm", "ffn_w_up", "ffn_conv_w", "ffn_conv_b",
            "ffn_w_down", "w_ada", "b_ada")
_ALIAS = {"s5_w_glu": "w_glu", "mla_w_uq": "w_uq", "mla_w_ukv": "w_ukv", "ffn_w_up": "w_up", "ffn_w_down": "w_down"}


def kernel(x, c, positions, w_in, s5_lambda_re, s5_lambda_im, s5_log_dt, s5_b_re, s5_b_im, s5_c_re, s5_c_im, s5_d, s5_w_glu, mla_q_norm, mla_w_uq, mla_kv_norm, mla_w_ukv, hg_lb_logits, hg_out_norm, w_out, mix_pre_norm, mix_post_norm, ffn_pre_norm, ffn_post_norm, ffn_w_up, ffn_conv_w, ffn_conv_b, ffn_w_down, w_ada, b_ada, loss_target, m_w_in, m_s5_lambda_re, m_s5_lambda_im, m_s5_log_dt, m_s5_b_re, m_s5_b_im, m_s5_c_re, m_s5_c_im, m_s5_d, m_s5_w_glu, m_mla_q_norm, m_mla_w_uq, m_mla_kv_norm, m_mla_w_ukv, m_hg_lb_logits, m_hg_out_norm, m_w_out, m_mix_pre_norm, m_mix_post_norm, m_ffn_pre_norm, m_ffn_post_norm, m_ffn_w_up, m_ffn_conv_w, m_ffn_conv_b, m_ffn_w_down, m_w_ada, m_b_ada, v_w_in, v_s5_lambda_re, v_s5_lambda_im, v_s5_log_dt, v_s5_b_re, v_s5_b_im, v_s5_c_re, v_s5_c_im, v_s5_d, v_s5_w_glu, v_mla_q_norm, v_mla_w_uq, v_mla_kv_norm, v_mla_w_ukv, v_hg_lb_logits, v_hg_out_norm, v_w_out, v_mix_pre_norm, v_mix_post_norm, v_ffn_pre_norm, v_ffn_post_norm, v_ffn_w_up, v_ffn_conv_w, v_ffn_conv_b, v_ffn_w_down, v_w_ada, v_b_ada):
    args = locals()
    key = lambda n: _ALIAS.get(n, n)
    w = {key(n): args[n] for n in _WEIGHTS}
    m = {key(n): args["m_" + n] for n in _WEIGHTS}
    v = {key(n): args["v_" + n] for n in _WEIGHTS}
    loss, grad_x, big, small = _step(x, c, positions, loss_target, w, m, v)

    def pick(n, idx):
        k = key(n)
        if k in big:
            return big[k][idx].reshape(w[k].shape)
        return small["gdmv"[idx]][k]

    outs = [loss, grad_x]
    for idx in range(4):
        outs += [pick(n, idx) for n in _WEIGHTS]
    return tuple(outs)
```

```python
import functools
import math

import numpy as np
import jax
import jax.numpy as jnp
from jax import lax
from jax.experimental import pallas as pl
from jax.experimental.pallas import tpu as pltpu

F32 = jnp.float32
BF16 = jnp.bfloat16
N_DEV = 8
V7X_VMEM_LIMIT = 56 * 1024 * 1024
MM_VMEM_BUDGET = 28 * 1024 * 1024
LANE = 128

D_MODEL = 2048
S5_W = 512
S5_G = 32
S5_C = 16
S5_P = 64
S5_N = S5_G * S5_P
S5_TL = 512
MLA_H = 8
MLA_NOPE = 128
MLA_ROPE = 64
MLA_V = 128
MLA_HW = 256
HG_H = 4
HG_D = 128
HG_CH = 16
D_FF = 5504
D_FFP = 5632
D_IN = 3392
D_INP = 3456
EPS = 1e-6
MASK_VALUE = -1e30
ROPE_THETA = 10000.0
ATT_SCALE = (MLA_NOPE + MLA_ROPE) ** -0.5

ADAM_LR = 0.001
ADAM_B1 = 0.9
ADAM_B2 = 0.999
ADAM_EPS = 1e-08
ADAM_WD = 0.01
ADAM_STEP = 10

_IN_PERM = np.concatenate([np.arange(0, 1024), np.arange(1344, 3392), np.arange(1024, 1344)])
_IN_INV = np.argsort(_IN_PERM)

_NN = (((1,), (0,)), ((), ()))
_NT = (((1,), (1,)), ((), ()))
_TN = (((0,), (0,)), ((), ()))


def _cp(*sem):
    return pltpu.CompilerParams(dimension_semantics=sem, vmem_limit_bytes=V7X_VMEM_LIMIT)


def _tile(n, cap, align=LANE):
    if n <= cap:
        return n
    t = (cap // align) * align
    while t >= align:
        if n % t == 0:
            return t
        t -= align
    raise ValueError(f"no tile for {n}")


def _bdot(a, b, dims):
    return lax.dot_general(a.astype(BF16), b.astype(BF16), dims, preferred_element_type=F32)


def _mm(a, b, mode, out_dtype=F32, a_col0=0, a_cols=None, name="mm"):
    if mode == "tn":
        K = a.shape[0]
        M = a_cols if a_cols is not None else a.shape[1]
        N = b.shape[1]
    else:
        M = a.shape[0]
        K = a_cols if a_cols is not None else a.shape[1]
        N = b.shape[0] if mode == "nt" else b.shape[1]
    tm = _tile(M, 1024, 8 if M < LANE else LANE)
    tn = _tile(N, 1024)
    kal = 8 if K < LANE else LANE
    tk = _tile(K, 2048, kal)
    osz = jnp.dtype(out_dtype).itemsize

    def vmem(tk_):
        acc = 0 if tk_ == K else tm * tn * 4
        return 2 * tk_ * (tm * a.dtype.itemsize + tn * b.dtype.itemsize) + 2 * tm * tn * osz + acc

    while vmem(tk) > MM_VMEM_BUDGET and tk > kal:
        tk = _tile(K, tk - kal, kal)
    nk = K // tk
    if mode == "tn":
        assert a_col0 % tm == 0
        a_spec = pl.BlockSpec((tk, tm), lambda i, j, k: (k, i + a_col0 // tm))
        b_spec = pl.BlockSpec((tk, tn), lambda i, j, k: (k, j))
        dims = _TN
    else:
        assert a_col0 % tk == 0
        a_spec = pl.BlockSpec((tm, tk), lambda i, j, k: (i, k + a_col0 // tk))
        if mode == "nn":
            b_spec = pl.BlockSpec((tk, tn), lambda i, j, k: (k, j))
            dims = _NN
        else:
            b_spec = pl.BlockSpec((tn, tk), lambda i, j, k: (j, k))
            dims = _NT

    if nk == 1:
        def body(a_ref, b_ref, o_ref):
            o_ref[...] = _bdot(a_ref[...], b_ref[...], dims).astype(o_ref.dtype)

        scratch = []
    else:
        def body(a_ref, b_ref, o_ref, acc_ref):
            k = pl.program_id(2)

            @pl.when(k == 0)
            def _():
                acc_ref[...] = jnp.zeros_like(acc_ref)

            acc_ref[...] += _bdot(a_ref[...], b_ref[...], dims)

            @pl.when(k == nk - 1)
            def _():
                o_ref[...] = acc_ref[...].astype(o_ref.dtype)

        scratch = [pltpu.VMEM((tm, tn), F32)]

    return pl.pallas_call(
        body,
        name=name,
        grid=(M // tm, N // tn, nk),
        in_specs=[a_spec, b_spec],
        out_specs=pl.BlockSpec((tm, tn), lambda i, j, k: (i, j)),
        out_shape=jax.ShapeDtypeStruct((M, N), out_dtype),
        scratch_shapes=scratch,
        compiler_params=_cp("parallel", "parallel", "arbitrary"),
    )(a, b)


def _row_spec(tm, width, cb):
    return pl.BlockSpec((tm, width), lambda i: (i, cb))


def _rowwise(fn, rows, params, outs, name, tm=256):
    S = rows[0][0].shape[0]
    nr, npar = len(rows), len(params)

    def body(*refs):
        xs = [r[...].astype(F32) for r in refs[:nr]]
        ps = [p[...] for p in refs[nr:nr + npar]]
        res = fn(*xs, *ps)
        for o, r in zip(refs[nr + npar:], res):
            o[...] = r.astype(o.dtype)

    res = pl.pallas_call(
        body,
        name=name,
        grid=(S // tm,),
        in_specs=[_row_spec(tm, w, cb) for _, w, cb in rows]
        + [pl.BlockSpec(p.shape, lambda i: (0, 0)) for p in params],
        out_specs=[_row_spec(tm, w, 0) for w, _ in outs],
        out_shape=[jax.ShapeDtypeStruct((S, w), dt) for w, dt in outs],
        compiler_params=_cp("parallel"),
    )(*[r[0] for r in rows], *params)
    return list(res)


def _rowwise_vjp(fn, rows, params, cts, name, row_grads, add_rows=None, tm=256):
    S = rows[0][0].shape[0]
    add_rows = add_rows or {}
    nr, npar = len(rows), len(params)
    flat_cts = [c for group in cts for c in group]
    ncts = len(flat_cts)
    add_keys = sorted(add_rows)
    nadd = len(add_keys)
    grad_idx = [i for i in range(nr) if row_grads[i]]

    def body(*refs):
        i = pl.program_id(0)
        xs = [r[...].astype(F32) for r in refs[:nr]]
        ps = [p[...] for p in refs[nr:nr + npar]]
        ct_refs = refs[nr + npar:nr + npar + ncts]
        add_refs = refs[nr + npar + ncts:nr + npar + ncts + nadd]
        out_refs = refs[nr + npar + ncts + nadd:]
        ct_vals, pos = [], 0
        for group in cts:
            v = ct_refs[pos][...].astype(F32)
            for r in ct_refs[pos + 1:pos + len(group)]:
                v = v + r[...].astype(F32)
            pos += len(group)
            ct_vals.append(v)
        _, vjp = jax.vjp(lambda *a: tuple(fn(*a)), *xs, *ps)
        grads = vjp(tuple(ct_vals))
        for o, gi in zip(out_refs[:len(grad_idx)], grad_idx):
            g = grads[gi]
            if gi in add_rows:
                g = g + add_refs[add_keys.index(gi)][...].astype(F32)
            o[...] = g.astype(o.dtype)
        dprefs = out_refs[len(grad_idx):]

        @pl.when(i == 0)
        def _():
            for dp in dprefs:
                dp[...] = jnp.zeros_like(dp)

        for dp, g in zip(dprefs, grads[nr:]):
            dp[...] += g

    res = pl.pallas_call(
        body,
        name=name,
        grid=(S // tm,),
        in_specs=[_row_spec(tm, w, cb) for _, w, cb in rows]
        + [pl.BlockSpec(p.shape, lambda i: (0, 0)) for p in params]
        + [_row_spec(tm, w, cb) for _, w, cb in flat_cts]
        + [_row_spec(tm, add_rows[k][1], add_rows[k][2]) for k in add_keys],
        out_specs=[_row_spec(tm, rows[gi][1], 0) for gi in grad_idx]
        + [pl.BlockSpec(p.shape, lambda i: (0, 0)) for p in params],
        out_shape=[jax.ShapeDtypeStruct((S, rows[gi][1]), F32) for gi in grad_idx]
        + [jax.ShapeDtypeStruct(p.shape, F32) for p in params],
        compiler_params=_cp("arbitrary"),
    )(*[r[0] for r in rows], *params, *[c[0] for c in flat_cts], *[add_rows[k][0] for k in add_keys])
    res = list(res)
    return res[:len(grad_idx)], res[len(grad_idx):]


def _rms(x, gain):
    return x * lax.rsqrt(jnp.mean(x * x, axis=-1, keepdims=True) + EPS) * gain


def _f_pre(x, gain, sc, sh):
    return (_rms(x, gain) * (1.0 + sc) + sh,)


def _f_post(x, y, gain, g):
    return (x + g * _rms(y, gain),)


def _f_norm(x, gain):
    return (_rms(x, gain),)


def _f_s5a(yc, u, d):
    return (jax.nn.gelu(yc + d * u, approximate=True),)


def _f_s5b(g, z):
    return (g * jax.nn.sigmoid(z),)


def _loss_head(y, target, tm=256):
    S, D = y.shape

    def body(y_ref, t_ref, dy_ref, acc_ref):
        i = pl.program_id(0)
        e = y_ref[...] - t_ref[...]
        dy_ref[...] = e * (1.0 / D)

        @pl.when(i == 0)
        def _():
            acc_ref[...] = jnp.zeros_like(acc_ref)

        acc_ref[...] += jnp.sum(e * e, axis=0, keepdims=True)

    dy, acc = pl.pallas_call(
        body,
        name="loss_head",
        grid=(S // tm,),
        in_specs=[_row_spec(tm, D, 0), _row_spec(tm, D, 0)],
        out_specs=[_row_spec(tm, D, 0), pl.BlockSpec((1, D), lambda i: (0, 0))],
        out_shape=[jax.ShapeDtypeStruct((S, D), F32), jax.ShapeDtypeStruct((1, D), F32)],
        compiler_params=_cp("arbitrary"),
    )(y, target)
    return 0.5 * jnp.sum(acc) / D, dy


def _s5_tile_scan(xr, xi, tab_ref, reverse, row8):
    for k in (1, 2, 4):
        pr = tab_ref[pl.ds(k - 1, 1), 0:S5_TL] if not reverse else tab_ref[pl.ds(8 - k, 1), 0:S5_TL]
        pi = tab_ref[pl.ds(k - 1, 1), S5_TL:2 * S5_TL] if not reverse else tab_ref[pl.ds(8 - k, 1), S5_TL:2 * S5_TL]
        if not reverse:
            keep = row8 >= k
            sr = jnp.where(keep, pltpu.roll(xr, k, 0), 0.0)
            si = jnp.where(keep, pltpu.roll(xi, k, 0), 0.0)
        else:
            keep = row8 < 8 - k
            sr = jnp.where(keep, pltpu.roll(xr, 8 - k, 0), 0.0)
            si = jnp.where(keep, pltpu.roll(xi, 8 - k, 0), 0.0)
        xr, xi = xr + pr * sr - pi * si, xi + pr * si + pi * sr
    return xr, xi


def _s5_scan(bu, tab, reverse=False, h=None, bu_fwd=None, tr=512):
    S = bu.shape[0]
    tr = min(tr, S)
    nl = S5_N // S5_TL
    nrb = S // tr
    w = 2 * S5_TL
    nt = tr // 8
    rmap = (lambda j, i: (i, j)) if not reverse else (lambda j, i: (nrb - 1 - i, j))

    def body(*refs):
        if reverse:
            x_ref, tab_ref, h_ref, b_ref, o_ref, acc_ref, cr_ref, ci_ref = refs
        else:
            x_ref, tab_ref, o_ref, cr_ref, ci_ref = refs
        i = pl.program_id(1)
        row8 = lax.broadcasted_iota(jnp.int32, (8, S5_TL), 0)

        @pl.when(i == 0)
        def _():
            cr_ref[...] = jnp.zeros_like(cr_ref)
            ci_ref[...] = jnp.zeros_like(ci_ref)
            if reverse:
                acc_ref[...] = jnp.zeros_like(acc_ref)

        tr_all = tab_ref[:, 0:S5_TL]
        ti_all = tab_ref[:, S5_TL:w]

        def tile(t, carry):
            tt = (nt - 1 - t) if reverse else t
            r = pl.ds(pl.multiple_of(tt * 8, 8), 8)
            xr, xi = _s5_tile_scan(x_ref[r, 0:S5_TL], x_ref[r, S5_TL:w], tab_ref, reverse, row8)
            cr = jnp.broadcast_to(cr_ref[...], (8, S5_TL))
            ci = jnp.broadcast_to(ci_ref[...], (8, S5_TL))
            hr = xr + tr_all * cr - ti_all * ci
            hi = xi + tr_all * ci + ti_all * cr
            o_ref[r, 0:S5_TL] = hr
            o_ref[r, S5_TL:w] = hi
            edge = pl.ds(tt * 8, 1) if reverse else pl.ds(tt * 8 + 7, 1)
            cr_ref[...] = o_ref[edge, 0:S5_TL]
            ci_ref[...] = o_ref[edge, S5_TL:w]
            if reverse:
                dr = h_ref[r, 0:S5_TL] - b_ref[r, 0:S5_TL]
                di = h_ref[r, S5_TL:w] - b_ref[r, S5_TL:w]
                acc_ref[:, 0:S5_TL] += hr * dr + hi * di
                acc_ref[:, S5_TL:w] += hi * dr - hr * di
            return carry

        lax.fori_loop(0, nt, tile, 0)

    blk = pl.BlockSpec((tr, w), rmap)
    in_specs = [blk, pl.BlockSpec((8, w), lambda j, i: (0, j))]
    out_specs = [blk]
    out_shape = [jax.ShapeDtypeStruct((S, 2 * S5_N), F32)]
    args = [bu, tab]
    if reverse:
        in_specs += [blk, blk]
        args += [h, bu_fwd]
        out_specs.append(pl.BlockSpec((8, w), lambda j, i: (0, j)))
        out_shape.append(jax.ShapeDtypeStruct((8, 2 * S5_N), F32))
    res = pl.pallas_call(
        body,
        name="s5_scan_bwd" if reverse else "s5_scan_fwd",
        grid=(nl, nrb),
        in_specs=in_specs,
        out_specs=out_specs,
        out_shape=out_shape,
        scratch_shapes=[pltpu.VMEM((1, S5_TL), F32), pltpu.VMEM((1, S5_TL), F32)],
        compiler_params=_cp("parallel", "arbitrary"),
    )(*args)
    return res if reverse else res[0]


def _ri_cols(re, im):
    lead = re.shape[:-1]
    nl = S5_N // S5_TL
    z = jnp.stack([re.reshape(*lead, nl, S5_TL), im.reshape(*lead, nl, S5_TL)], axis=-2)
    return z.reshape(*lead, 2 * S5_N)


def _ri_split(z):
    lead = z.shape[:-1]
    nl = S5_N // S5_TL
    z = z.reshape(*lead, nl, 2, S5_TL)
    return z[..., 0, :].reshape(*lead, S5_N), z[..., 1, :].reshape(*lead, S5_N)


def _s5_prep(lre, lim, logdt, bre, bim, cre, cim):
    lam = lax.complex(lre, lim)
    dt = jnp.exp(logdt)[:, None]
    lam_bar = jnp.exp(lam * dt)
    b = lax.complex(bre, bim)
    b_bar = ((lam_bar - 1.0) / lam)[..., None] * b
    eye = jnp.eye(S5_G, dtype=F32)
    bd_re = jnp.einsum("gpc,gh->gchp", jnp.real(b_bar), eye).reshape(S5_W, S5_N)
    bd_im = jnp.einsum("gpc,gh->gchp", jnp.imag(b_bar), eye).reshape(S5_W, S5_N)
    bd = _ri_cols(bd_re, bd_im)
    cd_re = jnp.einsum("gcp,gh->gchp", cre, eye).reshape(S5_W, S5_N)
    cd_im = jnp.einsum("gcp,gh->gchp", -cim, eye).reshape(S5_W, S5_N)
    cdt = _ri_cols(cd_re, cd_im)
    return jnp.real(lam_bar).reshape(1, S5_N), jnp.imag(lam_bar).reshape(1, S5_N), bd, cdt


def _s5_tables(lre, lim, logdt):
    lam = lax.complex(lre, lim)
    dt = jnp.exp(logdt)[:, None]
    k = jnp.arange(1, 9, dtype=F32)[:, None, None]
    pw = jnp.exp((lam * dt)[None] * k).reshape(8, S5_N)
    fwd = _ri_cols(jnp.real(pw), jnp.imag(pw))
    rev = _ri_cols(jnp.real(pw)[::-1], -jnp.imag(pw)[::-1])
    return fwd, rev


def _rope_tables(positions):
    inv_freq = 1.0 / (ROPE_THETA ** (jnp.arange(0, MLA_ROPE, 2, dtype=F32) / MLA_ROPE))
    ang = positions.astype(F32)[:, None] * inv_freq
    cos, sin = jnp.cos(ang), jnp.sin(ang)
    z = jnp.zeros_like(cos)
    cs = jnp.concatenate([cos, cos, z, z], axis=-1)
    sn = jnp.concatenate([-sin, sin, z, z], axis=-1)
    return cs, sn


def _rope_fwd(qraw, kvraw, proj, cs, sn, tm=256):
    S = qraw.shape[0]
    HW = MLA_H * MLA_HW

    def rope(x, c, s):
        lane = lax.broadcasted_iota(jnp.int32, x.shape, 1)
        sw = jnp.where(lane < 32, pltpu.roll(x, 96, 1), jnp.where(lane < 64, pltpu.roll(x, 32, 1), 0.0))
        return x * c + sw * s

    def body(q_ref, kv_ref, kr_ref, cs_ref, sn_ref, qo_ref, ko_ref, vo_ref):
        c, s = cs_ref[...], sn_ref[...]
        kr = rope(kr_ref[...], c, s).astype(BF16)
        for h in range(MLA_H):
            o = h * MLA_HW
            qo_ref[:, o:o + 128] = q_ref[:, o:o + 128].astype(BF16)
            qo_ref[:, o + 128:o + 256] = rope(q_ref[:, o + 128:o + 256], c, s).astype(BF16)
            ko_ref[:, o:o + 128] = kv_ref[:, o:o + 128].astype(BF16)
            ko_ref[:, o + 128:o + 256] = kr
            vo_ref[:, h * 128:(h + 1) * 128] = kv_ref[:, o + 128:o + 256].astype(BF16)

    return pl.pallas_call(
        body,
        name="rope_fwd",
        grid=(S // tm,),
        in_specs=[_row_spec(tm, HW, 0), _row_spec(tm, HW, 0), _row_spec(tm, 128, (D_INP - 128) // 128),
                  _row_spec(tm, 128, 0), _row_spec(tm, 128, 0)],
        out_specs=[_row_spec(tm, HW, 0), _row_spec(tm, HW, 0), _row_spec(tm, MLA_H * MLA_V, 0)],
        out_shape=[jax.ShapeDtypeStruct((S, HW), BF16), jax.ShapeDtypeStruct((S, HW), BF16),
                   jax.ShapeDtypeStruct((S, MLA_H * MLA_V), BF16)],
        compiler_params=_cp("parallel"),
    )(qraw, kvraw, proj, cs, sn)


def _rope_bwd(dq, dk, dv, cs, sn, tm=256):
    S = dq.shape[0]
    HW = MLA_H * MLA_HW

    def rope_t(x, c, s):
        lane = lax.broadcasted_iota(jnp.int32, x.shape, 1)
        w = x * s
        sw = jnp.where(lane < 32, pltpu.roll(w, 96, 1), jnp.where(lane < 64, pltpu.roll(w, 32, 1), 0.0))
        return x * c + sw

    def body(dq_ref, dk_ref, dv_ref, cs_ref, sn_ref, qo_ref, kvo_ref, kro_ref):
        c, s = cs_ref[...], sn_ref[...]
        kr = jnp.zeros((tm, 128), F32)
        for h in range(MLA_H):
            o = h * MLA_HW
            qo_ref[:, o:o + 128] = dq_ref[:, o:o + 128]
            qo_ref[:, o + 128:o + 256] = rope_t(dq_ref[:, o + 128:o + 256], c, s)
            kvo_ref[:, o:o + 128] = dk_ref[:, o:o + 128]
            kvo_ref[:, o + 128:o + 256] = dv_ref[:, h * 128:(h + 1) * 128]
            kr = kr + dk_ref[:, o + 128:o + 256]
        kro_ref[...] = rope_t(kr, c, s)

    return pl.pallas_call(
        body,
        name="rope_bwd",
        grid=(S // tm,),
        in_specs=[_row_spec(tm, HW, 0), _row_spec(tm, HW, 0), _row_spec(tm, MLA_H * MLA_V, 0),
                  _row_spec(tm, 128, 0), _row_spec(tm, 128, 0)],
        out_specs=[_row_spec(tm, HW, 0), _row_spec(tm, HW, 0), _row_spec(tm, 128, 0)],
        out_shape=[jax.ShapeDtypeStruct((S, HW), F32), jax.ShapeDtypeStruct((S, HW), F32),
                   jax.ShapeDtypeStruct((S, 128), F32)],
        compiler_params=_cp("parallel"),
    )(dq, dk, dv, cs, sn)


ATT_T = 512


def _diag_mask(t):
    return lax.broadcasted_iota(jnp.int32, (t, t), 1) <= lax.broadcasted_iota(jnp.int32, (t, t), 0)


def _flash_fwd(q, k, v, exch=None):
    S = q.shape[0]
    t = min(ATT_T, S)
    nq = S // t

    def body(q_ref, k_ref, v_ref, o_ref, lse_ref):
        i = pl.program_id(1)
        qb = q_ref[...]

        def step(j, carry, masked):
            m, l, acc = carry
            r = pl.ds(pl.multiple_of(j * t, t), t)
            s = _bdot(qb, k_ref[r, :], _NT) * ATT_SCALE
            if masked:
                s = jnp.where(_diag_mask(t), s, MASK_VALUE)
            m_new = jnp.maximum(m, jnp.max(s, axis=-1, keepdims=True))
            alpha = jnp.exp(m - m_new)
            p = jnp.exp(s - m_new)
            l = alpha * l + jnp.sum(p, axis=-1, keepdims=True)
            acc = alpha * acc + _bdot(p, v_ref[r, :], _NN)
            return m_new, l, acc

        m0 = jnp.full((t, 1), MASK_VALUE, F32)
        init = (m0, jnp.zeros((t, 1), F32), jnp.zeros((t, MLA_V), F32))
        m, l, acc = step(i, lax.fori_loop(0, i, lambda j, c: step(j, c, False), init), True)
        o_ref[...] = acc / l
        lse_ref[...] = jnp.broadcast_to(m + jnp.log(l), (t, 128))

    return _call_hosting(
        body, "flash_fwd", (MLA_H, nq),
        [pl.BlockSpec((t, MLA_HW), lambda h, i: (i, h)),
         pl.BlockSpec((S, MLA_HW), lambda h, i: (0, h)),
         pl.BlockSpec((S, MLA_V), lambda h, i: (0, h))],
        [pl.BlockSpec((t, MLA_V), lambda h, i: (i, h)), pl.BlockSpec((t, 128), lambda h, i: (i, h))],
        [jax.ShapeDtypeStruct((S, MLA_H * MLA_V), F32), jax.ShapeDtypeStruct((S, MLA_H * 128), F32)],
        [q, k, v], exch)


def _flash_bwd_dq(q, k, v, o, lse, dcat):
    S = q.shape[0]
    t = min(ATT_T, S)
    nq = S // t
    do_cb = S5_W // MLA_V

    def body(q_ref, k_ref, v_ref, o_ref, lse_ref, do_ref, dq_ref):
        i = pl.program_id(1)
        qb = q_ref[...]
        do = do_ref[...]
        delta = jnp.sum(do * o_ref[...], axis=-1, keepdims=True)
        lse1 = jnp.max(lse_ref[...], axis=-1, keepdims=True)
        dob = do.astype(BF16)

        def step(j, dq, masked):
            r = pl.ds(pl.multiple_of(j * t, t), t)
            kb = k_ref[r, :]
            s = _bdot(qb, kb, _NT) * ATT_SCALE
            p = jnp.exp(s - lse1)
            if masked:
                p = jnp.where(_diag_mask(t), p, 0.0)
            dp = _bdot(dob, v_ref[r, :], _NT)
            ds = p * (dp - delta) * ATT_SCALE
            return dq + _bdot(ds, kb, _NN)

        dq = lax.fori_loop(0, i, lambda j, c: step(j, c, False), jnp.zeros((t, MLA_HW), F32))
        dq_ref[...] = step(i, dq, True)

    return pl.pallas_call(
        body,
        name="flash_bwd_dq",
        grid=(MLA_H, nq),
        in_specs=[pl.BlockSpec((t, MLA_HW), lambda h, i: (i, h)),
                  pl.BlockSpec((S, MLA_HW), lambda h, i: (0, h)),
                  pl.BlockSpec((S, MLA_V), lambda h, i: (0, h)),
                  pl.BlockSpec((t, MLA_V), lambda h, i: (i, h)),
                  pl.BlockSpec((t, 128), lambda h, i: (i, h)),
                  pl.BlockSpec((t, MLA_V), lambda h, i: (i, do_cb + h))],
        out_specs=pl.BlockSpec((t, MLA_HW), lambda h, i: (i, h)),
        out_shape=jax.ShapeDtypeStruct((S, MLA_H * MLA_HW), F32),
        compiler_params=_cp("parallel", "parallel"),
    )(q, k, v, o, lse, dcat)


def _flash_bwd_dkv(q, k, v, o, lse, dcat, exch=None):
    S = q.shape[0]
    t = min(ATT_T, S)
    nq = S // t
    do_cb = S5_W // MLA_V

    def body(q_ref, k_ref, v_ref, o_ref, lse_ref, do_ref, dk_ref, dv_ref):
        j = pl.program_id(1)
        kb = k_ref[...]
        vb = v_ref[...]

        def step(i, carry, masked):
            dk, dv = carry
            r = pl.ds(pl.multiple_of(i * t, t), t)
            qb = q_ref[r, :]
            do = do_ref[r, :]
            delta = jnp.sum(do * o_ref[r, :], axis=-1, keepdims=True)
            lse1 = jnp.max(lse_ref[r, :], axis=-1, keepdims=True)
            s = _bdot(qb, kb, _NT) * ATT_SCALE
            p = jnp.exp(s - lse1)
            if masked:
                p = jnp.where(_diag_mask(t), p, 0.0)
            dob = do.astype(BF16)
            dv = dv + _bdot(p, dob, _TN)
            dp = _bdot(dob, vb, _NT)
            ds = p * (dp - delta) * ATT_SCALE
            dk = dk + _bdot(ds, qb, _TN)
            return dk, dv

        first = step(j, (jnp.zeros((t, MLA_HW), F32), jnp.zeros((t, MLA_V), F32)), True)
        dk, dv = lax.fori_loop(j + 1, nq, lambda i, c: step(i, c, False), first)
        dk_ref[...] = dk
        dv_ref[...] = dv

    return _call_hosting(
        body, "flash_bwd_dkv", (MLA_H, nq),
        [pl.BlockSpec((S, MLA_HW), lambda h, j: (0, h)),
         pl.BlockSpec((t, MLA_HW), lambda h, j: (j, h)),
         pl.BlockSpec((t, MLA_V), lambda h, j: (j, h)),
         pl.BlockSpec((S, MLA_V), lambda h, j: (0, h)),
         pl.BlockSpec((S, 128), lambda h, j: (0, h)),
         pl.BlockSpec((S, MLA_V), lambda h, j: (0, do_cb + h))],
        [pl.BlockSpec((t, MLA_HW), lambda h, j: (j, h)), pl.BlockSpec((t, MLA_V), lambda h, j: (j, h))],
        [jax.ShapeDtypeStruct((S, MLA_H * MLA_HW), F32), jax.ShapeDtypeStruct((S, MLA_H * MLA_V), F32)],
        [q, k, v, o, lse, dcat], exch)


def _split3(x):
    x1 = x.astype(BF16)
    r1 = x - x1.astype(F32)
    x2 = r1.astype(BF16)
    x3 = (r1 - x2.astype(F32)).astype(BF16)
    return x1, x2, x3


def _tri_matmul(x, upper):
    n = x.shape[0]
    r = lax.broadcasted_iota(jnp.int32, (n, n), 0)
    c = lax.broadcasted_iota(jnp.int32, (n, n), 1)
    tri = jnp.where((r <= c) if upper else (r >= c), 1.0, 0.0).astype(BF16)
    x1, x2, x3 = _split3(x)
    dot = lambda v: lax.dot_general(tri, v, _NN, preferred_element_type=F32)
    return dot(x1) + dot(x2) + dot(x3)


@jax.custom_vjp
def _cumsum_rows(x):
    return _tri_matmul(x, False)


def _cumsum_rows_fwd(x):
    return _tri_matmul(x, False), None


def _cumsum_rows_bwd(_, ct):
    return (_tri_matmul(ct, True),)


_cumsum_rows.defvjp(_cumsum_rows_fwd, _cumsum_rows_bwd)


def _hg_step(qin, fin, vin, gin, st, lb, on):
    n = qin.shape[0]
    sig = jax.nn.sigmoid(fin)
    g = jnp.log(lb + (1.0 - lb) * sig)
    k = (1.0 - lb) * jax.nn.sigmoid(-fin)
    q = qin * jax.nn.sigmoid(qin)
    b = _cumsum_rows(g)
    o = _bdot(q * jnp.exp(b), st, _NT)
    row = lax.broadcasted_iota(jnp.int32, (n, HG_D), 0)
    row1 = lax.broadcasted_iota(jnp.int32, (n, 1), 0)
    b_s = None
    for s in range(n):
        sel = row == s
        b_s = jnp.sum(jnp.where(sel, b, 0.0), axis=0, keepdims=True)
        k_s = jnp.sum(jnp.where(sel, k, 0.0), axis=0, keepdims=True)
        v_s = jnp.sum(jnp.where(sel, vin, 0.0), axis=0, keepdims=True)
        e = jnp.exp(jnp.minimum(b - b_s, 0.0))
        c = jnp.sum(q * e * k_s, axis=-1, keepdims=True)
        o = o + jnp.where(row1 >= s, c, 0.0) * v_s
    st_new = st * jnp.exp(b_s) + _bdot(vin, k * jnp.exp(b_s - b), _TN)
    y = _rms(o, on) * (gin * jax.nn.sigmoid(gin))
    return y, st_new


HG_W = HG_H * HG_D


def _hg_specs(tb, nb, reverse):
    rm = (lambda i: nb - 1 - i) if reverse else (lambda i: i)
    base = 1024 // HG_W
    return [pl.BlockSpec((tb, HG_W), lambda i, o=o: (rm(i), base + o)) for o in range(4)], rm


def _head(h):
    return slice(h * HG_D, (h + 1) * HG_D)


def _hg_fwd(proj, lb, on, exch=None, tb=256):
    S = proj.shape[0]
    nb = S // tb
    nc = tb // HG_CH
    in_specs, rm = _hg_specs(tb, nb, False)

    def body(q_ref, f_ref, v_ref, g_ref, lb_ref, on_ref, y_ref, sts_ref, st_ref):
        @pl.when(pl.program_id(0) == 0)
        def _():
            st_ref[...] = jnp.zeros_like(st_ref)

        def step(c, carry):
            r = pl.ds(pl.multiple_of(c * HG_CH, HG_CH), HG_CH)
            for h in range(HG_H):
                hs = _head(h)
                st = st_ref[h]
                sts_ref[h, c] = st
                y, st_new = _hg_step(q_ref[r, hs], f_ref[r, hs], v_ref[r, hs], g_ref[r, hs], st, lb_ref[:, hs],
                                     on_ref[...])
                y_ref[r, hs] = y
                st_ref[h] = st_new
            return carry

        lax.fori_loop(0, nc, step, 0)

    return _call_hosting(
        body, "hgrn2_fwd", (nb,),
        in_specs + [pl.BlockSpec((1, HG_W), lambda i: (0, 0)), pl.BlockSpec((1, HG_D), lambda i: (0, 0))],
        [pl.BlockSpec((tb, HG_W), lambda i: (i, 0)), pl.BlockSpec((HG_H, nc, HG_D, HG_D), lambda i: (0, i, 0, 0))],
        [jax.ShapeDtypeStruct((S, HG_W), F32), jax.ShapeDtypeStruct((HG_H, S // HG_CH, HG_D, HG_D), F32)],
        [proj, proj, proj, proj, lb, on], exch,
        scratch=[pltpu.VMEM((HG_H, HG_D, HG_D), F32)], sem=["arbitrary"])


def _hg_bwd(proj, sts, lb, on, dcat, exch=None, tb=256):
    S = proj.shape[0]
    nb = S // tb
    nc = tb // HG_CH
    in_specs, rm = _hg_specs(tb, nb, True)
    dy_cb = (S5_W + MLA_H * MLA_V) // HG_W

    def body(q_ref, f_ref, v_ref, g_ref, lb_ref, on_ref, sts_ref, dy_ref,
             dq_ref, df_ref, dv_ref, dg_ref, dlb_ref, don_ref, dst_ref):
        @pl.when(pl.program_id(0) == 0)
        def _():
            dst_ref[...] = jnp.zeros_like(dst_ref)
            dlb_ref[...] = jnp.zeros_like(dlb_ref)
            don_ref[...] = jnp.zeros_like(don_ref)

        def step(cc, carry):
            c = nc - 1 - cc
            r = pl.ds(pl.multiple_of(c * HG_CH, HG_CH), HG_CH)
            for h in range(HG_H):
                hs = _head(h)
                _, vjp = jax.vjp(_hg_step, q_ref[r, hs], f_ref[r, hs], v_ref[r, hs], g_ref[r, hs], sts_ref[h, c],
                                 lb_ref[:, hs], on_ref[...])
                dq, df, dv, dg, dst, dlb, don = vjp((dy_ref[r, hs], dst_ref[h]))
                dq_ref[r, hs] = dq
                df_ref[r, hs] = df
                dv_ref[r, hs] = dv
                dg_ref[r, hs] = dg
                dst_ref[h] = dst
                dlb_ref[:, hs] += dlb
                don_ref[:, hs] += don
            return carry

        lax.fori_loop(0, nc, step, 0)

    blk = pl.BlockSpec((tb, HG_W), lambda i: (rm(i), 0))
    par = pl.BlockSpec((1, HG_W), lambda i: (0, 0))
    return _call_hosting(
        body, "hgrn2_bwd", (nb,),
        in_specs + [par, pl.BlockSpec((1, HG_D), lambda i: (0, 0)),
                    pl.BlockSpec((HG_H, nc, HG_D, HG_D), lambda i: (0, rm(i), 0, 0)),
                    pl.BlockSpec((tb, HG_W), lambda i: (rm(i), dy_cb))],
        [blk, blk, blk, blk, par, par],
        [jax.ShapeDtypeStruct((S, HG_W), F32)] * 4 + [jax.ShapeDtypeStruct((1, HG_W), F32)] * 2,
        [proj, proj, proj, proj, lb, on, sts, dcat], exch,
        scratch=[pltpu.VMEM((HG_H, HG_D, HG_D), F32)], sem=["arbitrary"])


CONV_TC = 512
CONV_NC = D_FFP // CONV_TC


def _shift_down(cur, halo, k):
    tm = cur.shape[0]
    row = lax.broadcasted_iota(jnp.int32, cur.shape, 0)
    top = jnp.concatenate([pltpu.roll(halo, k, 0), jnp.zeros((tm - 8, cur.shape[1]), F32)], axis=0)
    return jnp.where(row < k, top, pltpu.roll(cur, k, 0))


def _shift_up(cur, halo, k):
    tm = cur.shape[0]
    row = lax.broadcasted_iota(jnp.int32, cur.shape, 0)
    bot = jnp.concatenate([jnp.zeros((tm - 8, cur.shape[1]), F32), pltpu.roll(halo, 8 - k, 0)], axis=0)
    return jnp.where(row >= tm - k, bot, pltpu.roll(cur, tm - k, 0))


def _conv3(cur, halo, w_ref, b_ref):
    return (b_ref[...] + _shift_down(cur, halo, 2) * w_ref[pl.ds(0, 1), :]
            + _shift_down(cur, halo, 1) * w_ref[pl.ds(1, 1), :] + cur * w_ref[pl.ds(2, 1), :])


def _conv_fwd(u0, cw, cb, tm=256):
    S = u0.shape[0]
    nc = CONV_NC
    m8 = tm // 8
    prev = lambda i: jnp.maximum(i * m8 - 1, 0)

    def body(u_ref, p_ref, w_ref, b_ref, a_ref):
        on = (pl.program_id(1) > 0).astype(F32)
        u = _conv3(u_ref[...], p_ref[...] * on, w_ref, b_ref)
        a_ref[...] = (jax.nn.gelu(u[:, :CONV_TC], approximate=True) * u[:, CONV_TC:]).astype(a_ref.dtype)

    tc = CONV_TC
    return pl.pallas_call(
        body,
        name="conv_geglu_fwd",
        grid=(nc, S // tm),
        in_specs=[pl.BlockSpec((tm, 2 * tc), lambda j, i: (i, j)), pl.BlockSpec((8, 2 * tc), lambda j, i: (prev(i), j)),
                  pl.BlockSpec((3, 2 * tc), lambda j, i: (0, j)), pl.BlockSpec((1, 2 * tc), lambda j, i: (0, j))],
        out_specs=pl.BlockSpec((tm, tc), lambda j, i: (i, j)),
        out_shape=jax.ShapeDtypeStruct((S, D_FFP), BF16),
        compiler_params=_cp("parallel", "parallel"),
    )(u0, u0, cw, cb)


def _conv_bwd_elem(da, u0, cw, cb, tm=256):
    S = u0.shape[0]
    nc = CONV_NC
    m8 = tm // 8
    prev = lambda i: jnp.maximum(i * m8 - 1, 0)

    def body(u_ref, p_ref, w_ref, b_ref, da_ref, du_ref, dw_ref, db_ref):
        i = pl.program_id(1)
        cur = u_ref[...]
        halo = p_ref[...] * (i > 0).astype(F32)
        u = _conv3(cur, halo, w_ref, b_ref)
        _, vjp = jax.vjp(lambda a, b: jax.nn.gelu(a, approximate=True) * b, u[:, :CONV_TC], u[:, CONV_TC:])
        dug, duv = vjp(da_ref[...])
        du = jnp.concatenate([dug, duv], axis=1)
        du_ref[...] = du

        @pl.when(i == 0)
        def _():
            dw_ref[...] = jnp.zeros_like(dw_ref)
            db_ref[...] = jnp.zeros_like(db_ref)

        dw_ref[pl.ds(0, 1), :] += jnp.sum(du * _shift_down(cur, halo, 2), axis=0, keepdims=True)
        dw_ref[pl.ds(1, 1), :] += jnp.sum(du * _shift_down(cur, halo, 1), axis=0, keepdims=True)
        dw_ref[pl.ds(2, 1), :] += jnp.sum(du * cur, axis=0, keepdims=True)
        db_ref[...] += jnp.sum(du, axis=0, keepdims=True)

    tc = CONV_TC
    return pl.pallas_call(
        body,
        name="conv_geglu_bwd",
        grid=(nc, S // tm),
        in_specs=[pl.BlockSpec((tm, 2 * tc), lambda j, i: (i, j)), pl.BlockSpec((8, 2 * tc), lambda j, i: (prev(i), j)),
                  pl.BlockSpec((3, 2 * tc), lambda j, i: (0, j)), pl.BlockSpec((1, 2 * tc), lambda j, i: (0, j)),
                  pl.BlockSpec((tm, tc), lambda j, i: (i, j))],
        out_specs=[pl.BlockSpec((tm, 2 * tc), lambda j, i: (i, j)), pl.BlockSpec((3, 2 * tc), lambda j, i: (0, j)),
                   pl.BlockSpec((1, 2 * tc), lambda j, i: (0, j))],
        out_shape=[jax.ShapeDtypeStruct((S, 2 * D_FFP), F32), jax.ShapeDtypeStruct((3, 2 * D_FFP), F32),
                   jax.ShapeDtypeStruct((1, 2 * D_FFP), F32)],
        compiler_params=_cp("parallel", "arbitrary"),
    )(u0, u0, cw, cb, da)


def _conv_bwd_input(du, cw, tm=256):
    S = du.shape[0]
    nrb = S // tm
    m8 = tm // 8
    nxt = lambda i: jnp.minimum((i + 1) * m8, S // 8 - 1)

    def body(d_ref, n_ref, w_ref, o_ref):
        on = (pl.program_id(1) < nrb - 1).astype(F32)
        cur = d_ref[...]
        halo = n_ref[...] * on
        o_ref[...] = (cur * w_ref[pl.ds(2, 1), :] + _shift_up(cur, halo, 1) * w_ref[pl.ds(1, 1), :]
                      + _shift_up(cur, halo, 2) * w_ref[pl.ds(0, 1), :])

    tc = 2 * CONV_TC
    return pl.pallas_call(
        body,
        name="conv_bwd_input",
        grid=(CONV_NC, nrb),
        in_specs=[pl.BlockSpec((tm, tc), lambda j, i: (i, j)), pl.BlockSpec((8, tc), lambda j, i: (nxt(i), j)),
                  pl.BlockSpec((3, tc), lambda j, i: (0, j))],
        out_specs=pl.BlockSpec((tm, tc), lambda j, i: (i, j)),
        out_shape=jax.ShapeDtypeStruct((S, 2 * D_FFP), F32),
        compiler_params=_cp("parallel", "parallel"),
    )(du, du, cw)


def _exchange(arrs, scatter, name):
    n = len(arrs)

    def body(*refs):
        args = (refs[:n], refs[n:2 * n], *refs[2 * n:], scatter)
        _exchange_start(*args)
        _exchange_wait(*args)

    hbm = pl.BlockSpec(memory_space=pltpu.HBM)
    out_shape, sems = _exchange_shapes(arrs, scatter)
    return pl.pallas_call(
        body,
        name=name,
        in_specs=[hbm] * n,
        out_specs=[hbm] * n,
        out_shape=out_shape,
        scratch_shapes=sems,
    )(*arrs)


def _exchange_copies(ins, outs, send, recv, loc, scatter):
    x, y, c = lax.axis_index("x"), lax.axis_index("y"), lax.axis_index("c")
    me = 4 * x + 2 * y + c
    sends, recvs, locs = [], [], []
    for a in range(len(ins)):
        locs.append(pltpu.make_async_copy(ins[a].at[me] if scatter else ins[a], outs[a].at[me], loc.at[a]))
        for k in range(1, N_DEV):
            px = 1 - x if k & 4 else x
            py = 1 - y if k & 2 else y
            pc = 1 - c if k & 1 else c
            peer = 4 * px + 2 * py + pc
            src = ins[a].at[peer] if scatter else ins[a]
            sems = dict(send_sem=send.at[a, k - 1], recv_sem=recv.at[a, k - 1], device_id=(px, py, pc),
                        device_id_type=pl.DeviceIdType.MESH)
            sends.append(pltpu.make_async_remote_copy(src_ref=src, dst_ref=outs[a].at[me], **sems))
            recvs.append(pltpu.make_async_remote_copy(src_ref=src, dst_ref=outs[a].at[peer], **sems))
    return locs, sends, recvs


def _exchange_start(*refs):
    locs, sends, _ = _exchange_copies(*refs)
    for cp in locs + sends:
        cp.start()


def _exchange_wait(*refs):
    locs, sends, recvs = _exchange_copies(*refs)
    for cp in recvs:
        cp.wait_recv()
    for cp in sends:
        cp.wait_send()
    for cp in locs:
        cp.wait()


def _exchange_shapes(arrs, scatter):
    n = len(arrs)
    out_shape = [jax.ShapeDtypeStruct(a.shape if scatter else (N_DEV,) + a.shape, a.dtype) for a in arrs]
    sems = [pltpu.SemaphoreType.DMA((n, N_DEV - 1)), pltpu.SemaphoreType.DMA((n, N_DEV - 1)),
            pltpu.SemaphoreType.DMA((n,))]
    return out_shape, sems


def _call_hosting(body, name, grid, in_specs, out_specs, out_shape, args, exch, scratch=(), sem=None):
    scratch = list(scratch)
    if exch is None:
        res = pl.pallas_call(body, name=name, grid=grid, in_specs=in_specs, out_specs=out_specs, out_shape=out_shape,
                             scratch_shapes=scratch,
                             compiler_params=_cp(*(sem or ["parallel"] * len(grid))))(*args)
        return list(res), []
    arrs, scatter = exch
    n, n_in, n_out, n_scr = len(arrs), len(in_specs), len(out_specs), len(scratch)
    hbm = pl.BlockSpec(memory_space=pltpu.HBM)
    x_shape, sems = _exchange_shapes(arrs, scatter)

    def hosting_body(*refs):
        cin, xin = refs[:n_in], refs[n_in:n_in + n]
        cout, xout = refs[n_in + n:n_in + n + n_out], refs[n_in + n + n_out:n_in + 2 * n + n_out]
        cscr = refs[n_in + 2 * n + n_out:n_in + 2 * n + n_out + n_scr]
        xsem = refs[n_in + 2 * n + n_out + n_scr:]
        ids = [pl.program_id(d) for d in range(len(grid))]
        first = functools.reduce(jnp.logical_and, [i == 0 for i in ids])
        last = functools.reduce(jnp.logical_and, [i == g - 1 for i, g in zip(ids, grid)])

        @pl.when(first)
        def _():
            _exchange_start(xin, xout, *xsem, scatter)

        body(*cin, *cout, *cscr)

        @pl.when(last)
        def _():
            _exchange_wait(xin, xout, *xsem, scatter)

    res = pl.pallas_call(
        hosting_body, name=name + "_x", grid=grid, in_specs=in_specs + [hbm] * n, out_specs=out_specs + [hbm] * n,
        out_shape=out_shape + x_shape, scratch_shapes=scratch + sems,
        compiler_params=_cp(*["arbitrary"] * len(grid)))(*args, *arrs)
    return list(res[:n_out]), list(res[n_out:])


def _adamw(recv, w, m, v, name="adamw"):
    L, n, R, C = recv.shape
    fits = [t for t in range(8, R + 1, 8) if R % t == 0 and t * C * 4 <= (1 << 19)]
    tr = max(fits) if fits else R

    def body(r_ref, w_ref, m_ref, v_ref, g_ref, d_ref, mo_ref, vo_ref):
        g = r_ref[0, 0].astype(F32)
        for d in range(1, n):
            g = g + r_ref[0, d].astype(F32)
        mm = ADAM_B1 * m_ref[0] + (1.0 - ADAM_B1) * g
        vv = ADAM_B2 * v_ref[0] + (1.0 - ADAM_B2) * (g * g)
        m_hat = mm / (1.0 - ADAM_B1 ** ADAM_STEP)
        v_hat = vv / (1.0 - ADAM_B2 ** ADAM_STEP)
        g_ref[0] = g
        d_ref[0] = -ADAM_LR * (m_hat / (jnp.sqrt(v_hat) + ADAM_EPS) + ADAM_WD * w_ref[0])
        mo_ref[0] = mm
        vo_ref[0] = vv

    blk = pl.BlockSpec((1, tr, C), lambda l, i: (l, i, 0))
    return pl.pallas_call(
        body,
        name=name,
        grid=(L, R // tr),
        in_specs=[pl.BlockSpec((1, n, tr, C), lambda l, i: (l, 0, i, 0)), blk, blk, blk],
        out_specs=[blk] * 4,
        out_shape=[jax.ShapeDtypeStruct((L, R, C), F32)] * 4,
        compiler_params=_cp("parallel", "parallel"),
    )(recv, w, m, v)


def _layer_fwd(x, mod, W, P, exch=None):
    sh1, sc1, g1, sh2, sc2, g2 = mod
    D = D_MODEL
    R = {"x": x}
    (h1,) = _rowwise(_f_pre, [(x, D, 0)], [P["n1"], sc1, sh1], [(D, BF16)], "pre_norm")
    proj = _mm(h1, W["w_in"], "nn", name="mm_in")
    R["h1"], R["proj"] = h1, proj
    bu = _mm(proj, W["bd"], "nn", a_col0=0, a_cols=S5_W, name="mm_s5_b")
    hs = _s5_scan(bu, P["tab_fwd"])
    yc = _mm(hs, W["cdt"], "nt", name="mm_s5_c")
    (gg,) = _rowwise(_f_s5a, [(yc, S5_W, 0), (proj, S5_W, 0)], [P["s5_d"]], [(S5_W, F32)], "s5_gelu")
    z = _mm(gg, W["w_glu"], "nn", name="mm_glu")
    (ys5,) = _rowwise(_f_s5b, [(gg, S5_W, 0), (z, S5_W, 0)], [], [(S5_W, BF16)], "s5_glu")
    R.update(bu=bu, hs=hs, yc=yc, gg=gg, z=z)
    (qn,) = _rowwise(_f_norm, [(proj, 512, 1)], [P["q_norm"]], [(512, BF16)], "q_norm")
    (kvn,) = _rowwise(_f_norm, [(proj, 256, 12)], [P["kv_norm"]], [(256, BF16)], "kv_norm")
    qraw = _mm(qn, W["w_uq"], "nn", name="mm_uq")
    kvraw = _mm(kvn, W["w_ukv"], "nn", name="mm_ukv")
    q, k, v = _rope_fwd(qraw, kvraw, proj, P["cs"], P["sn"])
    exch = exch or (None, None)
    (o, lse), got_a = _flash_fwd(q, k, v, exch[0])
    R.update(qn=qn, kvn=kvn, q=q, k=k, v=v, o=o, lse=lse)
    (yhg, sts), got_b = _hg_fwd(proj, P["lb"], P["hg_on"], exch[1])
    R["sts"] = sts
    cat = jnp.concatenate([ys5, o.astype(BF16), yhg.astype(BF16)], axis=-1)
    mixed = _mm(cat, W["w_out"], "nn", name="mm_out")
    (x2,) = _rowwise(_f_post, [(x, D, 0), (mixed, D, 0)], [P["n2"], g1], [(D, F32)], "post_norm")
    R.update(cat=cat, mixed=mixed, x2=x2)
    (h2,) = _rowwise(_f_pre, [(x2, D, 0)], [P["n3"], sc2, sh2], [(D, BF16)], "pre_norm")
    u0 = _mm(h2, W["w_up"], "nn", name="mm_up")
    a = _conv_fwd(u0, P["conv_w"], P["conv_b"])
    y = _mm(a, W["w_down"], "nn", name="mm_down")
    (x3,) = _rowwise(_f_post, [(x2, D, 0), (y, D, 0)], [P["n4"], g2], [(D, F32)], "post_norm")
    R.update(h2=h2, u0=u0, a=a, y=y)
    return x3, R, (got_a, got_b)


def _layer_bwd(dx3, mod, W, P, R, exch=None):
    sh1, sc1, g1, sh2, sc2, g2 = mod
    D = D_MODEL
    G = {}
    (dx2a, dy), (dn4, dg2) = _rowwise_vjp(_f_post, [(R["x2"], D, 0), (R["y"], D, 0)], [P["n4"], g2],
                                          [[(dx3, D, 0)]], "post_norm_bwd", [True, True])
    da = _mm(dy, W["w_down"], "nt", name="mm_down_dx")
    G["w_down"] = _mm(R["a"], dy, "tn", out_dtype=BF16, name="mm_down_dw")
    du, dcw, dcb = _conv_bwd_elem(da, R["u0"], P["conv_w"], P["conv_b"])
    du0 = _conv_bwd_input(du, P["conv_w"])
    dh2 = _mm(du0, W["w_up"], "nt", name="mm_up_dx")
    G["w_up"] = _mm(R["h2"], du0, "tn", out_dtype=BF16, name="mm_up_dw")
    (dx2,), (dn3, dsc2, dsh2) = _rowwise_vjp(_f_pre, [(R["x2"], D, 0)], [P["n3"], sc2, sh2], [[(dh2, D, 0)]],
                                             "pre_norm_bwd", [True], add_rows={0: (dx2a, D, 0)})
    (dxa, dmixed), (dn2, dg1) = _rowwise_vjp(_f_post, [(R["x"], D, 0), (R["mixed"], D, 0)], [P["n2"], g1],
                                             [[(dx2, D, 0)]], "post_norm_bwd", [True, True])
    dcat = _mm(dmixed, W["w_out"], "nt", name="mm_out_dx")
    G["w_out"] = _mm(R["cat"], dmixed, "tn", out_dtype=BF16, name="mm_out_dw")
    (dga, dz), _ = _rowwise_vjp(_f_s5b, [(R["gg"], S5_W, 0), (R["z"], S5_W, 0)], [], [[(dcat, S5_W, 0)]],
                                "s5_glu_bwd", [True, True])
    dgb = _mm(dz, W["w_glu"], "nt", name="mm_glu_dx")
    G["w_glu"] = _mm(R["gg"], dz, "tn", out_dtype=BF16, name="mm_glu_dw")
    (dyc, dua), (dd,) = _rowwise_vjp(_f_s5a, [(R["yc"], S5_W, 0), (R["proj"], S5_W, 0)], [P["s5_d"]],
                                     [[(dga, S5_W, 0), (dgb, S5_W, 0)]], "s5_gelu_bwd", [True, True])
    dhs = _mm(dyc, W["cdt"], "nn", name="mm_s5_c_dx")
    dcdt = _mm(dyc, R["hs"], "tn", name="mm_s5_c_dw")
    gs, acc = _s5_scan(dhs, P["tab_rev"], reverse=True, h=R["hs"], bu_fwd=R["bu"])
    dub = _mm(gs, W["bd"], "nt", name="mm_s5_b_dx")
    dbd = _mm(R["proj"], gs, "tn", a_col0=0, a_cols=S5_W, name="mm_s5_b_dw")
    dq = _flash_bwd_dq(R["q"], R["k"], R["v"], R["o"], R["lse"], dcat)
    exch = exch or (None, None)
    (dk, dv), got_a = _flash_bwd_dkv(R["q"], R["k"], R["v"], R["o"], R["lse"], dcat, exch[0])
    dqraw, dkvraw, dkr = _rope_bwd(dq, dk, dv, P["cs"], P["sn"])
    dqn = _mm(dqraw, W["w_uq"], "nt", name="mm_uq_dx")
    G["w_uq"] = _mm(R["qn"], dqraw, "tn", out_dtype=BF16, name="mm_uq_dw")
    dkvn = _mm(dkvraw, W["w_ukv"], "nt", name="mm_ukv_dx")
    G["w_ukv"] = _mm(R["kvn"], dkvraw, "tn", out_dtype=BF16, name="mm_ukv_dw")
    (dcq,), (dqnorm,) = _rowwise_vjp(_f_norm, [(R["proj"], 512, 1)], [P["q_norm"]], [[(dqn, 512, 0)]],
                                     "q_norm_bwd", [True])
    (dckv,), (dkvnorm,) = _rowwise_vjp(_f_norm, [(R["proj"], 256, 12)], [P["kv_norm"]], [[(dkvn, 256, 0)]],
                                       "kv_norm_bwd", [True])
    (dhq, dhf, dhi, dhg, dlb, don), got_b = _hg_bwd(R["proj"], R["sts"], P["lb"], P["hg_on"], dcat, exch[1])
    dproj = jnp.concatenate([dua + dub, dcq, dhq, dhf, dhi, dhg, dckv, dkr], axis=-1)
    dh1 = _mm(dproj, W["w_in"], "nt", name="mm_in_dx")
    G["w_in"] = _mm(R["h1"], dproj, "tn", out_dtype=BF16, name="mm_in_dw")
    (dx,), (dn1, dsc1, dsh1) = _rowwise_vjp(_f_pre, [(R["x"], D, 0)], [P["n1"], sc1, sh1], [[(dh1, D, 0)]],
                                            "pre_norm_bwd", [True], add_rows={0: (dxa, D, 0)})
    dmod = jnp.concatenate([dsh1, dsc1, dg1, dsh2, dsc2, dg2], axis=-1)
    small = dict(n1=dn1, n2=dn2, n3=dn3, n4=dn4, s5_d=dd, q_norm=dqnorm, kv_norm=dkvnorm,
                 lb=dlb, hg_on=jnp.sum(don.reshape(HG_H, HG_D), axis=0, keepdims=True),
                 conv_w=dcw, conv_b=dcb, bd=dbd, cdt=dcdt, acc=jnp.sum(acc, axis=0, keepdims=True))
    return dx, dmod, G, small, (got_a, got_b)


def _cols_from_shards(g):
    return jnp.transpose(g, (1, 0, 2)).reshape(g.shape[1], -1)


def _cols_to_shards(w):
    K = w.shape[0]
    return jnp.transpose(w.reshape(K, N_DEV, -1), (1, 0, 2))


def _pad_ff(w):
    zeros = jnp.zeros(w.shape[:-1] + (D_FFP - D_FF,), w.dtype)
    parts = []
    for j in range(CONV_NC):
        for half in (0, D_FF):
            lo, hi = j * CONV_TC, min((j + 1) * CONV_TC, D_FF)
            parts.append(w[..., half + lo:half + hi])
            if hi - lo < CONV_TC:
                parts.append(zeros)
    return jnp.concatenate(parts, axis=-1)


def _unpad_ff(w):
    parts = []
    for half in (0, CONV_TC):
        for j in range(CONV_NC):
            lo = 2 * j * CONV_TC + half
            parts.append(w[..., lo:lo + min(CONV_TC, D_FF - j * CONV_TC)])
    return jnp.concatenate(parts, axis=-1)


def _assemble(gathered):
    w_in = _cols_from_shards(gathered["w_in"])
    w_in = jnp.concatenate([w_in[:, 0:1024], w_in[:, 1344:D_IN], w_in[:, 1024:1344],
                            jnp.zeros((w_in.shape[0], D_INP - D_IN), w_in.dtype)], axis=1)
    w_uq = _cols_from_shards(gathered["w_uq"]).reshape(-1, MLA_H, MLA_NOPE + MLA_ROPE)
    w_uq = jnp.pad(w_uq, ((0, 0), (0, 0), (0, MLA_HW - MLA_NOPE - MLA_ROPE))).reshape(-1, MLA_H * MLA_HW)
    w_down = gathered["w_down"].reshape(D_FF, D_MODEL)
    return dict(
        w_in=w_in,
        w_glu=gathered["w_glu"].reshape(S5_W, S5_W),
        w_uq=w_uq,
        w_ukv=_cols_from_shards(gathered["w_ukv"]),
        w_out=gathered["w_out"].reshape(D_MODEL, D_MODEL),
        w_up=_pad_ff(_cols_from_shards(gathered["w_up"])),
        w_down=jnp.pad(w_down, ((0, D_FFP - D_FF), (0, 0))),
    )


def _grad_shards(G):
    w_in = jnp.concatenate([G["w_in"][:, 0:1024], G["w_in"][:, 3072:D_IN], G["w_in"][:, 1024:3072]], axis=1)
    w_uq = G["w_uq"].reshape(-1, MLA_H, MLA_HW)[:, :, :MLA_NOPE + MLA_ROPE].reshape(-1, MLA_H * (MLA_NOPE + MLA_ROPE))
    return dict(
        w_in=_cols_to_shards(w_in),
        w_glu=G["w_glu"].reshape(N_DEV, -1, S5_W),
        w_uq=_cols_to_shards(w_uq),
        w_ukv=_cols_to_shards(G["w_ukv"]),
        w_out=G["w_out"].reshape(N_DEV, -1, D_MODEL),
        w_up=_cols_to_shards(_unpad_ff(G["w_up"])),
        w_down=G["w_down"][:D_FF].reshape(N_DEV, -1, D_MODEL),
    )


_BIG = ("w_in", "w_glu", "w_uq", "w_ukv", "w_out", "w_up", "w_down")
_HOST_A = ("w_in", "w_glu", "w_uq", "w_ukv", "w_out", "ffn_conv_w")
_HOST_B = ("w_up", "w_down")
_GRAD_A = ("w_up",)
_GRAD_B = ("w_in", "w_glu", "w_uq", "w_ukv", "w_out", "w_down")
_SMALL = ("s5_lambda_re", "s5_lambda_im", "s5_log_dt", "s5_b_re", "s5_b_im", "s5_c_re", "s5_c_im", "s5_d",
          "mla_q_norm", "mla_kv_norm", "hg_lb_logits", "hg_out_norm", "mix_pre_norm", "mix_post_norm",
          "ffn_pre_norm", "ffn_post_norm", "ffn_conv_w_full", "ffn_conv_b", "b_ada")
PACK_ROW = 1024


def _pack(parts):
    flat = jnp.concatenate([p.reshape(-1) for p in parts])
    n = flat.shape[0]
    pad = (-n) % (8 * PACK_ROW)
    return jnp.pad(flat, (0, pad)).reshape(-1, PACK_ROW)


def _unpack(packed, shapes):
    flat = packed.reshape(-1)
    out, pos = [], 0
    for s in shapes:
        n = int(np.prod(s))
        out.append(flat[pos:pos + n].reshape(s))
        pos += n
    return out


def _step(x, c, positions, loss_target, w, m, v):
    x = x[0]
    S = x.shape[0]
    L = w["w_in"].shape[0]
    D = D_MODEL
    me = 4 * lax.axis_index("x") + 2 * lax.axis_index("y") + lax.axis_index("c")

    (c_all,) = _exchange([c], False, "gather_c")
    c_all = c_all.reshape(N_DEV, D)
    (c_act,) = _rowwise(lambda a: (a * jax.nn.sigmoid(a),), [(c_all, D, 0)], [], [(D, F32)], "silu_c", tm=N_DEV)
    mod_part = jnp.stack([_mm(c_act, w["w_ada"][l], "nn", name="mm_ada") for l in range(L)])
    (mod_all,) = _exchange([mod_part], False, "gather_mod")
    mod_mine = lax.dynamic_index_in_dim(mod_all, me, axis=2, keepdims=False)
    mod_full = jnp.transpose(mod_mine, (1, 0, 2)).reshape(L, 6 * D) + w["b_ada"]
    mods = [[mod_full[l:l + 1, i * D:(i + 1) * D] for i in range(6)] for l in range(L)]

    cs, sn = _rope_tables(positions[0])
    lower, lower_vjp = jax.vjp(lambda lg: jnp.cumsum(jax.nn.softmax(lg, axis=0), axis=0)
                               - jax.nn.softmax(lg, axis=0)[0:1], w["hg_lb_logits"])
    conv_w_full = []

    def weight_shards(l, names):
        return [w[n][l] if n == "ffn_conv_w" else w[n][l].astype(BF16) for n in names]

    def layer_params(l, gathered):
        Wl = _assemble(gathered)
        cw_full = _cols_from_shards(gathered["ffn_conv_w"])
        s5_args = (w["s5_lambda_re"][l], w["s5_lambda_im"][l], w["s5_log_dt"][l], w["s5_b_re"][l], w["s5_b_im"][l],
                   w["s5_c_re"][l], w["s5_c_im"][l])
        (lbr, lbi, bd, cdt), prep_vjp = jax.vjp(_s5_prep, *s5_args)
        tab_fwd, tab_rev = _s5_tables(*s5_args[:3])
        Wl["bd"], Wl["cdt"] = bd.astype(BF16), cdt.astype(BF16)
        row = lambda a: a.reshape(1, -1)
        Pl = dict(
            n1=row(w["mix_pre_norm"][l]), n2=row(w["mix_post_norm"][l]), n3=row(w["ffn_pre_norm"][l]),
            n4=row(w["ffn_post_norm"][l]), s5_d=row(w["s5_d"][l]), q_norm=row(w["mla_q_norm"][l]),
            kv_norm=row(w["mla_kv_norm"][l]), lb=row(lower[l]), hg_on=row(w["hg_out_norm"][l]),
            conv_w=_pad_ff(cw_full), conv_b=_pad_ff(row(w["ffn_conv_b"][l])),
            tab_fwd=tab_fwd, tab_rev=tab_rev, cs=cs, sn=sn, lam_bar=(lbr, lbi))
        return Wl, Pl, prep_vjp, cw_full

    Ws, Ps, preps, Rs = [], [], [], []
    h = x
    gathered = dict(zip(_HOST_A + _HOST_B, _exchange(weight_shards(0, _HOST_A + _HOST_B), False, "gather_weights")))
    for l in range(L):
        Wl, Pl, prep_vjp, cw_full = layer_params(l, gathered)
        Ws.append(Wl)
        Ps.append(Pl)
        preps.append(prep_vjp)
        conv_w_full.append(cw_full)
        nxt = ((weight_shards(l + 1, _HOST_A), False), (weight_shards(l + 1, _HOST_B), False)) if l + 1 < L else None
        h, R, (got_a, got_b) = _layer_fwd(h, mods[l], Wl, Pl, nxt)
        gathered = dict(zip(_HOST_A + _HOST_B, got_a + got_b))
        Rs.append(R)
    loss_local, dh = _loss_head(h, loss_target[0])
    loss = lax.psum(loss_local, ("x", "y", "c"))

    big_recv = {n: [None] * L for n in _BIG}
    small_g = {n: [None] * L for n in _SMALL if n != "hg_lb_logits"}
    dlower = [None] * L
    pending = None
    for l in reversed(range(L)):
        dh, dmod, G, sm, (got_a, got_b) = _layer_bwd(dh, mods[l], Ws[l], Ps[l], Rs[l], pending)
        Rs[l] = None
        for n, r in zip(_GRAD_A + _GRAD_B, got_a + got_b):
            big_recv[n][l + 1] = r
        shards = _grad_shards(G)
        pending = (([shards[n] for n in _GRAD_A], True), ([shards[n] for n in _GRAD_B], True))
        lbr, lbi = Ps[l]["lam_bar"]
        ar, ai = _ri_split(sm["acc"])
        dl = lax.complex(ar, ai) / lax.complex(lbr, -lbi)
        d_s5 = preps[l]((jnp.real(dl), jnp.imag(dl), sm["bd"], sm["cdt"]))
        for n, g in zip(("s5_lambda_re", "s5_lambda_im", "s5_log_dt", "s5_b_re", "s5_b_im", "s5_c_re", "s5_c_im"), d_s5):
            small_g[n][l] = g
        small_g["s5_d"][l] = sm["s5_d"][0]
        small_g["mla_q_norm"][l] = sm["q_norm"][0]
        small_g["mla_kv_norm"][l] = sm["kv_norm"][0]
        small_g["hg_out_norm"][l] = sm["hg_on"][0]
        small_g["mix_pre_norm"][l] = sm["n1"][0]
        small_g["mix_post_norm"][l] = sm["n2"][0]
        small_g["ffn_pre_norm"][l] = sm["n3"][0]
        small_g["ffn_post_norm"][l] = sm["n4"][0]
        small_g["ffn_conv_w_full"][l] = _unpad_ff(sm["conv_w"])
        small_g["ffn_conv_b"][l] = _unpad_ff(sm["conv_b"])[0]
        small_g["b_ada"][l] = dmod[0]
        dlower[l] = sm["lb"][0]
    for n, r in zip(_GRAD_A + _GRAD_B, _exchange(pending[0][0] + pending[1][0], True, "scatter_grads")):
        big_recv[n][0] = r
    small_g = {n: jnp.stack(gl) for n, gl in small_g.items()}
    (small_g["hg_lb_logits"],) = lower_vjp(jnp.stack(dlower))

    small_w = {n: w[n] for n in _SMALL if n != "ffn_conv_w_full"}
    small_w["ffn_conv_w_full"] = jnp.stack(conv_w_full)
    shapes = [small_w[n].shape for n in _SMALL]
    zeros_cw = jnp.zeros_like(small_w["ffn_conv_w_full"])
    pk_g = _pack([small_g[n] for n in _SMALL])
    pk_w = _pack([small_w[n] for n in _SMALL])
    pk_m = _pack([zeros_cw if n == "ffn_conv_w_full" else m[n] for n in _SMALL])
    pk_v = _pack([zeros_cw + 1.0 if n == "ffn_conv_w_full" else v[n] for n in _SMALL])
    (pk_all,) = _exchange([pk_g], False, "gather_small")
    sg, sd, sm_, sv = _adamw(pk_all[None], pk_w[None], pk_m[None], pk_v[None], name="adamw_small")
    small_out = {}
    for key, arr in (("g", sg), ("d", sd), ("m", sm_), ("v", sv)):
        small_out[key] = dict(zip(_SMALL, _unpack(arr[0], shapes)))

    n_cw = w["ffn_conv_w"].shape[-1]
    g_cw = lax.dynamic_slice_in_dim(small_out["g"]["ffn_conv_w_full"], me * n_cw, n_cw, axis=2)
    cw_out = _adamw(g_cw[:, None], w["ffn_conv_w"], m["ffn_conv_w"], v["ffn_conv_w"], name="adamw_conv_w")

    n_ada = w["w_ada"].shape[-1]
    flat_all = pk_all.reshape(N_DEV, -1)
    off = sum(int(np.prod(s)) for s in shapes[:-1])
    dmod_all = flat_all[:, off:off + L * 6 * D].reshape(N_DEV, L, 6 * D)
    dmod_cols = lax.dynamic_slice_in_dim(dmod_all, me * n_ada, n_ada, axis=2)
    g_ada = jnp.stack([_mm(c_act, dmod_cols[:, l], "tn", name="mm_ada_dw") for l in range(L)])
    ada_out = _adamw(g_ada[:, None], w["w_ada"], m["w_ada"], v["w_ada"], name="adamw_ada")

    big_out = {}
    for n in _BIG:
        recv = jnp.stack(big_recv[n])
        big_out[n] = _adamw(recv, w[n], m[n], v[n], name="adamw_" + n)
    big_out["w_ada"] = ada_out
    big_out["ffn_conv_w"] = cw_out
    return loss, dh[None], big_out, small_out


_WEIGHTS = ("w_in", "s5_lambda_re", "s5_lambda_im", "s5_log_dt", "s5_b_re", "s5_b_im", "s5_c_re", "s5_c_im", "s5_d",
            "s5_w_glu", "mla_q_norm", "mla_w_uq", "mla_kv_norm", "mla_w_ukv", "hg_lb_logits", "hg_out_norm", "w_out",
            "mix_pre_norm", "mix_post_norm", "ffn_pre_norm", "ffn_post_norm", "ffn_w_up", "ffn_conv_w", "ffn_conv_b",
            "ffn_w_down", "w_ada", "b_ada")
_ALIAS = {"s5_w_glu": "w_glu", "mla_w_uq": "w_uq", "mla_w_ukv": "w_ukv", "ffn_w_up": "w_up", "ffn_w_down": "w_down"}


def kernel(x, c, positions, w_in, s5_lambda_re, s5_lambda_im, s5_log_dt, s5_b_re, s5_b_im, s5_c_re, s5_c_im, s5_d, s5_w_glu, mla_q_norm, mla_w_uq, mla_kv_norm, mla_w_ukv, hg_lb_logits, hg_out_norm, w_out, mix_pre_norm, mix_post_norm, ffn_pre_norm, ffn_post_norm, ffn_w_up, ffn_conv_w, ffn_conv_b, ffn_w_down, w_ada, b_ada, loss_target, m_w_in, m_s5_lambda_re, m_s5_lambda_im, m_s5_log_dt, m_s5_b_re, m_s5_b_im, m_s5_c_re, m_s5_c_im, m_s5_d, m_s5_w_glu, m_mla_q_norm, m_mla_w_uq, m_mla_kv_norm, m_mla_w_ukv, m_hg_lb_logits, m_hg_out_norm, m_w_out, m_mix_pre_norm, m_mix_post_norm, m_ffn_pre_norm, m_ffn_post_norm, m_ffn_w_up, m_ffn_conv_w, m_ffn_conv_b, m_ffn_w_down, m_w_ada, m_b_ada, v_w_in, v_s5_lambda_re, v_s5_lambda_im, v_s5_log_dt, v_s5_b_re, v_s5_b_im, v_s5_c_re, v_s5_c_im, v_s5_d, v_s5_w_glu, v_mla_q_norm, v_mla_w_uq, v_mla_kv_norm, v_mla_w_ukv, v_hg_lb_logits, v_hg_out_norm, v_w_out, v_mix_pre_norm, v_mix_post_norm, v_ffn_pre_norm, v_ffn_post_norm, v_ffn_w_up, v_ffn_conv_w, v_ffn_conv_b, v_ffn_w_down, v_w_ada, v_b_ada):
    args = locals()
    key = lambda n: _ALIAS.get(n, n)
    w = {key(n): args[n] for n in _WEIGHTS}
    m = {key(n): args["m_" + n] for n in _WEIGHTS}
    v = {key(n): args["v_" + n] for n in _WEIGHTS}
    loss, grad_x, big, small = _step(x, c, positions, loss_target, w, m, v)

    def pick(n, idx):
        k = key(n)
        if k in big:
            return big[k][idx].reshape(w[k].shape)
        return small["gdmv"[idx]][k]

    outs = [loss, grad_x]
    for idx in range(4):
        outs += [pick(n, idx) for n in _WEIGHTS]
    return tuple(outs)
```

```python
import functools
import math

import numpy as np
import jax
import jax.numpy as jnp
from jax import lax
from jax.experimental import pallas as pl
from jax.experimental.pallas import tpu as pltpu

F32 = jnp.float32
BF16 = jnp.bfloat16
N_DEV = 8
V7X_VMEM_LIMIT = 56 * 1024 * 1024
MM_VMEM_BUDGET = 28 * 1024 * 1024
LANE = 128

D_MODEL = 2048
S5_W = 512
S5_G = 32
S5_C = 16
S5_P = 64
S5_N = S5_G * S5_P
S5_TL = 512
MLA_H = 8
MLA_NOPE = 128
MLA_ROPE = 64
MLA_V = 128
MLA_HW = 256
HG_H = 4
HG_D = 128
HG_CH = 16
D_FF = 5504
D_FFP = 5632
D_IN = 3392
D_INP = 3456
EPS = 1e-6
MASK_VALUE = -1e30
ROPE_THETA = 10000.0
ATT_SCALE = (MLA_NOPE + MLA_ROPE) ** -0.5

ADAM_LR = 0.001
ADAM_B1 = 0.9
ADAM_B2 = 0.999
ADAM_EPS = 1e-08
ADAM_WD = 0.01
ADAM_STEP = 10

_IN_PERM = np.concatenate([np.arange(0, 1024), np.arange(1344, 3392), np.arange(1024, 1344)])
_IN_INV = np.argsort(_IN_PERM)

_NN = (((1,), (0,)), ((), ()))
_NT = (((1,), (1,)), ((), ()))
_TN = (((0,), (0,)), ((), ()))


def _cp(*sem):
    return pltpu.CompilerParams(dimension_semantics=sem, vmem_limit_bytes=V7X_VMEM_LIMIT)


def _tile(n, cap, align=LANE):
    if n <= cap:
        return n
    t = (cap // align) * align
    while t >= align:
        if n % t == 0:
            return t
        t -= align
    raise ValueError(f"no tile for {n}")


def _bdot(a, b, dims):
    return lax.dot_general(a.astype(BF16), b.astype(BF16), dims, preferred_element_type=F32)


def _mm(a, b, mode, out_dtype=F32, a_col0=0, a_cols=None, name="mm", exch=None):
    if mode == "tn":
        K = a.shape[0]
        M = a_cols if a_cols is not None else a.shape[1]
        N = b.shape[1]
    else:
        M = a.shape[0]
        K = a_cols if a_cols is not None else a.shape[1]
        N = b.shape[0] if mode == "nt" else b.shape[1]
    tm = _tile(M, 1024, 8 if M < LANE else LANE)
    tn = _tile(N, 1024)
    kal = 8 if K < LANE else LANE
    tk = _tile(K, 2048, kal)
    osz = jnp.dtype(out_dtype).itemsize

    def vmem(tk_):
        acc = 0 if tk_ == K else tm * tn * 4
        return 2 * tk_ * (tm * a.dtype.itemsize + tn * b.dtype.itemsize) + 2 * tm * tn * osz + acc

    while vmem(tk) > MM_VMEM_BUDGET and tk > kal:
        tk = _tile(K, tk - kal, kal)
    nk = K // tk
    if mode == "tn":
        assert a_col0 % tm == 0
        a_spec = pl.BlockSpec((tk, tm), lambda i, j, k: (k, i + a_col0 // tm))
        b_spec = pl.BlockSpec((tk, tn), lambda i, j, k: (k, j))
        dims = _TN
    else:
        assert a_col0 % tk == 0
        a_spec = pl.BlockSpec((tm, tk), lambda i, j, k: (i, k + a_col0 // tk))
        if mode == "nn":
            b_spec = pl.BlockSpec((tk, tn), lambda i, j, k: (k, j))
            dims = _NN
        else:
            b_spec = pl.BlockSpec((tn, tk), lambda i, j, k: (j, k))
            dims = _NT

    if nk == 1:
        def body(a_ref, b_ref, o_ref):
            o_ref[...] = _bdot(a_ref[...], b_ref[...], dims).astype(o_ref.dtype)

        scratch = []
    else:
        def body(a_ref, b_ref, o_ref, acc_ref):
            k = pl.program_id(2)

            @pl.when(k == 0)
            def _():
                acc_ref[...] = jnp.zeros_like(acc_ref)

            acc_ref[...] += _bdot(a_ref[...], b_ref[...], dims)

            @pl.when(k == nk - 1)
            def _():
                o_ref[...] = acc_ref[...].astype(o_ref.dtype)

        scratch = [pltpu.VMEM((tm, tn), F32)]

    (out,), got = _call_hosting(
        body, name, (M // tm, N // tn, nk), [a_spec, b_spec], [pl.BlockSpec((tm, tn), lambda i, j, k: (i, j))],
        [jax.ShapeDtypeStruct((M, N), out_dtype)], [a, b], exch,
        scratch=scratch, sem=["parallel", "parallel", "arbitrary"])
    return out if exch is None else (out, got)


def _row_spec(tm, width, cb):
    return pl.BlockSpec((tm, width), lambda i: (i, cb))


def _rowwise(fn, rows, params, outs, name, tm=256):
    S = rows[0][0].shape[0]
    nr, npar = len(rows), len(params)

    def body(*refs):
        xs = [r[...].astype(F32) for r in refs[:nr]]
        ps = [p[...] for p in refs[nr:nr + npar]]
        res = fn(*xs, *ps)
        for o, r in zip(refs[nr + npar:], res):
            o[...] = r.astype(o.dtype)

    res = pl.pallas_call(
        body,
        name=name,
        grid=(S // tm,),
        in_specs=[_row_spec(tm, w, cb) for _, w, cb in rows]
        + [pl.BlockSpec(p.shape, lambda i: (0, 0)) for p in params],
        out_specs=[_row_spec(tm, w, 0) for w, _ in outs],
        out_shape=[jax.ShapeDtypeStruct((S, w), dt) for w, dt in outs],
        compiler_params=_cp("parallel"),
    )(*[r[0] for r in rows], *params)
    return list(res)


def _rowwise_vjp(fn, rows, params, cts, name, row_grads, add_rows=None, tm=256):
    S = rows[0][0].shape[0]
    add_rows = add_rows or {}
    nr, npar = len(rows), len(params)
    flat_cts = [c for group in cts for c in group]
    ncts = len(flat_cts)
    add_keys = sorted(add_rows)
    nadd = len(add_keys)
    grad_idx = [i for i in range(nr) if row_grads[i]]

    def body(*refs):
        i = pl.program_id(0)
        xs = [r[...].astype(F32) for r in refs[:nr]]
        ps = [p[...] for p in refs[nr:nr + npar]]
        ct_refs = refs[nr + npar:nr + npar + ncts]
        add_refs = refs[nr + npar + ncts:nr + npar + ncts + nadd]
        out_refs = refs[nr + npar + ncts + nadd:]
        ct_vals, pos = [], 0
        for group in cts:
            v = ct_refs[pos][...].astype(F32)
            for r in ct_refs[pos + 1:pos + len(group)]:
                v = v + r[...].astype(F32)
            pos += len(group)
            ct_vals.append(v)
        _, vjp = jax.vjp(lambda *a: tuple(fn(*a)), *xs, *ps)
        grads = vjp(tuple(ct_vals))
        for o, gi in zip(out_refs[:len(grad_idx)], grad_idx):
            g = grads[gi]
            if gi in add_rows:
                g = g + add_refs[add_keys.index(gi)][...].astype(F32)
            o[...] = g.astype(o.dtype)
        dprefs = out_refs[len(grad_idx):]

        @pl.when(i == 0)
        def _():
            for dp in dprefs:
                dp[...] = jnp.zeros_like(dp)

        for dp, g in zip(dprefs, grads[nr:]):
            dp[...] += g

    res = pl.pallas_call(
        body,
        name=name,
        grid=(S // tm,),
        in_specs=[_row_spec(tm, w, cb) for _, w, cb in rows]
        + [pl.BlockSpec(p.shape, lambda i: (0, 0)) for p in params]
        + [_row_spec(tm, w, cb) for _, w, cb in flat_cts]
        + [_row_spec(tm, add_rows[k][1], add_rows[k][2]) for k in add_keys],
        out_specs=[_row_spec(tm, rows[gi][1], 0) for gi in grad_idx]
        + [pl.BlockSpec(p.shape, lambda i: (0, 0)) for p in params],
        out_shape=[jax.ShapeDtypeStruct((S, rows[gi][1]), F32) for gi in grad_idx]
        + [jax.ShapeDtypeStruct(p.shape, F32) for p in params],
        compiler_params=_cp("arbitrary"),
    )(*[r[0] for r in rows], *params, *[c[0] for c in flat_cts], *[add_rows[k][0] for k in add_keys])
    res = list(res)
    return res[:len(grad_idx)], res[len(grad_idx):]


def _rms(x, gain):
    return x * lax.rsqrt(jnp.mean(x * x, axis=-1, keepdims=True) + EPS) * gain


def _f_pre(x, gain, sc, sh):
    return (_rms(x, gain) * (1.0 + sc) + sh,)


def _f_post(x, y, gain, g):
    return (x + g * _rms(y, gain),)


def _f_norm(x, gain):
    return (_rms(x, gain),)


def _f_s5a(yc, u, d):
    return (jax.nn.gelu(yc + d * u, approximate=True),)


def _f_s5b(g, z):
    return (g * jax.nn.sigmoid(z),)


def _loss_head(y, target, tm=256):
    S, D = y.shape

    def body(y_ref, t_ref, dy_ref, acc_ref):
        i = pl.program_id(0)
        e = y_ref[...] - t_ref[...]
        dy_ref[...] = e * (1.0 / D)

        @pl.when(i == 0)
        def _():
            acc_ref[...] = jnp.zeros_like(acc_ref)

        acc_ref[...] += jnp.sum(e * e, axis=0, keepdims=True)

    dy, acc = pl.pallas_call(
        body,
        name="loss_head",
        grid=(S // tm,),
        in_specs=[_row_spec(tm, D, 0), _row_spec(tm, D, 0)],
        out_specs=[_row_spec(tm, D, 0), pl.BlockSpec((1, D), lambda i: (0, 0))],
        out_shape=[jax.ShapeDtypeStruct((S, D), F32), jax.ShapeDtypeStruct((1, D), F32)],
        compiler_params=_cp("arbitrary"),
    )(y, target)
    return 0.5 * jnp.sum(acc) / D, dy


def _s5_tile_scan(xr, xi, tab_ref, reverse, row8):
    for k in (1, 2, 4):
        pr = tab_ref[pl.ds(k - 1, 1), 0:S5_TL] if not reverse else tab_ref[pl.ds(8 - k, 1), 0:S5_TL]
        pi = tab_ref[pl.ds(k - 1, 1), S5_TL:2 * S5_TL] if not reverse else tab_ref[pl.ds(8 - k, 1), S5_TL:2 * S5_TL]
        if not reverse:
            keep = row8 >= k
            sr = jnp.where(keep, pltpu.roll(xr, k, 0), 0.0)
            si = jnp.where(keep, pltpu.roll(xi, k, 0), 0.0)
        else:
            keep = row8 < 8 - k
            sr = jnp.where(keep, pltpu.roll(xr, 8 - k, 0), 0.0)
            si = jnp.where(keep, pltpu.roll(xi, 8 - k, 0), 0.0)
        xr, xi = xr + pr * sr - pi * si, xi + pr * si + pi * sr
    return xr, xi


def _s5_scan(bu, tab, reverse=False, h=None, bu_fwd=None, tr=512):
    S = bu.shape[0]
    tr = min(tr, S)
    nl = S5_N // S5_TL
    nrb = S // tr
    w = 2 * S5_TL
    nt = tr // 8
    rmap = (lambda j, i: (i, j)) if not reverse else (lambda j, i: (nrb - 1 - i, j))

    def body(*refs):
        if reverse:
            x_ref, tab_ref, h_ref, b_ref, o_ref, acc_ref, cr_ref, ci_ref = refs
        else:
            x_ref, tab_ref, o_ref, cr_ref, ci_ref = refs
        i = pl.program_id(1)
        row8 = lax.broadcasted_iota(jnp.int32, (8, S5_TL), 0)

        @pl.when(i == 0)
        def _():
            cr_ref[...] = jnp.zeros_like(cr_ref)
            ci_ref[...] = jnp.zeros_like(ci_ref)
            if reverse:
                acc_ref[...] = jnp.zeros_like(acc_ref)

        tr_all = tab_ref[:, 0:S5_TL]
        ti_all = tab_ref[:, S5_TL:w]

        def tile(t, carry):
            tt = (nt - 1 - t) if reverse else t
            r = pl.ds(pl.multiple_of(tt * 8, 8), 8)
            xr, xi = _s5_tile_scan(x_ref[r, 0:S5_TL], x_ref[r, S5_TL:w], tab_ref, reverse, row8)
            cr = jnp.broadcast_to(cr_ref[...], (8, S5_TL))
            ci = jnp.broadcast_to(ci_ref[...], (8, S5_TL))
            hr = xr + tr_all * cr - ti_all * ci
            hi = xi + tr_all * ci + ti_all * cr
            o_ref[r, 0:S5_TL] = hr
            o_ref[r, S5_TL:w] = hi
            edge = pl.ds(tt * 8, 1) if reverse else pl.ds(tt * 8 + 7, 1)
            cr_ref[...] = o_ref[edge, 0:S5_TL]
            ci_ref[...] = o_ref[edge, S5_TL:w]
            if reverse:
                dr = h_ref[r, 0:S5_TL] - b_ref[r, 0:S5_TL]
                di = h_ref[r, S5_TL:w] - b_ref[r, S5_TL:w]
                acc_ref[:, 0:S5_TL] += hr * dr + hi * di
                acc_ref[:, S5_TL:w] += hi * dr - hr * di
            return carry

        lax.fori_loop(0, nt, tile, 0)

    blk = pl.BlockSpec((tr, w), rmap)
    in_specs = [blk, pl.BlockSpec((8, w), lambda j, i: (0, j))]
    out_specs = [blk]
    out_shape = [jax.ShapeDtypeStruct((S, 2 * S5_N), F32)]
    args = [bu, tab]
    if reverse:
        in_specs += [blk, blk]
        args += [h, bu_fwd]
        out_specs.append(pl.BlockSpec((8, w), lambda j, i: (0, j)))
        out_shape.append(jax.ShapeDtypeStruct((8, 2 * S5_N), F32))
    res = pl.pallas_call(
        body,
        name="s5_scan_bwd" if reverse else "s5_scan_fwd",
        grid=(nl, nrb),
        in_specs=in_specs,
        out_specs=out_specs,
        out_shape=out_shape,
        scratch_shapes=[pltpu.VMEM((1, S5_TL), F32), pltpu.VMEM((1, S5_TL), F32)],
        compiler_params=_cp("parallel", "arbitrary"),
    )(*args)
    return res if reverse else res[0]


def _ri_cols(re, im):
    lead = re.shape[:-1]
    nl = S5_N // S5_TL
    z = jnp.stack([re.reshape(*lead, nl, S5_TL), im.reshape(*lead, nl, S5_TL)], axis=-2)
    return z.reshape(*lead, 2 * S5_N)


def _ri_split(z):
    lead = z.shape[:-1]
    nl = S5_N // S5_TL
    z = z.reshape(*lead, nl, 2, S5_TL)
    return z[..., 0, :].reshape(*lead, S5_N), z[..., 1, :].reshape(*lead, S5_N)


def _s5_prep(lre, lim, logdt, bre, bim, cre, cim):
    lam = lax.complex(lre, lim)
    dt = jnp.exp(logdt)[:, None]
    lam_bar = jnp.exp(lam * dt)
    b = lax.complex(bre, bim)
    b_bar = ((lam_bar - 1.0) / lam)[..., None] * b
    eye = jnp.eye(S5_G, dtype=F32)
    bd_re = jnp.einsum("gpc,gh->gchp", jnp.real(b_bar), eye).reshape(S5_W, S5_N)
    bd_im = jnp.einsum("gpc,gh->gchp", jnp.imag(b_bar), eye).reshape(S5_W, S5_N)
    bd = _ri_cols(bd_re, bd_im)
    cd_re = jnp.einsum("gcp,gh->gchp", cre, eye).reshape(S5_W, S5_N)
    cd_im = jnp.einsum("gcp,gh->gchp", -cim, eye).reshape(S5_W, S5_N)
    cdt = _ri_cols(cd_re, cd_im)
    return jnp.real(lam_bar).reshape(1, S5_N), jnp.imag(lam_bar).reshape(1, S5_N), bd, cdt


def _s5_tables(lre, lim, logdt):
    lam = lax.complex(lre, lim)
    dt = jnp.exp(logdt)[:, None]
    k = jnp.arange(1, 9, dtype=F32)[:, None, None]
    pw = jnp.exp((lam * dt)[None] * k).reshape(8, S5_N)
    fwd = _ri_cols(jnp.real(pw), jnp.imag(pw))
    rev = _ri_cols(jnp.real(pw)[::-1], -jnp.imag(pw)[::-1])
    return fwd, rev


def _rope_tables(positions):
    inv_freq = 1.0 / (ROPE_THETA ** (jnp.arange(0, MLA_ROPE, 2, dtype=F32) / MLA_ROPE))
    ang = positions.astype(F32)[:, None] * inv_freq
    cos, sin = jnp.cos(ang), jnp.sin(ang)
    z = jnp.zeros_like(cos)
    cs = jnp.concatenate([cos, cos, z, z], axis=-1)
    sn = jnp.concatenate([-sin, sin, z, z], axis=-1)
    return cs, sn


def _rope_fwd(qraw, kvraw, proj, cs, sn, tm=256):
    S = qraw.shape[0]
    HW = MLA_H * MLA_HW

    def rope(x, c, s):
        lane = lax.broadcasted_iota(jnp.int32, x.shape, 1)
        sw = jnp.where(lane < 32, pltpu.roll(x, 96, 1), jnp.where(lane < 64, pltpu.roll(x, 32, 1), 0.0))
        return x * c + sw * s

    def body(q_ref, kv_ref, kr_ref, cs_ref, sn_ref, qo_ref, ko_ref, vo_ref):
        c, s = cs_ref[...], sn_ref[...]
        kr = rope(kr_ref[...], c, s).astype(BF16)
        for h in range(MLA_H):
            o = h * MLA_HW
            qo_ref[:, o:o + 128] = q_ref[:, o:o + 128].astype(BF16)
            qo_ref[:, o + 128:o + 256] = rope(q_ref[:, o + 128:o + 256], c, s).astype(BF16)
            ko_ref[:, o:o + 128] = kv_ref[:, o:o + 128].astype(BF16)
            ko_ref[:, o + 128:o + 256] = kr
            vo_ref[:, h * 128:(h + 1) * 128] = kv_ref[:, o + 128:o + 256].astype(BF16)

    return pl.pallas_call(
        body,
        name="rope_fwd",
        grid=(S // tm,),
        in_specs=[_row_spec(tm, HW, 0), _row_spec(tm, HW, 0), _row_spec(tm, 128, (D_INP - 128) // 128),
                  _row_spec(tm, 128, 0), _row_spec(tm, 128, 0)],
        out_specs=[_row_spec(tm, HW, 0), _row_spec(tm, HW, 0), _row_spec(tm, MLA_H * MLA_V, 0)],
        out_shape=[jax.ShapeDtypeStruct((S, HW), BF16), jax.ShapeDtypeStruct((S, HW), BF16),
                   jax.ShapeDtypeStruct((S, MLA_H * MLA_V), BF16)],
        compiler_params=_cp("parallel"),
    )(qraw, kvraw, proj, cs, sn)


def _rope_bwd(dq, dk, dv, cs, sn, tm=256):
    S = dq.shape[0]
    HW = MLA_H * MLA_HW

    def rope_t(x, c, s):
        lane = lax.broadcasted_iota(jnp.int32, x.shape, 1)
        w = x * s
        sw = jnp.where(lane < 32, pltpu.roll(w, 96, 1), jnp.where(lane < 64, pltpu.roll(w, 32, 1), 0.0))
        return x * c + sw

    def body(dq_ref, dk_ref, dv_ref, cs_ref, sn_ref, qo_ref, kvo_ref, kro_ref):
        c, s = cs_ref[...], sn_ref[...]
        kr = jnp.zeros((tm, 128), F32)
        for h in range(MLA_H):
            o = h * MLA_HW
            qo_ref[:, o:o + 128] = dq_ref[:, o:o + 128]
            qo_ref[:, o + 128:o + 256] = rope_t(dq_ref[:, o + 128:o + 256], c, s)
            kvo_ref[:, o:o + 128] = dk_ref[:, o:o + 128]
            kvo_ref[:, o + 128:o + 256] = dv_ref[:, h * 128:(h + 1) * 128]
            kr = kr + dk_ref[:, o + 128:o + 256]
        kro_ref[...] = rope_t(kr, c, s)

    return pl.pallas_call(
        body,
        name="rope_bwd",
        grid=(S // tm,),
        in_specs=[_row_spec(tm, HW, 0), _row_spec(tm, HW, 0), _row_spec(tm, MLA_H * MLA_V, 0),
                  _row_spec(tm, 128, 0), _row_spec(tm, 128, 0)],
        out_specs=[_row_spec(tm, HW, 0), _row_spec(tm, HW, 0), _row_spec(tm, 128, 0)],
        out_shape=[jax.ShapeDtypeStruct((S, HW), F32), jax.ShapeDtypeStruct((S, HW), F32),
                   jax.ShapeDtypeStruct((S, 128), F32)],
        compiler_params=_cp("parallel"),
    )(dq, dk, dv, cs, sn)


ATT_T = 512


def _diag_mask(t):
    return lax.broadcasted_iota(jnp.int32, (t, t), 1) <= lax.broadcasted_iota(jnp.int32, (t, t), 0)


def _flash_fwd(q, k, v, exch=None):
    S = q.shape[0]
    t = min(ATT_T, S)
    nq = S // t

    def body(q_ref, k_ref, v_ref, o_ref, lse_ref):
        i = pl.program_id(1)
        qb = q_ref[...]

        def step(j, carry, masked):
            m, l, acc = carry
            r = pl.ds(pl.multiple_of(j * t, t), t)
            s = _bdot(qb, k_ref[r, :], _NT) * ATT_SCALE
            if masked:
                s = jnp.where(_diag_mask(t), s, MASK_VALUE)
            m_new = jnp.maximum(m, jnp.max(s, axis=-1, keepdims=True))
            alpha = jnp.exp(m - m_new)
            p = jnp.exp(s - m_new)
            l = alpha * l + jnp.sum(p, axis=-1, keepdims=True)
            acc = alpha * acc + _bdot(p, v_ref[r, :], _NN)
            return m_new, l, acc

        m0 = jnp.full((t, 1), MASK_VALUE, F32)
        init = (m0, jnp.zeros((t, 1), F32), jnp.zeros((t, MLA_V), F32))
        m, l, acc = step(i, lax.fori_loop(0, i, lambda j, c: step(j, c, False), init), True)
        o_ref[...] = acc / l
        lse_ref[...] = jnp.broadcast_to(m + jnp.log(l), (t, 128))

    return _call_hosting(
        body, "flash_fwd", (MLA_H, nq),
        [pl.BlockSpec((t, MLA_HW), lambda h, i: (i, h)),
         pl.BlockSpec((S, MLA_HW), lambda h, i: (0, h)),
         pl.BlockSpec((S, MLA_V), lambda h, i: (0, h))],
        [pl.BlockSpec((t, MLA_V), lambda h, i: (i, h)), pl.BlockSpec((t, 128), lambda h, i: (i, h))],
        [jax.ShapeDtypeStruct((S, MLA_H * MLA_V), F32), jax.ShapeDtypeStruct((S, MLA_H * 128), F32)],
        [q, k, v], exch)


def _flash_bwd_dq(q, k, v, o, lse, dcat, exch=None):
    S = q.shape[0]
    t = min(ATT_T, S)
    nq = S // t
    do_cb = S5_W // MLA_V

    def body(q_ref, k_ref, v_ref, o_ref, lse_ref, do_ref, dq_ref):
        i = pl.program_id(1)
        qb = q_ref[...]
        do = do_ref[...]
        delta = jnp.sum(do * o_ref[...], axis=-1, keepdims=True)
        lse1 = jnp.max(lse_ref[...], axis=-1, keepdims=True)
        dob = do.astype(BF16)

        def step(j, dq, masked):
            r = pl.ds(pl.multiple_of(j * t, t), t)
            kb = k_ref[r, :]
            s = _bdot(qb, kb, _NT) * ATT_SCALE
            p = jnp.exp(s - lse1)
            if masked:
                p = jnp.where(_diag_mask(t), p, 0.0)
            dp = _bdot(dob, v_ref[r, :], _NT)
            ds = p * (dp - delta) * ATT_SCALE
            return dq + _bdot(ds, kb, _NN)

        dq = lax.fori_loop(0, i, lambda j, c: step(j, c, False), jnp.zeros((t, MLA_HW), F32))
        dq_ref[...] = step(i, dq, True)

    (dq,), got = _call_hosting(
        body, "flash_bwd_dq", (MLA_H, nq),
        [pl.BlockSpec((t, MLA_HW), lambda h, i: (i, h)),
         pl.BlockSpec((S, MLA_HW), lambda h, i: (0, h)),
         pl.BlockSpec((S, MLA_V), lambda h, i: (0, h)),
         pl.BlockSpec((t, MLA_V), lambda h, i: (i, h)),
         pl.BlockSpec((t, 128), lambda h, i: (i, h)),
         pl.BlockSpec((t, MLA_V), lambda h, i: (i, do_cb + h))],
        [pl.BlockSpec((t, MLA_HW), lambda h, i: (i, h))], [jax.ShapeDtypeStruct((S, MLA_H * MLA_HW), F32)],
        [q, k, v, o, lse, dcat], exch)
    return dq, got


def _flash_bwd_dkv(q, k, v, o, lse, dcat, exch=None):
    S = q.shape[0]
    t = min(ATT_T, S)
    nq = S // t
    do_cb = S5_W // MLA_V

    def body(q_ref, k_ref, v_ref, o_ref, lse_ref, do_ref, dk_ref, dv_ref):
        j = pl.program_id(1)
        kb = k_ref[...]
        vb = v_ref[...]

        def step(i, carry, masked):
            dk, dv = carry
            r = pl.ds(pl.multiple_of(i * t, t), t)
            qb = q_ref[r, :]
            do = do_ref[r, :]
            delta = jnp.sum(do * o_ref[r, :], axis=-1, keepdims=True)
            lse1 = jnp.max(lse_ref[r, :], axis=-1, keepdims=True)
            s = _bdot(qb, kb, _NT) * ATT_SCALE
            p = jnp.exp(s - lse1)
            if masked:
                p = jnp.where(_diag_mask(t), p, 0.0)
            dob = do.astype(BF16)
            dv = dv + _bdot(p, dob, _TN)
            dp = _bdot(dob, vb, _NT)
            ds = p * (dp - delta) * ATT_SCALE
            dk = dk + _bdot(ds, qb, _TN)
            return dk, dv

        first = step(j, (jnp.zeros((t, MLA_HW), F32), jnp.zeros((t, MLA_V), F32)), True)
        dk, dv = lax.fori_loop(j + 1, nq, lambda i, c: step(i, c, False), first)
        dk_ref[...] = dk
        dv_ref[...] = dv

    return _call_hosting(
        body, "flash_bwd_dkv", (MLA_H, nq),
        [pl.BlockSpec((S, MLA_HW), lambda h, j: (0, h)),
         pl.BlockSpec((t, MLA_HW), lambda h, j: (j, h)),
         pl.BlockSpec((t, MLA_V), lambda h, j: (j, h)),
         pl.BlockSpec((S, MLA_V), lambda h, j: (0, h)),
         pl.BlockSpec((S, 128), lambda h, j: (0, h)),
         pl.BlockSpec((S, MLA_V), lambda h, j: (0, do_cb + h))],
        [pl.BlockSpec((t, MLA_HW), lambda h, j: (j, h)), pl.BlockSpec((t, MLA_V), lambda h, j: (j, h))],
        [jax.ShapeDtypeStruct((S, MLA_H * MLA_HW), F32), jax.ShapeDtypeStruct((S, MLA_H * MLA_V), F32)],
        [q, k, v, o, lse, dcat], exch)


def _split3(x):
    x1 = x.astype(BF16)
    r1 = x - x1.astype(F32)
    x2 = r1.astype(BF16)
    x3 = (r1 - x2.astype(F32)).astype(BF16)
    return x1, x2, x3


def _tri_matmul(x, upper):
    n = x.shape[0]
    r = lax.broadcasted_iota(jnp.int32, (n, n), 0)
    c = lax.broadcasted_iota(jnp.int32, (n, n), 1)
    tri = jnp.where((r <= c) if upper else (r >= c), 1.0, 0.0).astype(BF16)
    x1, x2, x3 = _split3(x)
    dot = lambda v: lax.dot_general(tri, v, _NN, preferred_element_type=F32)
    return dot(x1) + dot(x2) + dot(x3)


@jax.custom_vjp
def _cumsum_rows(x):
    return _tri_matmul(x, False)


def _cumsum_rows_fwd(x):
    return _tri_matmul(x, False), None


def _cumsum_rows_bwd(_, ct):
    return (_tri_matmul(ct, True),)


_cumsum_rows.defvjp(_cumsum_rows_fwd, _cumsum_rows_bwd)


def _hg_step(qin, fin, vin, gin, st, lb, on):
    n = qin.shape[0]
    sig = jax.nn.sigmoid(fin)
    g = jnp.log(lb + (1.0 - lb) * sig)
    k = (1.0 - lb) * jax.nn.sigmoid(-fin)
    q = qin * jax.nn.sigmoid(qin)
    b = _cumsum_rows(g)
    o = _bdot(q * jnp.exp(b), st, _NT)
    row = lax.broadcasted_iota(jnp.int32, (n, HG_D), 0)
    row1 = lax.broadcasted_iota(jnp.int32, (n, 1), 0)
    b_s = None
    for s in range(n):
        sel = row == s
        b_s = jnp.sum(jnp.where(sel, b, 0.0), axis=0, keepdims=True)
        k_s = jnp.sum(jnp.where(sel, k, 0.0), axis=0, keepdims=True)
        v_s = jnp.sum(jnp.where(sel, vin, 0.0), axis=0, keepdims=True)
        e = jnp.exp(jnp.minimum(b - b_s, 0.0))
        c = jnp.sum(q * e * k_s, axis=-1, keepdims=True)
        o = o + jnp.where(row1 >= s, c, 0.0) * v_s
    st_new = st * jnp.exp(b_s) + _bdot(vin, k * jnp.exp(b_s - b), _TN)
    y = _rms(o, on) * (gin * jax.nn.sigmoid(gin))
    return y, st_new


HG_W = HG_H * HG_D


def _hg_specs(tb, nb, reverse):
    rm = (lambda i: nb - 1 - i) if reverse else (lambda i: i)
    base = 1024 // HG_W
    return [pl.BlockSpec((tb, HG_W), lambda i, o=o: (rm(i), base + o)) for o in range(4)], rm


def _head(h):
    return slice(h * HG_D, (h + 1) * HG_D)


def _hg_fwd(proj, lb, on, exch=None, tb=256):
    S = proj.shape[0]
    nb = S // tb
    nc = tb // HG_CH
    in_specs, rm = _hg_specs(tb, nb, False)

    def body(q_ref, f_ref, v_ref, g_ref, lb_ref, on_ref, y_ref, sts_ref, st_ref):
        @pl.when(pl.program_id(0) == 0)
        def _():
            st_ref[...] = jnp.zeros_like(st_ref)

        def step(c, carry):
            r = pl.ds(pl.multiple_of(c * HG_CH, HG_CH), HG_CH)
            for h in range(HG_H):
                hs = _head(h)
                st = st_ref[h]
                sts_ref[h, c] = st
                y, st_new = _hg_step(q_ref[r, hs], f_ref[r, hs], v_ref[r, hs], g_ref[r, hs], st, lb_ref[:, hs],
                                     on_ref[...])
                y_ref[r, hs] = y
                st_ref[h] = st_new
            return carry

        lax.fori_loop(0, nc, step, 0)

    return _call_hosting(
        body, "hgrn2_fwd", (nb,),
        in_specs + [pl.BlockSpec((1, HG_W), lambda i: (0, 0)), pl.BlockSpec((1, HG_D), lambda i: (0, 0))],
        [pl.BlockSpec((tb, HG_W), lambda i: (i, 0)), pl.BlockSpec((HG_H, nc, HG_D, HG_D), lambda i: (0, i, 0, 0))],
        [jax.ShapeDtypeStruct((S, HG_W), F32), jax.ShapeDtypeStruct((HG_H, S // HG_CH, HG_D, HG_D), F32)],
        [proj, proj, proj, proj, lb, on], exch,
        scratch=[pltpu.VMEM((HG_H, HG_D, HG_D), F32)], sem=["arbitrary"])


def _hg_bwd(proj, sts, lb, on, dcat, exch=None, tb=256):
    S = proj.shape[0]
    nb = S // tb
    nc = tb // HG_CH
    in_specs, rm = _hg_specs(tb, nb, True)
    dy_cb = (S5_W + MLA_H * MLA_V) // HG_W

    def body(q_ref, f_ref, v_ref, g_ref, lb_ref, on_ref, sts_ref, dy_ref,
             dq_ref, df_ref, dv_ref, dg_ref, dlb_ref, don_ref, dst_ref):
        @pl.when(pl.program_id(0) == 0)
        def _():
            dst_ref[...] = jnp.zeros_like(dst_ref)
            dlb_ref[...] = jnp.zeros_like(dlb_ref)
            don_ref[...] = jnp.zeros_like(don_ref)

        def step(cc, carry):
            c = nc - 1 - cc
            r = pl.ds(pl.multiple_of(c * HG_CH, HG_CH), HG_CH)
            for h in range(HG_H):
                hs = _head(h)
                _, vjp = jax.vjp(_hg_step, q_ref[r, hs], f_ref[r, hs], v_ref[r, hs], g_ref[r, hs], sts_ref[h, c],
                                 lb_ref[:, hs], on_ref[...])
                dq, df, dv, dg, dst, dlb, don = vjp((dy_ref[r, hs], dst_ref[h]))
                dq_ref[r, hs] = dq
                df_ref[r, hs] = df
                dv_ref[r, hs] = dv
                dg_ref[r, hs] = dg
                dst_ref[h] = dst
                dlb_ref[:, hs] += dlb
                don_ref[:, hs] += don
            return carry

        lax.fori_loop(0, nc, step, 0)

    blk = pl.BlockSpec((tb, HG_W), lambda i: (rm(i), 0))
    par = pl.BlockSpec((1, HG_W), lambda i: (0, 0))
    return _call_hosting(
        body, "hgrn2_bwd", (nb,),
        in_specs + [par, pl.BlockSpec((1, HG_D), lambda i: (0, 0)),
                    pl.BlockSpec((HG_H, nc, HG_D, HG_D), lambda i: (0, rm(i), 0, 0)),
                    pl.BlockSpec((tb, HG_W), lambda i: (rm(i), dy_cb))],
        [blk, blk, blk, blk, par, par],
        [jax.ShapeDtypeStruct((S, HG_W), F32)] * 4 + [jax.ShapeDtypeStruct((1, HG_W), F32)] * 2,
        [proj, proj, proj, proj, lb, on, sts, dcat], exch,
        scratch=[pltpu.VMEM((HG_H, HG_D, HG_D), F32)], sem=["arbitrary"])


CONV_NC = 4
CONV_TC = D_FFP // CONV_NC


def _shift_down(cur, halo, k):
    tm = cur.shape[0]
    row = lax.broadcasted_iota(jnp.int32, cur.shape, 0)
    top = jnp.concatenate([pltpu.roll(halo, k, 0), jnp.zeros((tm - 8, cur.shape[1]), F32)], axis=0)
    return jnp.where(row < k, top, pltpu.roll(cur, k, 0))


def _shift_up(cur, halo, k):
    tm = cur.shape[0]
    row = lax.broadcasted_iota(jnp.int32, cur.shape, 0)
    bot = jnp.concatenate([jnp.zeros((tm - 8, cur.shape[1]), F32), pltpu.roll(halo, 8 - k, 0)], axis=0)
    return jnp.where(row >= tm - k, bot, pltpu.roll(cur, tm - k, 0))


def _conv3(cur, halo, w_ref, b_ref):
    return (b_ref[...] + _shift_down(cur, halo, 2) * w_ref[pl.ds(0, 1), :]
            + _shift_down(cur, halo, 1) * w_ref[pl.ds(1, 1), :] + cur * w_ref[pl.ds(2, 1), :])


def _conv_fwd(u0, cw, cb, exch=None, tm=256):
    S = u0.shape[0]
    nc = CONV_NC
    m8 = tm // 8
    prev = lambda i: jnp.maximum(i * m8 - 1, 0)

    def body(u_ref, p_ref, w_ref, b_ref, a_ref):
        on = (pl.program_id(1) > 0).astype(F32)
        u = _conv3(u_ref[...], p_ref[...] * on, w_ref, b_ref)
        a_ref[...] = (jax.nn.gelu(u[:, :CONV_TC], approximate=True) * u[:, CONV_TC:]).astype(a_ref.dtype)

    tc = CONV_TC
    (a,), got = _call_hosting(
        body, "conv_geglu_fwd", (nc, S // tm),
        [pl.BlockSpec((tm, 2 * tc), lambda j, i: (i, j)), pl.BlockSpec((8, 2 * tc), lambda j, i: (prev(i), j)),
         pl.BlockSpec((3, 2 * tc), lambda j, i: (0, j)), pl.BlockSpec((1, 2 * tc), lambda j, i: (0, j))],
        [pl.BlockSpec((tm, tc), lambda j, i: (i, j))], [jax.ShapeDtypeStruct((S, D_FFP), BF16)],
        [u0, u0, cw, cb], exch)
    return a, got


def _conv_bwd_elem(da, u0, cw, cb, tm=256):
    S = u0.shape[0]
    nc = CONV_NC
    m8 = tm // 8
    prev = lambda i: jnp.maximum(i * m8 - 1, 0)

    def body(u_ref, p_ref, w_ref, b_ref, da_ref, du_ref, dw_ref, db_ref):
        i = pl.program_id(1)
        cur = u_ref[...]
        halo = p_ref[...] * (i > 0).astype(F32)
        u = _conv3(cur, halo, w_ref, b_ref)
        _, vjp = jax.vjp(lambda a, b: jax.nn.gelu(a, approximate=True) * b, u[:, :CONV_TC], u[:, CONV_TC:])
        dug, duv = vjp(da_ref[...])
        du = jnp.concatenate([dug, duv], axis=1)
        du_ref[...] = du

        @pl.when(i == 0)
        def _():
            dw_ref[...] = jnp.zeros_like(dw_ref)
            db_ref[...] = jnp.zeros_like(db_ref)

        dw_ref[pl.ds(0, 1), :] += jnp.sum(du * _shift_down(cur, halo, 2), axis=0, keepdims=True)
        dw_ref[pl.ds(1, 1), :] += jnp.sum(du * _shift_down(cur, halo, 1), axis=0, keepdims=True)
        dw_ref[pl.ds(2, 1), :] += jnp.sum(du * cur, axis=0, keepdims=True)
        db_ref[...] += jnp.sum(du, axis=0, keepdims=True)

    tc = CONV_TC
    return pl.pallas_call(
        body,
        name="conv_geglu_bwd",
        grid=(nc, S // tm),
        in_specs=[pl.BlockSpec((tm, 2 * tc), lambda j, i: (i, j)), pl.BlockSpec((8, 2 * tc), lambda j, i: (prev(i), j)),
                  pl.BlockSpec((3, 2 * tc), lambda j, i: (0, j)), pl.BlockSpec((1, 2 * tc), lambda j, i: (0, j)),
                  pl.BlockSpec((tm, tc), lambda j, i: (i, j))],
        out_specs=[pl.BlockSpec((tm, 2 * tc), lambda j, i: (i, j)), pl.BlockSpec((3, 2 * tc), lambda j, i: (0, j)),
                   pl.BlockSpec((1, 2 * tc), lambda j, i: (0, j))],
        out_shape=[jax.ShapeDtypeStruct((S, 2 * D_FFP), F32), jax.ShapeDtypeStruct((3, 2 * D_FFP), F32),
                   jax.ShapeDtypeStruct((1, 2 * D_FFP), F32)],
        compiler_params=_cp("parallel", "arbitrary"),
    )(u0, u0, cw, cb, da)


def _conv_bwd_input(du, cw, tm=256):
    S = du.shape[0]
    nrb = S // tm
    m8 = tm // 8
    nxt = lambda i: jnp.minimum((i + 1) * m8, S // 8 - 1)

    def body(d_ref, n_ref, w_ref, o_ref):
        on = (pl.program_id(1) < nrb - 1).astype(F32)
        cur = d_ref[...]
        halo = n_ref[...] * on
        o_ref[...] = (cur * w_ref[pl.ds(2, 1), :] + _shift_up(cur, halo, 1) * w_ref[pl.ds(1, 1), :]
                      + _shift_up(cur, halo, 2) * w_ref[pl.ds(0, 1), :])

    tc = 2 * CONV_TC
    return pl.pallas_call(
        body,
        name="conv_bwd_input",
        grid=(CONV_NC, nrb),
        in_specs=[pl.BlockSpec((tm, tc), lambda j, i: (i, j)), pl.BlockSpec((8, tc), lambda j, i: (nxt(i), j)),
                  pl.BlockSpec((3, tc), lambda j, i: (0, j))],
        out_specs=pl.BlockSpec((tm, tc), lambda j, i: (i, j)),
        out_shape=jax.ShapeDtypeStruct((S, 2 * D_FFP), F32),
        compiler_params=_cp("parallel", "parallel"),
    )(du, du, cw)


def _exchange(arrs, scatter, name):
    n = len(arrs)

    def body(*refs):
        args = (refs[:n], refs[n:2 * n], *refs[2 * n:], scatter)
        _exchange_start(*args)
        _exchange_wait(*args)

    hbm = pl.BlockSpec(memory_space=pltpu.HBM)
    out_shape, sems = _exchange_shapes(arrs, scatter)
    return pl.pallas_call(
        body,
        name=name,
        in_specs=[hbm] * n,
        out_specs=[hbm] * n,
        out_shape=out_shape,
        scratch_shapes=sems,
    )(*arrs)


def _exchange_copies(ins, outs, send, recv, loc, scatter):
    x, y, c = lax.axis_index("x"), lax.axis_index("y"), lax.axis_index("c")
    me = 4 * x + 2 * y + c
    sends, recvs, locs = [], [], []
    for a in range(len(ins)):
        locs.append(pltpu.make_async_copy(ins[a].at[me] if scatter else ins[a], outs[a].at[me], loc.at[a]))
        for k in range(1, N_DEV):
            px = 1 - x if k & 4 else x
            py = 1 - y if k & 2 else y
            pc = 1 - c if k & 1 else c
            peer = 4 * px + 2 * py + pc
            src = ins[a].at[peer] if scatter else ins[a]
            sems = dict(send_sem=send.at[a, k - 1], recv_sem=recv.at[a, k - 1], device_id=(px, py, pc),
                        device_id_type=pl.DeviceIdType.MESH)
            sends.append(pltpu.make_async_remote_copy(src_ref=src, dst_ref=outs[a].at[me], **sems))
            recvs.append(pltpu.make_async_remote_copy(src_ref=src, dst_ref=outs[a].at[peer], **sems))
    return locs, sends, recvs


def _exchange_start(*refs):
    locs, sends, _ = _exchange_copies(*refs)
    for cp in locs + sends:
        cp.start()


def _exchange_wait(*refs):
    locs, sends, recvs = _exchange_copies(*refs)
    for cp in recvs:
        cp.wait_recv()
    for cp in sends:
        cp.wait_send()
    for cp in locs:
        cp.wait()


def _exchange_shapes(arrs, scatter):
    n = len(arrs)
    out_shape = [jax.ShapeDtypeStruct(a.shape if scatter else (N_DEV,) + a.shape, a.dtype) for a in arrs]
    sems = [pltpu.SemaphoreType.DMA((n, N_DEV - 1)), pltpu.SemaphoreType.DMA((n, N_DEV - 1)),
            pltpu.SemaphoreType.DMA((n,))]
    return out_shape, sems


def _call_hosting(body, name, grid, in_specs, out_specs, out_shape, args, exch, scratch=(), sem=None):
    scratch = list(scratch)
    if exch is None:
        res = pl.pallas_call(body, name=name, grid=grid, in_specs=in_specs, out_specs=out_specs, out_shape=out_shape,
                             scratch_shapes=scratch,
                             compiler_params=_cp(*(sem or ["parallel"] * len(grid))))(*args)
        return list(res), []
    arrs, scatter = exch
    n, n_in, n_out, n_scr = len(arrs), len(in_specs), len(out_specs), len(scratch)
    hbm = pl.BlockSpec(memory_space=pltpu.HBM)
    x_shape, sems = _exchange_shapes(arrs, scatter)

    def hosting_body(*refs):
        cin, xin = refs[:n_in], refs[n_in:n_in + n]
        cout, xout = refs[n_in + n:n_in + n + n_out], refs[n_in + n + n_out:n_in + 2 * n + n_out]
        cscr = refs[n_in + 2 * n + n_out:n_in + 2 * n + n_out + n_scr]
        xsem = refs[n_in + 2 * n + n_out + n_scr:]
        ids = [pl.program_id(d) for d in range(len(grid))]
        first = functools.reduce(jnp.logical_and, [i == 0 for i in ids])
        last = functools.reduce(jnp.logical_and, [i == g - 1 for i, g in zip(ids, grid)])

        @pl.when(first)
        def _():
            _exchange_start(xin, xout, *xsem, scatter)

        body(*cin, *cout, *cscr)

        @pl.when(last)
        def _():
            _exchange_wait(xin, xout, *xsem, scatter)

    res = pl.pallas_call(
        hosting_body, name=name + "_x", grid=grid, in_specs=in_specs + [hbm] * n, out_specs=out_specs + [hbm] * n,
        out_shape=out_shape + x_shape, scratch_shapes=scratch + sems,
        compiler_params=_cp(*["arbitrary"] * len(grid)))(*args, *arrs)
    return list(res[:n_out]), list(res[n_out:])


class _Riders:
    def __init__(self):
        self.make, self.done = {}, {}

    def add(self, host, make, done):
        self.make[host], self.done[host] = make, done

    def give(self, host, ctx=None):
        return self.make[host](ctx) if host in self.make else None

    def take(self, host, got):
        if host in self.done:
            self.done[host](got)


def _adamw(recv, w, m, v, name="adamw"):
    L, n, R, C = recv.shape
    fits = [t for t in range(8, R + 1, 8) if R % t == 0 and t * C * 4 <= (1 << 19)]
    tr = max(fits) if fits else R

    def body(r_ref, w_ref, m_ref, v_ref, g_ref, d_ref, mo_ref, vo_ref):
        g = r_ref[0, 0].astype(F32)
        for d in range(1, n):
            g = g + r_ref[0, d].astype(F32)
        mm = ADAM_B1 * m_ref[0] + (1.0 - ADAM_B1) * g
        vv = ADAM_B2 * v_ref[0] + (1.0 - ADAM_B2) * (g * g)
        m_hat = mm / (1.0 - ADAM_B1 ** ADAM_STEP)
        v_hat = vv / (1.0 - ADAM_B2 ** ADAM_STEP)
        g_ref[0] = g
        d_ref[0] = -ADAM_LR * (m_hat / (jnp.sqrt(v_hat) + ADAM_EPS) + ADAM_WD * w_ref[0])
        mo_ref[0] = mm
        vo_ref[0] = vv

    blk = pl.BlockSpec((1, tr, C), lambda l, i: (l, i, 0))
    return pl.pallas_call(
        body,
        name=name,
        grid=(L, R // tr),
        in_specs=[pl.BlockSpec((1, n, tr, C), lambda l, i: (l, 0, i, 0)), blk, blk, blk],
        out_specs=[blk] * 4,
        out_shape=[jax.ShapeDtypeStruct((L, R, C), F32)] * 4,
        compiler_params=_cp("parallel", "parallel"),
    )(recv, w, m, v)


def _layer_fwd(x, mod, W, P, riders=None):
    riders = riders or _Riders()
    sh1, sc1, g1, sh2, sc2, g2 = mod
    D = D_MODEL
    R = {"x": x}
    (h1,) = _rowwise(_f_pre, [(x, D, 0)], [P["n1"], sc1, sh1], [(D, BF16)], "pre_norm")
    proj = _mm(h1, W["w_in"], "nn", name="mm_in")
    R["h1"], R["proj"] = h1, proj
    bu = _mm(proj, W["bd"], "nn", a_col0=0, a_cols=S5_W, name="mm_s5_b")
    hs = _s5_scan(bu, P["tab_fwd"])
    yc = _mm(hs, W["cdt"], "nt", name="mm_s5_c")
    (gg,) = _rowwise(_f_s5a, [(yc, S5_W, 0), (proj, S5_W, 0)], [P["s5_d"]], [(S5_W, F32)], "s5_gelu")
    z = _mm(gg, W["w_glu"], "nn", name="mm_glu")
    (ys5,) = _rowwise(_f_s5b, [(gg, S5_W, 0), (z, S5_W, 0)], [], [(S5_W, BF16)], "s5_glu")
    R.update(bu=bu, hs=hs, yc=yc, gg=gg, z=z)
    (qn,) = _rowwise(_f_norm, [(proj, 512, 1)], [P["q_norm"]], [(512, BF16)], "q_norm")
    (kvn,) = _rowwise(_f_norm, [(proj, 256, 12)], [P["kv_norm"]], [(256, BF16)], "kv_norm")
    qraw = _mm(qn, W["w_uq"], "nn", name="mm_uq")
    kvraw = _mm(kvn, W["w_ukv"], "nn", name="mm_ukv")
    q, k, v = _rope_fwd(qraw, kvraw, proj, P["cs"], P["sn"])
    (o, lse), got = _flash_fwd(q, k, v, riders.give("flash_fwd"))
    riders.take("flash_fwd", got)
    R.update(qn=qn, kvn=kvn, q=q, k=k, v=v, o=o, lse=lse)
    (yhg, sts), got = _hg_fwd(proj, P["lb"], P["hg_on"], riders.give("hgrn2_fwd"))
    riders.take("hgrn2_fwd", got)
    R["sts"] = sts
    cat = jnp.concatenate([ys5, o.astype(BF16), yhg.astype(BF16)], axis=-1)
    mixed = _mm(cat, W["w_out"], "nn", name="mm_out")
    (x2,) = _rowwise(_f_post, [(x, D, 0), (mixed, D, 0)], [P["n2"], g1], [(D, F32)], "post_norm")
    R.update(cat=cat, mixed=mixed, x2=x2)
    (h2,) = _rowwise(_f_pre, [(x2, D, 0)], [P["n3"], sc2, sh2], [(D, BF16)], "pre_norm")
    rider = riders.give("mm_up")
    u0 = _mm(h2, W["w_up"], "nn", name="mm_up", exch=rider)
    if rider is not None:
        u0, got = u0
        riders.take("mm_up", got)
    a, got = _conv_fwd(u0, P["conv_w"], P["conv_b"], riders.give("conv_fwd"))
    riders.take("conv_fwd", got)
    y = _mm(a, W["w_down"], "nn", name="mm_down")
    (x3,) = _rowwise(_f_post, [(x2, D, 0), (y, D, 0)], [P["n4"], g2], [(D, F32)], "post_norm")
    R.update(h2=h2, u0=u0, a=a, y=y)
    return x3, R


def _layer_bwd(dx3, mod, W, P, R, riders=None):
    riders = riders or _Riders()
    sh1, sc1, g1, sh2, sc2, g2 = mod
    D = D_MODEL
    G = {}
    (dx2a, dy), (dn4, dg2) = _rowwise_vjp(_f_post, [(R["x2"], D, 0), (R["y"], D, 0)], [P["n4"], g2],
                                          [[(dx3, D, 0)]], "post_norm_bwd", [True, True])
    da = _mm(dy, W["w_down"], "nt", name="mm_down_dx")
    G["w_down"] = _mm(R["a"], dy, "tn", out_dtype=BF16, name="mm_down_dw")
    du, dcw, dcb = _conv_bwd_elem(da, R["u0"], P["conv_w"], P["conv_b"])
    du0 = _conv_bwd_input(du, P["conv_w"])
    dh2 = _mm(du0, W["w_up"], "nt", name="mm_up_dx")
    G["w_up"] = _mm(R["h2"], du0, "tn", out_dtype=BF16, name="mm_up_dw")
    (dx2,), (dn3, dsc2, dsh2) = _rowwise_vjp(_f_pre, [(R["x2"], D, 0)], [P["n3"], sc2, sh2], [[(dh2, D, 0)]],
                                             "pre_norm_bwd", [True], add_rows={0: (dx2a, D, 0)})
    (dxa, dmixed), (dn2, dg1) = _rowwise_vjp(_f_post, [(R["x"], D, 0), (R["mixed"], D, 0)], [P["n2"], g1],
                                             [[(dx2, D, 0)]], "post_norm_bwd", [True, True])
    dcat = _mm(dmixed, W["w_out"], "nt", name="mm_out_dx")
    G["w_out"] = _mm(R["cat"], dmixed, "tn", out_dtype=BF16, name="mm_out_dw")
    (dga, dz), _ = _rowwise_vjp(_f_s5b, [(R["gg"], S5_W, 0), (R["z"], S5_W, 0)], [], [[(dcat, S5_W, 0)]],
                                "s5_glu_bwd", [True, True])
    dgb = _mm(dz, W["w_glu"], "nt", name="mm_glu_dx")
    G["w_glu"] = _mm(R["gg"], dz, "tn", out_dtype=BF16, name="mm_glu_dw")
    (dyc, dua), (dd,) = _rowwise_vjp(_f_s5a, [(R["yc"], S5_W, 0), (R["proj"], S5_W, 0)], [P["s5_d"]],
                                     [[(dga, S5_W, 0), (dgb, S5_W, 0)]], "s5_gelu_bwd", [True, True])
    dhs = _mm(dyc, W["cdt"], "nn", name="mm_s5_c_dx")
    dcdt = _mm(dyc, R["hs"], "tn", name="mm_s5_c_dw")
    gs, acc = _s5_scan(dhs, P["tab_rev"], reverse=True, h=R["hs"], bu_fwd=R["bu"])
    dub = _mm(gs, W["bd"], "nt", name="mm_s5_b_dx")
    dbd = _mm(R["proj"], gs, "tn", a_col0=0, a_cols=S5_W, name="mm_s5_b_dw")
    dq, got = _flash_bwd_dq(R["q"], R["k"], R["v"], R["o"], R["lse"], dcat, riders.give("flash_bwd_dq", G))
    riders.take("flash_bwd_dq", got)
    (dk, dv), got = _flash_bwd_dkv(R["q"], R["k"], R["v"], R["o"], R["lse"], dcat, riders.give("flash_bwd_dkv", G))
    riders.take("flash_bwd_dkv", got)
    dqraw, dkvraw, dkr = _rope_bwd(dq, dk, dv, P["cs"], P["sn"])
    dqn = _mm(dqraw, W["w_uq"], "nt", name="mm_uq_dx")
    G["w_uq"] = _mm(R["qn"], dqraw, "tn", out_dtype=BF16, name="mm_uq_dw")
    dkvn = _mm(dkvraw, W["w_ukv"], "nt", name="mm_ukv_dx")
    G["w_ukv"] = _mm(R["kvn"], dkvraw, "tn", out_dtype=BF16, name="mm_ukv_dw")
    (dcq,), (dqnorm,) = _rowwise_vjp(_f_norm, [(R["proj"], 512, 1)], [P["q_norm"]], [[(dqn, 512, 0)]],
                                     "q_norm_bwd", [True])
    (dckv,), (dkvnorm,) = _rowwise_vjp(_f_norm, [(R["proj"], 256, 12)], [P["kv_norm"]], [[(dkvn, 256, 0)]],
                                       "kv_norm_bwd", [True])
    (dhq, dhf, dhi, dhg, dlb, don), got = _hg_bwd(R["proj"], R["sts"], P["lb"], P["hg_on"], dcat,
                                                  riders.give("hgrn2_bwd", G))
    riders.take("hgrn2_bwd", got)
    dproj = jnp.concatenate([dua + dub, dcq, dhq, dhf, dhi, dhg, dckv, dkr], axis=-1)
    dh1 = _mm(dproj, W["w_in"], "nt", name="mm_in_dx")
    G["w_in"] = _mm(R["h1"], dproj, "tn", out_dtype=BF16, name="mm_in_dw")
    (dx,), (dn1, dsc1, dsh1) = _rowwise_vjp(_f_pre, [(R["x"], D, 0)], [P["n1"], sc1, sh1], [[(dh1, D, 0)]],
                                            "pre_norm_bwd", [True], add_rows={0: (dxa, D, 0)})
    dmod = jnp.concatenate([dsh1, dsc1, dg1, dsh2, dsc2, dg2], axis=-1)
    small = dict(n1=dn1, n2=dn2, n3=dn3, n4=dn4, s5_d=dd, q_norm=dqnorm, kv_norm=dkvnorm,
                 lb=dlb, hg_on=jnp.sum(don.reshape(HG_H, HG_D), axis=0, keepdims=True),
                 conv_w=dcw, conv_b=dcb, bd=dbd, cdt=dcdt, acc=jnp.sum(acc, axis=0, keepdims=True))
    return dx, dmod, G, small


def _cols_from_shards(g):
    return jnp.transpose(g, (1, 0, 2)).reshape(g.shape[1], -1)


def _cols_to_shards(w):
    K = w.shape[0]
    return jnp.transpose(w.reshape(K, N_DEV, -1), (1, 0, 2))


FF_SHARD = 2 * D_FF // N_DEV
FF_PAD = CONV_TC - FF_SHARD


def _pad_ff(w):
    lead = w.shape[:-1]
    w = jnp.swapaxes(w.reshape(*lead, 2, CONV_NC, FF_SHARD), -3, -2)
    return jnp.pad(w, [(0, 0)] * (w.ndim - 1) + [(0, FF_PAD)]).reshape(*lead, 2 * D_FFP)


def _unpad_ff(w):
    lead = w.shape[:-1]
    w = w.reshape(*lead, CONV_NC, 2, CONV_TC)[..., :FF_SHARD]
    return jnp.swapaxes(w, -3, -2).reshape(*lead, 2 * D_FF)


def _asm_up(g):
    K = g.shape[1]
    w = jnp.transpose(g.reshape(2, CONV_NC, K, FF_SHARD), (2, 1, 0, 3))
    return jnp.pad(w, ((0, 0), (0, 0), (0, 0), (0, FF_PAD))).reshape(K, 2 * D_FFP)


def _grad_up(g):
    K = g.shape[0]
    w = g.reshape(K, CONV_NC, 2, CONV_TC)[..., :FF_SHARD]
    return jnp.transpose(w, (2, 1, 0, 3)).reshape(N_DEV, K, FF_SHARD)


def _asm_down(g):
    w = g.reshape(CONV_NC, FF_SHARD, D_MODEL)
    return jnp.pad(w, ((0, 0), (0, FF_PAD), (0, 0))).reshape(D_FFP, D_MODEL)


def _grad_down(g):
    return g.reshape(CONV_NC, CONV_TC, D_MODEL)[:, :FF_SHARD].reshape(N_DEV, -1, D_MODEL)


def _asm_in(g):
    w = _cols_from_shards(g)
    return jnp.concatenate([w[:, 0:1024], w[:, 1344:D_IN], w[:, 1024:1344],
                            jnp.zeros((w.shape[0], D_INP - D_IN), w.dtype)], axis=1)


def _asm_uq(g):
    w = _cols_from_shards(g).reshape(-1, MLA_H, MLA_NOPE + MLA_ROPE)
    return jnp.pad(w, ((0, 0), (0, 0), (0, MLA_HW - MLA_NOPE - MLA_ROPE))).reshape(-1, MLA_H * MLA_HW)


_ASSEMBLE = dict(
    w_in=_asm_in,
    w_glu=lambda g: g.reshape(S5_W, S5_W),
    w_uq=_asm_uq,
    w_ukv=_cols_from_shards,
    w_out=lambda g: g.reshape(D_MODEL, D_MODEL),
    w_up=_asm_up,
    w_down=_asm_down,
)

_GRAD_SHARDS = dict(
    w_in=lambda g: _cols_to_shards(jnp.concatenate([g[:, 0:1024], g[:, 3072:D_IN], g[:, 1024:3072]], axis=1)),
    w_glu=lambda g: g.reshape(N_DEV, -1, S5_W),
    w_uq=lambda g: _cols_to_shards(
        g.reshape(-1, MLA_H, MLA_HW)[:, :, :MLA_NOPE + MLA_ROPE].reshape(-1, MLA_H * (MLA_NOPE + MLA_ROPE))),
    w_ukv=_cols_to_shards,
    w_out=lambda g: g.reshape(N_DEV, -1, D_MODEL),
    w_up=_grad_up,
    w_down=_grad_down,
)


_BIG = ("w_in", "w_glu", "w_uq", "w_ukv", "w_out", "w_up", "w_down")
_EARLY = ("w_in", "w_glu", "w_uq", "w_ukv", "ffn_conv_w")
_SMALL = ("s5_lambda_re", "s5_lambda_im", "s5_log_dt", "s5_b_re", "s5_b_im", "s5_c_re", "s5_c_im", "s5_d",
          "mla_q_norm", "mla_kv_norm", "hg_lb_logits", "hg_out_norm", "mix_pre_norm", "mix_post_norm",
          "ffn_pre_norm", "ffn_post_norm", "ffn_conv_w_full", "ffn_conv_b", "b_ada")
PACK_ROW = 1024


def _pack(parts):
    flat = jnp.concatenate([p.reshape(-1) for p in parts])
    n = flat.shape[0]
    pad = (-n) % (8 * PACK_ROW)
    return jnp.pad(flat, (0, pad)).reshape(-1, PACK_ROW)


def _unpack(packed, shapes):
    flat = packed.reshape(-1)
    out, pos = [], 0
    for s in shapes:
        n = int(np.prod(s))
        out.append(flat[pos:pos + n].reshape(s))
        pos += n
    return out


def _step(x, c, positions, loss_target, w, m, v):
    x = x[0]
    S = x.shape[0]
    L = w["w_in"].shape[0]
    D = D_MODEL
    me = 4 * lax.axis_index("x") + 2 * lax.axis_index("y") + lax.axis_index("c")

    (c_all,) = _exchange([c], False, "gather_c")
    c_all = c_all.reshape(N_DEV, D)
    (c_act,) = _rowwise(lambda a: (a * jax.nn.sigmoid(a),), [(c_all, D, 0)], [], [(D, F32)], "silu_c", tm=N_DEV)
    mod_part = jnp.stack([_mm(c_act, w["w_ada"][l], "nn", name="mm_ada") for l in range(L)])
    (mod_all,) = _exchange([mod_part], False, "gather_mod")
    mod_mine = lax.dynamic_index_in_dim(mod_all, me, axis=2, keepdims=False)
    mod_full = jnp.transpose(mod_mine, (1, 0, 2)).reshape(L, 6 * D) + w["b_ada"]
    mods = [[mod_full[l:l + 1, i * D:(i + 1) * D] for i in range(6)] for l in range(L)]

    cs, sn = _rope_tables(positions[0])
    lower, lower_vjp = jax.vjp(lambda lg: jnp.cumsum(jax.nn.softmax(lg, axis=0), axis=0)
                               - jax.nn.softmax(lg, axis=0)[0:1], w["hg_lb_logits"])
    conv_w_full = []

    def shard(n, l):
        return w[n][l] if n == "ffn_conv_w" else w[n][l].astype(BF16)

    def layer_params(l, early):
        Wl = {n: _ASSEMBLE[n](early[n]) for n in _EARLY[:-1]}
        cw_full = _cols_from_shards(early["ffn_conv_w"])
        s5_args = (w["s5_lambda_re"][l], w["s5_lambda_im"][l], w["s5_log_dt"][l], w["s5_b_re"][l], w["s5_b_im"][l],
                   w["s5_c_re"][l], w["s5_c_im"][l])
        (lbr, lbi, bd, cdt), prep_vjp = jax.vjp(_s5_prep, *s5_args)
        tab_fwd, tab_rev = _s5_tables(*s5_args[:3])
        Wl["bd"], Wl["cdt"] = bd.astype(BF16), cdt.astype(BF16)
        row = lambda a: a.reshape(1, -1)
        Pl = dict(
            n1=row(w["mix_pre_norm"][l]), n2=row(w["mix_post_norm"][l]), n3=row(w["ffn_pre_norm"][l]),
            n4=row(w["ffn_post_norm"][l]), s5_d=row(w["s5_d"][l]), q_norm=row(w["mla_q_norm"][l]),
            kv_norm=row(w["mla_kv_norm"][l]), lb=row(lower[l]), hg_on=row(w["hg_out_norm"][l]),
            conv_w=_pad_ff(cw_full), conv_b=_pad_ff(row(w["ffn_conv_b"][l])),
            tab_fwd=tab_fwd, tab_rev=tab_rev, cs=cs, sn=sn, lam_bar=(lbr, lbi))
        return Wl, Pl, prep_vjp, cw_full

    Ws, Ps, preps, Rs = [], [], [], []
    h = x
    half = w["w_up"].shape[1] // 2
    early = dict(zip(_EARLY, _exchange([shard(n, 0) for n in _EARLY], False, "gather_weights")))
    for l in range(L):
        Wl, Pl, prep_vjp, cw_full = layer_params(l, early)
        Ws.append(Wl)
        Ps.append(Pl)
        preps.append(prep_vjp)
        conv_w_full.append(cw_full)
        riders, stash, early = _Riders(), {}, {}

        def got_flash(got, Wl=Wl, stash=stash):
            Wl["w_out"] = _ASSEMBLE["w_out"](got[0])
            stash["up"] = got[1]

        def got_hg(got, Wl=Wl, stash=stash):
            Wl["w_up"] = _ASSEMBLE["w_up"](jnp.concatenate([stash["up"], got[0]], axis=1))

        def got_up(got, Wl=Wl):
            Wl["w_down"] = _ASSEMBLE["w_down"](got[0])

        riders.add("flash_fwd", lambda _, l=l: ([shard("w_out", l), shard("w_up", l)[:half]], False), got_flash)
        riders.add("hgrn2_fwd", lambda _, l=l: ([shard("w_up", l)[half:]], False), got_hg)
        riders.add("mm_up", lambda _, l=l: ([shard("w_down", l)], False), got_up)
        if l + 1 < L:
            riders.add("conv_fwd", lambda _, l=l: ([shard(n, l + 1) for n in _EARLY], False),
                       lambda got, early=early: early.update(zip(_EARLY, got)))
        h, R = _layer_fwd(h, mods[l], Wl, Pl, riders)
        Rs.append(R)
    loss_local, dh = _loss_head(h, loss_target[0])
    loss = lax.psum(loss_local, ("x", "y", "c"))

    big_recv = {n: [None] * L for n in _BIG}
    small_g = {n: [None] * L for n in _SMALL if n != "hg_lb_logits"}
    dlower = [None] * L
    late = ("w_out", "w_glu", "w_uq", "w_ukv")
    pending_in = None
    for l in reversed(range(L)):
        riders = _Riders()

        def store(names, l=l):
            def done(got):
                for n, r in zip(names, got):
                    big_recv[n][l] = r
            return done

        def make_late(G, prev=pending_in):
            return [_GRAD_SHARDS[n](G[n]) for n in late] + ([prev[1]] if prev else []), True

        def done_late(got, l=l, prev=pending_in):
            store(late, l)(got)
            if prev:
                big_recv["w_in"][prev[0]] = got[len(late)]

        riders.add("flash_bwd_dq", lambda G: ([_GRAD_SHARDS["w_down"](G["w_down"])], True), store(("w_down",)))
        riders.add("flash_bwd_dkv", lambda G: ([_GRAD_SHARDS["w_up"](G["w_up"])], True), store(("w_up",)))
        riders.add("hgrn2_bwd", make_late, done_late)
        dh, dmod, G, sm = _layer_bwd(dh, mods[l], Ws[l], Ps[l], Rs[l], riders)
        Rs[l] = None
        pending_in = (l, _GRAD_SHARDS["w_in"](G["w_in"]))
        lbr, lbi = Ps[l]["lam_bar"]
        ar, ai = _ri_split(sm["acc"])
        dl = lax.complex(ar, ai) / lax.complex(lbr, -lbi)
        d_s5 = preps[l]((jnp.real(dl), jnp.imag(dl), sm["bd"], sm["cdt"]))
        for n, g in zip(("s5_lambda_re", "s5_lambda_im", "s5_log_dt", "s5_b_re", "s5_b_im", "s5_c_re", "s5_c_im"), d_s5):
            small_g[n][l] = g
        small_g["s5_d"][l] = sm["s5_d"][0]
        small_g["mla_q_norm"][l] = sm["q_norm"][0]
        small_g["mla_kv_norm"][l] = sm["kv_norm"][0]
        small_g["hg_out_norm"][l] = sm["hg_on"][0]
        small_g["mix_pre_norm"][l] = sm["n1"][0]
        small_g["mix_post_norm"][l] = sm["n2"][0]
        small_g["ffn_pre_norm"][l] = sm["n3"][0]
        small_g["ffn_post_norm"][l] = sm["n4"][0]
        small_g["ffn_conv_w_full"][l] = _unpad_ff(sm["conv_w"])
        small_g["ffn_conv_b"][l] = _unpad_ff(sm["conv_b"])[0]
        small_g["b_ada"][l] = dmod[0]
        dlower[l] = sm["lb"][0]
    (big_recv["w_in"][0],) = _exchange([pending_in[1]], True, "scatter_grads")
    small_g = {n: jnp.stack(gl) for n, gl in small_g.items()}
    (small_g["hg_lb_logits"],) = lower_vjp(jnp.stack(dlower))

    small_w = {n: w[n] for n in _SMALL if n != "ffn_conv_w_full"}
    small_w["ffn_conv_w_full"] = jnp.stack(conv_w_full)
    shapes = [small_w[n].shape for n in _SMALL]
    zeros_cw = jnp.zeros_like(small_w["ffn_conv_w_full"])
    pk_g = _pack([small_g[n] for n in _SMALL])
    pk_w = _pack([small_w[n] for n in _SMALL])
    pk_m = _pack([zeros_cw if n == "ffn_conv_w_full" else m[n] for n in _SMALL])
    pk_v = _pack([zeros_cw + 1.0 if n == "ffn_conv_w_full" else v[n] for n in _SMALL])
    (pk_all,) = _exchange([pk_g], False, "gather_small")
    sg, sd, sm_, sv = _adamw(pk_all[None], pk_w[None], pk_m[None], pk_v[None], name="adamw_small")
    small_out = {}
    for key, arr in (("g", sg), ("d", sd), ("m", sm_), ("v", sv)):
        small_out[key] = dict(zip(_SMALL, _unpack(arr[0], shapes)))

    n_cw = w["ffn_conv_w"].shape[-1]
    g_cw = lax.dynamic_slice_in_dim(small_out["g"]["ffn_conv_w_full"], me * n_cw, n_cw, axis=2)
    cw_out = _adamw(g_cw[:, None], w["ffn_conv_w"], m["ffn_conv_w"], v["ffn_conv_w"], name="adamw_conv_w")

    n_ada = w["w_ada"].shape[-1]
    flat_all = pk_all.reshape(N_DEV, -1)
    off = sum(int(np.prod(s)) for s in shapes[:-1])
    dmod_all = flat_all[:, off:off + L * 6 * D].reshape(N_DEV, L, 6 * D)
    dmod_cols = lax.dynamic_slice_in_dim(dmod_all, me * n_ada, n_ada, axis=2)
    g_ada = jnp.stack([_mm(c_act, dmod_cols[:, l], "tn", name="mm_ada_dw") for l in range(L)])
    ada_out = _adamw(g_ada[:, None], w["w_ada"], m["w_ada"], v["w_ada"], name="adamw_ada")

    big_out = {}
    for n in _BIG:
        recv = jnp.stack(big_recv[n])
        big_out[n] = _adamw(recv, w[n], m[n], v[n], name="adamw_" + n)
    big_out["w_ada"] = ada_out
    big_out["ffn_conv_w"] = cw_out
    return loss, dh[None], big_out, small_out


_WEIGHTS = ("w_in", "s5_lambda_re", "s5_lambda_im", "s5_log_dt", "s5_b_re", "s5_b_im", "s5_c_re", "s5_c_im", "s5_d",
            "s5_w_glu", "mla_q_norm", "mla_w_uq", "mla_kv_norm", "mla_w_ukv", "hg_lb_logits", "hg_out_norm", "w_out",
            "mix_pre_norm", "mix_post_norm", "ffn_pre_norm", "ffn_post_norm", "ffn_w_up", "ffn_conv_w", "ffn_conv_b",
            "ffn_w_down", "w_ada", "b_ada")
_ALIAS = {"s5_w_glu": "w_glu", "mla_w_uq": "w_uq", "mla_w_ukv": "w_ukv", "ffn_w_up": "w_up", "ffn_w_down": "w_down"}


def kernel(x, c, positions, w_in, s5_lambda_re, s5_lambda_im, s5_log_dt, s5_b_re, s5_b_im, s5_c_re, s5_c_im, s5_d, s5_w_glu, mla_q_norm, mla_w_uq, mla_kv_norm, mla_w_ukv, hg_lb_logits, hg_out_norm, w_out, mix_pre_norm, mix_post_norm, ffn_pre_norm, ffn_post_norm, ffn_w_up, ffn_conv_w, ffn_conv_b, ffn_w_down, w_ada, b_ada, loss_target, m_w_in, m_s5_lambda_re, m_s5_lambda_im, m_s5_log_dt, m_s5_b_re, m_s5_b_im, m_s5_c_re, m_s5_c_im, m_s5_d, m_s5_w_glu, m_mla_q_norm, m_mla_w_uq, m_mla_kv_norm, m_mla_w_ukv, m_hg_lb_logits, m_hg_out_norm, m_w_out, m_mix_pre_norm, m_mix_post_norm, m_ffn_pre_norm, m_ffn_post_norm, m_ffn_w_up, m_ffn_conv_w, m_ffn_conv_b, m_ffn_w_down, m_w_ada, m_b_ada, v_w_in, v_s5_lambda_re, v_s5_lambda_im, v_s5_log_dt, v_s5_b_re, v_s5_b_im, v_s5_c_re, v_s5_c_im, v_s5_d, v_s5_w_glu, v_mla_q_norm, v_mla_w_uq, v_mla_kv_norm, v_mla_w_ukv, v_hg_lb_logits, v_hg_out_norm, v_w_out, v_mix_pre_norm, v_mix_post_norm, v_ffn_pre_norm, v_ffn_post_norm, v_ffn_w_up, v_ffn_conv_w, v_ffn_conv_b, v_ffn_w_down, v_w_ada, v_b_ada):
    args = locals()
    key = lambda n: _ALIAS.get(n, n)
    w = {key(n): args[n] for n in _WEIGHTS}
    m = {key(n): args["m_" + n] for n in _WEIGHTS}
    v = {key(n): args["v_" + n] for n in _WEIGHTS}
    loss, grad_x, big, small = _step(x, c, positions, loss_target, w, m, v)

    def pick(n, idx):
        k = key(n)
        if k in big:
            return big[k][idx].reshape(w[k].shape)
        return small["gdmv"[idx]][k]

    outs = [loss, grad_x]
    for idx in range(4):
        outs += [pick(n, idx) for n in _WEIGHTS]
    return tuple(outs)
```

```python
import functools
import math

import numpy as np
import jax
import jax.numpy as jnp
from jax import lax
from jax.experimental import pallas as pl
from jax.experimental.pallas import tpu as pltpu

F32 = jnp.float32
BF16 = jnp.bfloat16
N_DEV = 8
V7X_VMEM_LIMIT = 56 * 1024 * 1024
MM_VMEM_BUDGET = 28 * 1024 * 1024
LANE = 128

D_MODEL = 2048
S5_W = 512
S5_G = 32
S5_C = 16
S5_P = 64
S5_N = S5_G * S5_P
S5_TL = 512
MLA_H = 8
MLA_NOPE = 128
MLA_ROPE = 64
MLA_V = 128
MLA_HW = 256
HG_H = 4
HG_D = 128
HG_CH = 16
D_FF = 5504
D_FFP = 5632
D_IN = 3392
D_INP = 3456
EPS = 1e-6
MASK_VALUE = -1e30
ROPE_THETA = 10000.0
ATT_SCALE = (MLA_NOPE + MLA_ROPE) ** -0.5

ADAM_LR = 0.001
ADAM_B1 = 0.9
ADAM_B2 = 0.999
ADAM_EPS = 1e-08
ADAM_WD = 0.01
ADAM_STEP = 10

_IN_PERM = np.concatenate([np.arange(0, 1024), np.arange(1344, 3392), np.arange(1024, 1344)])
_IN_INV = np.argsort(_IN_PERM)

_NN = (((1,), (0,)), ((), ()))
_NT = (((1,), (1,)), ((), ()))
_TN = (((0,), (0,)), ((), ()))


def _cp(*sem):
    return pltpu.CompilerParams(dimension_semantics=sem, vmem_limit_bytes=V7X_VMEM_LIMIT)


def _tile(n, cap, align=LANE):
    if n <= cap:
        return n
    t = (cap // align) * align
    while t >= align:
        if n % t == 0:
            return t
        t -= align
    raise ValueError(f"no tile for {n}")


def _bdot(a, b, dims):
    return lax.dot_general(a.astype(BF16), b.astype(BF16), dims, preferred_element_type=F32)


def _mm(a, b, mode, out_dtype=F32, a_col0=0, a_cols=None, name="mm", exch=None):
    if mode == "tn":
        K = a.shape[0]
        M = a_cols if a_cols is not None else a.shape[1]
        N = b.shape[1]
    else:
        M = a.shape[0]
        K = a_cols if a_cols is not None else a.shape[1]
        N = b.shape[0] if mode == "nt" else b.shape[1]
    tm = _tile(M, 1024, 8 if M < LANE else LANE)
    tn = _tile(N, 1024)
    kal = 8 if K < LANE else LANE
    tk = _tile(K, 2048, kal)
    osz = jnp.dtype(out_dtype).itemsize

    def vmem(tk_):
        acc = 0 if tk_ == K else tm * tn * 4
        return 2 * tk_ * (tm * a.dtype.itemsize + tn * b.dtype.itemsize) + 2 * tm * tn * osz + acc

    while vmem(tk) > MM_VMEM_BUDGET and tk > kal:
        tk = _tile(K, tk - kal, kal)
    nk = K // tk
    if mode == "tn":
        assert a_col0 % tm == 0
        a_spec = pl.BlockSpec((tk, tm), lambda i, j, k: (k, i + a_col0 // tm))
        b_spec = pl.BlockSpec((tk, tn), lambda i, j, k: (k, j))
        dims = _TN
    else:
        assert a_col0 % tk == 0
        a_spec = pl.BlockSpec((tm, tk), lambda i, j, k: (i, k + a_col0 // tk))
        if mode == "nn":
            b_spec = pl.BlockSpec((tk, tn), lambda i, j, k: (k, j))
            dims = _NN
        else:
            b_spec = pl.BlockSpec((tn, tk), lambda i, j, k: (j, k))
            dims = _NT

    if nk == 1:
        def body(a_ref, b_ref, o_ref):
            o_ref[...] = _bdot(a_ref[...], b_ref[...], dims).astype(o_ref.dtype)

        scratch = []
    else:
        def body(a_ref, b_ref, o_ref, acc_ref):
            k = pl.program_id(2)

            @pl.when(k == 0)
            def _():
                acc_ref[...] = jnp.zeros_like(acc_ref)

            acc_ref[...] += _bdot(a_ref[...], b_ref[...], dims)

            @pl.when(k == nk - 1)
            def _():
                o_ref[...] = acc_ref[...].astype(o_ref.dtype)

        scratch = [pltpu.VMEM((tm, tn), F32)]

    (out,), got = _call_hosting(
        body, name, (M // tm, N // tn, nk), [a_spec, b_spec], [pl.BlockSpec((tm, tn), lambda i, j, k: (i, j))],
        [jax.ShapeDtypeStruct((M, N), out_dtype)], [a, b], exch,
        scratch=scratch, sem=["parallel", "parallel", "arbitrary"])
    return out if exch is None else (out, got)


def _row_spec(tm, width, cb):
    return pl.BlockSpec((tm, width), lambda i: (i, cb))


def _rowwise(fn, rows, params, outs, name, tm=256):
    S = rows[0][0].shape[0]
    nr, npar = len(rows), len(params)

    def body(*refs):
        xs = [r[...].astype(F32) for r in refs[:nr]]
        ps = [p[...] for p in refs[nr:nr + npar]]
        res = fn(*xs, *ps)
        for o, r in zip(refs[nr + npar:], res):
            o[...] = r.astype(o.dtype)

    res = pl.pallas_call(
        body,
        name=name,
        grid=(S // tm,),
        in_specs=[_row_spec(tm, w, cb) for _, w, cb in rows]
        + [pl.BlockSpec(p.shape, lambda i: (0, 0)) for p in params],
        out_specs=[_row_spec(tm, w, 0) for w, _ in outs],
        out_shape=[jax.ShapeDtypeStruct((S, w), dt) for w, dt in outs],
        compiler_params=_cp("parallel"),
    )(*[r[0] for r in rows], *params)
    return list(res)


def _rowwise_vjp(fn, rows, params, cts, name, row_grads, add_rows=None, tm=256):
    S = rows[0][0].shape[0]
    add_rows = add_rows or {}
    nr, npar = len(rows), len(params)
    flat_cts = [c for group in cts for c in group]
    ncts = len(flat_cts)
    add_keys = sorted(add_rows)
    nadd = len(add_keys)
    grad_idx = [i for i in range(nr) if row_grads[i]]

    def body(*refs):
        i = pl.program_id(0)
        xs = [r[...].astype(F32) for r in refs[:nr]]
        ps = [p[...] for p in refs[nr:nr + npar]]
        ct_refs = refs[nr + npar:nr + npar + ncts]
        add_refs = refs[nr + npar + ncts:nr + npar + ncts + nadd]
        out_refs = refs[nr + npar + ncts + nadd:]
        ct_vals, pos = [], 0
        for group in cts:
            v = ct_refs[pos][...].astype(F32)
            for r in ct_refs[pos + 1:pos + len(group)]:
                v = v + r[...].astype(F32)
            pos += len(group)
            ct_vals.append(v)
        _, vjp = jax.vjp(lambda *a: tuple(fn(*a)), *xs, *ps)
        grads = vjp(tuple(ct_vals))
        for o, gi in zip(out_refs[:len(grad_idx)], grad_idx):
            g = grads[gi]
            if gi in add_rows:
                g = g + add_refs[add_keys.index(gi)][...].astype(F32)
            o[...] = g.astype(o.dtype)
        dprefs = out_refs[len(grad_idx):]

        @pl.when(i == 0)
        def _():
            for dp in dprefs:
                dp[...] = jnp.zeros_like(dp)

        for dp, g in zip(dprefs, grads[nr:]):
            dp[...] += g

    res = pl.pallas_call(
        body,
        name=name,
        grid=(S // tm,),
        in_specs=[_row_spec(tm, w, cb) for _, w, cb in rows]
        + [pl.BlockSpec(p.shape, lambda i: (0, 0)) for p in params]
        + [_row_spec(tm, w, cb) for _, w, cb in flat_cts]
        + [_row_spec(tm, add_rows[k][1], add_rows[k][2]) for k in add_keys],
        out_specs=[_row_spec(tm, rows[gi][1], 0) for gi in grad_idx]
        + [pl.BlockSpec(p.shape, lambda i: (0, 0)) for p in params],
        out_shape=[jax.ShapeDtypeStruct((S, rows[gi][1]), F32) for gi in grad_idx]
        + [jax.ShapeDtypeStruct(p.shape, F32) for p in params],
        compiler_params=_cp("arbitrary"),
    )(*[r[0] for r in rows], *params, *[c[0] for c in flat_cts], *[add_rows[k][0] for k in add_keys])
    res = list(res)
    return res[:len(grad_idx)], res[len(grad_idx):]


def _rms(x, gain):
    return x * lax.rsqrt(jnp.mean(x * x, axis=-1, keepdims=True) + EPS) * gain


def _f_pre(x, gain, sc, sh):
    return (_rms(x, gain) * (1.0 + sc) + sh,)


def _f_post(x, y, gain, g):
    return (x + g * _rms(y, gain),)


def _f_norm(x, gain):
    return (_rms(x, gain),)


def _f_s5a(yc, u, d):
    return (jax.nn.gelu(yc + d * u, approximate=True),)


def _f_s5b(g, z):
    return (g * jax.nn.sigmoid(z),)


def _loss_head(y, target, tm=256):
    S, D = y.shape

    def body(y_ref, t_ref, dy_ref, acc_ref):
        i = pl.program_id(0)
        e = y_ref[...] - t_ref[...]
        dy_ref[...] = e * (1.0 / D)

        @pl.when(i == 0)
        def _():
            acc_ref[...] = jnp.zeros_like(acc_ref)

        acc_ref[...] += jnp.sum(e * e, axis=0, keepdims=True)

    dy, acc = pl.pallas_call(
        body,
        name="loss_head",
        grid=(S // tm,),
        in_specs=[_row_spec(tm, D, 0), _row_spec(tm, D, 0)],
        out_specs=[_row_spec(tm, D, 0), pl.BlockSpec((1, D), lambda i: (0, 0))],
        out_shape=[jax.ShapeDtypeStruct((S, D), F32), jax.ShapeDtypeStruct((1, D), F32)],
        compiler_params=_cp("arbitrary"),
    )(y, target)
    return 0.5 * jnp.sum(acc) / D, dy


def _s5_tile_scan(xr, xi, tab_ref, reverse, row8):
    for k in (1, 2, 4):
        pr = tab_ref[pl.ds(k - 1, 1), 0:S5_TL] if not reverse else tab_ref[pl.ds(8 - k, 1), 0:S5_TL]
        pi = tab_ref[pl.ds(k - 1, 1), S5_TL:2 * S5_TL] if not reverse else tab_ref[pl.ds(8 - k, 1), S5_TL:2 * S5_TL]
        if not reverse:
            keep = row8 >= k
            sr = jnp.where(keep, pltpu.roll(xr, k, 0), 0.0)
            si = jnp.where(keep, pltpu.roll(xi, k, 0), 0.0)
        else:
            keep = row8 < 8 - k
            sr = jnp.where(keep, pltpu.roll(xr, 8 - k, 0), 0.0)
            si = jnp.where(keep, pltpu.roll(xi, 8 - k, 0), 0.0)
        xr, xi = xr + pr * sr - pi * si, xi + pr * si + pi * sr
    return xr, xi


def _s5_scan(bu, tab, reverse=False, h=None, bu_fwd=None, tr=512):
    S = bu.shape[0]
    tr = min(tr, S)
    nl = S5_N // S5_TL
    nrb = S // tr
    w = 2 * S5_TL
    nt = tr // 8
    rmap = (lambda j, i: (i, j)) if not reverse else (lambda j, i: (nrb - 1 - i, j))

    def body(*refs):
        if reverse:
            x_ref, tab_ref, h_ref, b_ref, o_ref, acc_ref, cr_ref, ci_ref = refs
        else:
            x_ref, tab_ref, o_ref, cr_ref, ci_ref = refs
        i = pl.program_id(1)
        row8 = lax.broadcasted_iota(jnp.int32, (8, S5_TL), 0)

        @pl.when(i == 0)
        def _():
            cr_ref[...] = jnp.zeros_like(cr_ref)
            ci_ref[...] = jnp.zeros_like(ci_ref)
            if reverse:
                acc_ref[...] = jnp.zeros_like(acc_ref)

        tr_all = tab_ref[:, 0:S5_TL]
        ti_all = tab_ref[:, S5_TL:w]

        def tile(t, carry):
            tt = (nt - 1 - t) if reverse else t
            r = pl.ds(pl.multiple_of(tt * 8, 8), 8)
            xr, xi = _s5_tile_scan(x_ref[r, 0:S5_TL], x_ref[r, S5_TL:w], tab_ref, reverse, row8)
            cr = jnp.broadcast_to(cr_ref[...], (8, S5_TL))
            ci = jnp.broadcast_to(ci_ref[...], (8, S5_TL))
            hr = xr + tr_all * cr - ti_all * ci
            hi = xi + tr_all * ci + ti_all * cr
            o_ref[r, 0:S5_TL] = hr
            o_ref[r, S5_TL:w] = hi
            edge = pl.ds(tt * 8, 1) if reverse else pl.ds(tt * 8 + 7, 1)
            cr_ref[...] = o_ref[edge, 0:S5_TL]
            ci_ref[...] = o_ref[edge, S5_TL:w]
            if reverse:
                dr = h_ref[r, 0:S5_TL] - b_ref[r, 0:S5_TL]
                di = h_ref[r, S5_TL:w] - b_ref[r, S5_TL:w]
                acc_ref[:, 0:S5_TL] += hr * dr + hi * di
                acc_ref[:, S5_TL:w] += hi * dr - hr * di
            return carry

        lax.fori_loop(0, nt, tile, 0)

    blk = pl.BlockSpec((tr, w), rmap)
    in_specs = [blk, pl.BlockSpec((8, w), lambda j, i: (0, j))]
    out_specs = [blk]
    out_shape = [jax.ShapeDtypeStruct((S, 2 * S5_N), F32)]
    args = [bu, tab]
    if reverse:
        in_specs += [blk, blk]
        args += [h, bu_fwd]
        out_specs.append(pl.BlockSpec((8, w), lambda j, i: (0, j)))
        out_shape.append(jax.ShapeDtypeStruct((8, 2 * S5_N), F32))
    res = pl.pallas_call(
        body,
        name="s5_scan_bwd" if reverse else "s5_scan_fwd",
        grid=(nl, nrb),
        in_specs=in_specs,
        out_specs=out_specs,
        out_shape=out_shape,
        scratch_shapes=[pltpu.VMEM((1, S5_TL), F32), pltpu.VMEM((1, S5_TL), F32)],
        compiler_params=_cp("parallel", "arbitrary"),
    )(*args)
    return res if reverse else res[0]


def _ri_cols(re, im):
    lead = re.shape[:-1]
    nl = S5_N // S5_TL
    z = jnp.stack([re.reshape(*lead, nl, S5_TL), im.reshape(*lead, nl, S5_TL)], axis=-2)
    return z.reshape(*lead, 2 * S5_N)


def _ri_split(z):
    lead = z.shape[:-1]
    nl = S5_N // S5_TL
    z = z.reshape(*lead, nl, 2, S5_TL)
    return z[..., 0, :].reshape(*lead, S5_N), z[..., 1, :].reshape(*lead, S5_N)


def _s5_prep(lre, lim, logdt, bre, bim, cre, cim):
    lam = lax.complex(lre, lim)
    dt = jnp.exp(logdt)[:, None]
    lam_bar = jnp.exp(lam * dt)
    b = lax.complex(bre, bim)
    b_bar = ((lam_bar - 1.0) / lam)[..., None] * b
    eye = jnp.eye(S5_G, dtype=F32)
    bd_re = jnp.einsum("gpc,gh->gchp", jnp.real(b_bar), eye).reshape(S5_W, S5_N)
    bd_im = jnp.einsum("gpc,gh->gchp", jnp.imag(b_bar), eye).reshape(S5_W, S5_N)
    bd = _ri_cols(bd_re, bd_im)
    cd_re = jnp.einsum("gcp,gh->gchp", cre, eye).reshape(S5_W, S5_N)
    cd_im = jnp.einsum("gcp,gh->gchp", -cim, eye).reshape(S5_W, S5_N)
    cdt = _ri_cols(cd_re, cd_im)
    return jnp.real(lam_bar).reshape(1, S5_N), jnp.imag(lam_bar).reshape(1, S5_N), bd, cdt


def _s5_tables(lre, lim, logdt):
    lam = lax.complex(lre, lim)
    dt = jnp.exp(logdt)[:, None]
    k = jnp.arange(1, 9, dtype=F32)[:, None, None]
    pw = jnp.exp((lam * dt)[None] * k).reshape(8, S5_N)
    fwd = _ri_cols(jnp.real(pw), jnp.imag(pw))
    rev = _ri_cols(jnp.real(pw)[::-1], -jnp.imag(pw)[::-1])
    return fwd, rev


def _rope_tables(positions):
    inv_freq = 1.0 / (ROPE_THETA ** (jnp.arange(0, MLA_ROPE, 2, dtype=F32) / MLA_ROPE))
    ang = positions.astype(F32)[:, None] * inv_freq
    cos, sin = jnp.cos(ang), jnp.sin(ang)
    z = jnp.zeros_like(cos)
    cs = jnp.concatenate([cos, cos, z, z], axis=-1)
    sn = jnp.concatenate([-sin, sin, z, z], axis=-1)
    return cs, sn


def _rope_fwd(qraw, kvraw, proj, cs, sn, tm=256):
    S = qraw.shape[0]
    HW = MLA_H * MLA_HW

    def rope(x, c, s):
        lane = lax.broadcasted_iota(jnp.int32, x.shape, 1)
        sw = jnp.where(lane < 32, pltpu.roll(x, 96, 1), jnp.where(lane < 64, pltpu.roll(x, 32, 1), 0.0))
        return x * c + sw * s

    def body(q_ref, kv_ref, kr_ref, cs_ref, sn_ref, qo_ref, ko_ref, vo_ref):
        c, s = cs_ref[...], sn_ref[...]
        kr = rope(kr_ref[...], c, s).astype(BF16)
        for h in range(MLA_H):
            o = h * MLA_HW
            qo_ref[:, o:o + 128] = q_ref[:, o:o + 128].astype(BF16)
            qo_ref[:, o + 128:o + 256] = rope(q_ref[:, o + 128:o + 256], c, s).astype(BF16)
            ko_ref[:, o:o + 128] = kv_ref[:, o:o + 128].astype(BF16)
            ko_ref[:, o + 128:o + 256] = kr
            vo_ref[:, h * 128:(h + 1) * 128] = kv_ref[:, o + 128:o + 256].astype(BF16)

    return pl.pallas_call(
        body,
        name="rope_fwd",
        grid=(S // tm,),
        in_specs=[_row_spec(tm, HW, 0), _row_spec(tm, HW, 0), _row_spec(tm, 128, (D_INP - 128) // 128),
                  _row_spec(tm, 128, 0), _row_spec(tm, 128, 0)],
        out_specs=[_row_spec(tm, HW, 0), _row_spec(tm, HW, 0), _row_spec(tm, MLA_H * MLA_V, 0)],
        out_shape=[jax.ShapeDtypeStruct((S, HW), BF16), jax.ShapeDtypeStruct((S, HW), BF16),
                   jax.ShapeDtypeStruct((S, MLA_H * MLA_V), BF16)],
        compiler_params=_cp("parallel"),
    )(qraw, kvraw, proj, cs, sn)


def _rope_bwd(dq, dk, dv, cs, sn, tm=256):
    S = dq.shape[0]
    HW = MLA_H * MLA_HW

    def rope_t(x, c, s):
        lane = lax.broadcasted_iota(jnp.int32, x.shape, 1)
        w = x * s
        sw = jnp.where(lane < 32, pltpu.roll(w, 96, 1), jnp.where(lane < 64, pltpu.roll(w, 32, 1), 0.0))
        return x * c + sw

    def body(dq_ref, dk_ref, dv_ref, cs_ref, sn_ref, qo_ref, kvo_ref, kro_ref):
        c, s = cs_ref[...], sn_ref[...]
        kr = jnp.zeros((tm, 128), F32)
        for h in range(MLA_H):
            o = h * MLA_HW
            qo_ref[:, o:o + 128] = dq_ref[:, o:o + 128]
            qo_ref[:, o + 128:o + 256] = rope_t(dq_ref[:, o + 128:o + 256], c, s)
            kvo_ref[:, o:o + 128] = dk_ref[:, o:o + 128]
            kvo_ref[:, o + 128:o + 256] = dv_ref[:, h * 128:(h + 1) * 128]
            kr = kr + dk_ref[:, o + 128:o + 256]
        kro_ref[...] = rope_t(kr, c, s)

    return pl.pallas_call(
        body,
        name="rope_bwd",
        grid=(S // tm,),
        in_specs=[_row_spec(tm, HW, 0), _row_spec(tm, HW, 0), _row_spec(tm, MLA_H * MLA_V, 0),
                  _row_spec(tm, 128, 0), _row_spec(tm, 128, 0)],
        out_specs=[_row_spec(tm, HW, 0), _row_spec(tm, HW, 0), _row_spec(tm, 128, 0)],
        out_shape=[jax.ShapeDtypeStruct((S, HW), F32), jax.ShapeDtypeStruct((S, HW), F32),
                   jax.ShapeDtypeStruct((S, 128), F32)],
        compiler_params=_cp("parallel"),
    )(dq, dk, dv, cs, sn)


ATT_T = 512


def _diag_mask(t):
    return lax.broadcasted_iota(jnp.int32, (t, t), 1) <= lax.broadcasted_iota(jnp.int32, (t, t), 0)


def _flash_fwd(q, k, v, exch=None):
    S = q.shape[0]
    t = min(ATT_T, S)
    nq = S // t

    def body(q_ref, k_ref, v_ref, o_ref, lse_ref):
        i = pl.program_id(1)
        qb = q_ref[...]

        def step(j, carry, masked):
            m, l, acc = carry
            r = pl.ds(pl.multiple_of(j * t, t), t)
            s = _bdot(qb, k_ref[r, :], _NT) * ATT_SCALE
            if masked:
                s = jnp.where(_diag_mask(t), s, MASK_VALUE)
            m_new = jnp.maximum(m, jnp.max(s, axis=-1, keepdims=True))
            alpha = jnp.exp(m - m_new)
            p = jnp.exp(s - m_new)
            l = alpha * l + jnp.sum(p, axis=-1, keepdims=True)
            acc = alpha * acc + _bdot(p, v_ref[r, :], _NN)
            return m_new, l, acc

        m0 = jnp.full((t, 1), MASK_VALUE, F32)
        init = (m0, jnp.zeros((t, 1), F32), jnp.zeros((t, MLA_V), F32))
        m, l, acc = step(i, lax.fori_loop(0, i, lambda j, c: step(j, c, False), init), True)
        o_ref[...] = acc / l
        lse_ref[...] = jnp.broadcast_to(m + jnp.log(l), (t, 128))

    return _call_hosting(
        body, "flash_fwd", (MLA_H, nq),
        [pl.BlockSpec((t, MLA_HW), lambda h, i: (i, h)),
         pl.BlockSpec((S, MLA_HW), lambda h, i: (0, h)),
         pl.BlockSpec((S, MLA_V), lambda h, i: (0, h))],
        [pl.BlockSpec((t, MLA_V), lambda h, i: (i, h)), pl.BlockSpec((t, 128), lambda h, i: (i, h))],
        [jax.ShapeDtypeStruct((S, MLA_H * MLA_V), F32), jax.ShapeDtypeStruct((S, MLA_H * 128), F32)],
        [q, k, v], exch)


def _flash_bwd_dq(q, k, v, o, lse, dcat, exch=None):
    S = q.shape[0]
    t = min(ATT_T, S)
    nq = S // t
    do_cb = S5_W // MLA_V

    def body(q_ref, k_ref, v_ref, o_ref, lse_ref, do_ref, dq_ref):
        i = pl.program_id(1)
        qb = q_ref[...]
        do = do_ref[...]
        delta = jnp.sum(do * o_ref[...], axis=-1, keepdims=True)
        lse1 = jnp.max(lse_ref[...], axis=-1, keepdims=True)
        dob = do.astype(BF16)

        def step(j, dq, masked):
            r = pl.ds(pl.multiple_of(j * t, t), t)
            kb = k_ref[r, :]
            s = _bdot(qb, kb, _NT) * ATT_SCALE
            p = jnp.exp(s - lse1)
            if masked:
                p = jnp.where(_diag_mask(t), p, 0.0)
            dp = _bdot(dob, v_ref[r, :], _NT)
            ds = p * (dp - delta) * ATT_SCALE
            return dq + _bdot(ds, kb, _NN)

        dq = lax.fori_loop(0, i, lambda j, c: step(j, c, False), jnp.zeros((t, MLA_HW), F32))
        dq_ref[...] = step(i, dq, True)

    (dq,), got = _call_hosting(
        body, "flash_bwd_dq", (MLA_H, nq),
        [pl.BlockSpec((t, MLA_HW), lambda h, i: (i, h)),
         pl.BlockSpec((S, MLA_HW), lambda h, i: (0, h)),
         pl.BlockSpec((S, MLA_V), lambda h, i: (0, h)),
         pl.BlockSpec((t, MLA_V), lambda h, i: (i, h)),
         pl.BlockSpec((t, 128), lambda h, i: (i, h)),
         pl.BlockSpec((t, MLA_V), lambda h, i: (i, do_cb + h))],
        [pl.BlockSpec((t, MLA_HW), lambda h, i: (i, h))], [jax.ShapeDtypeStruct((S, MLA_H * MLA_HW), F32)],
        [q, k, v, o, lse, dcat], exch)
    return dq, got


def _flash_bwd_dkv(q, k, v, o, lse, dcat, exch=None):
    S = q.shape[0]
    t = min(ATT_T, S)
    nq = S // t
    do_cb = S5_W // MLA_V

    def body(q_ref, k_ref, v_ref, o_ref, lse_ref, do_ref, dk_ref, dv_ref):
        j = pl.program_id(1)
        kb = k_ref[...]
        vb = v_ref[...]

        def step(i, carry, masked):
            dk, dv = carry
            r = pl.ds(pl.multiple_of(i * t, t), t)
            qb = q_ref[r, :]
            do = do_ref[r, :]
            delta = jnp.sum(do * o_ref[r, :], axis=-1, keepdims=True)
            lse1 = jnp.max(lse_ref[r, :], axis=-1, keepdims=True)
            s = _bdot(qb, kb, _NT) * ATT_SCALE
            p = jnp.exp(s - lse1)
            if masked:
                p = jnp.where(_diag_mask(t), p, 0.0)
            dob = do.astype(BF16)
            dv = dv + _bdot(p, dob, _TN)
            dp = _bdot(dob, vb, _NT)
            ds = p * (dp - delta) * ATT_SCALE
            dk = dk + _bdot(ds, qb, _TN)
            return dk, dv

        first = step(j, (jnp.zeros((t, MLA_HW), F32), jnp.zeros((t, MLA_V), F32)), True)
        dk, dv = lax.fori_loop(j + 1, nq, lambda i, c: step(i, c, False), first)
        dk_ref[...] = dk
        dv_ref[...] = dv

    return _call_hosting(
        body, "flash_bwd_dkv", (MLA_H, nq),
        [pl.BlockSpec((S, MLA_HW), lambda h, j: (0, h)),
         pl.BlockSpec((t, MLA_HW), lambda h, j: (j, h)),
         pl.BlockSpec((t, MLA_V), lambda h, j: (j, h)),
         pl.BlockSpec((S, MLA_V), lambda h, j: (0, h)),
         pl.BlockSpec((S, 128), lambda h, j: (0, h)),
         pl.BlockSpec((S, MLA_V), lambda h, j: (0, do_cb + h))],
        [pl.BlockSpec((t, MLA_HW), lambda h, j: (j, h)), pl.BlockSpec((t, MLA_V), lambda h, j: (j, h))],
        [jax.ShapeDtypeStruct((S, MLA_H * MLA_HW), F32), jax.ShapeDtypeStruct((S, MLA_H * MLA_V), F32)],
        [q, k, v, o, lse, dcat], exch)


def _split3(x):
    x1 = x.astype(BF16)
    r1 = x - x1.astype(F32)
    x2 = r1.astype(BF16)
    x3 = (r1 - x2.astype(F32)).astype(BF16)
    return x1, x2, x3


def _tri_matmul(x, upper):
    n = x.shape[0]
    r = lax.broadcasted_iota(jnp.int32, (n, n), 0)
    c = lax.broadcasted_iota(jnp.int32, (n, n), 1)
    tri = jnp.where((r <= c) if upper else (r >= c), 1.0, 0.0).astype(BF16)
    x1, x2, x3 = _split3(x)
    dot = lambda v: lax.dot_general(tri, v, _NN, preferred_element_type=F32)
    return dot(x1) + dot(x2) + dot(x3)


@jax.custom_vjp
def _cumsum_rows(x):
    return _tri_matmul(x, False)


def _cumsum_rows_fwd(x):
    return _tri_matmul(x, False), None


def _cumsum_rows_bwd(_, ct):
    return (_tri_matmul(ct, True),)


_cumsum_rows.defvjp(_cumsum_rows_fwd, _cumsum_rows_bwd)


def _hg_step(qin, fin, vin, gin, st, lb, on):
    n = qin.shape[0]
    sig = jax.nn.sigmoid(fin)
    g = jnp.log(lb + (1.0 - lb) * sig)
    k = (1.0 - lb) * jax.nn.sigmoid(-fin)
    q = qin * jax.nn.sigmoid(qin)
    b = _cumsum_rows(g)
    o = _bdot(q * jnp.exp(b), st, _NT)
    row = lax.broadcasted_iota(jnp.int32, (n, HG_D), 0)
    row1 = lax.broadcasted_iota(jnp.int32, (n, 1), 0)
    b_s = None
    for s in range(n):
        sel = row == s
        b_s = jnp.sum(jnp.where(sel, b, 0.0), axis=0, keepdims=True)
        k_s = jnp.sum(jnp.where(sel, k, 0.0), axis=0, keepdims=True)
        v_s = jnp.sum(jnp.where(sel, vin, 0.0), axis=0, keepdims=True)
        e = jnp.exp(jnp.minimum(b - b_s, 0.0))
        c = jnp.sum(q * e * k_s, axis=-1, keepdims=True)
        o = o + jnp.where(row1 >= s, c, 0.0) * v_s
    st_new = st * jnp.exp(b_s) + _bdot(vin, k * jnp.exp(b_s - b), _TN)
    y = _rms(o, on) * (gin * jax.nn.sigmoid(gin))
    return y, st_new


HG_W = HG_H * HG_D


def _hg_specs(tb, nb, reverse):
    rm = (lambda i: nb - 1 - i) if reverse else (lambda i: i)
    base = 1024 // HG_W
    return [pl.BlockSpec((tb, HG_W), lambda i, o=o: (rm(i), base + o)) for o in range(4)], rm


def _head(h):
    return slice(h * HG_D, (h + 1) * HG_D)


def _hg_fwd(proj, lb, on, exch=None, tb=256):
    S = proj.shape[0]
    nb = S // tb
    nc = tb // HG_CH
    in_specs, rm = _hg_specs(tb, nb, False)

    def body(q_ref, f_ref, v_ref, g_ref, lb_ref, on_ref, y_ref, sts_ref, st_ref):
        @pl.when(pl.program_id(0) == 0)
        def _():
            st_ref[...] = jnp.zeros_like(st_ref)

        def step(c, carry):
            r = pl.ds(pl.multiple_of(c * HG_CH, HG_CH), HG_CH)
            for h in range(HG_H):
                hs = _head(h)
                st = st_ref[h]
                sts_ref[h, c] = st
                y, st_new = _hg_step(q_ref[r, hs], f_ref[r, hs], v_ref[r, hs], g_ref[r, hs], st, lb_ref[:, hs],
                                     on_ref[...])
                y_ref[r, hs] = y
                st_ref[h] = st_new
            return carry

        lax.fori_loop(0, nc, step, 0)

    return _call_hosting(
        body, "hgrn2_fwd", (nb,),
        in_specs + [pl.BlockSpec((1, HG_W), lambda i: (0, 0)), pl.BlockSpec((1, HG_D), lambda i: (0, 0))],
        [pl.BlockSpec((tb, HG_W), lambda i: (i, 0)), pl.BlockSpec((HG_H, nc, HG_D, HG_D), lambda i: (0, i, 0, 0))],
        [jax.ShapeDtypeStruct((S, HG_W), F32), jax.ShapeDtypeStruct((HG_H, S // HG_CH, HG_D, HG_D), F32)],
        [proj, proj, proj, proj, lb, on], exch,
        scratch=[pltpu.VMEM((HG_H, HG_D, HG_D), F32)], sem=["arbitrary"])


def _hg_bwd(proj, sts, lb, on, dcat, exch=None, tb=256):
    S = proj.shape[0]
    nb = S // tb
    nc = tb // HG_CH
    in_specs, rm = _hg_specs(tb, nb, True)
    dy_cb = (S5_W + MLA_H * MLA_V) // HG_W

    def body(q_ref, f_ref, v_ref, g_ref, lb_ref, on_ref, sts_ref, dy_ref,
             dq_ref, df_ref, dv_ref, dg_ref, dlb_ref, don_ref, dst_ref):
        @pl.when(pl.program_id(0) == 0)
        def _():
            dst_ref[...] = jnp.zeros_like(dst_ref)
            dlb_ref[...] = jnp.zeros_like(dlb_ref)
            don_ref[...] = jnp.zeros_like(don_ref)

        def step(cc, carry):
            c = nc - 1 - cc
            r = pl.ds(pl.multiple_of(c * HG_CH, HG_CH), HG_CH)
            for h in range(HG_H):
                hs = _head(h)
                _, vjp = jax.vjp(_hg_step, q_ref[r, hs], f_ref[r, hs], v_ref[r, hs], g_ref[r, hs], sts_ref[h, c],
                                 lb_ref[:, hs], on_ref[...])
                dq, df, dv, dg, dst, dlb, don = vjp((dy_ref[r, hs], dst_ref[h]))
                dq_ref[r, hs] = dq
                df_ref[r, hs] = df
                dv_ref[r, hs] = dv
                dg_ref[r, hs] = dg
                dst_ref[h] = dst
                dlb_ref[:, hs] += dlb
                don_ref[:, hs] += don
            return carry

        lax.fori_loop(0, nc, step, 0)

    blk = pl.BlockSpec((tb, HG_W), lambda i: (rm(i), 0))
    par = pl.BlockSpec((1, HG_W), lambda i: (0, 0))
    return _call_hosting(
        body, "hgrn2_bwd", (nb,),
        in_specs + [par, pl.BlockSpec((1, HG_D), lambda i: (0, 0)),
                    pl.BlockSpec((HG_H, nc, HG_D, HG_D), lambda i: (0, rm(i), 0, 0)),
                    pl.BlockSpec((tb, HG_W), lambda i: (rm(i), dy_cb))],
        [blk, blk, blk, blk, par, par],
        [jax.ShapeDtypeStruct((S, HG_W), F32)] * 4 + [jax.ShapeDtypeStruct((1, HG_W), F32)] * 2,
        [proj, proj, proj, proj, lb, on, sts, dcat], exch,
        scratch=[pltpu.VMEM((HG_H, HG_D, HG_D), F32)], sem=["arbitrary"])


CONV_NC = 4
CONV_TC = D_FFP // CONV_NC


def _shift_down(cur, halo, k):
    tm = cur.shape[0]
    row = lax.broadcasted_iota(jnp.int32, cur.shape, 0)
    top = jnp.concatenate([pltpu.roll(halo, k, 0), jnp.zeros((tm - 8, cur.shape[1]), F32)], axis=0)
    return jnp.where(row < k, top, pltpu.roll(cur, k, 0))


def _shift_up(cur, halo, k):
    tm = cur.shape[0]
    row = lax.broadcasted_iota(jnp.int32, cur.shape, 0)
    bot = jnp.concatenate([jnp.zeros((tm - 8, cur.shape[1]), F32), pltpu.roll(halo, 8 - k, 0)], axis=0)
    return jnp.where(row >= tm - k, bot, pltpu.roll(cur, tm - k, 0))


def _conv3(cur, halo, w_ref, b_ref):
    return (b_ref[...] + _shift_down(cur, halo, 2) * w_ref[pl.ds(0, 1), :]
            + _shift_down(cur, halo, 1) * w_ref[pl.ds(1, 1), :] + cur * w_ref[pl.ds(2, 1), :])


def _conv_fwd(u0, cw, cb, exch=None, tm=256):
    S = u0.shape[0]
    nc = CONV_NC
    m8 = tm // 8
    prev = lambda i: jnp.maximum(i * m8 - 1, 0)

    def body(u_ref, p_ref, w_ref, b_ref, a_ref):
        on = (pl.program_id(1) > 0).astype(F32)
        u = _conv3(u_ref[...], p_ref[...] * on, w_ref, b_ref)
        a_ref[...] = (jax.nn.gelu(u[:, :CONV_TC], approximate=True) * u[:, CONV_TC:]).astype(a_ref.dtype)

    tc = CONV_TC
    (a,), got = _call_hosting(
        body, "conv_geglu_fwd", (nc, S // tm),
        [pl.BlockSpec((tm, 2 * tc), lambda j, i: (i, j)), pl.BlockSpec((8, 2 * tc), lambda j, i: (prev(i), j)),
         pl.BlockSpec((3, 2 * tc), lambda j, i: (0, j)), pl.BlockSpec((1, 2 * tc), lambda j, i: (0, j))],
        [pl.BlockSpec((tm, tc), lambda j, i: (i, j))], [jax.ShapeDtypeStruct((S, D_FFP), BF16)],
        [u0, u0, cw, cb], exch)
    return a, got


def _conv_bwd_elem(da, u0, cw, cb, tm=256):
    S = u0.shape[0]
    nc = CONV_NC
    m8 = tm // 8
    prev = lambda i: jnp.maximum(i * m8 - 1, 0)

    def body(u_ref, p_ref, w_ref, b_ref, da_ref, du_ref, dw_ref, db_ref):
        i = pl.program_id(1)
        cur = u_ref[...]
        halo = p_ref[...] * (i > 0).astype(F32)
        u = _conv3(cur, halo, w_ref, b_ref)
        _, vjp = jax.vjp(lambda a, b: jax.nn.gelu(a, approximate=True) * b, u[:, :CONV_TC], u[:, CONV_TC:])
        dug, duv = vjp(da_ref[...])
        du = jnp.concatenate([dug, duv], axis=1)
        du_ref[...] = du

        @pl.when(i == 0)
        def _():
            dw_ref[...] = jnp.zeros_like(dw_ref)
            db_ref[...] = jnp.zeros_like(db_ref)

        dw_ref[pl.ds(0, 1), :] += jnp.sum(du * _shift_down(cur, halo, 2), axis=0, keepdims=True)
        dw_ref[pl.ds(1, 1), :] += jnp.sum(du * _shift_down(cur, halo, 1), axis=0, keepdims=True)
        dw_ref[pl.ds(2, 1), :] += jnp.sum(du * cur, axis=0, keepdims=True)
        db_ref[...] += jnp.sum(du, axis=0, keepdims=True)

    tc = CONV_TC
    return pl.pallas_call(
        body,
        name="conv_geglu_bwd",
        grid=(nc, S // tm),
        in_specs=[pl.BlockSpec((tm, 2 * tc), lambda j, i: (i, j)), pl.BlockSpec((8, 2 * tc), lambda j, i: (prev(i), j)),
                  pl.BlockSpec((3, 2 * tc), lambda j, i: (0, j)), pl.BlockSpec((1, 2 * tc), lambda j, i: (0, j)),
                  pl.BlockSpec((tm, tc), lambda j, i: (i, j))],
        out_specs=[pl.BlockSpec((tm, 2 * tc), lambda j, i: (i, j)), pl.BlockSpec((3, 2 * tc), lambda j, i: (0, j)),
                   pl.BlockSpec((1, 2 * tc), lambda j, i: (0, j))],
        out_shape=[jax.ShapeDtypeStruct((S, 2 * D_FFP), F32), jax.ShapeDtypeStruct((3, 2 * D_FFP), F32),
                   jax.ShapeDtypeStruct((1, 2 * D_FFP), F32)],
        compiler_params=_cp("parallel", "arbitrary"),
    )(u0, u0, cw, cb, da)


def _conv_bwd_input(du, cw, tm=256):
    S = du.shape[0]
    nrb = S // tm
    m8 = tm // 8
    nxt = lambda i: jnp.minimum((i + 1) * m8, S // 8 - 1)

    def body(d_ref, n_ref, w_ref, o_ref):
        on = (pl.program_id(1) < nrb - 1).astype(F32)
        cur = d_ref[...]
        halo = n_ref[...] * on
        o_ref[...] = (cur * w_ref[pl.ds(2, 1), :] + _shift_up(cur, halo, 1) * w_ref[pl.ds(1, 1), :]
                      + _shift_up(cur, halo, 2) * w_ref[pl.ds(0, 1), :])

    tc = 2 * CONV_TC
    return pl.pallas_call(
        body,
        name="conv_bwd_input",
        grid=(CONV_NC, nrb),
        in_specs=[pl.BlockSpec((tm, tc), lambda j, i: (i, j)), pl.BlockSpec((8, tc), lambda j, i: (nxt(i), j)),
                  pl.BlockSpec((3, tc), lambda j, i: (0, j))],
        out_specs=pl.BlockSpec((tm, tc), lambda j, i: (i, j)),
        out_shape=jax.ShapeDtypeStruct((S, 2 * D_FFP), F32),
        compiler_params=_cp("parallel", "parallel"),
    )(du, du, cw)


def _exchange(arrs, scatter, name, ff=None):
    n = len(arrs)

    def body(*refs):
        args = (refs[:n], refs[n:2 * n], *refs[2 * n:], scatter, ff)
        _exchange_start(*args)
        _exchange_wait(*args)

    hbm = pl.BlockSpec(memory_space=pltpu.HBM)
    out_shape, sems = _exchange_shapes(arrs, scatter)
    return pl.pallas_call(
        body,
        name=name,
        in_specs=[hbm] * n,
        out_specs=[hbm] * n,
        out_shape=out_shape,
        scratch_shapes=sems,
    )(*arrs)


def _ff_slot(d):
    return (d % 4) * 2 + d // 4


def _exchange_copies(ins, outs, send, recv, loc, scatter, ff=None):
    x, y, c = lax.axis_index("x"), lax.axis_index("y"), lax.axis_index("c")
    me = 4 * x + 2 * y + c
    sends, recvs, locs = [], [], []
    for a in range(len(ins)):
        slot = _ff_slot if ff and ff[a] else (lambda d: d)
        mine = me if scatter else slot(me)
        locs.append(pltpu.make_async_copy(ins[a].at[slot(me)] if scatter else ins[a], outs[a].at[mine], loc.at[a]))
        for k in range(1, N_DEV):
            px = 1 - x if k & 4 else x
            py = 1 - y if k & 2 else y
            pc = 1 - c if k & 1 else c
            peer = 4 * px + 2 * py + pc
            src = ins[a].at[slot(peer)] if scatter else ins[a]
            sems = dict(send_sem=send.at[a, k - 1], recv_sem=recv.at[a, k - 1], device_id=(px, py, pc),
                        device_id_type=pl.DeviceIdType.MESH)
            sends.append(pltpu.make_async_remote_copy(src_ref=src, dst_ref=outs[a].at[mine], **sems))
            theirs = peer if scatter else slot(peer)
            recvs.append(pltpu.make_async_remote_copy(src_ref=src, dst_ref=outs[a].at[theirs], **sems))
    return locs, sends, recvs


def _exchange_start(*refs):
    locs, sends, _ = _exchange_copies(*refs)
    for cp in locs + sends:
        cp.start()


def _exchange_wait(*refs):
    locs, sends, recvs = _exchange_copies(*refs)
    for cp in recvs:
        cp.wait_recv()
    for cp in sends:
        cp.wait_send()
    for cp in locs:
        cp.wait()


def _exchange_shapes(arrs, scatter):
    n = len(arrs)
    out_shape = [jax.ShapeDtypeStruct(a.shape if scatter else (N_DEV,) + a.shape, a.dtype) for a in arrs]
    sems = [pltpu.SemaphoreType.DMA((n, N_DEV - 1)), pltpu.SemaphoreType.DMA((n, N_DEV - 1)),
            pltpu.SemaphoreType.DMA((n,))]
    return out_shape, sems


def _call_hosting(body, name, grid, in_specs, out_specs, out_shape, args, exch, scratch=(), sem=None):
    scratch = list(scratch)
    if exch is None:
        res = pl.pallas_call(body, name=name, grid=grid, in_specs=in_specs, out_specs=out_specs, out_shape=out_shape,
                             scratch_shapes=scratch,
                             compiler_params=_cp(*(sem or ["parallel"] * len(grid))))(*args)
        return list(res), []
    arrs, scatter, ff = (*exch, None)[:3]
    n, n_in, n_out, n_scr = len(arrs), len(in_specs), len(out_specs), len(scratch)
    hbm = pl.BlockSpec(memory_space=pltpu.HBM)
    x_shape, sems = _exchange_shapes(arrs, scatter)

    def hosting_body(*refs):
        cin, xin = refs[:n_in], refs[n_in:n_in + n]
        cout, xout = refs[n_in + n:n_in + n + n_out], refs[n_in + n + n_out:n_in + 2 * n + n_out]
        cscr = refs[n_in + 2 * n + n_out:n_in + 2 * n + n_out + n_scr]
        xsem = refs[n_in + 2 * n + n_out + n_scr:]
        ids = [pl.program_id(d) for d in range(len(grid))]
        first = functools.reduce(jnp.logical_and, [i == 0 for i in ids])
        last = functools.reduce(jnp.logical_and, [i == g - 1 for i, g in zip(ids, grid)])

        @pl.when(first)
        def _():
            _exchange_start(xin, xout, *xsem, scatter, ff)

        body(*cin, *cout, *cscr)

        @pl.when(last)
        def _():
            _exchange_wait(xin, xout, *xsem, scatter, ff)

    res = pl.pallas_call(
        hosting_body, name=name + "_x", grid=grid, in_specs=in_specs + [hbm] * n, out_specs=out_specs + [hbm] * n,
        out_shape=out_shape + x_shape, scratch_shapes=scratch + sems,
        compiler_params=_cp(*["arbitrary"] * len(grid)))(*args, *arrs)
    return list(res[:n_out]), list(res[n_out:])


class _Riders:
    def __init__(self):
        self.make, self.done = {}, {}

    def add(self, host, make, done):
        self.make[host], self.done[host] = make, done

    def give(self, host, ctx=None):
        return self.make[host](ctx) if host in self.make else None

    def take(self, host, got):
        if host in self.done:
            self.done[host](got)


def _adamw(recv, w, m, v, name="adamw"):
    L, n, R, C = recv.shape
    fits = [t for t in range(8, R + 1, 8) if R % t == 0 and t * C * 4 <= (1 << 19)]
    tr = max(fits) if fits else R

    def body(r_ref, w_ref, m_ref, v_ref, g_ref, d_ref, mo_ref, vo_ref):
        g = r_ref[0, 0].astype(F32)
        for d in range(1, n):
            g = g + r_ref[0, d].astype(F32)
        mm = ADAM_B1 * m_ref[0] + (1.0 - ADAM_B1) * g
        vv = ADAM_B2 * v_ref[0] + (1.0 - ADAM_B2) * (g * g)
        m_hat = mm / (1.0 - ADAM_B1 ** ADAM_STEP)
        v_hat = vv / (1.0 - ADAM_B2 ** ADAM_STEP)
        g_ref[0] = g
        d_ref[0] = -ADAM_LR * (m_hat / (jnp.sqrt(v_hat) + ADAM_EPS) + ADAM_WD * w_ref[0])
        mo_ref[0] = mm
        vo_ref[0] = vv

    blk = pl.BlockSpec((1, tr, C), lambda l, i: (l, i, 0))
    return pl.pallas_call(
        body,
        name=name,
        grid=(L, R // tr),
        in_specs=[pl.BlockSpec((1, n, tr, C), lambda l, i: (l, 0, i, 0)), blk, blk, blk],
        out_specs=[blk] * 4,
        out_shape=[jax.ShapeDtypeStruct((L, R, C), F32)] * 4,
        compiler_params=_cp("parallel", "parallel"),
    )(recv, w, m, v)


def _layer_fwd(x, mod, W, P, riders=None):
    riders = riders or _Riders()
    sh1, sc1, g1, sh2, sc2, g2 = mod
    D = D_MODEL
    R = {"x": x}
    (h1,) = _rowwise(_f_pre, [(x, D, 0)], [P["n1"], sc1, sh1], [(D, BF16)], "pre_norm")
    proj = _mm(h1, W["w_in"], "nn", name="mm_in")
    R["h1"], R["proj"] = h1, proj
    bu = _mm(proj, W["bd"], "nn", a_col0=0, a_cols=S5_W, name="mm_s5_b")
    hs = _s5_scan(bu, P["tab_fwd"])
    yc = _mm(hs, W["cdt"], "nt", name="mm_s5_c")
    (gg,) = _rowwise(_f_s5a, [(yc, S5_W, 0), (proj, S5_W, 0)], [P["s5_d"]], [(S5_W, F32)], "s5_gelu")
    z = _mm(gg, W["w_glu"], "nn", name="mm_glu")
    (ys5,) = _rowwise(_f_s5b, [(gg, S5_W, 0), (z, S5_W, 0)], [], [(S5_W, BF16)], "s5_glu")
    R.update(bu=bu, hs=hs, yc=yc, gg=gg, z=z)
    (qn,) = _rowwise(_f_norm, [(proj, 512, 1)], [P["q_norm"]], [(512, BF16)], "q_norm")
    (kvn,) = _rowwise(_f_norm, [(proj, 256, 12)], [P["kv_norm"]], [(256, BF16)], "kv_norm")
    qraw = _mm(qn, W["w_uq"], "nn", name="mm_uq")
    kvraw = _mm(kvn, W["w_ukv"], "nn", name="mm_ukv")
    q, k, v = _rope_fwd(qraw, kvraw, proj, P["cs"], P["sn"])
    (o, lse), got = _flash_fwd(q, k, v, riders.give("flash_fwd"))
    riders.take("flash_fwd", got)
    R.update(qn=qn, kvn=kvn, q=q, k=k, v=v, o=o, lse=lse)
    (yhg, sts), got = _hg_fwd(proj, P["lb"], P["hg_on"], riders.give("hgrn2_fwd"))
    riders.take("hgrn2_fwd", got)
    R["sts"] = sts
    cat = jnp.concatenate([ys5, o.astype(BF16), yhg.astype(BF16)], axis=-1)
    mixed = _mm(cat, W["w_out"], "nn", name="mm_out")
    (x2,) = _rowwise(_f_post, [(x, D, 0), (mixed, D, 0)], [P["n2"], g1], [(D, F32)], "post_norm")
    R.update(cat=cat, mixed=mixed, x2=x2)
    (h2,) = _rowwise(_f_pre, [(x2, D, 0)], [P["n3"], sc2, sh2], [(D, BF16)], "pre_norm")
    rider = riders.give("mm_up")
    u0 = _mm(h2, W["w_up"], "nn", name="mm_up", exch=rider)
    if rider is not None:
        u0, got = u0
        riders.take("mm_up", got)
    a, got = _conv_fwd(u0, P["conv_w"], P["conv_b"], riders.give("conv_fwd"))
    riders.take("conv_fwd", got)
    y = _mm(a, W["w_down"], "nn", name="mm_down")
    (x3,) = _rowwise(_f_post, [(x2, D, 0), (y, D, 0)], [P["n4"], g2], [(D, F32)], "post_norm")
    R.update(h2=h2, u0=u0, a=a, y=y)
    return x3, R


def _layer_bwd(dx3, mod, W, P, R, riders=None):
    riders = riders or _Riders()
    sh1, sc1, g1, sh2, sc2, g2 = mod
    D = D_MODEL
    G = {}
    (dx2a, dy), (dn4, dg2) = _rowwise_vjp(_f_post, [(R["x2"], D, 0), (R["y"], D, 0)], [P["n4"], g2],
                                          [[(dx3, D, 0)]], "post_norm_bwd", [True, True])
    da = _mm(dy, W["w_down"], "nt", name="mm_down_dx")
    G["w_down"] = _mm(R["a"], dy, "tn", out_dtype=BF16, name="mm_down_dw")
    du, dcw, dcb = _conv_bwd_elem(da, R["u0"], P["conv_w"], P["conv_b"])
    du0 = _conv_bwd_input(du, P["conv_w"])
    dh2 = _mm(du0, W["w_up"], "nt", name="mm_up_dx")
    G["w_up"] = _mm(R["h2"], du0, "tn", out_dtype=BF16, name="mm_up_dw")
    (dx2,), (dn3, dsc2, dsh2) = _rowwise_vjp(_f_pre, [(R["x2"], D, 0)], [P["n3"], sc2, sh2], [[(dh2, D, 0)]],
                                             "pre_norm_bwd", [True], add_rows={0: (dx2a, D, 0)})
    (dxa, dmixed), (dn2, dg1) = _rowwise_vjp(_f_post, [(R["x"], D, 0), (R["mixed"], D, 0)], [P["n2"], g1],
                                             [[(dx2, D, 0)]], "post_norm_bwd", [True, True])
    dcat = _mm(dmixed, W["w_out"], "nt", name="mm_out_dx")
    G["w_out"] = _mm(R["cat"], dmixed, "tn", out_dtype=BF16, name="mm_out_dw")
    (dga, dz), _ = _rowwise_vjp(_f_s5b, [(R["gg"], S5_W, 0), (R["z"], S5_W, 0)], [], [[(dcat, S5_W, 0)]],
                                "s5_glu_bwd", [True, True])
    dgb = _mm(dz, W["w_glu"], "nt", name="mm_glu_dx")
    G["w_glu"] = _mm(R["gg"], dz, "tn", out_dtype=BF16, name="mm_glu_dw")
    (dyc, dua), (dd,) = _rowwise_vjp(_f_s5a, [(R["yc"], S5_W, 0), (R["proj"], S5_W, 0)], [P["s5_d"]],
                                     [[(dga, S5_W, 0), (dgb, S5_W, 0)]], "s5_gelu_bwd", [True, True])
    dhs = _mm(dyc, W["cdt"], "nn", name="mm_s5_c_dx")
    dcdt = _mm(dyc, R["hs"], "tn", name="mm_s5_c_dw")
    gs, acc = _s5_scan(dhs, P["tab_rev"], reverse=True, h=R["hs"], bu_fwd=R["bu"])
    dub = _mm(gs, W["bd"], "nt", name="mm_s5_b_dx")
    dbd = _mm(R["proj"], gs, "tn", a_col0=0, a_cols=S5_W, name="mm_s5_b_dw")
    dq, got = _flash_bwd_dq(R["q"], R["k"], R["v"], R["o"], R["lse"], dcat, riders.give("flash_bwd_dq", G))
    riders.take("flash_bwd_dq", got)
    (dk, dv), got = _flash_bwd_dkv(R["q"], R["k"], R["v"], R["o"], R["lse"], dcat, riders.give("flash_bwd_dkv", G))
    riders.take("flash_bwd_dkv", got)
    dqraw, dkvraw, dkr = _rope_bwd(dq, dk, dv, P["cs"], P["sn"])
    dqn = _mm(dqraw, W["w_uq"], "nt", name="mm_uq_dx")
    G["w_uq"] = _mm(R["qn"], dqraw, "tn", out_dtype=BF16, name="mm_uq_dw")
    dkvn = _mm(dkvraw, W["w_ukv"], "nt", name="mm_ukv_dx")
    G["w_ukv"] = _mm(R["kvn"], dkvraw, "tn", out_dtype=BF16, name="mm_ukv_dw")
    (dcq,), (dqnorm,) = _rowwise_vjp(_f_norm, [(R["proj"], 512, 1)], [P["q_norm"]], [[(dqn, 512, 0)]],
                                     "q_norm_bwd", [True])
    (dckv,), (dkvnorm,) = _rowwise_vjp(_f_norm, [(R["proj"], 256, 12)], [P["kv_norm"]], [[(dkvn, 256, 0)]],
                                       "kv_norm_bwd", [True])
    (dhq, dhf, dhi, dhg, dlb, don), got = _hg_bwd(R["proj"], R["sts"], P["lb"], P["hg_on"], dcat,
                                                  riders.give("hgrn2_bwd", G))
    riders.take("hgrn2_bwd", got)
    dproj = jnp.concatenate([dua + dub, dcq, dhq, dhf, dhi, dhg, dckv, dkr], axis=-1)
    dh1 = _mm(dproj, W["w_in"], "nt", name="mm_in_dx")
    G["w_in"] = _mm(R["h1"], dproj, "tn", out_dtype=BF16, name="mm_in_dw")
    (dx,), (dn1, dsc1, dsh1) = _rowwise_vjp(_f_pre, [(R["x"], D, 0)], [P["n1"], sc1, sh1], [[(dh1, D, 0)]],
                                            "pre_norm_bwd", [True], add_rows={0: (dxa, D, 0)})
    dmod = jnp.concatenate([dsh1, dsc1, dg1, dsh2, dsc2, dg2], axis=-1)
    small = dict(n1=dn1, n2=dn2, n3=dn3, n4=dn4, s5_d=dd, q_norm=dqnorm, kv_norm=dkvnorm,
                 lb=dlb, hg_on=jnp.sum(don.reshape(HG_H, HG_D), axis=0, keepdims=True),
                 conv_w=dcw, conv_b=dcb, bd=dbd, cdt=dcdt, acc=jnp.sum(acc, axis=0, keepdims=True))
    return dx, dmod, G, small


def _cols_from_shards(g):
    return jnp.transpose(g, (1, 0, 2)).reshape(g.shape[1], -1)


def _cols_to_shards(w):
    K = w.shape[0]
    return jnp.transpose(w.reshape(K, N_DEV, -1), (1, 0, 2))


FF_SHARD = 2 * D_FF // N_DEV
FF_PAD = CONV_TC - FF_SHARD


def _pad_ff(w):
    lead = w.shape[:-1]
    w = jnp.swapaxes(w.reshape(*lead, 2, CONV_NC, FF_SHARD), -3, -2)
    return jnp.pad(w, [(0, 0)] * (w.ndim - 1) + [(0, FF_PAD)]).reshape(*lead, 2 * D_FFP)


def _unpad_ff(w):
    lead = w.shape[:-1]
    w = w.reshape(*lead, CONV_NC, 2, CONV_TC)[..., :FF_SHARD]
    return jnp.swapaxes(w, -3, -2).reshape(*lead, 2 * D_FF)


def _asm_up(g):
    return _cols_from_shards(jnp.pad(g, ((0, 0), (0, 0), (0, FF_PAD))))


def _grad_up(g):
    return jnp.transpose(g.reshape(g.shape[0], N_DEV, CONV_TC), (1, 0, 2))[..., :FF_SHARD]


def _asm_down(g):
    w = g.reshape(CONV_NC, FF_SHARD, D_MODEL)
    return jnp.pad(w, ((0, 0), (0, FF_PAD), (0, 0))).reshape(D_FFP, D_MODEL)


def _grad_down(g):
    return g.reshape(CONV_NC, CONV_TC, D_MODEL)[:, :FF_SHARD].reshape(N_DEV, -1, D_MODEL)


def _asm_in(g):
    w = _cols_from_shards(g)
    return jnp.concatenate([w[:, 0:1024], w[:, 1344:D_IN], w[:, 1024:1344],
                            jnp.zeros((w.shape[0], D_INP - D_IN), w.dtype)], axis=1)


def _asm_uq(g):
    w = _cols_from_shards(g).reshape(-1, MLA_H, MLA_NOPE + MLA_ROPE)
    return jnp.pad(w, ((0, 0), (0, 0), (0, MLA_HW - MLA_NOPE - MLA_ROPE))).reshape(-1, MLA_H * MLA_HW)


_ASSEMBLE = dict(
    w_in=_asm_in,
    w_glu=lambda g: g.reshape(S5_W, S5_W),
    w_uq=_asm_uq,
    w_ukv=_cols_from_shards,
    w_out=lambda g: g.reshape(D_MODEL, D_MODEL),
    w_up=_asm_up,
    w_down=_asm_down,
)

_GRAD_SHARDS = dict(
    w_in=lambda g: _cols_to_shards(jnp.concatenate([g[:, 0:1024], g[:, 3072:D_IN], g[:, 1024:3072]], axis=1)),
    w_glu=lambda g: g.reshape(N_DEV, -1, S5_W),
    w_uq=lambda g: _cols_to_shards(
        g.reshape(-1, MLA_H, MLA_HW)[:, :, :MLA_NOPE + MLA_ROPE].reshape(-1, MLA_H * (MLA_NOPE + MLA_ROPE))),
    w_ukv=_cols_to_shards,
    w_out=lambda g: g.reshape(N_DEV, -1, D_MODEL),
    w_up=_grad_up,
    w_down=_grad_down,
)


_BIG = ("w_in", "w_glu", "w_uq", "w_ukv", "w_out", "w_up", "w_down")
_EARLY = ("w_in", "w_glu", "w_uq", "w_ukv", "ffn_conv_w")
_SMALL = ("s5_lambda_re", "s5_lambda_im", "s5_log_dt", "s5_b_re", "s5_b_im", "s5_c_re", "s5_c_im", "s5_d",
          "mla_q_norm", "mla_kv_norm", "hg_lb_logits", "hg_out_norm", "mix_pre_norm", "mix_post_norm",
          "ffn_pre_norm", "ffn_post_norm", "ffn_conv_w_full", "ffn_conv_b", "b_ada")
PACK_ROW = 1024


def _pack(parts):
    flat = jnp.concatenate([p.reshape(-1) for p in parts])
    n = flat.shape[0]
    pad = (-n) % (8 * PACK_ROW)
    return jnp.pad(flat, (0, pad)).reshape(-1, PACK_ROW)


def _unpack(packed, shapes):
    flat = packed.reshape(-1)
    out, pos = [], 0
    for s in shapes:
        n = int(np.prod(s))
        out.append(flat[pos:pos + n].reshape(s))
        pos += n
    return out


def _step(x, c, positions, loss_target, w, m, v):
    x = x[0]
    S = x.shape[0]
    L = w["w_in"].shape[0]
    D = D_MODEL
    me = 4 * lax.axis_index("x") + 2 * lax.axis_index("y") + lax.axis_index("c")

    (c_all,) = _exchange([c], False, "gather_c")
    c_all = c_all.reshape(N_DEV, D)
    (c_act,) = _rowwise(lambda a: (a * jax.nn.sigmoid(a),), [(c_all, D, 0)], [], [(D, F32)], "silu_c", tm=N_DEV)
    mod_part = jnp.stack([_mm(c_act, w["w_ada"][l], "nn", name="mm_ada") for l in range(L)])
    (mod_all,) = _exchange([mod_part], False, "gather_mod")
    mod_mine = lax.dynamic_index_in_dim(mod_all, me, axis=2, keepdims=False)
    mod_full = jnp.transpose(mod_mine, (1, 0, 2)).reshape(L, 6 * D) + w["b_ada"]
    mods = [[mod_full[l:l + 1, i * D:(i + 1) * D] for i in range(6)] for l in range(L)]

    cs, sn = _rope_tables(positions[0])
    lower, lower_vjp = jax.vjp(lambda lg: jnp.cumsum(jax.nn.softmax(lg, axis=0), axis=0)
                               - jax.nn.softmax(lg, axis=0)[0:1], w["hg_lb_logits"])
    conv_w_full = []

    def shard(n, l):
        return w[n][l] if n == "ffn_conv_w" else w[n][l].astype(BF16)

    def layer_params(l, early):
        Wl = {n: _ASSEMBLE[n](early[n]) for n in _EARLY[:-1]}
        cw_full = _cols_from_shards(early["ffn_conv_w"])
        s5_args = (w["s5_lambda_re"][l], w["s5_lambda_im"][l], w["s5_log_dt"][l], w["s5_b_re"][l], w["s5_b_im"][l],
                   w["s5_c_re"][l], w["s5_c_im"][l])
        (lbr, lbi, bd, cdt), prep_vjp = jax.vjp(_s5_prep, *s5_args)
        tab_fwd, tab_rev = _s5_tables(*s5_args[:3])
        Wl["bd"], Wl["cdt"] = bd.astype(BF16), cdt.astype(BF16)
        row = lambda a: a.reshape(1, -1)
        Pl = dict(
            n1=row(w["mix_pre_norm"][l]), n2=row(w["mix_post_norm"][l]), n3=row(w["ffn_pre_norm"][l]),
            n4=row(w["ffn_post_norm"][l]), s5_d=row(w["s5_d"][l]), q_norm=row(w["mla_q_norm"][l]),
            kv_norm=row(w["mla_kv_norm"][l]), lb=row(lower[l]), hg_on=row(w["hg_out_norm"][l]),
            conv_w=_pad_ff(cw_full), conv_b=_pad_ff(row(w["ffn_conv_b"][l])),
            tab_fwd=tab_fwd, tab_rev=tab_rev, cs=cs, sn=sn, lam_bar=(lbr, lbi))
        return Wl, Pl, prep_vjp, cw_full

    Ws, Ps, preps, Rs = [], [], [], []
    h = x
    half = w["w_up"].shape[1] // 2
    early = dict(zip(_EARLY, _exchange([shard(n, 0) for n in _EARLY], False, "gather_weights")))
    for l in range(L):
        Wl, Pl, prep_vjp, cw_full = layer_params(l, early)
        Ws.append(Wl)
        Ps.append(Pl)
        preps.append(prep_vjp)
        conv_w_full.append(cw_full)
        riders, stash, early = _Riders(), {}, {}

        def got_flash(got, Wl=Wl, stash=stash):
            Wl["w_out"] = _ASSEMBLE["w_out"](got[0])
            stash["up"] = got[1]

        def got_hg(got, Wl=Wl, stash=stash):
            Wl["w_up"] = _ASSEMBLE["w_up"](jnp.concatenate([stash["up"], got[0]], axis=1))

        def got_up(got, Wl=Wl):
            Wl["w_down"] = _ASSEMBLE["w_down"](got[0])

        riders.add("flash_fwd", lambda _, l=l: ([shard("w_out", l), shard("w_up", l)[:half]], False, [False, True]),
                   got_flash)
        riders.add("hgrn2_fwd", lambda _, l=l: ([shard("w_up", l)[half:]], False, [True]), got_hg)
        riders.add("mm_up", lambda _, l=l: ([shard("w_down", l)], False), got_up)
        if l + 1 < L:
            riders.add("conv_fwd", lambda _, l=l: ([shard(n, l + 1) for n in _EARLY], False),
                       lambda got, early=early: early.update(zip(_EARLY, got)))
        h, R = _layer_fwd(h, mods[l], Wl, Pl, riders)
        Rs.append(R)
    loss_local, dh = _loss_head(h, loss_target[0])
    loss = lax.psum(loss_local, ("x", "y", "c"))

    big_recv = {n: [None] * L for n in _BIG}
    small_g = {n: [None] * L for n in _SMALL if n != "hg_lb_logits"}
    dlower = [None] * L
    late = ("w_out", "w_glu", "w_uq", "w_ukv")
    pending_in = None
    for l in reversed(range(L)):
        riders = _Riders()

        def store(names, l=l):
            def done(got):
                for n, r in zip(names, got):
                    big_recv[n][l] = r
            return done

        def make_late(G, prev=pending_in):
            return [_GRAD_SHARDS[n](G[n]) for n in late] + ([prev[1]] if prev else []), True

        def done_late(got, l=l, prev=pending_in):
            store(late, l)(got)
            if prev:
                big_recv["w_in"][prev[0]] = got[len(late)]

        riders.add("flash_bwd_dq", lambda G: ([_GRAD_SHARDS["w_down"](G["w_down"])], True), store(("w_down",)))
        riders.add("flash_bwd_dkv", lambda G: ([_GRAD_SHARDS["w_up"](G["w_up"])], True, [True]), store(("w_up",)))
        riders.add("hgrn2_bwd", make_late, done_late)
        dh, dmod, G, sm = _layer_bwd(dh, mods[l], Ws[l], Ps[l], Rs[l], riders)
        Rs[l] = None
        pending_in = (l, _GRAD_SHARDS["w_in"](G["w_in"]))
        lbr, lbi = Ps[l]["lam_bar"]
        ar, ai = _ri_split(sm["acc"])
        dl = lax.complex(ar, ai) / lax.complex(lbr, -lbi)
        d_s5 = preps[l]((jnp.real(dl), jnp.imag(dl), sm["bd"], sm["cdt"]))
        for n, g in zip(("s5_lambda_re", "s5_lambda_im", "s5_log_dt", "s5_b_re", "s5_b_im", "s5_c_re", "s5_c_im"), d_s5):
            small_g[n][l] = g
        small_g["s5_d"][l] = sm["s5_d"][0]
        small_g["mla_q_norm"][l] = sm["q_norm"][0]
        small_g["mla_kv_norm"][l] = sm["kv_norm"][0]
        small_g["hg_out_norm"][l] = sm["hg_on"][0]
        small_g["mix_pre_norm"][l] = sm["n1"][0]
        small_g["mix_post_norm"][l] = sm["n2"][0]
        small_g["ffn_pre_norm"][l] = sm["n3"][0]
        small_g["ffn_post_norm"][l] = sm["n4"][0]
        small_g["ffn_conv_w_full"][l] = _unpad_ff(sm["conv_w"])
        small_g["ffn_conv_b"][l] = _unpad_ff(sm["conv_b"])[0]
        small_g["b_ada"][l] = dmod[0]
        dlower[l] = sm["lb"][0]
    (big_recv["w_in"][0],) = _exchange([pending_in[1]], True, "scatter_grads")
    small_g = {n: jnp.stack(gl) for n, gl in small_g.items()}
    (small_g["hg_lb_logits"],) = lower_vjp(jnp.stack(dlower))

    small_w = {n: w[n] for n in _SMALL if n != "ffn_conv_w_full"}
    small_w["ffn_conv_w_full"] = jnp.stack(conv_w_full)
    shapes = [small_w[n].shape for n in _SMALL]
    zeros_cw = jnp.zeros_like(small_w["ffn_conv_w_full"])
    pk_g = _pack([small_g[n] for n in _SMALL])
    pk_w = _pack([small_w[n] for n in _SMALL])
    pk_m = _pack([zeros_cw if n == "ffn_conv_w_full" else m[n] for n in _SMALL])
    pk_v = _pack([zeros_cw + 1.0 if n == "ffn_conv_w_full" else v[n] for n in _SMALL])
    (pk_all,) = _exchange([pk_g], False, "gather_small")
    sg, sd, sm_, sv = _adamw(pk_all[None], pk_w[None], pk_m[None], pk_v[None], name="adamw_small")
    small_out = {}
    for key, arr in (("g", sg), ("d", sd), ("m", sm_), ("v", sv)):
        small_out[key] = dict(zip(_SMALL, _unpack(arr[0], shapes)))

    n_cw = w["ffn_conv_w"].shape[-1]
    g_cw = lax.dynamic_slice_in_dim(small_out["g"]["ffn_conv_w_full"], me * n_cw, n_cw, axis=2)
    cw_out = _adamw(g_cw[:, None], w["ffn_conv_w"], m["ffn_conv_w"], v["ffn_conv_w"], name="adamw_conv_w")

    n_ada = w["w_ada"].shape[-1]
    flat_all = pk_all.reshape(N_DEV, -1)
    off = sum(int(np.prod(s)) for s in shapes[:-1])
    dmod_all = flat_all[:, off:off + L * 6 * D].reshape(N_DEV, L, 6 * D)
    dmod_cols = lax.dynamic_slice_in_dim(dmod_all, me * n_ada, n_ada, axis=2)
    g_ada = jnp.stack([_mm(c_act, dmod_cols[:, l], "tn", name="mm_ada_dw") for l in range(L)])
    ada_out = _adamw(g_ada[:, None], w["w_ada"], m["w_ada"], v["w_ada"], name="adamw_ada")

    big_out = {}
    for n in _BIG:
        recv = jnp.stack(big_recv[n])
        big_out[n] = _adamw(recv, w[n], m[n], v[n], name="adamw_" + n)
    big_out["w_ada"] = ada_out
    big_out["ffn_conv_w"] = cw_out
    return loss, dh[None], big_out, small_out


_WEIGHTS = ("w_in", "s5_lambda_re", "s5_lambda_im", "s5_log_dt", "s5_b_re", "s5_b_im", "s5_c_re", "s5_c_im", "s5_d",
            "s5_w_glu", "mla_q_norm", "mla_w_uq", "mla_kv_norm", "mla_w_ukv", "hg_lb_logits", "hg_out_norm", "w_out",
            "mix_pre_norm", "mix_post_norm", "ffn_pre_norm", "ffn_post_norm", "ffn_w_up", "ffn_conv_w", "ffn_conv_b",
            "ffn_w_down", "w_ada", "b_ada")
_ALIAS = {"s5_w_glu": "w_glu", "mla_w_uq": "w_uq", "mla_w_ukv": "w_ukv", "ffn_w_up": "w_up", "ffn_w_down": "w_down"}


def kernel(x, c, positions, w_in, s5_lambda_re, s5_lambda_im, s5_log_dt, s5_b_re, s5_b_im, s5_c_re, s5_c_im, s5_d, s5_w_glu, mla_q_norm, mla_w_uq, mla_kv_norm, mla_w_ukv, hg_lb_logits, hg_out_norm, w_out, mix_pre_norm, mix_post_norm, ffn_pre_norm, ffn_post_norm, ffn_w_up, ffn_conv_w, ffn_conv_b, ffn_w_down, w_ada, b_ada, loss_target, m_w_in, m_s5_lambda_re, m_s5_lambda_im, m_s5_log_dt, m_s5_b_re, m_s5_b_im, m_s5_c_re, m_s5_c_im, m_s5_d, m_s5_w_glu, m_mla_q_norm, m_mla_w_uq, m_mla_kv_norm, m_mla_w_ukv, m_hg_lb_logits, m_hg_out_norm, m_w_out, m_mix_pre_norm, m_mix_post_norm, m_ffn_pre_norm, m_ffn_post_norm, m_ffn_w_up, m_ffn_conv_w, m_ffn_conv_b, m_ffn_w_down, m_w_ada, m_b_ada, v_w_in, v_s5_lambda_re, v_s5_lambda_im, v_s5_log_dt, v_s5_b_re, v_s5_b_im, v_s5_c_re, v_s5_c_im, v_s5_d, v_s5_w_glu, v_mla_q_norm, v_mla_w_uq, v_mla_kv_norm, v_mla_w_ukv, v_hg_lb_logits, v_hg_out_norm, v_w_out, v_mix_pre_norm, v_mix_post_norm, v_ffn_pre_norm, v_ffn_post_norm, v_ffn_w_up, v_ffn_conv_w, v_ffn_conv_b, v_ffn_w_down, v_w_ada, v_b_ada):
    args = locals()
    key = lambda n: _ALIAS.get(n, n)
    w = {key(n): args[n] for n in _WEIGHTS}
    m = {key(n): args["m_" + n] for n in _WEIGHTS}
    v = {key(n): args["v_" + n] for n in _WEIGHTS}
    loss, grad_x, big, small = _step(x, c, positions, loss_target, w, m, v)

    def pick(n, idx):
        k = key(n)
        if k in big:
            return big[k][idx].reshape(w[k].shape)
        return small["gdmv"[idx]][k]

    outs = [loss, grad_x]
    for idx in range(4):
        outs += [pick(n, idx) for n in _WEIGHTS]
    return tuple(outs)
```

```python
import functools
import math

import numpy as np
import jax
import jax.numpy as jnp
from jax import lax
from jax.experimental import pallas as pl
from jax.experimental.pallas import tpu as pltpu

F32 = jnp.float32
BF16 = jnp.bfloat16
N_DEV = 8
V7X_VMEM_LIMIT = 56 * 1024 * 1024
MM_VMEM_BUDGET = 28 * 1024 * 1024
LANE = 128

D_MODEL = 2048
S5_W = 512
S5_G = 32
S5_C = 16
S5_P = 64
S5_N = S5_G * S5_P
S5_TL = 512
MLA_H = 8
MLA_NOPE = 128
MLA_ROPE = 64
MLA_V = 128
MLA_HW = 256
HG_H = 4
HG_D = 128
HG_CH = 16
D_FF = 5504
D_FFP = 5632
D_IN = 3392
D_INP = 3456
EPS = 1e-6
MASK_VALUE = -1e30
ROPE_THETA = 10000.0
ATT_SCALE = (MLA_NOPE + MLA_ROPE) ** -0.5

ADAM_LR = 0.001
ADAM_B1 = 0.9
ADAM_B2 = 0.999
ADAM_EPS = 1e-08
ADAM_WD = 0.01
ADAM_STEP = 10

_IN_PERM = np.concatenate([np.arange(0, 1024), np.arange(1344, 3392), np.arange(1024, 1344)])
_IN_INV = np.argsort(_IN_PERM)

_NN = (((1,), (0,)), ((), ()))
_NT = (((1,), (1,)), ((), ()))
_TN = (((0,), (0,)), ((), ()))


def _cp(*sem):
    return pltpu.CompilerParams(dimension_semantics=sem, vmem_limit_bytes=V7X_VMEM_LIMIT)


def _tile(n, cap, align=LANE):
    if n <= cap:
        return n
    t = (cap // align) * align
    while t >= align:
        if n % t == 0:
            return t
        t -= align
    raise ValueError(f"no tile for {n}")


def _bdot(a, b, dims):
    return lax.dot_general(a.astype(BF16), b.astype(BF16), dims, preferred_element_type=F32)


def _mm(a, b, mode, out_dtype=F32, a_col0=0, a_cols=None, name="mm", exch=None):
    if mode == "tn":
        K = a.shape[0]
        M = a_cols if a_cols is not None else a.shape[1]
        N = b.shape[1]
    else:
        M = a.shape[0]
        K = a_cols if a_cols is not None else a.shape[1]
        N = b.shape[0] if mode == "nt" else b.shape[1]
    tm = _tile(M, 1024, 8 if M < LANE else LANE)
    tn = _tile(N, 1024)
    kal = 8 if K < LANE else LANE
    tk = _tile(K, 2048, kal)
    osz = jnp.dtype(out_dtype).itemsize

    def vmem(tk_):
        acc = 0 if tk_ == K else tm * tn * 4
        return 2 * tk_ * (tm * a.dtype.itemsize + tn * b.dtype.itemsize) + 2 * tm * tn * osz + acc

    while vmem(tk) > MM_VMEM_BUDGET and tk > kal:
        tk = _tile(K, tk - kal, kal)
    nk = K // tk
    if mode == "tn":
        assert a_col0 % tm == 0
        a_spec = pl.BlockSpec((tk, tm), lambda i, j, k: (k, i + a_col0 // tm))
        b_spec = pl.BlockSpec((tk, tn), lambda i, j, k: (k, j))
        dims = _TN
    else:
        assert a_col0 % tk == 0
        a_spec = pl.BlockSpec((tm, tk), lambda i, j, k: (i, k + a_col0 // tk))
        if mode == "nn":
            b_spec = pl.BlockSpec((tk, tn), lambda i, j, k: (k, j))
            dims = _NN
        else:
            b_spec = pl.BlockSpec((tn, tk), lambda i, j, k: (j, k))
            dims = _NT

    if nk == 1:
        def body(a_ref, b_ref, o_ref):
            o_ref[...] = _bdot(a_ref[...], b_ref[...], dims).astype(o_ref.dtype)

        scratch = []
    else:
        def body(a_ref, b_ref, o_ref, acc_ref):
            k = pl.program_id(2)

            @pl.when(k == 0)
            def _():
                acc_ref[...] = jnp.zeros_like(acc_ref)

            acc_ref[...] += _bdot(a_ref[...], b_ref[...], dims)

            @pl.when(k == nk - 1)
            def _():
                o_ref[...] = acc_ref[...].astype(o_ref.dtype)

        scratch = [pltpu.VMEM((tm, tn), F32)]

    (out,), got = _call_hosting(
        body, name, (M // tm, N // tn, nk), [a_spec, b_spec], [pl.BlockSpec((tm, tn), lambda i, j, k: (i, j))],
        [jax.ShapeDtypeStruct((M, N), out_dtype)], [a, b], exch,
        scratch=scratch, sem=["parallel", "parallel", "arbitrary"])
    return out if exch is None else (out, got)


def _row_spec(tm, width, cb):
    return pl.BlockSpec((tm, width), lambda i: (i, cb))


def _rowwise(fn, rows, params, outs, name, tm=256):
    S = rows[0][0].shape[0]
    nr, npar = len(rows), len(params)

    def body(*refs):
        xs = [r[...].astype(F32) for r in refs[:nr]]
        ps = [p[...] for p in refs[nr:nr + npar]]
        res = fn(*xs, *ps)
        for o, r in zip(refs[nr + npar:], res):
            o[...] = r.astype(o.dtype)

    res = pl.pallas_call(
        body,
        name=name,
        grid=(S // tm,),
        in_specs=[_row_spec(tm, w, cb) for _, w, cb in rows]
        + [pl.BlockSpec(p.shape, lambda i: (0, 0)) for p in params],
        out_specs=[_row_spec(tm, w, 0) for w, _ in outs],
        out_shape=[jax.ShapeDtypeStruct((S, w), dt) for w, dt in outs],
        compiler_params=_cp("parallel"),
    )(*[r[0] for r in rows], *params)
    return list(res)


def _rowwise_vjp(fn, rows, params, cts, name, row_grads, add_rows=None, tm=256):
    S = rows[0][0].shape[0]
    add_rows = add_rows or {}
    nr, npar = len(rows), len(params)
    flat_cts = [c for group in cts for c in group]
    ncts = len(flat_cts)
    add_keys = sorted(add_rows)
    nadd = len(add_keys)
    grad_idx = [i for i in range(nr) if row_grads[i]]

    def body(*refs):
        i = pl.program_id(0)
        xs = [r[...].astype(F32) for r in refs[:nr]]
        ps = [p[...] for p in refs[nr:nr + npar]]
        ct_refs = refs[nr + npar:nr + npar + ncts]
        add_refs = refs[nr + npar + ncts:nr + npar + ncts + nadd]
        out_refs = refs[nr + npar + ncts + nadd:]
        ct_vals, pos = [], 0
        for group in cts:
            v = ct_refs[pos][...].astype(F32)
            for r in ct_refs[pos + 1:pos + len(group)]:
                v = v + r[...].astype(F32)
            pos += len(group)
            ct_vals.append(v)
        _, vjp = jax.vjp(lambda *a: tuple(fn(*a)), *xs, *ps)
        grads = vjp(tuple(ct_vals))
        for o, gi in zip(out_refs[:len(grad_idx)], grad_idx):
            g = grads[gi]
            if gi in add_rows:
                g = g + add_refs[add_keys.index(gi)][...].astype(F32)
            o[...] = g.astype(o.dtype)
        dprefs = out_refs[len(grad_idx):]

        @pl.when(i == 0)
        def _():
            for dp in dprefs:
                dp[...] = jnp.zeros_like(dp)

        for dp, g in zip(dprefs, grads[nr:]):
            dp[...] += g

    res = pl.pallas_call(
        body,
        name=name,
        grid=(S // tm,),
        in_specs=[_row_spec(tm, w, cb) for _, w, cb in rows]
        + [pl.BlockSpec(p.shape, lambda i: (0, 0)) for p in params]
        + [_row_spec(tm, w, cb) for _, w, cb in flat_cts]
        + [_row_spec(tm, add_rows[k][1], add_rows[k][2]) for k in add_keys],
        out_specs=[_row_spec(tm, rows[gi][1], 0) for gi in grad_idx]
        + [pl.BlockSpec(p.shape, lambda i: (0, 0)) for p in params],
        out_shape=[jax.ShapeDtypeStruct((S, rows[gi][1]), F32) for gi in grad_idx]
        + [jax.ShapeDtypeStruct(p.shape, F32) for p in params],
        compiler_params=_cp("arbitrary"),
    )(*[r[0] for r in rows], *params, *[c[0] for c in flat_cts], *[add_rows[k][0] for k in add_keys])
    res = list(res)
    return res[:len(grad_idx)], res[len(grad_idx):]


def _rms(x, gain):
    return x * lax.rsqrt(jnp.mean(x * x, axis=-1, keepdims=True) + EPS) * gain


def _f_pre(x, gain, sc, sh):
    return (_rms(x, gain) * (1.0 + sc) + sh,)


def _f_post(x, y, gain, g):
    return (x + g * _rms(y, gain),)


def _f_norm(x, gain):
    return (_rms(x, gain),)


def _f_s5a(yc, u, d):
    return (jax.nn.gelu(yc + d * u, approximate=True),)


def _f_s5b(g, z):
    return (g * jax.nn.sigmoid(z),)


def _loss_head(y, target, tm=256):
    S, D = y.shape

    def body(y_ref, t_ref, dy_ref, acc_ref):
        i = pl.program_id(0)
        e = y_ref[...] - t_ref[...]
        dy_ref[...] = e * (1.0 / D)

        @pl.when(i == 0)
        def _():
            acc_ref[...] = jnp.zeros_like(acc_ref)

        acc_ref[...] += jnp.sum(e * e, axis=0, keepdims=True)

    dy, acc = pl.pallas_call(
        body,
        name="loss_head",
        grid=(S // tm,),
        in_specs=[_row_spec(tm, D, 0), _row_spec(tm, D, 0)],
        out_specs=[_row_spec(tm, D, 0), pl.BlockSpec((1, D), lambda i: (0, 0))],
        out_shape=[jax.ShapeDtypeStruct((S, D), F32), jax.ShapeDtypeStruct((1, D), F32)],
        compiler_params=_cp("arbitrary"),
    )(y, target)
    return 0.5 * jnp.sum(acc) / D, dy


def _s5_tile_scan(xr, xi, tab_ref, reverse, row8):
    for k in (1, 2, 4):
        pr = tab_ref[pl.ds(k - 1, 1), 0:S5_TL] if not reverse else tab_ref[pl.ds(8 - k, 1), 0:S5_TL]
        pi = tab_ref[pl.ds(k - 1, 1), S5_TL:2 * S5_TL] if not reverse else tab_ref[pl.ds(8 - k, 1), S5_TL:2 * S5_TL]
        if not reverse:
            keep = row8 >= k
            sr = jnp.where(keep, pltpu.roll(xr, k, 0), 0.0)
            si = jnp.where(keep, pltpu.roll(xi, k, 0), 0.0)
        else:
            keep = row8 < 8 - k
            sr = jnp.where(keep, pltpu.roll(xr, 8 - k, 0), 0.0)
            si = jnp.where(keep, pltpu.roll(xi, 8 - k, 0), 0.0)
        xr, xi = xr + pr * sr - pi * si, xi + pr * si + pi * sr
    return xr, xi


def _s5_scan(bu, tab, reverse=False, h=None, bu_fwd=None, tr=512):
    S = bu.shape[0]
    tr = min(tr, S)
    nl = S5_N // S5_TL
    nrb = S // tr
    w = 2 * S5_TL
    nt = tr // 8
    rmap = (lambda j, i: (i, j)) if not reverse else (lambda j, i: (nrb - 1 - i, j))

    def body(*refs):
        if reverse:
            x_ref, tab_ref, h_ref, b_ref, o_ref, acc_ref, cr_ref, ci_ref = refs
        else:
            x_ref, tab_ref, o_ref, cr_ref, ci_ref = refs
        i = pl.program_id(1)
        row8 = lax.broadcasted_iota(jnp.int32, (8, S5_TL), 0)

        @pl.when(i == 0)
        def _():
            cr_ref[...] = jnp.zeros_like(cr_ref)
            ci_ref[...] = jnp.zeros_like(ci_ref)
            if reverse:
                acc_ref[...] = jnp.zeros_like(acc_ref)

        tr_all = tab_ref[:, 0:S5_TL]
        ti_all = tab_ref[:, S5_TL:w]

        def tile(t, carry):
            tt = (nt - 1 - t) if reverse else t
            r = pl.ds(pl.multiple_of(tt * 8, 8), 8)
            xr, xi = _s5_tile_scan(x_ref[r, 0:S5_TL], x_ref[r, S5_TL:w], tab_ref, reverse, row8)
            cr = jnp.broadcast_to(cr_ref[...], (8, S5_TL))
            ci = jnp.broadcast_to(ci_ref[...], (8, S5_TL))
            hr = xr + tr_all * cr - ti_all * ci
            hi = xi + tr_all * ci + ti_all * cr
            o_ref[r, 0:S5_TL] = hr
            o_ref[r, S5_TL:w] = hi
            edge = pl.ds(tt * 8, 1) if reverse else pl.ds(tt * 8 + 7, 1)
            cr_ref[...] = o_ref[edge, 0:S5_TL]
            ci_ref[...] = o_ref[edge, S5_TL:w]
            if reverse:
                dr = h_ref[r, 0:S5_TL] - b_ref[r, 0:S5_TL]
                di = h_ref[r, S5_TL:w] - b_ref[r, S5_TL:w]
                acc_ref[:, 0:S5_TL] += hr * dr + hi * di
                acc_ref[:, S5_TL:w] += hi * dr - hr * di
            return carry

        lax.fori_loop(0, nt, tile, 0)

    blk = pl.BlockSpec((tr, w), rmap)
    in_specs = [blk, pl.BlockSpec((8, w), lambda j, i: (0, j))]
    out_specs = [blk]
    out_shape = [jax.ShapeDtypeStruct((S, 2 * S5_N), F32)]
    args = [bu, tab]
    if reverse:
        in_specs += [blk, blk]
        args += [h, bu_fwd]
        out_specs.append(pl.BlockSpec((8, w), lambda j, i: (0, j)))
        out_shape.append(jax.ShapeDtypeStruct((8, 2 * S5_N), F32))
    res = pl.pallas_call(
        body,
        name="s5_scan_bwd" if reverse else "s5_scan_fwd",
        grid=(nl, nrb),
        in_specs=in_specs,
        out_specs=out_specs,
        out_shape=out_shape,
        scratch_shapes=[pltpu.VMEM((1, S5_TL), F32), pltpu.VMEM((1, S5_TL), F32)],
        compiler_params=_cp("parallel", "arbitrary"),
    )(*args)
    return res if reverse else res[0]


def _ri_cols(re, im):
    lead = re.shape[:-1]
    nl = S5_N // S5_TL
    z = jnp.stack([re.reshape(*lead, nl, S5_TL), im.reshape(*lead, nl, S5_TL)], axis=-2)
    return z.reshape(*lead, 2 * S5_N)


def _ri_split(z):
    lead = z.shape[:-1]
    nl = S5_N // S5_TL
    z = z.reshape(*lead, nl, 2, S5_TL)
    return z[..., 0, :].reshape(*lead, S5_N), z[..., 1, :].reshape(*lead, S5_N)


def _s5_prep(lre, lim, logdt, bre, bim, cre, cim):
    lam = lax.complex(lre, lim)
    dt = jnp.exp(logdt)[:, None]
    lam_bar = jnp.exp(lam * dt)
    b = lax.complex(bre, bim)
    b_bar = ((lam_bar - 1.0) / lam)[..., None] * b
    eye = jnp.eye(S5_G, dtype=F32)
    bd_re = jnp.einsum("gpc,gh->gchp", jnp.real(b_bar), eye).reshape(S5_W, S5_N)
    bd_im = jnp.einsum("gpc,gh->gchp", jnp.imag(b_bar), eye).reshape(S5_W, S5_N)
    bd = _ri_cols(bd_re, bd_im)
    cd_re = jnp.einsum("gcp,gh->gchp", cre, eye).reshape(S5_W, S5_N)
    cd_im = jnp.einsum("gcp,gh->gchp", -cim, eye).reshape(S5_W, S5_N)
    cdt = _ri_cols(cd_re, cd_im)

    def diag(m):
        return jnp.stack([m[j * S5_BW:(j + 1) * S5_BW, j * 2 * S5_TL:(j + 1) * 2 * S5_TL] for j in range(S5_NB)])

    return jnp.real(lam_bar).reshape(1, S5_N), jnp.imag(lam_bar).reshape(1, S5_N), diag(bd), diag(cdt)


S5_NB = S5_N // S5_TL
S5_BW = S5_W // S5_NB


def _mm_bd(a, b, mode, name):
    S = a.shape[0]
    wide = 2 * S5_TL
    t = min(S, 1024)
    wspec = pl.BlockSpec((1, S5_BW, wide), lambda i, j: (j, 0, 0))
    if mode == "grad":
        def body(a_ref, b_ref, o_ref):
            @pl.when(pl.program_id(1) == 0)
            def _():
                o_ref[...] = jnp.zeros_like(o_ref)

            o_ref[0] += _bdot(a_ref[...], b_ref[...], _TN)

        return pl.pallas_call(
            body, name=name, grid=(S5_NB, S // t),
            in_specs=[pl.BlockSpec((t, S5_BW), lambda j, k: (k, j)), pl.BlockSpec((t, wide), lambda j, k: (k, j))],
            out_specs=pl.BlockSpec((1, S5_BW, wide), lambda j, k: (j, 0, 0)),
            out_shape=jax.ShapeDtypeStruct((S5_NB, S5_BW, wide), F32),
            compiler_params=_cp("parallel", "arbitrary"))(a, b)
    win, wout, dims = (S5_BW, wide, _NN) if mode == "expand" else (wide, S5_BW, _NT)

    def body(a_ref, w_ref, o_ref):
        o_ref[...] = _bdot(a_ref[...], w_ref[0], dims)

    return pl.pallas_call(
        body, name=name, grid=(S // t, S5_NB),
        in_specs=[pl.BlockSpec((t, win), lambda i, j: (i, j)), wspec],
        out_specs=pl.BlockSpec((t, wout), lambda i, j: (i, j)),
        out_shape=jax.ShapeDtypeStruct((S, S5_NB * wout), F32),
        compiler_params=_cp("parallel", "parallel"))(a, b)


def _s5_tables(lre, lim, logdt):
    lam = lax.complex(lre, lim)
    dt = jnp.exp(logdt)[:, None]
    k = jnp.arange(1, 9, dtype=F32)[:, None, None]
    pw = jnp.exp((lam * dt)[None] * k).reshape(8, S5_N)
    fwd = _ri_cols(jnp.real(pw), jnp.imag(pw))
    rev = _ri_cols(jnp.real(pw)[::-1], -jnp.imag(pw)[::-1])
    return fwd, rev


def _rope_tables(positions):
    inv_freq = 1.0 / (ROPE_THETA ** (jnp.arange(0, MLA_ROPE, 2, dtype=F32) / MLA_ROPE))
    ang = positions.astype(F32)[:, None] * inv_freq
    cos, sin = jnp.cos(ang), jnp.sin(ang)
    z = jnp.zeros_like(cos)
    cs = jnp.concatenate([cos, cos, z, z], axis=-1)
    sn = jnp.concatenate([-sin, sin, z, z], axis=-1)
    return cs, sn


def _rope_fwd(qraw, kvraw, proj, cs, sn, tm=256):
    S = qraw.shape[0]
    HW = MLA_H * MLA_HW

    def rope(x, c, s):
        lane = lax.broadcasted_iota(jnp.int32, x.shape, 1)
        sw = jnp.where(lane < 32, pltpu.roll(x, 96, 1), jnp.where(lane < 64, pltpu.roll(x, 32, 1), 0.0))
        return x * c + sw * s

    def body(q_ref, kv_ref, kr_ref, cs_ref, sn_ref, qo_ref, ko_ref, vo_ref):
        c, s = cs_ref[...], sn_ref[...]
        kr = rope(kr_ref[...], c, s).astype(BF16)
        for h in range(MLA_H):
            o = h * MLA_HW
            qo_ref[:, o:o + 128] = q_ref[:, o:o + 128].astype(BF16)
            qo_ref[:, o + 128:o + 256] = rope(q_ref[:, o + 128:o + 256], c, s).astype(BF16)
            ko_ref[:, o:o + 128] = kv_ref[:, o:o + 128].astype(BF16)
            ko_ref[:, o + 128:o + 256] = kr
            vo_ref[:, h * 128:(h + 1) * 128] = kv_ref[:, o + 128:o + 256].astype(BF16)

    return pl.pallas_call(
        body,
        name="rope_fwd",
        grid=(S // tm,),
        in_specs=[_row_spec(tm, HW, 0), _row_spec(tm, HW, 0), _row_spec(tm, 128, (D_INP - 128) // 128),
                  _row_spec(tm, 128, 0), _row_spec(tm, 128, 0)],
        out_specs=[_row_spec(tm, HW, 0), _row_spec(tm, HW, 0), _row_spec(tm, MLA_H * MLA_V, 0)],
        out_shape=[jax.ShapeDtypeStruct((S, HW), BF16), jax.ShapeDtypeStruct((S, HW), BF16),
                   jax.ShapeDtypeStruct((S, MLA_H * MLA_V), BF16)],
        compiler_params=_cp("parallel"),
    )(qraw, kvraw, proj, cs, sn)


def _rope_bwd(dq, dk, dv, cs, sn, tm=256):
    S = dq.shape[0]
    HW = MLA_H * MLA_HW

    def rope_t(x, c, s):
        lane = lax.broadcasted_iota(jnp.int32, x.shape, 1)
        w = x * s
        sw = jnp.where(lane < 32, pltpu.roll(w, 96, 1), jnp.where(lane < 64, pltpu.roll(w, 32, 1), 0.0))
        return x * c + sw

    def body(dq_ref, dk_ref, dv_ref, cs_ref, sn_ref, qo_ref, kvo_ref, kro_ref):
        c, s = cs_ref[...], sn_ref[...]
        kr = jnp.zeros((tm, 128), F32)
        for h in range(MLA_H):
            o = h * MLA_HW
            qo_ref[:, o:o + 128] = dq_ref[:, o:o + 128]
            qo_ref[:, o + 128:o + 256] = rope_t(dq_ref[:, o + 128:o + 256], c, s)
            kvo_ref[:, o:o + 128] = dk_ref[:, o:o + 128]
            kvo_ref[:, o + 128:o + 256] = dv_ref[:, h * 128:(h + 1) * 128]
            kr = kr + dk_ref[:, o + 128:o + 256]
        kro_ref[...] = rope_t(kr, c, s)

    return pl.pallas_call(
        body,
        name="rope_bwd",
        grid=(S // tm,),
        in_specs=[_row_spec(tm, HW, 0), _row_spec(tm, HW, 0), _row_spec(tm, MLA_H * MLA_V, 0),
                  _row_spec(tm, 128, 0), _row_spec(tm, 128, 0)],
        out_specs=[_row_spec(tm, HW, 0), _row_spec(tm, HW, 0), _row_spec(tm, 128, 0)],
        out_shape=[jax.ShapeDtypeStruct((S, HW), F32), jax.ShapeDtypeStruct((S, HW), F32),
                   jax.ShapeDtypeStruct((S, 128), F32)],
        compiler_params=_cp("parallel"),
    )(dq, dk, dv, cs, sn)


ATT_T = 512


def _diag_mask(t):
    return lax.broadcasted_iota(jnp.int32, (t, t), 1) <= lax.broadcasted_iota(jnp.int32, (t, t), 0)


def _flash_fwd(q, k, v, exch=None):
    S = q.shape[0]
    t = min(ATT_T, S)
    nq = S // t

    def body(q_ref, k_ref, v_ref, o_ref, lse_ref):
        i = pl.program_id(1)
        qb = q_ref[...]

        def step(j, carry, masked):
            m, l, acc = carry
            r = pl.ds(pl.multiple_of(j * t, t), t)
            s = _bdot(qb, k_ref[r, :], _NT) * ATT_SCALE
            if masked:
                s = jnp.where(_diag_mask(t), s, MASK_VALUE)
            m_new = jnp.maximum(m, jnp.max(s, axis=-1, keepdims=True))
            alpha = jnp.exp(m - m_new)
            p = jnp.exp(s - m_new)
            l = alpha * l + jnp.sum(p, axis=-1, keepdims=True)
            acc = alpha * acc + _bdot(p, v_ref[r, :], _NN)
            return m_new, l, acc

        m0 = jnp.full((t, 1), MASK_VALUE, F32)
        init = (m0, jnp.zeros((t, 1), F32), jnp.zeros((t, MLA_V), F32))
        m, l, acc = step(i, lax.fori_loop(0, i, lambda j, c: step(j, c, False), init), True)
        o_ref[...] = acc / l
        lse_ref[...] = jnp.broadcast_to(m + jnp.log(l), (t, 128))

    return _call_hosting(
        body, "flash_fwd", (MLA_H, nq),
        [pl.BlockSpec((t, MLA_HW), lambda h, i: (i, h)),
         pl.BlockSpec((S, MLA_HW), lambda h, i: (0, h)),
         pl.BlockSpec((S, MLA_V), lambda h, i: (0, h))],
        [pl.BlockSpec((t, MLA_V), lambda h, i: (i, h)), pl.BlockSpec((t, 128), lambda h, i: (i, h))],
        [jax.ShapeDtypeStruct((S, MLA_H * MLA_V), F32), jax.ShapeDtypeStruct((S, MLA_H * 128), F32)],
        [q, k, v], exch)


def _flash_bwd_dq(q, k, v, o, lse, dcat, exch=None):
    S = q.shape[0]
    t = min(ATT_T, S)
    nq = S // t
    do_cb = S5_W // MLA_V

    def body(q_ref, k_ref, v_ref, o_ref, lse_ref, do_ref, dq_ref):
        i = pl.program_id(1)
        qb = q_ref[...]
        do = do_ref[...]
        delta = jnp.sum(do * o_ref[...], axis=-1, keepdims=True)
        lse1 = jnp.max(lse_ref[...], axis=-1, keepdims=True)
        dob = do.astype(BF16)

        def step(j, dq, masked):
            r = pl.ds(pl.multiple_of(j * t, t), t)
            kb = k_ref[r, :]
            s = _bdot(qb, kb, _NT) * ATT_SCALE
            p = jnp.exp(s - lse1)
            if masked:
                p = jnp.where(_diag_mask(t), p, 0.0)
            dp = _bdot(dob, v_ref[r, :], _NT)
            ds = p * (dp - delta) * ATT_SCALE
            return dq + _bdot(ds, kb, _NN)

        dq = lax.fori_loop(0, i, lambda j, c: step(j, c, False), jnp.zeros((t, MLA_HW), F32))
        dq_ref[...] = step(i, dq, True)

    (dq,), got = _call_hosting(
        body, "flash_bwd_dq", (MLA_H, nq),
        [pl.BlockSpec((t, MLA_HW), lambda h, i: (i, h)),
         pl.BlockSpec((S, MLA_HW), lambda h, i: (0, h)),
         pl.BlockSpec((S, MLA_V), lambda h, i: (0, h)),
         pl.BlockSpec((t, MLA_V), lambda h, i: (i, h)),
         pl.BlockSpec((t, 128), lambda h, i: (i, h)),
         pl.BlockSpec((t, MLA_V), lambda h, i: (i, do_cb + h))],
        [pl.BlockSpec((t, MLA_HW), lambda h, i: (i, h))], [jax.ShapeDtypeStruct((S, MLA_H * MLA_HW), F32)],
        [q, k, v, o, lse, dcat], exch)
    return dq, got


def _flash_bwd_dkv(q, k, v, o, lse, dcat, exch=None):
    S = q.shape[0]
    t = min(ATT_T, S)
    nq = S // t
    do_cb = S5_W // MLA_V

    def body(q_ref, k_ref, v_ref, o_ref, lse_ref, do_ref, dk_ref, dv_ref):
        j = pl.program_id(1)
        kb = k_ref[...]
        vb = v_ref[...]

        def step(i, carry, masked):
            dk, dv = carry
            r = pl.ds(pl.multiple_of(i * t, t), t)
            qb = q_ref[r, :]
            do = do_ref[r, :]
            delta = jnp.sum(do * o_ref[r, :], axis=-1, keepdims=True)
            lse1 = jnp.max(lse_ref[r, :], axis=-1, keepdims=True)
            s = _bdot(qb, kb, _NT) * ATT_SCALE
            p = jnp.exp(s - lse1)
            if masked:
                p = jnp.where(_diag_mask(t), p, 0.0)
            dob = do.astype(BF16)
            dv = dv + _bdot(p, dob, _TN)
            dp = _bdot(dob, vb, _NT)
            ds = p * (dp - delta) * ATT_SCALE
            dk = dk + _bdot(ds, qb, _TN)
            return dk, dv

        first = step(j, (jnp.zeros((t, MLA_HW), F32), jnp.zeros((t, MLA_V), F32)), True)
        dk, dv = lax.fori_loop(j + 1, nq, lambda i, c: step(i, c, False), first)
        dk_ref[...] = dk
        dv_ref[...] = dv

    return _call_hosting(
        body, "flash_bwd_dkv", (MLA_H, nq),
        [pl.BlockSpec((S, MLA_HW), lambda h, j: (0, h)),
         pl.BlockSpec((t, MLA_HW), lambda h, j: (j, h)),
         pl.BlockSpec((t, MLA_V), lambda h, j: (j, h)),
         pl.BlockSpec((S, MLA_V), lambda h, j: (0, h)),
         pl.BlockSpec((S, 128), lambda h, j: (0, h)),
         pl.BlockSpec((S, MLA_V), lambda h, j: (0, do_cb + h))],
        [pl.BlockSpec((t, MLA_HW), lambda h, j: (j, h)), pl.BlockSpec((t, MLA_V), lambda h, j: (j, h))],
        [jax.ShapeDtypeStruct((S, MLA_H * MLA_HW), F32), jax.ShapeDtypeStruct((S, MLA_H * MLA_V), F32)],
        [q, k, v, o, lse, dcat], exch)


def _split3(x):
    x1 = x.astype(BF16)
    r1 = x - x1.astype(F32)
    x2 = r1.astype(BF16)
    x3 = (r1 - x2.astype(F32)).astype(BF16)
    return x1, x2, x3


def _tri_matmul(x, upper):
    n = x.shape[0]
    r = lax.broadcasted_iota(jnp.int32, (n, n), 0)
    c = lax.broadcasted_iota(jnp.int32, (n, n), 1)
    tri = jnp.where((r <= c) if upper else (r >= c), 1.0, 0.0).astype(BF16)
    x1, x2, x3 = _split3(x)
    dot = lambda v: lax.dot_general(tri, v, _NN, preferred_element_type=F32)
    return dot(x1) + dot(x2) + dot(x3)


@jax.custom_vjp
def _cumsum_rows(x):
    return _tri_matmul(x, False)


def _cumsum_rows_fwd(x):
    return _tri_matmul(x, False), None


def _cumsum_rows_bwd(_, ct):
    return (_tri_matmul(ct, True),)


_cumsum_rows.defvjp(_cumsum_rows_fwd, _cumsum_rows_bwd)


def _hg_step(qin, fin, vin, gin, st, lb, on):
    n = qin.shape[0]
    sig = jax.nn.sigmoid(fin)
    g = jnp.log(lb + (1.0 - lb) * sig)
    k = (1.0 - lb) * jax.nn.sigmoid(-fin)
    q = qin * jax.nn.sigmoid(qin)
    b = _cumsum_rows(g)
    o = _bdot(q * jnp.exp(b), st, _NT)
    row = lax.broadcasted_iota(jnp.int32, (n, HG_D), 0)
    row1 = lax.broadcasted_iota(jnp.int32, (n, 1), 0)
    b_s = None
    for s in range(n):
        sel = row == s
        b_s = jnp.sum(jnp.where(sel, b, 0.0), axis=0, keepdims=True)
        k_s = jnp.sum(jnp.where(sel, k, 0.0), axis=0, keepdims=True)
        v_s = jnp.sum(jnp.where(sel, vin, 0.0), axis=0, keepdims=True)
        e = jnp.exp(jnp.minimum(b - b_s, 0.0))
        c = jnp.sum(q * e * k_s, axis=-1, keepdims=True)
        o = o + jnp.where(row1 >= s, c, 0.0) * v_s
    st_new = st * jnp.exp(b_s) + _bdot(vin, k * jnp.exp(b_s - b), _TN)
    y = _rms(o, on) * (gin * jax.nn.sigmoid(gin))
    return y, st_new


HG_W = HG_H * HG_D


def _hg_specs(tb, nb, reverse):
    rm = (lambda i: nb - 1 - i) if reverse else (lambda i: i)
    base = 1024 // HG_W
    return [pl.BlockSpec((tb, HG_W), lambda i, o=o: (rm(i), base + o)) for o in range(4)], rm


def _head(h):
    return slice(h * HG_D, (h + 1) * HG_D)


def _hg_fwd(proj, lb, on, exch=None, tb=256):
    S = proj.shape[0]
    nb = S // tb
    nc = tb // HG_CH
    in_specs, rm = _hg_specs(tb, nb, False)

    def body(q_ref, f_ref, v_ref, g_ref, lb_ref, on_ref, y_ref, sts_ref, st_ref):
        @pl.when(pl.program_id(0) == 0)
        def _():
            st_ref[...] = jnp.zeros_like(st_ref)

        def step(c, carry):
            r = pl.ds(pl.multiple_of(c * HG_CH, HG_CH), HG_CH)
            for h in range(HG_H):
                hs = _head(h)
                st = st_ref[h]
                sts_ref[h, c] = st
                y, st_new = _hg_step(q_ref[r, hs], f_ref[r, hs], v_ref[r, hs], g_ref[r, hs], st, lb_ref[:, hs],
                                     on_ref[...])
                y_ref[r, hs] = y
                st_ref[h] = st_new
            return carry

        lax.fori_loop(0, nc, step, 0)

    return _call_hosting(
        body, "hgrn2_fwd", (nb,),
        in_specs + [pl.BlockSpec((1, HG_W), lambda i: (0, 0)), pl.BlockSpec((1, HG_D), lambda i: (0, 0))],
        [pl.BlockSpec((tb, HG_W), lambda i: (i, 0)), pl.BlockSpec((HG_H, nc, HG_D, HG_D), lambda i: (0, i, 0, 0))],
        [jax.ShapeDtypeStruct((S, HG_W), F32), jax.ShapeDtypeStruct((HG_H, S // HG_CH, HG_D, HG_D), F32)],
        [proj, proj, proj, proj, lb, on], exch,
        scratch=[pltpu.VMEM((HG_H, HG_D, HG_D), F32)], sem=["arbitrary"])


def _hg_bwd(proj, sts, lb, on, dcat, exch=None, tb=256):
    S = proj.shape[0]
    nb = S // tb
    nc = tb // HG_CH
    in_specs, rm = _hg_specs(tb, nb, True)
    dy_cb = (S5_W + MLA_H * MLA_V) // HG_W

    def body(q_ref, f_ref, v_ref, g_ref, lb_ref, on_ref, sts_ref, dy_ref,
             dq_ref, df_ref, dv_ref, dg_ref, dlb_ref, don_ref, dst_ref):
        @pl.when(pl.program_id(0) == 0)
        def _():
            dst_ref[...] = jnp.zeros_like(dst_ref)
            dlb_ref[...] = jnp.zeros_like(dlb_ref)
            don_ref[...] = jnp.zeros_like(don_ref)

        def step(cc, carry):
            c = nc - 1 - cc
            r = pl.ds(pl.multiple_of(c * HG_CH, HG_CH), HG_CH)
            for h in range(HG_H):
                hs = _head(h)
                _, vjp = jax.vjp(_hg_step, q_ref[r, hs], f_ref[r, hs], v_ref[r, hs], g_ref[r, hs], sts_ref[h, c],
                                 lb_ref[:, hs], on_ref[...])
                dq, df, dv, dg, dst, dlb, don = vjp((dy_ref[r, hs], dst_ref[h]))
                dq_ref[r, hs] = dq
                df_ref[r, hs] = df
                dv_ref[r, hs] = dv
                dg_ref[r, hs] = dg
                dst_ref[h] = dst
                dlb_ref[:, hs] += dlb
                don_ref[:, hs] += don
            return carry

        lax.fori_loop(0, nc, step, 0)

    blk = pl.BlockSpec((tb, HG_W), lambda i: (rm(i), 0))
    par = pl.BlockSpec((1, HG_W), lambda i: (0, 0))
    return _call_hosting(
        body, "hgrn2_bwd", (nb,),
        in_specs + [par, pl.BlockSpec((1, HG_D), lambda i: (0, 0)),
                    pl.BlockSpec((HG_H, nc, HG_D, HG_D), lambda i: (0, rm(i), 0, 0)),
                    pl.BlockSpec((tb, HG_W), lambda i: (rm(i), dy_cb))],
        [blk, blk, blk, blk, par, par],
        [jax.ShapeDtypeStruct((S, HG_W), F32)] * 4 + [jax.ShapeDtypeStruct((1, HG_W), F32)] * 2,
        [proj, proj, proj, proj, lb, on, sts, dcat], exch,
        scratch=[pltpu.VMEM((HG_H, HG_D, HG_D), F32)], sem=["arbitrary"])


CONV_NC = 4
CONV_TC = D_FFP // CONV_NC


def _shift_down(cur, halo, k):
    tm = cur.shape[0]
    row = lax.broadcasted_iota(jnp.int32, cur.shape, 0)
    top = jnp.concatenate([pltpu.roll(halo, k, 0), jnp.zeros((tm - 8, cur.shape[1]), F32)], axis=0)
    return jnp.where(row < k, top, pltpu.roll(cur, k, 0))


def _shift_up(cur, halo, k):
    tm = cur.shape[0]
    row = lax.broadcasted_iota(jnp.int32, cur.shape, 0)
    bot = jnp.concatenate([jnp.zeros((tm - 8, cur.shape[1]), F32), pltpu.roll(halo, 8 - k, 0)], axis=0)
    return jnp.where(row >= tm - k, bot, pltpu.roll(cur, tm - k, 0))


def _conv3(cur, halo, w_ref, b_ref):
    return (b_ref[...] + _shift_down(cur, halo, 2) * w_ref[pl.ds(0, 1), :]
            + _shift_down(cur, halo, 1) * w_ref[pl.ds(1, 1), :] + cur * w_ref[pl.ds(2, 1), :])


def _conv_fwd(u0, cw, cb, exch=None, tm=256):
    S = u0.shape[0]
    nc = CONV_NC
    m8 = tm // 8
    prev = lambda i: jnp.maximum(i * m8 - 1, 0)

    def body(u_ref, p_ref, w_ref, b_ref, a_ref):
        on = (pl.program_id(1) > 0).astype(F32)
        u = _conv3(u_ref[...], p_ref[...] * on, w_ref, b_ref)
        a_ref[...] = (jax.nn.gelu(u[:, :CONV_TC], approximate=True) * u[:, CONV_TC:]).astype(a_ref.dtype)

    tc = CONV_TC
    (a,), got = _call_hosting(
        body, "conv_geglu_fwd", (nc, S // tm),
        [pl.BlockSpec((tm, 2 * tc), lambda j, i: (i, j)), pl.BlockSpec((8, 2 * tc), lambda j, i: (prev(i), j)),
         pl.BlockSpec((3, 2 * tc), lambda j, i: (0, j)), pl.BlockSpec((1, 2 * tc), lambda j, i: (0, j))],
        [pl.BlockSpec((tm, tc), lambda j, i: (i, j))], [jax.ShapeDtypeStruct((S, D_FFP), BF16)],
        [u0, u0, cw, cb], exch)
    return a, got


def _conv_bwd_elem(da, u0, cw, cb, tm=256):
    S = u0.shape[0]
    nc = CONV_NC
    m8 = tm // 8
    prev = lambda i: jnp.maximum(i * m8 - 1, 0)

    def body(u_ref, p_ref, w_ref, b_ref, da_ref, du_ref, dw_ref, db_ref):
        i = pl.program_id(1)
        cur = u_ref[...]
        halo = p_ref[...] * (i > 0).astype(F32)
        u = _conv3(cur, halo, w_ref, b_ref)
        _, vjp = jax.vjp(lambda a, b: jax.nn.gelu(a, approximate=True) * b, u[:, :CONV_TC], u[:, CONV_TC:])
        dug, duv = vjp(da_ref[...])
        du = jnp.concatenate([dug, duv], axis=1)
        du_ref[...] = du

        @pl.when(i == 0)
        def _():
            dw_ref[...] = jnp.zeros_like(dw_ref)
            db_ref[...] = jnp.zeros_like(db_ref)

        dw_ref[pl.ds(0, 1), :] += jnp.sum(du * _shift_down(cur, halo, 2), axis=0, keepdims=True)
        dw_ref[pl.ds(1, 1), :] += jnp.sum(du * _shift_down(cur, halo, 1), axis=0, keepdims=True)
        dw_ref[pl.ds(2, 1), :] += jnp.sum(du * cur, axis=0, keepdims=True)
        db_ref[...] += jnp.sum(du, axis=0, keepdims=True)

    tc = CONV_TC
    return pl.pallas_call(
        body,
        name="conv_geglu_bwd",
        grid=(nc, S // tm),
        in_specs=[pl.BlockSpec((tm, 2 * tc), lambda j, i: (i, j)), pl.BlockSpec((8, 2 * tc), lambda j, i: (prev(i), j)),
                  pl.BlockSpec((3, 2 * tc), lambda j, i: (0, j)), pl.BlockSpec((1, 2 * tc), lambda j, i: (0, j)),
                  pl.BlockSpec((tm, tc), lambda j, i: (i, j))],
        out_specs=[pl.BlockSpec((tm, 2 * tc), lambda j, i: (i, j)), pl.BlockSpec((3, 2 * tc), lambda j, i: (0, j)),
                   pl.BlockSpec((1, 2 * tc), lambda j, i: (0, j))],
        out_shape=[jax.ShapeDtypeStruct((S, 2 * D_FFP), F32), jax.ShapeDtypeStruct((3, 2 * D_FFP), F32),
                   jax.ShapeDtypeStruct((1, 2 * D_FFP), F32)],
        compiler_params=_cp("parallel", "arbitrary"),
    )(u0, u0, cw, cb, da)


def _conv_bwd_input(du, cw, tm=256):
    S = du.shape[0]
    nrb = S // tm
    m8 = tm // 8
    nxt = lambda i: jnp.minimum((i + 1) * m8, S // 8 - 1)

    def body(d_ref, n_ref, w_ref, o_ref):
        on = (pl.program_id(1) < nrb - 1).astype(F32)
        cur = d_ref[...]
        halo = n_ref[...] * on
        o_ref[...] = (cur * w_ref[pl.ds(2, 1), :] + _shift_up(cur, halo, 1) * w_ref[pl.ds(1, 1), :]
                      + _shift_up(cur, halo, 2) * w_ref[pl.ds(0, 1), :]).astype(o_ref.dtype)

    tc = 2 * CONV_TC
    return pl.pallas_call(
        body,
        name="conv_bwd_input",
        grid=(CONV_NC, nrb),
        in_specs=[pl.BlockSpec((tm, tc), lambda j, i: (i, j)), pl.BlockSpec((8, tc), lambda j, i: (nxt(i), j)),
                  pl.BlockSpec((3, tc), lambda j, i: (0, j))],
        out_specs=pl.BlockSpec((tm, tc), lambda j, i: (i, j)),
        out_shape=jax.ShapeDtypeStruct((S, 2 * D_FFP), BF16),
        compiler_params=_cp("parallel", "parallel"),
    )(du, du, cw)


def _exchange(arrs, scatter, name, ff=None):
    n = len(arrs)

    def body(*refs):
        args = (refs[:n], refs[n:2 * n], *refs[2 * n:], scatter, ff)
        _exchange_start(*args)
        _exchange_wait(*args)

    hbm = pl.BlockSpec(memory_space=pltpu.HBM)
    out_shape, sems = _exchange_shapes(arrs, scatter)
    return pl.pallas_call(
        body,
        name=name,
        in_specs=[hbm] * n,
        out_specs=[hbm] * n,
        out_shape=out_shape,
        scratch_shapes=sems,
    )(*arrs)


def _ff_slot(d):
    return (d % 4) * 2 + d // 4


def _exchange_copies(ins, outs, send, recv, loc, scatter, ff=None):
    x, y, c = lax.axis_index("x"), lax.axis_index("y"), lax.axis_index("c")
    me = 4 * x + 2 * y + c
    sends, recvs, locs = [], [], []
    for a in range(len(ins)):
        slot = _ff_slot if ff and ff[a] else (lambda d: d)
        mine = me if scatter else slot(me)
        locs.append(pltpu.make_async_copy(ins[a].at[slot(me)] if scatter else ins[a], outs[a].at[mine], loc.at[a]))
        for k in range(1, N_DEV):
            px = 1 - x if k & 4 else x
            py = 1 - y if k & 2 else y
            pc = 1 - c if k & 1 else c
            peer = 4 * px + 2 * py + pc
            src = ins[a].at[slot(peer)] if scatter else ins[a]
            sems = dict(send_sem=send.at[a, k - 1], recv_sem=recv.at[a, k - 1], device_id=(px, py, pc),
                        device_id_type=pl.DeviceIdType.MESH)
            sends.append(pltpu.make_async_remote_copy(src_ref=src, dst_ref=outs[a].at[mine], **sems))
            theirs = peer if scatter else slot(peer)
            recvs.append(pltpu.make_async_remote_copy(src_ref=src, dst_ref=outs[a].at[theirs], **sems))
    return locs, sends, recvs


def _exchange_start(*refs):
    locs, sends, _ = _exchange_copies(*refs)
    for cp in locs + sends:
        cp.start()


def _exchange_wait(*refs):
    locs, sends, recvs = _exchange_copies(*refs)
    for cp in recvs:
        cp.wait_recv()
    for cp in sends:
        cp.wait_send()
    for cp in locs:
        cp.wait()


def _exchange_shapes(arrs, scatter):
    n = len(arrs)
    out_shape = [jax.ShapeDtypeStruct(a.shape if scatter else (N_DEV,) + a.shape, a.dtype) for a in arrs]
    sems = [pltpu.SemaphoreType.DMA((n, N_DEV - 1)), pltpu.SemaphoreType.DMA((n, N_DEV - 1)),
            pltpu.SemaphoreType.DMA((n,))]
    return out_shape, sems


def _call_hosting(body, name, grid, in_specs, out_specs, out_shape, args, exch, scratch=(), sem=None):
    scratch = list(scratch)
    if exch is None:
        res = pl.pallas_call(body, name=name, grid=grid, in_specs=in_specs, out_specs=out_specs, out_shape=out_shape,
                             scratch_shapes=scratch,
                             compiler_params=_cp(*(sem or ["parallel"] * len(grid))))(*args)
        return list(res), []
    arrs, scatter, ff = (*exch, None)[:3]
    n, n_in, n_out, n_scr = len(arrs), len(in_specs), len(out_specs), len(scratch)
    hbm = pl.BlockSpec(memory_space=pltpu.HBM)
    x_shape, sems = _exchange_shapes(arrs, scatter)

    def hosting_body(*refs):
        cin, xin = refs[:n_in], refs[n_in:n_in + n]
        cout, xout = refs[n_in + n:n_in + n + n_out], refs[n_in + n + n_out:n_in + 2 * n + n_out]
        cscr = refs[n_in + 2 * n + n_out:n_in + 2 * n + n_out + n_scr]
        xsem = refs[n_in + 2 * n + n_out + n_scr:]
        ids = [pl.program_id(d) for d in range(len(grid))]
        first = functools.reduce(jnp.logical_and, [i == 0 for i in ids])
        last = functools.reduce(jnp.logical_and, [i == g - 1 for i, g in zip(ids, grid)])

        @pl.when(first)
        def _():
            _exchange_start(xin, xout, *xsem, scatter, ff)

        body(*cin, *cout, *cscr)

        @pl.when(last)
        def _():
            _exchange_wait(xin, xout, *xsem, scatter, ff)

    res = pl.pallas_call(
        hosting_body, name=name + "_x", grid=grid, in_specs=in_specs + [hbm] * n, out_specs=out_specs + [hbm] * n,
        out_shape=out_shape + x_shape, scratch_shapes=scratch + sems,
        compiler_params=_cp(*["arbitrary"] * len(grid)))(*args, *arrs)
    return list(res[:n_out]), list(res[n_out:])


class _Riders:
    def __init__(self):
        self.make, self.done = {}, {}

    def add(self, host, make, done):
        self.make[host], self.done[host] = make, done

    def give(self, host, ctx=None):
        return self.make[host](ctx) if host in self.make else None

    def take(self, host, got):
        if host in self.done:
            self.done[host](got)


def _adamw(recv, w, m, v, name="adamw"):
    L, n, R, C = recv.shape
    fits = [t for t in range(8, R + 1, 8) if R % t == 0 and t * C * 4 <= (1 << 19)]
    tr = max(fits) if fits else R

    def body(r_ref, w_ref, m_ref, v_ref, g_ref, d_ref, mo_ref, vo_ref):
        g = r_ref[0, 0].astype(F32)
        for d in range(1, n):
            g = g + r_ref[0, d].astype(F32)
        mm = ADAM_B1 * m_ref[0] + (1.0 - ADAM_B1) * g
        vv = ADAM_B2 * v_ref[0] + (1.0 - ADAM_B2) * (g * g)
        m_hat = mm / (1.0 - ADAM_B1 ** ADAM_STEP)
        v_hat = vv / (1.0 - ADAM_B2 ** ADAM_STEP)
        g_ref[0] = g
        d_ref[0] = -ADAM_LR * (m_hat / (jnp.sqrt(v_hat) + ADAM_EPS) + ADAM_WD * w_ref[0])
        mo_ref[0] = mm
        vo_ref[0] = vv

    blk = pl.BlockSpec((1, tr, C), lambda l, i: (l, i, 0))
    return pl.pallas_call(
        body,
        name=name,
        grid=(L, R // tr),
        in_specs=[pl.BlockSpec((1, n, tr, C), lambda l, i: (l, 0, i, 0)), blk, blk, blk],
        out_specs=[blk] * 4,
        out_shape=[jax.ShapeDtypeStruct((L, R, C), F32)] * 4,
        compiler_params=_cp("parallel", "parallel"),
    )(recv, w, m, v)


def _layer_fwd(x, mod, W, P, riders=None):
    riders = riders or _Riders()
    sh1, sc1, g1, sh2, sc2, g2 = mod
    D = D_MODEL
    R = {"x": x}
    (h1,) = _rowwise(_f_pre, [(x, D, 0)], [P["n1"], sc1, sh1], [(D, BF16)], "pre_norm")
    proj = _mm(h1, W["w_in"], "nn", name="mm_in")
    R["h1"], R["proj"] = h1, proj
    bu = _mm_bd(proj, W["bd"], "expand", "mm_s5_b")
    hs = _s5_scan(bu, P["tab_fwd"])
    yc = _mm_bd(hs, W["cdt"], "reduce", "mm_s5_c")
    (gg,) = _rowwise(_f_s5a, [(yc, S5_W, 0), (proj, S5_W, 0)], [P["s5_d"]], [(S5_W, F32)], "s5_gelu")
    z = _mm(gg, W["w_glu"], "nn", name="mm_glu")
    (ys5,) = _rowwise(_f_s5b, [(gg, S5_W, 0), (z, S5_W, 0)], [], [(S5_W, BF16)], "s5_glu")
    R.update(bu=bu, hs=hs, yc=yc, gg=gg, z=z)
    (qn,) = _rowwise(_f_norm, [(proj, 512, 1)], [P["q_norm"]], [(512, BF16)], "q_norm")
    (kvn,) = _rowwise(_f_norm, [(proj, 256, 12)], [P["kv_norm"]], [(256, BF16)], "kv_norm")
    qraw = _mm(qn, W["w_uq"], "nn", name="mm_uq")
    kvraw = _mm(kvn, W["w_ukv"], "nn", name="mm_ukv")
    q, k, v = _rope_fwd(qraw, kvraw, proj, P["cs"], P["sn"])
    (o, lse), got = _flash_fwd(q, k, v, riders.give("flash_fwd"))
    riders.take("flash_fwd", got)
    R.update(qn=qn, kvn=kvn, q=q, k=k, v=v, o=o, lse=lse)
    (yhg, sts), got = _hg_fwd(proj, P["lb"], P["hg_on"], riders.give("hgrn2_fwd"))
    riders.take("hgrn2_fwd", got)
    R["sts"] = sts
    cat = jnp.concatenate([ys5, o.astype(BF16), yhg.astype(BF16)], axis=-1)
    mixed = _mm(cat, W["w_out"], "nn", name="mm_out")
    (x2,) = _rowwise(_f_post, [(x, D, 0), (mixed, D, 0)], [P["n2"], g1], [(D, F32)], "post_norm")
    R.update(cat=cat, mixed=mixed, x2=x2)
    (h2,) = _rowwise(_f_pre, [(x2, D, 0)], [P["n3"], sc2, sh2], [(D, BF16)], "pre_norm")
    rider = riders.give("mm_up")
    u0 = _mm(h2, W["w_up"], "nn", name="mm_up", exch=rider)
    if rider is not None:
        u0, got = u0
        riders.take("mm_up", got)
    a, got = _conv_fwd(u0, P["conv_w"], P["conv_b"], riders.give("conv_fwd"))
    riders.take("conv_fwd", got)
    y = _mm(a, W["w_down"], "nn", name="mm_down")
    (x3,) = _rowwise(_f_post, [(x2, D, 0), (y, D, 0)], [P["n4"], g2], [(D, F32)], "post_norm")
    R.update(h2=h2, u0=u0, a=a, y=y)
    return x3, R


def _layer_bwd(dx3, mod, W, P, R, riders=None):
    riders = riders or _Riders()
    sh1, sc1, g1, sh2, sc2, g2 = mod
    D = D_MODEL
    G = {}
    (dx2a, dy), (dn4, dg2) = _rowwise_vjp(_f_post, [(R["x2"], D, 0), (R["y"], D, 0)], [P["n4"], g2],
                                          [[(dx3, D, 0)]], "post_norm_bwd", [True, True])
    da = _mm(dy, W["w_down"], "nt", name="mm_down_dx")
    G["w_down"] = _mm(R["a"], dy, "tn", out_dtype=BF16, name="mm_down_dw")
    du, dcw, dcb = _conv_bwd_elem(da, R["u0"], P["conv_w"], P["conv_b"])
    du0 = _conv_bwd_input(du, P["conv_w"])
    dh2 = _mm(du0, W["w_up"], "nt", name="mm_up_dx")
    G["w_up"] = _mm(R["h2"], du0, "tn", out_dtype=BF16, name="mm_up_dw")
    (dx2,), (dn3, dsc2, dsh2) = _rowwise_vjp(_f_pre, [(R["x2"], D, 0)], [P["n3"], sc2, sh2], [[(dh2, D, 0)]],
                                             "pre_norm_bwd", [True], add_rows={0: (dx2a, D, 0)})
    (dxa, dmixed), (dn2, dg1) = _rowwise_vjp(_f_post, [(R["x"], D, 0), (R["mixed"], D, 0)], [P["n2"], g1],
                                             [[(dx2, D, 0)]], "post_norm_bwd", [True, True])
    dcat = _mm(dmixed, W["w_out"], "nt", name="mm_out_dx")
    G["w_out"] = _mm(R["cat"], dmixed, "tn", out_dtype=BF16, name="mm_out_dw")
    (dga, dz), _ = _rowwise_vjp(_f_s5b, [(R["gg"], S5_W, 0), (R["z"], S5_W, 0)], [], [[(dcat, S5_W, 0)]],
                                "s5_glu_bwd", [True, True])
    dgb = _mm(dz, W["w_glu"], "nt", name="mm_glu_dx")
    G["w_glu"] = _mm(R["gg"], dz, "tn", out_dtype=BF16, name="mm_glu_dw")
    (dyc, dua), (dd,) = _rowwise_vjp(_f_s5a, [(R["yc"], S5_W, 0), (R["proj"], S5_W, 0)], [P["s5_d"]],
                                     [[(dga, S5_W, 0), (dgb, S5_W, 0)]], "s5_gelu_bwd", [True, True])
    dhs = _mm_bd(dyc, W["cdt"], "expand", "mm_s5_c_dx")
    dcdt = _mm_bd(dyc, R["hs"], "grad", "mm_s5_c_dw")
    gs, acc = _s5_scan(dhs, P["tab_rev"], reverse=True, h=R["hs"], bu_fwd=R["bu"])
    dub = _mm_bd(gs, W["bd"], "reduce", "mm_s5_b_dx")
    dbd = _mm_bd(R["proj"], gs, "grad", "mm_s5_b_dw")
    dq, got = _flash_bwd_dq(R["q"], R["k"], R["v"], R["o"], R["lse"], dcat, riders.give("flash_bwd_dq", G))
    riders.take("flash_bwd_dq", got)
    (dk, dv), got = _flash_bwd_dkv(R["q"], R["k"], R["v"], R["o"], R["lse"], dcat, riders.give("flash_bwd_dkv", G))
    riders.take("flash_bwd_dkv", got)
    dqraw, dkvraw, dkr = _rope_bwd(dq, dk, dv, P["cs"], P["sn"])
    dqn = _mm(dqraw, W["w_uq"], "nt", name="mm_uq_dx")
    G["w_uq"] = _mm(R["qn"], dqraw, "tn", out_dtype=BF16, name="mm_uq_dw")
    dkvn = _mm(dkvraw, W["w_ukv"], "nt", name="mm_ukv_dx")
    G["w_ukv"] = _mm(R["kvn"], dkvraw, "tn", out_dtype=BF16, name="mm_ukv_dw")
    (dcq,), (dqnorm,) = _rowwise_vjp(_f_norm, [(R["proj"], 512, 1)], [P["q_norm"]], [[(dqn, 512, 0)]],
                                     "q_norm_bwd", [True])
    (dckv,), (dkvnorm,) = _rowwise_vjp(_f_norm, [(R["proj"], 256, 12)], [P["kv_norm"]], [[(dkvn, 256, 0)]],
                                       "kv_norm_bwd", [True])
    (dhq, dhf, dhi, dhg, dlb, don), got = _hg_bwd(R["proj"], R["sts"], P["lb"], P["hg_on"], dcat,
                                                  riders.give("hgrn2_bwd", G))
    riders.take("hgrn2_bwd", got)
    dproj = jnp.concatenate([dua + dub, dcq, dhq, dhf, dhi, dhg, dckv, dkr], axis=-1).astype(BF16)
    dh1 = _mm(dproj, W["w_in"], "nt", name="mm_in_dx")
    G["w_in"] = _mm(R["h1"], dproj, "tn", out_dtype=BF16, name="mm_in_dw")
    (dx,), (dn1, dsc1, dsh1) = _rowwise_vjp(_f_pre, [(R["x"], D, 0)], [P["n1"], sc1, sh1], [[(dh1, D, 0)]],
                                            "pre_norm_bwd", [True], add_rows={0: (dxa, D, 0)})
    dmod = jnp.concatenate([dsh1, dsc1, dg1, dsh2, dsc2, dg2], axis=-1)
    small = dict(n1=dn1, n2=dn2, n3=dn3, n4=dn4, s5_d=dd, q_norm=dqnorm, kv_norm=dkvnorm,
                 lb=dlb, hg_on=jnp.sum(don.reshape(HG_H, HG_D), axis=0, keepdims=True),
                 conv_w=dcw, conv_b=dcb, bd=dbd, cdt=dcdt, acc=jnp.sum(acc, axis=0, keepdims=True))
    return dx, dmod, G, small


def _cols_from_shards(g):
    return jnp.transpose(g, (1, 0, 2)).reshape(g.shape[1], -1)


def _cols_to_shards(w):
    K = w.shape[0]
    return jnp.transpose(w.reshape(K, N_DEV, -1), (1, 0, 2))


FF_SHARD = 2 * D_FF // N_DEV
FF_PAD = CONV_TC - FF_SHARD


def _pad_ff(w):
    lead = w.shape[:-1]
    w = jnp.swapaxes(w.reshape(*lead, 2, CONV_NC, FF_SHARD), -3, -2)
    return jnp.pad(w, [(0, 0)] * (w.ndim - 1) + [(0, FF_PAD)]).reshape(*lead, 2 * D_FFP)


def _unpad_ff(w):
    lead = w.shape[:-1]
    w = w.reshape(*lead, CONV_NC, 2, CONV_TC)[..., :FF_SHARD]
    return jnp.swapaxes(w, -3, -2).reshape(*lead, 2 * D_FF)


def _asm_up(g):
    return _cols_from_shards(jnp.pad(g, ((0, 0), (0, 0), (0, FF_PAD))))


def _grad_up(g):
    return jnp.transpose(g.reshape(g.shape[0], N_DEV, CONV_TC), (1, 0, 2))[..., :FF_SHARD]


def _asm_down(g):
    w = g.reshape(CONV_NC, FF_SHARD, D_MODEL)
    return jnp.pad(w, ((0, 0), (0, FF_PAD), (0, 0))).reshape(D_FFP, D_MODEL)


def _grad_down(g):
    return g.reshape(CONV_NC, CONV_TC, D_MODEL)[:, :FF_SHARD].reshape(N_DEV, -1, D_MODEL)


def _asm_in(g):
    w = _cols_from_shards(g)
    return jnp.concatenate([w[:, 0:1024], w[:, 1344:D_IN], w[:, 1024:1344],
                            jnp.zeros((w.shape[0], D_INP - D_IN), w.dtype)], axis=1)


def _asm_uq(g):
    w = _cols_from_shards(g).reshape(-1, MLA_H, MLA_NOPE + MLA_ROPE)
    return jnp.pad(w, ((0, 0), (0, 0), (0, MLA_HW - MLA_NOPE - MLA_ROPE))).reshape(-1, MLA_H * MLA_HW)


_ASSEMBLE = dict(
    w_in=_asm_in,
    w_glu=lambda g: g.reshape(S5_W, S5_W),
    w_uq=_asm_uq,
    w_ukv=_cols_from_shards,
    w_out=lambda g: g.reshape(D_MODEL, D_MODEL),
    w_up=_asm_up,
    w_down=_asm_down,
)

_GRAD_SHARDS = dict(
    w_in=lambda g: _cols_to_shards(jnp.concatenate([g[:, 0:1024], g[:, 3072:D_IN], g[:, 1024:3072]], axis=1)),
    w_glu=lambda g: g.reshape(N_DEV, -1, S5_W),
    w_uq=lambda g: _cols_to_shards(
        g.reshape(-1, MLA_H, MLA_HW)[:, :, :MLA_NOPE + MLA_ROPE].reshape(-1, MLA_H * (MLA_NOPE + MLA_ROPE))),
    w_ukv=_cols_to_shards,
    w_out=lambda g: g.reshape(N_DEV, -1, D_MODEL),
    w_up=_grad_up,
    w_down=_grad_down,
)


_BIG = ("w_in", "w_glu", "w_uq", "w_ukv", "w_out", "w_up", "w_down")
_EARLY = ("w_in", "w_glu", "w_uq", "w_ukv", "ffn_conv_w")
_SMALL = ("s5_lambda_re", "s5_lambda_im", "s5_log_dt", "s5_b_re", "s5_b_im", "s5_c_re", "s5_c_im", "s5_d",
          "mla_q_norm", "mla_kv_norm", "hg_lb_logits", "hg_out_norm", "mix_pre_norm", "mix_post_norm",
          "ffn_pre_norm", "ffn_post_norm", "ffn_conv_w_full", "ffn_conv_b", "b_ada")
PACK_ROW = 1024


def _pack(parts):
    flat = jnp.concatenate([p.reshape(-1) for p in parts])
    n = flat.shape[0]
    pad = (-n) % (8 * PACK_ROW)
    return jnp.pad(flat, (0, pad)).reshape(-1, PACK_ROW)


def _unpack(packed, shapes):
    flat = packed.reshape(-1)
    out, pos = [], 0
    for s in shapes:
        n = int(np.prod(s))
        out.append(flat[pos:pos + n].reshape(s))
        pos += n
    return out


def _step(x, c, positions, loss_target, w, m, v):
    x = x[0]
    S = x.shape[0]
    L = w["w_in"].shape[0]
    D = D_MODEL
    me = 4 * lax.axis_index("x") + 2 * lax.axis_index("y") + lax.axis_index("c")

    (c_all,) = _exchange([c], False, "gather_c")
    c_all = c_all.reshape(N_DEV, D)
    (c_act,) = _rowwise(lambda a: (a * jax.nn.sigmoid(a),), [(c_all, D, 0)], [], [(D, F32)], "silu_c", tm=N_DEV)
    mod_part = jnp.stack([_mm(c_act, w["w_ada"][l], "nn", name="mm_ada") for l in range(L)])
    (mod_all,) = _exchange([mod_part], False, "gather_mod")
    mod_mine = lax.dynamic_index_in_dim(mod_all, me, axis=2, keepdims=False)
    mod_full = jnp.transpose(mod_mine, (1, 0, 2)).reshape(L, 6 * D) + w["b_ada"]
    mods = [[mod_full[l:l + 1, i * D:(i + 1) * D] for i in range(6)] for l in range(L)]

    cs, sn = _rope_tables(positions[0])
    lower, lower_vjp = jax.vjp(lambda lg: jnp.cumsum(jax.nn.softmax(lg, axis=0), axis=0)
                               - jax.nn.softmax(lg, axis=0)[0:1], w["hg_lb_logits"])
    conv_w_full = []

    def shard(n, l):
        return w[n][l] if n == "ffn_conv_w" else w[n][l].astype(BF16)

    def layer_params(l, early):
        Wl = {n: _ASSEMBLE[n](early[n]) for n in _EARLY[:-1]}
        cw_full = _cols_from_shards(early["ffn_conv_w"])
        s5_args = (w["s5_lambda_re"][l], w["s5_lambda_im"][l], w["s5_log_dt"][l], w["s5_b_re"][l], w["s5_b_im"][l],
                   w["s5_c_re"][l], w["s5_c_im"][l])
        (lbr, lbi, bd, cdt), prep_vjp = jax.vjp(_s5_prep, *s5_args)
        tab_fwd, tab_rev = _s5_tables(*s5_args[:3])
        Wl["bd"], Wl["cdt"] = bd.astype(BF16), cdt.astype(BF16)
        row = lambda a: a.reshape(1, -1)
        Pl = dict(
            n1=row(w["mix_pre_norm"][l]), n2=row(w["mix_post_norm"][l]), n3=row(w["ffn_pre_norm"][l]),
            n4=row(w["ffn_post_norm"][l]), s5_d=row(w["s5_d"][l]), q_norm=row(w["mla_q_norm"][l]),
            kv_norm=row(w["mla_kv_norm"][l]), lb=row(lower[l]), hg_on=row(w["hg_out_norm"][l]),
            conv_w=_pad_ff(cw_full), conv_b=_pad_ff(row(w["ffn_conv_b"][l])),
            tab_fwd=tab_fwd, tab_rev=tab_rev, cs=cs, sn=sn, lam_bar=(lbr, lbi))
        return Wl, Pl, prep_vjp, cw_full

    Ws, Ps, preps, Rs = [], [], [], []
    h = x
    half = w["w_up"].shape[1] // 2
    early = dict(zip(_EARLY, _exchange([shard(n, 0) for n in _EARLY], False, "gather_weights")))
    for l in range(L):
        Wl, Pl, prep_vjp, cw_full = layer_params(l, early)
        Ws.append(Wl)
        Ps.append(Pl)
        preps.append(prep_vjp)
        conv_w_full.append(cw_full)
        riders, stash, early = _Riders(), {}, {}

        def got_flash(got, Wl=Wl, stash=stash):
            Wl["w_out"] = _ASSEMBLE["w_out"](got[0])
            stash["up"] = got[1]

        def got_hg(got, Wl=Wl, stash=stash):
            Wl["w_up"] = _ASSEMBLE["w_up"](jnp.concatenate([stash["up"], got[0]], axis=1))

        def got_up(got, Wl=Wl):
            Wl["w_down"] = _ASSEMBLE["w_down"](got[0])

        riders.add("flash_fwd", lambda _, l=l: ([shard("w_out", l), shard("w_up", l)[:half]], False, [False, True]),
                   got_flash)
        riders.add("hgrn2_fwd", lambda _, l=l: ([shard("w_up", l)[half:]], False, [True]), got_hg)
        riders.add("mm_up", lambda _, l=l: ([shard("w_down", l)], False), got_up)
        if l + 1 < L:
            riders.add("conv_fwd", lambda _, l=l: ([shard(n, l + 1) for n in _EARLY], False),
                       lambda got, early=early: early.update(zip(_EARLY, got)))
        h, R = _layer_fwd(h, mods[l], Wl, Pl, riders)
        Rs.append(R)
    loss_local, dh = _loss_head(h, loss_target[0])
    loss = lax.psum(loss_local, ("x", "y", "c"))

    big_recv = {n: [None] * L for n in _BIG}
    small_g = {n: [None] * L for n in _SMALL if n != "hg_lb_logits"}
    dlower = [None] * L
    late = ("w_out", "w_glu", "w_uq", "w_ukv")
    pending_in = None
    for l in reversed(range(L)):
        riders = _Riders()

        def store(names, l=l):
            def done(got):
                for n, r in zip(names, got):
                    big_recv[n][l] = r
            return done

        def make_late(G, prev=pending_in):
            return [_GRAD_SHARDS[n](G[n]) for n in late] + ([prev[1]] if prev else []), True

        def done_late(got, l=l, prev=pending_in):
            store(late, l)(got)
            if prev:
                big_recv["w_in"][prev[0]] = got[len(late)]

        riders.add("flash_bwd_dq", lambda G: ([_GRAD_SHARDS["w_down"](G["w_down"])], True), store(("w_down",)))
        riders.add("flash_bwd_dkv", lambda G: ([_GRAD_SHARDS["w_up"](G["w_up"])], True, [True]), store(("w_up",)))
        riders.add("hgrn2_bwd", make_late, done_late)
        dh, dmod, G, sm = _layer_bwd(dh, mods[l], Ws[l], Ps[l], Rs[l], riders)
        Rs[l] = None
        pending_in = (l, _GRAD_SHARDS["w_in"](G["w_in"]))
        lbr, lbi = Ps[l]["lam_bar"]
        ar, ai = _ri_split(sm["acc"])
        dl = lax.complex(ar, ai) / lax.complex(lbr, -lbi)
        d_s5 = preps[l]((jnp.real(dl), jnp.imag(dl), sm["bd"], sm["cdt"]))
        for n, g in zip(("s5_lambda_re", "s5_lambda_im", "s5_log_dt", "s5_b_re", "s5_b_im", "s5_c_re", "s5_c_im"), d_s5):
            small_g[n][l] = g
        small_g["s5_d"][l] = sm["s5_d"][0]
        small_g["mla_q_norm"][l] = sm["q_norm"][0]
        small_g["mla_kv_norm"][l] = sm["kv_norm"][0]
        small_g["hg_out_norm"][l] = sm["hg_on"][0]
        small_g["mix_pre_norm"][l] = sm["n1"][0]
        small_g["mix_post_norm"][l] = sm["n2"][0]
        small_g["ffn_pre_norm"][l] = sm["n3"][0]
        small_g["ffn_post_norm"][l] = sm["n4"][0]
        small_g["ffn_conv_w_full"][l] = _unpad_ff(sm["conv_w"])
        small_g["ffn_conv_b"][l] = _unpad_ff(sm["conv_b"])[0]
        small_g["b_ada"][l] = dmod[0]
        dlower[l] = sm["lb"][0]
    (big_recv["w_in"][0],) = _exchange([pending_in[1]], True, "scatter_grads")
    small_g = {n: jnp.stack(gl) for n, gl in small_g.items()}
    (small_g["hg_lb_logits"],) = lower_vjp(jnp.stack(dlower))

    small_w = {n: w[n] for n in _SMALL if n != "ffn_conv_w_full"}
    small_w["ffn_conv_w_full"] = jnp.stack(conv_w_full)
    shapes = [small_w[n].shape for n in _SMALL]
    zeros_cw = jnp.zeros_like(small_w["ffn_conv_w_full"])
    pk_g = _pack([small_g[n] for n in _SMALL])
    pk_w = _pack([small_w[n] for n in _SMALL])
    pk_m = _pack([zeros_cw if n == "ffn_conv_w_full" else m[n] for n in _SMALL])
    pk_v = _pack([zeros_cw + 1.0 if n == "ffn_conv_w_full" else v[n] for n in _SMALL])
    (pk_all,) = _exchange([pk_g], False, "gather_small")
    sg, sd, sm_, sv = _adamw(pk_all[None], pk_w[None], pk_m[None], pk_v[None], name="adamw_small")
    small_out = {}
    for key, arr in (("g", sg), ("d", sd), ("m", sm_), ("v", sv)):
        small_out[key] = dict(zip(_SMALL, _unpack(arr[0], shapes)))

    n_cw = w["ffn_conv_w"].shape[-1]
    g_cw = lax.dynamic_slice_in_dim(small_out["g"]["ffn_conv_w_full"], me * n_cw, n_cw, axis=2)
    cw_out = _adamw(g_cw[:, None], w["ffn_conv_w"], m["ffn_conv_w"], v["ffn_conv_w"], name="adamw_conv_w")

    n_ada = w["w_ada"].shape[-1]
    flat_all = pk_all.reshape(N_DEV, -1)
    off = sum(int(np.prod(s)) for s in shapes[:-1])
    dmod_all = flat_all[:, off:off + L * 6 * D].reshape(N_DEV, L, 6 * D)
    dmod_cols = lax.dynamic_slice_in_dim(dmod_all, me * n_ada, n_ada, axis=2)
    g_ada = jnp.stack([_mm(c_act, dmod_cols[:, l], "tn", name="mm_ada_dw") for l in range(L)])
    ada_out = _adamw(g_ada[:, None], w["w_ada"], m["w_ada"], v["w_ada"], name="adamw_ada")

    big_out = {}
    for n in _BIG:
        recv = jnp.stack(big_recv[n])
        big_out[n] = _adamw(recv, w[n], m[n], v[n], name="adamw_" + n)
    big_out["w_ada"] = ada_out
    big_out["ffn_conv_w"] = cw_out
    return loss, dh[None], big_out, small_out


_WEIGHTS = ("w_in", "s5_lambda_re", "s5_lambda_im", "s5_log_dt", "s5_b_re", "s5_b_im", "s5_c_re", "s5_c_im", "s5_d",
            "s5_w_glu", "mla_q_norm", "mla_w_uq", "mla_kv_norm", "mla_w_ukv", "hg_lb_logits", "hg_out_norm", "w_out",
            "mix_pre_norm", "mix_post_norm", "ffn_pre_norm", "ffn_post_norm", "ffn_w_up", "ffn_conv_w", "ffn_conv_b",
            "ffn_w_down", "w_ada", "b_ada")
_ALIAS = {"s5_w_glu": "w_glu", "mla_w_uq": "w_uq", "mla_w_ukv": "w_ukv", "ffn_w_up": "w_up", "ffn_w_down": "w_down"}


def kernel(x, c, positions, w_in, s5_lambda_re, s5_lambda_im, s5_log_dt, s5_b_re, s5_b_im, s5_c_re, s5_c_im, s5_d, s5_w_glu, mla_q_norm, mla_w_uq, mla_kv_norm, mla_w_ukv, hg_lb_logits, hg_out_norm, w_out, mix_pre_norm, mix_post_norm, ffn_pre_norm, ffn_post_norm, ffn_w_up, ffn_conv_w, ffn_conv_b, ffn_w_down, w_ada, b_ada, loss_target, m_w_in, m_s5_lambda_re, m_s5_lambda_im, m_s5_log_dt, m_s5_b_re, m_s5_b_im, m_s5_c_re, m_s5_c_im, m_s5_d, m_s5_w_glu, m_mla_q_norm, m_mla_w_uq, m_mla_kv_norm, m_mla_w_ukv, m_hg_lb_logits, m_hg_out_norm, m_w_out, m_mix_pre_norm, m_mix_post_norm, m_ffn_pre_norm, m_ffn_post_norm, m_ffn_w_up, m_ffn_conv_w, m_ffn_conv_b, m_ffn_w_down, m_w_ada, m_b_ada, v_w_in, v_s5_lambda_re, v_s5_lambda_im, v_s5_log_dt, v_s5_b_re, v_s5_b_im, v_s5_c_re, v_s5_c_im, v_s5_d, v_s5_w_glu, v_mla_q_norm, v_mla_w_uq, v_mla_kv_norm, v_mla_w_ukv, v_hg_lb_logits, v_hg_out_norm, v_w_out, v_mix_pre_norm, v_mix_post_norm, v_ffn_pre_norm, v_ffn_post_norm, v_ffn_w_up, v_ffn_conv_w, v_ffn_conv_b, v_ffn_w_down, v_w_ada, v_b_ada):
    args = locals()
    key = lambda n: _ALIAS.get(n, n)
    w = {key(n): args[n] for n in _WEIGHTS}
    m = {key(n): args["m_" + n] for n in _WEIGHTS}
    v = {key(n): args["v_" + n] for n in _WEIGHTS}
    loss, grad_x, big, small = _step(x, c, positions, loss_target, w, m, v)

    def pick(n, idx):
        k = key(n)
        if k in big:
            return big[k][idx].reshape(w[k].shape)
        return small["gdmv"[idx]][k]

    outs = [loss, grad_x]
    for idx in range(4):
        outs += [pick(n, idx) for n in _WEIGHTS]
    return tuple(outs)
```

```python
import functools
import math

import numpy as np
import jax
import jax.numpy as jnp
from jax import lax
from jax.experimental import pallas as pl
from jax.experimental.pallas import tpu as pltpu

F32 = jnp.float32
BF16 = jnp.bfloat16
N_DEV = 8
V7X_VMEM_LIMIT = 56 * 1024 * 1024
MM_VMEM_BUDGET = 28 * 1024 * 1024
LANE = 128

D_MODEL = 2048
S5_W = 512
S5_G = 32
S5_C = 16
S5_P = 64
S5_N = S5_G * S5_P
S5_TL = 512
MLA_H = 8
MLA_NOPE = 128
MLA_ROPE = 64
MLA_V = 128
MLA_HW = 256
HG_H = 4
HG_D = 128
HG_CH = 16
D_FF = 5504
D_FFP = 5632
D_IN = 3392
D_INP = 3456
EPS = 1e-6
MASK_VALUE = -1e30
ROPE_THETA = 10000.0
ATT_SCALE = (MLA_NOPE + MLA_ROPE) ** -0.5

ADAM_LR = 0.001
ADAM_B1 = 0.9
ADAM_B2 = 0.999
ADAM_EPS = 1e-08
ADAM_WD = 0.01
ADAM_STEP = 10

_IN_PERM = np.concatenate([np.arange(0, 1024), np.arange(1344, 3392), np.arange(1024, 1344)])
_IN_INV = np.argsort(_IN_PERM)

_NN = (((1,), (0,)), ((), ()))
_NT = (((1,), (1,)), ((), ()))
_TN = (((0,), (0,)), ((), ()))


def _cp(*sem):
    return pltpu.CompilerParams(dimension_semantics=sem, vmem_limit_bytes=V7X_VMEM_LIMIT)


def _tile(n, cap, align=LANE):
    if n <= cap:
        return n
    t = (cap // align) * align
    while t >= align:
        if n % t == 0:
            return t
        t -= align
    raise ValueError(f"no tile for {n}")


def _bdot(a, b, dims):
    return lax.dot_general(a.astype(BF16), b.astype(BF16), dims, preferred_element_type=F32)


def _mm(a, b, mode, out_dtype=F32, a_col0=0, a_cols=None, name="mm", exch=None):
    if mode == "tn":
        K = a.shape[0]
        M = a_cols if a_cols is not None else a.shape[1]
        N = b.shape[1]
    else:
        M = a.shape[0]
        K = a_cols if a_cols is not None else a.shape[1]
        N = b.shape[0] if mode == "nt" else b.shape[1]
    tm = _tile(M, 1024, 8 if M < LANE else LANE)
    tn = _tile(N, 1024)
    kal = 8 if K < LANE else LANE
    tk = _tile(K, 2048, kal)
    osz = jnp.dtype(out_dtype).itemsize

    def vmem(tk_):
        acc = 0 if tk_ == K else tm * tn * 4
        return 2 * tk_ * (tm * a.dtype.itemsize + tn * b.dtype.itemsize) + 2 * tm * tn * osz + acc

    while vmem(tk) > MM_VMEM_BUDGET and tk > kal:
        tk = _tile(K, tk - kal, kal)
    nk = K // tk
    if mode == "tn":
        assert a_col0 % tm == 0
        a_spec = pl.BlockSpec((tk, tm), lambda i, j, k: (k, i + a_col0 // tm))
        b_spec = pl.BlockSpec((tk, tn), lambda i, j, k: (k, j))
        dims = _TN
    else:
        assert a_col0 % tk == 0
        a_spec = pl.BlockSpec((tm, tk), lambda i, j, k: (i, k + a_col0 // tk))
        if mode == "nn":
            b_spec = pl.BlockSpec((tk, tn), lambda i, j, k: (k, j))
            dims = _NN
        else:
            b_spec = pl.BlockSpec((tn, tk), lambda i, j, k: (j, k))
            dims = _NT

    if nk == 1:
        def body(a_ref, b_ref, o_ref):
            o_ref[...] = _bdot(a_ref[...], b_ref[...], dims).astype(o_ref.dtype)

        scratch = []
    else:
        def body(a_ref, b_ref, o_ref, acc_ref):
            k = pl.program_id(2)

            @pl.when(k == 0)
            def _():
                acc_ref[...] = jnp.zeros_like(acc_ref)

            acc_ref[...] += _bdot(a_ref[...], b_ref[...], dims)

            @pl.when(k == nk - 1)
            def _():
                o_ref[...] = acc_ref[...].astype(o_ref.dtype)

        scratch = [pltpu.VMEM((tm, tn), F32)]

    (out,), got = _call_hosting(
        body, name, (M // tm, N // tn, nk), [a_spec, b_spec], [pl.BlockSpec((tm, tn), lambda i, j, k: (i, j))],
        [jax.ShapeDtypeStruct((M, N), out_dtype)], [a, b], exch,
        scratch=scratch, sem=["parallel", "parallel", "arbitrary"])
    return out if exch is None else (out, got)


def _row_spec(tm, width, cb):
    return pl.BlockSpec((tm, width), lambda i: (i, cb))


def _rowwise(fn, rows, params, outs, name, tm=256):
    S = rows[0][0].shape[0]
    nr, npar = len(rows), len(params)

    def body(*refs):
        xs = [r[...].astype(F32) for r in refs[:nr]]
        ps = [p[...] for p in refs[nr:nr + npar]]
        res = fn(*xs, *ps)
        for o, r in zip(refs[nr + npar:], res):
            o[...] = r.astype(o.dtype)

    res = pl.pallas_call(
        body,
        name=name,
        grid=(S // tm,),
        in_specs=[_row_spec(tm, w, cb) for _, w, cb in rows]
        + [pl.BlockSpec(p.shape, lambda i: (0, 0)) for p in params],
        out_specs=[_row_spec(tm, w, 0) for w, _ in outs],
        out_shape=[jax.ShapeDtypeStruct((S, w), dt) for w, dt in outs],
        compiler_params=_cp("parallel"),
    )(*[r[0] for r in rows], *params)
    return list(res)


def _rowwise_vjp(fn, rows, params, cts, name, row_grads, add_rows=None, tm=256):
    S = rows[0][0].shape[0]
    add_rows = add_rows or {}
    nr, npar = len(rows), len(params)
    flat_cts = [c for group in cts for c in group]
    ncts = len(flat_cts)
    add_keys = sorted(add_rows)
    nadd = len(add_keys)
    grad_idx = [i for i in range(nr) if row_grads[i]]

    def body(*refs):
        i = pl.program_id(0)
        xs = [r[...].astype(F32) for r in refs[:nr]]
        ps = [p[...] for p in refs[nr:nr + npar]]
        ct_refs = refs[nr + npar:nr + npar + ncts]
        add_refs = refs[nr + npar + ncts:nr + npar + ncts + nadd]
        out_refs = refs[nr + npar + ncts + nadd:]
        ct_vals, pos = [], 0
        for group in cts:
            v = ct_refs[pos][...].astype(F32)
            for r in ct_refs[pos + 1:pos + len(group)]:
                v = v + r[...].astype(F32)
            pos += len(group)
            ct_vals.append(v)
        _, vjp = jax.vjp(lambda *a: tuple(fn(*a)), *xs, *ps)
        grads = vjp(tuple(ct_vals))
        for o, gi in zip(out_refs[:len(grad_idx)], grad_idx):
            g = grads[gi]
            if gi in add_rows:
                g = g + add_refs[add_keys.index(gi)][...].astype(F32)
            o[...] = g.astype(o.dtype)
        dprefs = out_refs[len(grad_idx):]

        @pl.when(i == 0)
        def _():
            for dp in dprefs:
                dp[...] = jnp.zeros_like(dp)

        for dp, g in zip(dprefs, grads[nr:]):
            dp[...] += g

    res = pl.pallas_call(
        body,
        name=name,
        grid=(S // tm,),
        in_specs=[_row_spec(tm, w, cb) for _, w, cb in rows]
        + [pl.BlockSpec(p.shape, lambda i: (0, 0)) for p in params]
        + [_row_spec(tm, w, cb) for _, w, cb in flat_cts]
        + [_row_spec(tm, add_rows[k][1], add_rows[k][2]) for k in add_keys],
        out_specs=[_row_spec(tm, rows[gi][1], 0) for gi in grad_idx]
        + [pl.BlockSpec(p.shape, lambda i: (0, 0)) for p in params],
        out_shape=[jax.ShapeDtypeStruct((S, rows[gi][1]), F32) for gi in grad_idx]
        + [jax.ShapeDtypeStruct(p.shape, F32) for p in params],
        compiler_params=_cp("arbitrary"),
    )(*[r[0] for r in rows], *params, *[c[0] for c in flat_cts], *[add_rows[k][0] for k in add_keys])
    res = list(res)
    return res[:len(grad_idx)], res[len(grad_idx):]


def _rms(x, gain):
    return x * lax.rsqrt(jnp.mean(x * x, axis=-1, keepdims=True) + EPS) * gain


def _f_pre(x, gain, sc, sh):
    return (_rms(x, gain) * (1.0 + sc) + sh,)


def _f_post(x, y, gain, g):
    return (x + g * _rms(y, gain),)


def _f_norm(x, gain):
    return (_rms(x, gain),)


def _f_s5a(yc, u, d):
    return (jax.nn.gelu(yc + d * u, approximate=True),)


def _f_s5b(g, z):
    return (g * jax.nn.sigmoid(z),)


def _loss_head(y, target, tm=256):
    S, D = y.shape

    def body(y_ref, t_ref, dy_ref, acc_ref):
        i = pl.program_id(0)
        e = y_ref[...] - t_ref[...]
        dy_ref[...] = e * (1.0 / D)

        @pl.when(i == 0)
        def _():
            acc_ref[...] = jnp.zeros_like(acc_ref)

        acc_ref[...] += jnp.sum(e * e, axis=0, keepdims=True)

    dy, acc = pl.pallas_call(
        body,
        name="loss_head",
        grid=(S // tm,),
        in_specs=[_row_spec(tm, D, 0), _row_spec(tm, D, 0)],
        out_specs=[_row_spec(tm, D, 0), pl.BlockSpec((1, D), lambda i: (0, 0))],
        out_shape=[jax.ShapeDtypeStruct((S, D), F32), jax.ShapeDtypeStruct((1, D), F32)],
        compiler_params=_cp("arbitrary"),
    )(y, target)
    return 0.5 * jnp.sum(acc) / D, dy


def _s5_tile_scan(xr, xi, tab_ref, reverse, row8):
    for k in (1, 2, 4):
        pr = tab_ref[pl.ds(k - 1, 1), 0:S5_TL] if not reverse else tab_ref[pl.ds(8 - k, 1), 0:S5_TL]
        pi = tab_ref[pl.ds(k - 1, 1), S5_TL:2 * S5_TL] if not reverse else tab_ref[pl.ds(8 - k, 1), S5_TL:2 * S5_TL]
        if not reverse:
            keep = row8 >= k
            sr = jnp.where(keep, pltpu.roll(xr, k, 0), 0.0)
            si = jnp.where(keep, pltpu.roll(xi, k, 0), 0.0)
        else:
            keep = row8 < 8 - k
            sr = jnp.where(keep, pltpu.roll(xr, 8 - k, 0), 0.0)
            si = jnp.where(keep, pltpu.roll(xi, 8 - k, 0), 0.0)
        xr, xi = xr + pr * sr - pi * si, xi + pr * si + pi * sr
    return xr, xi


def _s5_scan(bu, tab, reverse=False, h=None, bu_fwd=None, tr=512):
    S = bu.shape[0]
    tr = min(tr, S)
    nl = S5_N // S5_TL
    nrb = S // tr
    w = 2 * S5_TL
    nt = tr // 8
    rmap = (lambda j, i: (i, j)) if not reverse else (lambda j, i: (nrb - 1 - i, j))

    def body(*refs):
        if reverse:
            x_ref, tab_ref, h_ref, b_ref, o_ref, acc_ref, cr_ref, ci_ref = refs
        else:
            x_ref, tab_ref, o_ref, cr_ref, ci_ref = refs
        i = pl.program_id(1)
        row8 = lax.broadcasted_iota(jnp.int32, (8, S5_TL), 0)

        @pl.when(i == 0)
        def _():
            cr_ref[...] = jnp.zeros_like(cr_ref)
            ci_ref[...] = jnp.zeros_like(ci_ref)
            if reverse:
                acc_ref[...] = jnp.zeros_like(acc_ref)

        tr_all = tab_ref[:, 0:S5_TL]
        ti_all = tab_ref[:, S5_TL:w]

        def tile(t, carry):
            tt = (nt - 1 - t) if reverse else t
            r = pl.ds(pl.multiple_of(tt * 8, 8), 8)
            xr, xi = _s5_tile_scan(x_ref[r, 0:S5_TL], x_ref[r, S5_TL:w], tab_ref, reverse, row8)
            cr = jnp.broadcast_to(cr_ref[...], (8, S5_TL))
            ci = jnp.broadcast_to(ci_ref[...], (8, S5_TL))
            hr = xr + tr_all * cr - ti_all * ci
            hi = xi + tr_all * ci + ti_all * cr
            o_ref[r, 0:S5_TL] = hr
            o_ref[r, S5_TL:w] = hi
            edge = pl.ds(tt * 8, 1) if reverse else pl.ds(tt * 8 + 7, 1)
            cr_ref[...] = o_ref[edge, 0:S5_TL]
            ci_ref[...] = o_ref[edge, S5_TL:w]
            if reverse:
                dr = h_ref[r, 0:S5_TL] - b_ref[r, 0:S5_TL]
                di = h_ref[r, S5_TL:w] - b_ref[r, S5_TL:w]
                acc_ref[:, 0:S5_TL] += hr * dr + hi * di
                acc_ref[:, S5_TL:w] += hi * dr - hr * di
            return carry

        lax.fori_loop(0, nt, tile, 0)

    blk = pl.BlockSpec((tr, w), rmap)
    in_specs = [blk, pl.BlockSpec((8, w), lambda j, i: (0, j))]
    out_specs = [blk]
    out_shape = [jax.ShapeDtypeStruct((S, 2 * S5_N), F32)]
    args = [bu, tab]
    if reverse:
        in_specs += [blk, blk]
        args += [h, bu_fwd]
        out_specs.append(pl.BlockSpec((8, w), lambda j, i: (0, j)))
        out_shape.append(jax.ShapeDtypeStruct((8, 2 * S5_N), F32))
    res = pl.pallas_call(
        body,
        name="s5_scan_bwd" if reverse else "s5_scan_fwd",
        grid=(nl, nrb),
        in_specs=in_specs,
        out_specs=out_specs,
        out_shape=out_shape,
        scratch_shapes=[pltpu.VMEM((1, S5_TL), F32), pltpu.VMEM((1, S5_TL), F32)],
        compiler_params=_cp("parallel", "arbitrary"),
    )(*args)
    return res if reverse else res[0]


def _ri_cols(re, im):
    lead = re.shape[:-1]
    nl = S5_N // S5_TL
    z = jnp.stack([re.reshape(*lead, nl, S5_TL), im.reshape(*lead, nl, S5_TL)], axis=-2)
    return z.reshape(*lead, 2 * S5_N)


def _ri_split(z):
    lead = z.shape[:-1]
    nl = S5_N // S5_TL
    z = z.reshape(*lead, nl, 2, S5_TL)
    return z[..., 0, :].reshape(*lead, S5_N), z[..., 1, :].reshape(*lead, S5_N)


def _s5_prep(lre, lim, logdt, bre, bim, cre, cim):
    lam = lax.complex(lre, lim)
    dt = jnp.exp(logdt)[:, None]
    lam_bar = jnp.exp(lam * dt)
    b = lax.complex(bre, bim)
    b_bar = ((lam_bar - 1.0) / lam)[..., None] * b
    eye = jnp.eye(S5_G, dtype=F32)
    bd_re = jnp.einsum("gpc,gh->gchp", jnp.real(b_bar), eye).reshape(S5_W, S5_N)
    bd_im = jnp.einsum("gpc,gh->gchp", jnp.imag(b_bar), eye).reshape(S5_W, S5_N)
    bd = _ri_cols(bd_re, bd_im)
    cd_re = jnp.einsum("gcp,gh->gchp", cre, eye).reshape(S5_W, S5_N)
    cd_im = jnp.einsum("gcp,gh->gchp", -cim, eye).reshape(S5_W, S5_N)
    cdt = _ri_cols(cd_re, cd_im)

    def diag(m):
        return jnp.stack([m[j * S5_BW:(j + 1) * S5_BW, j * 2 * S5_TL:(j + 1) * 2 * S5_TL] for j in range(S5_NB)])

    return jnp.real(lam_bar).reshape(1, S5_N), jnp.imag(lam_bar).reshape(1, S5_N), diag(bd), diag(cdt)


S5_NB = S5_N // S5_TL
S5_BW = S5_W // S5_NB


def _mm_bd(a, b, mode, name):
    S = a.shape[0]
    wide = 2 * S5_TL
    t = min(S, 1024)
    wspec = pl.BlockSpec((1, S5_BW, wide), lambda i, j: (j, 0, 0))
    if mode == "grad":
        def body(a_ref, b_ref, o_ref):
            @pl.when(pl.program_id(1) == 0)
            def _():
                o_ref[...] = jnp.zeros_like(o_ref)

            o_ref[0] += _bdot(a_ref[...], b_ref[...], _TN)

        return pl.pallas_call(
            body, name=name, grid=(S5_NB, S // t),
            in_specs=[pl.BlockSpec((t, S5_BW), lambda j, k: (k, j)), pl.BlockSpec((t, wide), lambda j, k: (k, j))],
            out_specs=pl.BlockSpec((1, S5_BW, wide), lambda j, k: (j, 0, 0)),
            out_shape=jax.ShapeDtypeStruct((S5_NB, S5_BW, wide), F32),
            compiler_params=_cp("parallel", "arbitrary"))(a, b)
    win, wout, dims = (S5_BW, wide, _NN) if mode == "expand" else (wide, S5_BW, _NT)

    def body(a_ref, w_ref, o_ref):
        o_ref[...] = _bdot(a_ref[...], w_ref[0], dims)

    return pl.pallas_call(
        body, name=name, grid=(S // t, S5_NB),
        in_specs=[pl.BlockSpec((t, win), lambda i, j: (i, j)), wspec],
        out_specs=pl.BlockSpec((t, wout), lambda i, j: (i, j)),
        out_shape=jax.ShapeDtypeStruct((S, S5_NB * wout), F32),
        compiler_params=_cp("parallel", "parallel"))(a, b)


def _s5_tables(lre, lim, logdt):
    lam = lax.complex(lre, lim)
    dt = jnp.exp(logdt)[:, None]
    k = jnp.arange(1, 9, dtype=F32)[:, None, None]
    pw = jnp.exp((lam * dt)[None] * k).reshape(8, S5_N)
    fwd = _ri_cols(jnp.real(pw), jnp.imag(pw))
    rev = _ri_cols(jnp.real(pw)[::-1], -jnp.imag(pw)[::-1])
    return fwd, rev


def _rope_tables(positions):
    inv_freq = 1.0 / (ROPE_THETA ** (jnp.arange(0, MLA_ROPE, 2, dtype=F32) / MLA_ROPE))
    ang = positions.astype(F32)[:, None] * inv_freq
    cos, sin = jnp.cos(ang), jnp.sin(ang)
    z = jnp.zeros_like(cos)
    cs = jnp.concatenate([cos, cos, z, z], axis=-1)
    sn = jnp.concatenate([-sin, sin, z, z], axis=-1)
    return cs, sn


def _rope_fwd(qraw, kvraw, proj, cs, sn, tm=256):
    S = qraw.shape[0]
    HW = MLA_H * MLA_HW

    def rope(x, c, s):
        lane = lax.broadcasted_iota(jnp.int32, x.shape, 1)
        sw = jnp.where(lane < 32, pltpu.roll(x, 96, 1), jnp.where(lane < 64, pltpu.roll(x, 32, 1), 0.0))
        return x * c + sw * s

    def body(q_ref, kv_ref, kr_ref, cs_ref, sn_ref, qo_ref, ko_ref, vo_ref):
        c, s = cs_ref[...], sn_ref[...]
        kr = rope(kr_ref[...], c, s).astype(BF16)
        for h in range(MLA_H):
            o = h * MLA_HW
            qo_ref[:, o:o + 128] = q_ref[:, o:o + 128].astype(BF16)
            qo_ref[:, o + 128:o + 256] = rope(q_ref[:, o + 128:o + 256], c, s).astype(BF16)
            ko_ref[:, o:o + 128] = kv_ref[:, o:o + 128].astype(BF16)
            ko_ref[:, o + 128:o + 256] = kr
            vo_ref[:, h * 128:(h + 1) * 128] = kv_ref[:, o + 128:o + 256].astype(BF16)

    return pl.pallas_call(
        body,
        name="rope_fwd",
        grid=(S // tm,),
        in_specs=[_row_spec(tm, HW, 0), _row_spec(tm, HW, 0), _row_spec(tm, 128, (D_INP - 128) // 128),
                  _row_spec(tm, 128, 0), _row_spec(tm, 128, 0)],
        out_specs=[_row_spec(tm, HW, 0), _row_spec(tm, HW, 0), _row_spec(tm, MLA_H * MLA_V, 0)],
        out_shape=[jax.ShapeDtypeStruct((S, HW), BF16), jax.ShapeDtypeStruct((S, HW), BF16),
                   jax.ShapeDtypeStruct((S, MLA_H * MLA_V), BF16)],
        compiler_params=_cp("parallel"),
    )(qraw, kvraw, proj, cs, sn)


def _rope_bwd(dq, dk, dv, cs, sn, tm=256):
    S = dq.shape[0]
    HW = MLA_H * MLA_HW

    def rope_t(x, c, s):
        lane = lax.broadcasted_iota(jnp.int32, x.shape, 1)
        w = x * s
        sw = jnp.where(lane < 32, pltpu.roll(w, 96, 1), jnp.where(lane < 64, pltpu.roll(w, 32, 1), 0.0))
        return x * c + sw

    def body(dq_ref, dk_ref, dv_ref, cs_ref, sn_ref, qo_ref, kvo_ref, kro_ref):
        c, s = cs_ref[...], sn_ref[...]
        kr = jnp.zeros((tm, 128), F32)
        for h in range(MLA_H):
            o = h * MLA_HW
            qo_ref[:, o:o + 128] = dq_ref[:, o:o + 128]
            qo_ref[:, o + 128:o + 256] = rope_t(dq_ref[:, o + 128:o + 256], c, s)
            kvo_ref[:, o:o + 128] = dk_ref[:, o:o + 128]
            kvo_ref[:, o + 128:o + 256] = dv_ref[:, h * 128:(h + 1) * 128]
            kr = kr + dk_ref[:, o + 128:o + 256]
        kro_ref[...] = rope_t(kr, c, s)

    return pl.pallas_call(
        body,
        name="rope_bwd",
        grid=(S // tm,),
        in_specs=[_row_spec(tm, HW, 0), _row_spec(tm, HW, 0), _row_spec(tm, MLA_H * MLA_V, 0),
                  _row_spec(tm, 128, 0), _row_spec(tm, 128, 0)],
        out_specs=[_row_spec(tm, HW, 0), _row_spec(tm, HW, 0), _row_spec(tm, 128, 0)],
        out_shape=[jax.ShapeDtypeStruct((S, HW), F32), jax.ShapeDtypeStruct((S, HW), F32),
                   jax.ShapeDtypeStruct((S, 128), F32)],
        compiler_params=_cp("parallel"),
    )(dq, dk, dv, cs, sn)


ATT_T = 512


def _diag_mask(t):
    return lax.broadcasted_iota(jnp.int32, (t, t), 1) <= lax.broadcasted_iota(jnp.int32, (t, t), 0)


def _flash_fwd(q, k, v, exch=None):
    S = q.shape[0]
    t = min(ATT_T, S)
    nq = S // t

    def body(q_ref, k_ref, v_ref, o_ref, lse_ref):
        i = pl.program_id(1)
        qb = q_ref[...]

        def step(j, carry, masked):
            m, l, acc = carry
            r = pl.ds(pl.multiple_of(j * t, t), t)
            s = _bdot(qb, k_ref[r, :], _NT) * ATT_SCALE
            if masked:
                s = jnp.where(_diag_mask(t), s, MASK_VALUE)
            m_new = jnp.maximum(m, jnp.max(s, axis=-1, keepdims=True))
            alpha = jnp.exp(m - m_new)
            p = jnp.exp(s - m_new)
            l = alpha * l + jnp.sum(p, axis=-1, keepdims=True)
            acc = alpha * acc + _bdot(p, v_ref[r, :], _NN)
            return m_new, l, acc

        m0 = jnp.full((t, 1), MASK_VALUE, F32)
        init = (m0, jnp.zeros((t, 1), F32), jnp.zeros((t, MLA_V), F32))
        m, l, acc = step(i, lax.fori_loop(0, i, lambda j, c: step(j, c, False), init), True)
        o_ref[...] = acc / l
        lse_ref[...] = jnp.broadcast_to(m + jnp.log(l), (t, 128))

    return _call_hosting(
        body, "flash_fwd", (MLA_H, nq),
        [pl.BlockSpec((t, MLA_HW), lambda h, i: (i, h)),
         pl.BlockSpec((S, MLA_HW), lambda h, i: (0, h)),
         pl.BlockSpec((S, MLA_V), lambda h, i: (0, h))],
        [pl.BlockSpec((t, MLA_V), lambda h, i: (i, h)), pl.BlockSpec((t, 128), lambda h, i: (i, h))],
        [jax.ShapeDtypeStruct((S, MLA_H * MLA_V), F32), jax.ShapeDtypeStruct((S, MLA_H * 128), F32)],
        [q, k, v], exch)


def _flash_bwd_dq(q, k, v, o, lse, dcat, exch=None):
    S = q.shape[0]
    t = min(ATT_T, S)
    nq = S // t
    do_cb = S5_W // MLA_V

    def body(q_ref, k_ref, v_ref, o_ref, lse_ref, do_ref, dq_ref):
        i = pl.program_id(1)
        qb = q_ref[...]
        do = do_ref[...]
        delta = jnp.sum(do * o_ref[...], axis=-1, keepdims=True)
        lse1 = jnp.max(lse_ref[...], axis=-1, keepdims=True)
        dob = do.astype(BF16)

        def step(j, dq, masked):
            r = pl.ds(pl.multiple_of(j * t, t), t)
            kb = k_ref[r, :]
            s = _bdot(qb, kb, _NT) * ATT_SCALE
            p = jnp.exp(s - lse1)
            if masked:
                p = jnp.where(_diag_mask(t), p, 0.0)
            dp = _bdot(dob, v_ref[r, :], _NT)
            ds = p * (dp - delta) * ATT_SCALE
            return dq + _bdot(ds, kb, _NN)

        dq = lax.fori_loop(0, i, lambda j, c: step(j, c, False), jnp.zeros((t, MLA_HW), F32))
        dq_ref[...] = step(i, dq, True)

    (dq,), got = _call_hosting(
        body, "flash_bwd_dq", (MLA_H, nq),
        [pl.BlockSpec((t, MLA_HW), lambda h, i: (i, h)),
         pl.BlockSpec((S, MLA_HW), lambda h, i: (0, h)),
         pl.BlockSpec((S, MLA_V), lambda h, i: (0, h)),
         pl.BlockSpec((t, MLA_V), lambda h, i: (i, h)),
         pl.BlockSpec((t, 128), lambda h, i: (i, h)),
         pl.BlockSpec((t, MLA_V), lambda h, i: (i, do_cb + h))],
        [pl.BlockSpec((t, MLA_HW), lambda h, i: (i, h))], [jax.ShapeDtypeStruct((S, MLA_H * MLA_HW), F32)],
        [q, k, v, o, lse, dcat], exch)
    return dq, got


def _flash_bwd_dkv(q, k, v, o, lse, dcat, exch=None):
    S = q.shape[0]
    t = min(ATT_T, S)
    nq = S // t
    do_cb = S5_W // MLA_V

    def body(q_ref, k_ref, v_ref, o_ref, lse_ref, do_ref, dk_ref, dv_ref):
        j = pl.program_id(1)
        kb = k_ref[...]
        vb = v_ref[...]

        def step(i, carry, masked):
            dk, dv = carry
            r = pl.ds(pl.multiple_of(i * t, t), t)
            qb = q_ref[r, :]
            do = do_ref[r, :]
            delta = jnp.sum(do * o_ref[r, :], axis=-1, keepdims=True)
            lse1 = jnp.max(lse_ref[r, :], axis=-1, keepdims=True)
            s = _bdot(qb, kb, _NT) * ATT_SCALE
            p = jnp.exp(s - lse1)
            if masked:
                p = jnp.where(_diag_mask(t), p, 0.0)
            dob = do.astype(BF16)
            dv = dv + _bdot(p, dob, _TN)
            dp = _bdot(dob, vb, _NT)
            ds = p * (dp - delta) * ATT_SCALE
            dk = dk + _bdot(ds, qb, _TN)
            return dk, dv

        first = step(j, (jnp.zeros((t, MLA_HW), F32), jnp.zeros((t, MLA_V), F32)), True)
        dk, dv = lax.fori_loop(j + 1, nq, lambda i, c: step(i, c, False), first)
        dk_ref[...] = dk
        dv_ref[...] = dv

    return _call_hosting(
        body, "flash_bwd_dkv", (MLA_H, nq),
        [pl.BlockSpec((S, MLA_HW), lambda h, j: (0, h)),
         pl.BlockSpec((t, MLA_HW), lambda h, j: (j, h)),
         pl.BlockSpec((t, MLA_V), lambda h, j: (j, h)),
         pl.BlockSpec((S, MLA_V), lambda h, j: (0, h)),
         pl.BlockSpec((S, 128), lambda h, j: (0, h)),
         pl.BlockSpec((S, MLA_V), lambda h, j: (0, do_cb + h))],
        [pl.BlockSpec((t, MLA_HW), lambda h, j: (j, h)), pl.BlockSpec((t, MLA_V), lambda h, j: (j, h))],
        [jax.ShapeDtypeStruct((S, MLA_H * MLA_HW), F32), jax.ShapeDtypeStruct((S, MLA_H * MLA_V), F32)],
        [q, k, v, o, lse, dcat], exch)


def _split3(x):
    x1 = x.astype(BF16)
    r1 = x - x1.astype(F32)
    x2 = r1.astype(BF16)
    x3 = (r1 - x2.astype(F32)).astype(BF16)
    return x1, x2, x3


def _tri_matmul(x, upper):
    n = x.shape[0]
    r = lax.broadcasted_iota(jnp.int32, (n, n), 0)
    c = lax.broadcasted_iota(jnp.int32, (n, n), 1)
    tri = jnp.where((r <= c) if upper else (r >= c), 1.0, 0.0).astype(BF16)
    x1, x2, x3 = _split3(x)
    dot = lambda v: lax.dot_general(tri, v, _NN, preferred_element_type=F32)
    return dot(x1) + dot(x2) + dot(x3)


@jax.custom_vjp
def _cumsum_rows(x):
    return _tri_matmul(x, False)


def _cumsum_rows_fwd(x):
    return _tri_matmul(x, False), None


def _cumsum_rows_bwd(_, ct):
    return (_tri_matmul(ct, True),)


_cumsum_rows.defvjp(_cumsum_rows_fwd, _cumsum_rows_bwd)


def _hg_step(qin, fin, vin, gin, st, lb, on):
    n = qin.shape[0]
    sig = jax.nn.sigmoid(fin)
    g = jnp.log(lb + (1.0 - lb) * sig)
    k = (1.0 - lb) * jax.nn.sigmoid(-fin)
    q = qin * jax.nn.sigmoid(qin)
    b = _cumsum_rows(g)
    o = _bdot(q * jnp.exp(b), st, _NT)
    row = lax.broadcasted_iota(jnp.int32, (n, HG_D), 0)
    row1 = lax.broadcasted_iota(jnp.int32, (n, 1), 0)
    b_s = None
    for s in range(n):
        sel = row == s
        b_s = jnp.sum(jnp.where(sel, b, 0.0), axis=0, keepdims=True)
        k_s = jnp.sum(jnp.where(sel, k, 0.0), axis=0, keepdims=True)
        v_s = jnp.sum(jnp.where(sel, vin, 0.0), axis=0, keepdims=True)
        e = jnp.exp(jnp.minimum(b - b_s, 0.0))
        c = jnp.sum(q * e * k_s, axis=-1, keepdims=True)
        o = o + jnp.where(row1 >= s, c, 0.0) * v_s
    st_new = st * jnp.exp(b_s) + _bdot(vin, k * jnp.exp(b_s - b), _TN)
    y = _rms(o, on) * (gin * jax.nn.sigmoid(gin))
    return y, st_new


HG_W = HG_H * HG_D


def _hg_specs(tb, nb, reverse):
    rm = (lambda i: nb - 1 - i) if reverse else (lambda i: i)
    base = 1024 // HG_W
    return [pl.BlockSpec((tb, HG_W), lambda i, o=o: (rm(i), base + o)) for o in range(4)], rm


def _head(h):
    return slice(h * HG_D, (h + 1) * HG_D)


def _hg_fwd(proj, lb, on, exch=None, tb=256):
    S = proj.shape[0]
    nb = S // tb
    nc = tb // HG_CH
    in_specs, rm = _hg_specs(tb, nb, False)

    def body(q_ref, f_ref, v_ref, g_ref, lb_ref, on_ref, y_ref, sts_ref, st_ref):
        @pl.when(pl.program_id(0) == 0)
        def _():
            st_ref[...] = jnp.zeros_like(st_ref)

        def step(c, carry):
            r = pl.ds(pl.multiple_of(c * HG_CH, HG_CH), HG_CH)
            for h in range(HG_H):
                hs = _head(h)
                st = st_ref[h]
                sts_ref[h, c] = st
                y, st_new = _hg_step(q_ref[r, hs], f_ref[r, hs], v_ref[r, hs], g_ref[r, hs], st, lb_ref[:, hs],
                                     on_ref[...])
                y_ref[r, hs] = y
                st_ref[h] = st_new
            return carry

        lax.fori_loop(0, nc, step, 0)

    return _call_hosting(
        body, "hgrn2_fwd", (nb,),
        in_specs + [pl.BlockSpec((1, HG_W), lambda i: (0, 0)), pl.BlockSpec((1, HG_D), lambda i: (0, 0))],
        [pl.BlockSpec((tb, HG_W), lambda i: (i, 0)), pl.BlockSpec((HG_H, nc, HG_D, HG_D), lambda i: (0, i, 0, 0))],
        [jax.ShapeDtypeStruct((S, HG_W), F32), jax.ShapeDtypeStruct((HG_H, S // HG_CH, HG_D, HG_D), F32)],
        [proj, proj, proj, proj, lb, on], exch,
        scratch=[pltpu.VMEM((HG_H, HG_D, HG_D), F32)], sem=["arbitrary"])


def _hg_bwd(proj, sts, lb, on, dcat, exch=None, tb=256):
    S = proj.shape[0]
    nb = S // tb
    nc = tb // HG_CH
    in_specs, rm = _hg_specs(tb, nb, True)
    dy_cb = (S5_W + MLA_H * MLA_V) // HG_W

    def body(q_ref, f_ref, v_ref, g_ref, lb_ref, on_ref, sts_ref, dy_ref,
             dq_ref, df_ref, dv_ref, dg_ref, dlb_ref, don_ref, dst_ref):
        @pl.when(pl.program_id(0) == 0)
        def _():
            dst_ref[...] = jnp.zeros_like(dst_ref)
            dlb_ref[...] = jnp.zeros_like(dlb_ref)
            don_ref[...] = jnp.zeros_like(don_ref)

        def step(cc, carry):
            c = nc - 1 - cc
            r = pl.ds(pl.multiple_of(c * HG_CH, HG_CH), HG_CH)
            for h in range(HG_H):
                hs = _head(h)
                _, vjp = jax.vjp(_hg_step, q_ref[r, hs], f_ref[r, hs], v_ref[r, hs], g_ref[r, hs], sts_ref[h, c],
                                 lb_ref[:, hs], on_ref[...])
                dq, df, dv, dg, dst, dlb, don = vjp((dy_ref[r, hs], dst_ref[h]))
                dq_ref[r, hs] = dq
                df_ref[r, hs] = df
                dv_ref[r, hs] = dv
                dg_ref[r, hs] = dg
                dst_ref[h] = dst
                dlb_ref[:, hs] += dlb
                don_ref[:, hs] += don
            return carry

        lax.fori_loop(0, nc, step, 0)

    blk = pl.BlockSpec((tb, HG_W), lambda i: (rm(i), 0))
    par = pl.BlockSpec((1, HG_W), lambda i: (0, 0))
    return _call_hosting(
        body, "hgrn2_bwd", (nb,),
        in_specs + [par, pl.BlockSpec((1, HG_D), lambda i: (0, 0)),
                    pl.BlockSpec((HG_H, nc, HG_D, HG_D), lambda i: (0, rm(i), 0, 0)),
                    pl.BlockSpec((tb, HG_W), lambda i: (rm(i), dy_cb))],
        [blk, blk, blk, blk, par, par],
        [jax.ShapeDtypeStruct((S, HG_W), F32)] * 4 + [jax.ShapeDtypeStruct((1, HG_W), F32)] * 2,
        [proj, proj, proj, proj, lb, on, sts, dcat], exch,
        scratch=[pltpu.VMEM((HG_H, HG_D, HG_D), F32)], sem=["arbitrary"])


CONV_NC = 4
CONV_TC = D_FFP // CONV_NC


def _shift_down(cur, halo, k):
    tm = cur.shape[0]
    row = lax.broadcasted_iota(jnp.int32, cur.shape, 0)
    top = jnp.concatenate([pltpu.roll(halo, k, 0), jnp.zeros((tm - 8, cur.shape[1]), F32)], axis=0)
    return jnp.where(row < k, top, pltpu.roll(cur, k, 0))


def _conv3(cur, halo, w_ref, b_ref):
    return (b_ref[...] + _shift_down(cur, halo, 2) * w_ref[pl.ds(0, 1), :]
            + _shift_down(cur, halo, 1) * w_ref[pl.ds(1, 1), :] + cur * w_ref[pl.ds(2, 1), :])


def _conv_fwd(u0, cw, cb, exch=None, tm=256):
    S = u0.shape[0]
    nc = CONV_NC
    m8 = tm // 8
    prev = lambda i: jnp.maximum(i * m8 - 1, 0)

    def body(u_ref, p_ref, w_ref, b_ref, a_ref):
        on = (pl.program_id(1) > 0).astype(F32)
        u = _conv3(u_ref[...], p_ref[...] * on, w_ref, b_ref)
        a_ref[...] = (jax.nn.gelu(u[:, :CONV_TC], approximate=True) * u[:, CONV_TC:]).astype(a_ref.dtype)

    tc = CONV_TC
    (a,), got = _call_hosting(
        body, "conv_geglu_fwd", (nc, S // tm),
        [pl.BlockSpec((tm, 2 * tc), lambda j, i: (i, j)), pl.BlockSpec((8, 2 * tc), lambda j, i: (prev(i), j)),
         pl.BlockSpec((3, 2 * tc), lambda j, i: (0, j)), pl.BlockSpec((1, 2 * tc), lambda j, i: (0, j))],
        [pl.BlockSpec((tm, tc), lambda j, i: (i, j))], [jax.ShapeDtypeStruct((S, D_FFP), BF16)],
        [u0, u0, cw, cb], exch)
    return a, got


def _conv_bwd(da, u0, cw, cb, tm=256):
    S = u0.shape[0]
    nrb = S // tm
    m8 = tm // 8
    n = tm + 8
    tc = CONV_TC
    prev = lambda i: jnp.maximum(i * m8 - 1, 0)
    nxt = lambda i: jnp.minimum((i + 1) * m8, S // 8 - 1)

    def body(u_ref, p_ref, n_ref, w_ref, b_ref, da_ref, dan_ref, o_ref, dw_ref, db_ref):
        i = pl.program_id(1)
        cur = jnp.concatenate([u_ref[...], n_ref[...]], axis=0)
        halo = p_ref[...] * (i > 0).astype(F32)
        u = _conv3(cur, halo, w_ref, b_ref)
        dact = jnp.concatenate([da_ref[...], dan_ref[...] * (i < nrb - 1).astype(F32)], axis=0)
        _, vjp = jax.vjp(lambda g, v: jax.nn.gelu(g, approximate=True) * v, u[:, :tc], u[:, tc:])
        du = jnp.concatenate(vjp(dact), axis=1)
        du0 = (du * w_ref[pl.ds(2, 1), :] + pltpu.roll(du, n - 1, 0) * w_ref[pl.ds(1, 1), :]
               + pltpu.roll(du, n - 2, 0) * w_ref[pl.ds(0, 1), :])
        o_ref[...] = du0[:tm].astype(o_ref.dtype)

        @pl.when(i == 0)
        def _():
            dw_ref[...] = jnp.zeros_like(dw_ref)
            db_ref[...] = jnp.zeros_like(db_ref)

        own = jnp.where(lax.broadcasted_iota(jnp.int32, du.shape, 0) < tm, du, 0.0)
        dw_ref[pl.ds(0, 1), :] += jnp.sum(own * _shift_down(cur, halo, 2), axis=0, keepdims=True)
        dw_ref[pl.ds(1, 1), :] += jnp.sum(own * _shift_down(cur, halo, 1), axis=0, keepdims=True)
        dw_ref[pl.ds(2, 1), :] += jnp.sum(own * cur, axis=0, keepdims=True)
        db_ref[...] += jnp.sum(own, axis=0, keepdims=True)

    wide = lambda r, f: pl.BlockSpec((r, 2 * tc), f)
    return pl.pallas_call(
        body,
        name="conv_geglu_bwd",
        grid=(CONV_NC, nrb),
        in_specs=[wide(tm, lambda j, i: (i, j)), wide(8, lambda j, i: (prev(i), j)), wide(8, lambda j, i: (nxt(i), j)),
                  wide(3, lambda j, i: (0, j)), wide(1, lambda j, i: (0, j)),
                  pl.BlockSpec((tm, tc), lambda j, i: (i, j)), pl.BlockSpec((8, tc), lambda j, i: (nxt(i), j))],
        out_specs=[wide(tm, lambda j, i: (i, j)), wide(3, lambda j, i: (0, j)), wide(1, lambda j, i: (0, j))],
        out_shape=[jax.ShapeDtypeStruct((S, 2 * D_FFP), BF16), jax.ShapeDtypeStruct((3, 2 * D_FFP), F32),
                   jax.ShapeDtypeStruct((1, 2 * D_FFP), F32)],
        compiler_params=_cp("parallel", "arbitrary"),
    )(u0, u0, u0, cw, cb, da, da)


def _exchange(arrs, scatter, name, ff=None):
    n = len(arrs)

    def body(*refs):
        args = (refs[:n], refs[n:2 * n], *refs[2 * n:], scatter, ff)
        _exchange_start(*args)
        _exchange_wait(*args)

    hbm = pl.BlockSpec(memory_space=pltpu.HBM)
    out_shape, sems = _exchange_shapes(arrs, scatter)
    return pl.pallas_call(
        body,
        name=name,
        in_specs=[hbm] * n,
        out_specs=[hbm] * n,
        out_shape=out_shape,
        scratch_shapes=sems,
    )(*arrs)


def _ff_slot(d):
    return (d % 4) * 2 + d // 4


def _exchange_copies(ins, outs, send, recv, loc, scatter, ff=None):
    x, y, c = lax.axis_index("x"), lax.axis_index("y"), lax.axis_index("c")
    me = 4 * x + 2 * y + c
    sends, recvs, locs = [], [], []
    for a in range(len(ins)):
        slot = _ff_slot if ff and ff[a] else (lambda d: d)
        mine = me if scatter else slot(me)
        locs.append(pltpu.make_async_copy(ins[a].at[slot(me)] if scatter else ins[a], outs[a].at[mine], loc.at[a]))
        for k in range(1, N_DEV):
            px = 1 - x if k & 4 else x
            py = 1 - y if k & 2 else y
            pc = 1 - c if k & 1 else c
            peer = 4 * px + 2 * py + pc
            src = ins[a].at[slot(peer)] if scatter else ins[a]
            sems = dict(send_sem=send.at[a, k - 1], recv_sem=recv.at[a, k - 1], device_id=(px, py, pc),
                        device_id_type=pl.DeviceIdType.MESH)
            sends.append(pltpu.make_async_remote_copy(src_ref=src, dst_ref=outs[a].at[mine], **sems))
            theirs = peer if scatter else slot(peer)
            recvs.append(pltpu.make_async_remote_copy(src_ref=src, dst_ref=outs[a].at[theirs], **sems))
    return locs, sends, recvs


def _exchange_start(*refs):
    locs, sends, _ = _exchange_copies(*refs)
    for cp in locs + sends:
        cp.start()


def _exchange_wait(*refs):
    locs, sends, recvs = _exchange_copies(*refs)
    for cp in recvs:
        cp.wait_recv()
    for cp in sends:
        cp.wait_send()
    for cp in locs:
        cp.wait()


def _exchange_shapes(arrs, scatter):
    n = len(arrs)
    out_shape = [jax.ShapeDtypeStruct(a.shape if scatter else (N_DEV,) + a.shape, a.dtype) for a in arrs]
    sems = [pltpu.SemaphoreType.DMA((n, N_DEV - 1)), pltpu.SemaphoreType.DMA((n, N_DEV - 1)),
            pltpu.SemaphoreType.DMA((n,))]
    return out_shape, sems


def _call_hosting(body, name, grid, in_specs, out_specs, out_shape, args, exch, scratch=(), sem=None):
    scratch = list(scratch)
    if exch is None:
        res = pl.pallas_call(body, name=name, grid=grid, in_specs=in_specs, out_specs=out_specs, out_shape=out_shape,
                             scratch_shapes=scratch,
                             compiler_params=_cp(*(sem or ["parallel"] * len(grid))))(*args)
        return list(res), []
    arrs, scatter, ff = (*exch, None)[:3]
    n, n_in, n_out, n_scr = len(arrs), len(in_specs), len(out_specs), len(scratch)
    hbm = pl.BlockSpec(memory_space=pltpu.HBM)
    x_shape, sems = _exchange_shapes(arrs, scatter)

    def hosting_body(*refs):
        cin, xin = refs[:n_in], refs[n_in:n_in + n]
        cout, xout = refs[n_in + n:n_in + n + n_out], refs[n_in + n + n_out:n_in + 2 * n + n_out]
        cscr = refs[n_in + 2 * n + n_out:n_in + 2 * n + n_out + n_scr]
        xsem = refs[n_in + 2 * n + n_out + n_scr:]
        ids = [pl.program_id(d) for d in range(len(grid))]
        first = functools.reduce(jnp.logical_and, [i == 0 for i in ids])
        last = functools.reduce(jnp.logical_and, [i == g - 1 for i, g in zip(ids, grid)])

        @pl.when(first)
        def _():
            _exchange_start(xin, xout, *xsem, scatter, ff)

        body(*cin, *cout, *cscr)

        @pl.when(last)
        def _():
            _exchange_wait(xin, xout, *xsem, scatter, ff)

    res = pl.pallas_call(
        hosting_body, name=name + "_x", grid=grid, in_specs=in_specs + [hbm] * n, out_specs=out_specs + [hbm] * n,
        out_shape=out_shape + x_shape, scratch_shapes=scratch + sems,
        compiler_params=_cp(*["arbitrary"] * len(grid)))(*args, *arrs)
    return list(res[:n_out]), list(res[n_out:])


class _Riders:
    def __init__(self):
        self.make, self.done = {}, {}

    def add(self, host, make, done):
        self.make[host], self.done[host] = make, done

    def give(self, host, ctx=None):
        return self.make[host](ctx) if host in self.make else None

    def take(self, host, got):
        if host in self.done:
            self.done[host](got)


def _adamw(recv, w, m, v, name="adamw"):
    L, n, R, C = recv.shape
    fits = [t for t in range(8, R + 1, 8) if R % t == 0 and t * C * 4 <= (1 << 19)]
    tr = max(fits) if fits else R

    def body(r_ref, w_ref, m_ref, v_ref, g_ref, d_ref, mo_ref, vo_ref):
        g = r_ref[0, 0].astype(F32)
        for d in range(1, n):
            g = g + r_ref[0, d].astype(F32)
        mm = ADAM_B1 * m_ref[0] + (1.0 - ADAM_B1) * g
        vv = ADAM_B2 * v_ref[0] + (1.0 - ADAM_B2) * (g * g)
        m_hat = mm / (1.0 - ADAM_B1 ** ADAM_STEP)
        v_hat = vv / (1.0 - ADAM_B2 ** ADAM_STEP)
        g_ref[0] = g
        d_ref[0] = -ADAM_LR * (m_hat / (jnp.sqrt(v_hat) + ADAM_EPS) + ADAM_WD * w_ref[0])
        mo_ref[0] = mm
        vo_ref[0] = vv

    blk = pl.BlockSpec((1, tr, C), lambda l, i: (l, i, 0))
    return pl.pallas_call(
        body,
        name=name,
        grid=(L, R // tr),
        in_specs=[pl.BlockSpec((1, n, tr, C), lambda l, i: (l, 0, i, 0)), blk, blk, blk],
        out_specs=[blk] * 4,
        out_shape=[jax.ShapeDtypeStruct((L, R, C), F32)] * 4,
        compiler_params=_cp("parallel", "parallel"),
    )(recv, w, m, v)


def _layer_fwd(x, mod, W, P, riders=None):
    riders = riders or _Riders()
    sh1, sc1, g1, sh2, sc2, g2 = mod
    D = D_MODEL
    R = {"x": x}
    (h1,) = _rowwise(_f_pre, [(x, D, 0)], [P["n1"], sc1, sh1], [(D, BF16)], "pre_norm")
    proj = _mm(h1, W["w_in"], "nn", name="mm_in")
    R["h1"], R["proj"] = h1, proj
    bu = _mm_bd(proj, W["bd"], "expand", "mm_s5_b")
    hs = _s5_scan(bu, P["tab_fwd"])
    yc = _mm_bd(hs, W["cdt"], "reduce", "mm_s5_c")
    (gg,) = _rowwise(_f_s5a, [(yc, S5_W, 0), (proj, S5_W, 0)], [P["s5_d"]], [(S5_W, F32)], "s5_gelu")
    z = _mm(gg, W["w_glu"], "nn", name="mm_glu")
    (ys5,) = _rowwise(_f_s5b, [(gg, S5_W, 0), (z, S5_W, 0)], [], [(S5_W, BF16)], "s5_glu")
    R.update(bu=bu, hs=hs, yc=yc, gg=gg, z=z)
    (qn,) = _rowwise(_f_norm, [(proj, 512, 1)], [P["q_norm"]], [(512, BF16)], "q_norm")
    (kvn,) = _rowwise(_f_norm, [(proj, 256, 12)], [P["kv_norm"]], [(256, BF16)], "kv_norm")
    qraw = _mm(qn, W["w_uq"], "nn", name="mm_uq")
    kvraw = _mm(kvn, W["w_ukv"], "nn", name="mm_ukv")
    q, k, v = _rope_fwd(qraw, kvraw, proj, P["cs"], P["sn"])
    (o, lse), got = _flash_fwd(q, k, v, riders.give("flash_fwd"))
    riders.take("flash_fwd", got)
    R.update(qn=qn, kvn=kvn, q=q, k=k, v=v, o=o, lse=lse)
    (yhg, sts), got = _hg_fwd(proj, P["lb"], P["hg_on"], riders.give("hgrn2_fwd"))
    riders.take("hgrn2_fwd", got)
    R["sts"] = sts
    cat = jnp.concatenate([ys5, o.astype(BF16), yhg.astype(BF16)], axis=-1)
    mixed = _mm(cat, W["w_out"], "nn", name="mm_out")
    (x2,) = _rowwise(_f_post, [(x, D, 0), (mixed, D, 0)], [P["n2"], g1], [(D, F32)], "post_norm")
    R.update(cat=cat, mixed=mixed, x2=x2)
    (h2,) = _rowwise(_f_pre, [(x2, D, 0)], [P["n3"], sc2, sh2], [(D, BF16)], "pre_norm")
    rider = riders.give("mm_up")
    u0 = _mm(h2, W["w_up"], "nn", name="mm_up", exch=rider)
    if rider is not None:
        u0, got = u0
        riders.take("mm_up", got)
    a, got = _conv_fwd(u0, P["conv_w"], P["conv_b"], riders.give("conv_fwd"))
    riders.take("conv_fwd", got)
    rider = riders.give("mm_down")
    y = _mm(a, W["w_down"], "nn", name="mm_down", exch=rider)
    if rider is not None:
        y, got = y
        riders.take("mm_down", got)
    (x3,) = _rowwise(_f_post, [(x2, D, 0), (y, D, 0)], [P["n4"], g2], [(D, F32)], "post_norm")
    R.update(h2=h2, u0=u0, a=a, y=y)
    return x3, R


def _layer_bwd(dx3, mod, W, P, R, riders=None):
    riders = riders or _Riders()
    sh1, sc1, g1, sh2, sc2, g2 = mod
    D = D_MODEL
    G = {}
    (dx2a, dy), (dn4, dg2) = _rowwise_vjp(_f_post, [(R["x2"], D, 0), (R["y"], D, 0)], [P["n4"], g2],
                                          [[(dx3, D, 0)]], "post_norm_bwd", [True, True])
    da = _mm(dy, W["w_down"], "nt", name="mm_down_dx")
    G["w_down"] = _mm(R["a"], dy, "tn", out_dtype=BF16, name="mm_down_dw")
    du0, dcw, dcb = _conv_bwd(da, R["u0"], P["conv_w"], P["conv_b"])
    dh2 = _mm(du0, W["w_up"], "nt", name="mm_up_dx")
    G["w_up"] = _mm(R["h2"], du0, "tn", out_dtype=BF16, name="mm_up_dw")
    (dx2,), (dn3, dsc2, dsh2) = _rowwise_vjp(_f_pre, [(R["x2"], D, 0)], [P["n3"], sc2, sh2], [[(dh2, D, 0)]],
                                             "pre_norm_bwd", [True], add_rows={0: (dx2a, D, 0)})
    (dxa, dmixed), (dn2, dg1) = _rowwise_vjp(_f_post, [(R["x"], D, 0), (R["mixed"], D, 0)], [P["n2"], g1],
                                             [[(dx2, D, 0)]], "post_norm_bwd", [True, True])
    dcat = _mm(dmixed, W["w_out"], "nt", name="mm_out_dx")
    G["w_out"] = _mm(R["cat"], dmixed, "tn", out_dtype=BF16, name="mm_out_dw")
    (dga, dz), _ = _rowwise_vjp(_f_s5b, [(R["gg"], S5_W, 0), (R["z"], S5_W, 0)], [], [[(dcat, S5_W, 0)]],
                                "s5_glu_bwd", [True, True])
    dgb = _mm(dz, W["w_glu"], "nt", name="mm_glu_dx")
    G["w_glu"] = _mm(R["gg"], dz, "tn", out_dtype=BF16, name="mm_glu_dw")
    (dyc, dua), (dd,) = _rowwise_vjp(_f_s5a, [(R["yc"], S5_W, 0), (R["proj"], S5_W, 0)], [P["s5_d"]],
                                     [[(dga, S5_W, 0), (dgb, S5_W, 0)]], "s5_gelu_bwd", [True, True])
    dhs = _mm_bd(dyc, W["cdt"], "expand", "mm_s5_c_dx")
    dcdt = _mm_bd(dyc, R["hs"], "grad", "mm_s5_c_dw")
    gs, acc = _s5_scan(dhs, P["tab_rev"], reverse=True, h=R["hs"], bu_fwd=R["bu"])
    dub = _mm_bd(gs, W["bd"], "reduce", "mm_s5_b_dx")
    dbd = _mm_bd(R["proj"], gs, "grad", "mm_s5_b_dw")
    dq, got = _flash_bwd_dq(R["q"], R["k"], R["v"], R["o"], R["lse"], dcat, riders.give("flash_bwd_dq", G))
    riders.take("flash_bwd_dq", got)
    (dk, dv), got = _flash_bwd_dkv(R["q"], R["k"], R["v"], R["o"], R["lse"], dcat, riders.give("flash_bwd_dkv", G))
    riders.take("flash_bwd_dkv", got)
    dqraw, dkvraw, dkr = _rope_bwd(dq, dk, dv, P["cs"], P["sn"])
    dqn = _mm(dqraw, W["w_uq"], "nt", name="mm_uq_dx")
    G["w_uq"] = _mm(R["qn"], dqraw, "tn", out_dtype=BF16, name="mm_uq_dw")
    dkvn = _mm(dkvraw, W["w_ukv"], "nt", name="mm_ukv_dx")
    G["w_ukv"] = _mm(R["kvn"], dkvraw, "tn", out_dtype=BF16, name="mm_ukv_dw")
    (dcq,), (dqnorm,) = _rowwise_vjp(_f_norm, [(R["proj"], 512, 1)], [P["q_norm"]], [[(dqn, 512, 0)]],
                                     "q_norm_bwd", [True])
    (dckv,), (dkvnorm,) = _rowwise_vjp(_f_norm, [(R["proj"], 256, 12)], [P["kv_norm"]], [[(dkvn, 256, 0)]],
                                       "kv_norm_bwd", [True])
    (dhq, dhf, dhi, dhg, dlb, don), got = _hg_bwd(R["proj"], R["sts"], P["lb"], P["hg_on"], dcat,
                                                  riders.give("hgrn2_bwd", G))
    riders.take("hgrn2_bwd", got)
    dproj = jnp.concatenate([dua + dub, dcq, dhq, dhf, dhi, dhg, dckv, dkr], axis=-1).astype(BF16)
    dh1 = _mm(dproj, W["w_in"], "nt", name="mm_in_dx")
    G["w_in"] = _mm(R["h1"], dproj, "tn", out_dtype=BF16, name="mm_in_dw")
    (dx,), (dn1, dsc1, dsh1) = _rowwise_vjp(_f_pre, [(R["x"], D, 0)], [P["n1"], sc1, sh1], [[(dh1, D, 0)]],
                                            "pre_norm_bwd", [True], add_rows={0: (dxa, D, 0)})
    dmod = jnp.concatenate([dsh1, dsc1, dg1, dsh2, dsc2, dg2], axis=-1)
    small = dict(n1=dn1, n2=dn2, n3=dn3, n4=dn4, s5_d=dd, q_norm=dqnorm, kv_norm=dkvnorm,
                 lb=dlb, hg_on=jnp.sum(don.reshape(HG_H, HG_D), axis=0, keepdims=True),
                 conv_w=dcw, conv_b=dcb, bd=dbd, cdt=dcdt, acc=jnp.sum(acc, axis=0, keepdims=True))
    return dx, dmod, G, small


def _cols_from_shards(g):
    return jnp.transpose(g, (1, 0, 2)).reshape(g.shape[1], -1)


def _cols_to_shards(w):
    K = w.shape[0]
    return jnp.transpose(w.reshape(K, N_DEV, -1), (1, 0, 2))


FF_SHARD = 2 * D_FF // N_DEV
FF_PAD = CONV_TC - FF_SHARD


def _pad_ff(w):
    lead = w.shape[:-1]
    w = jnp.swapaxes(w.reshape(*lead, 2, CONV_NC, FF_SHARD), -3, -2)
    return jnp.pad(w, [(0, 0)] * (w.ndim - 1) + [(0, FF_PAD)]).reshape(*lead, 2 * D_FFP)


def _unpad_ff(w):
    lead = w.shape[:-1]
    w = w.reshape(*lead, CONV_NC, 2, CONV_TC)[..., :FF_SHARD]
    return jnp.swapaxes(w, -3, -2).reshape(*lead, 2 * D_FF)


def _asm_up(g):
    return _cols_from_shards(jnp.pad(g, ((0, 0), (0, 0), (0, FF_PAD))))


def _grad_up(g):
    return jnp.transpose(g.reshape(g.shape[0], N_DEV, CONV_TC), (1, 0, 2))[..., :FF_SHARD]


def _asm_down(g):
    w = g.reshape(CONV_NC, FF_SHARD, D_MODEL)
    return jnp.pad(w, ((0, 0), (0, FF_PAD), (0, 0))).reshape(D_FFP, D_MODEL)


def _grad_down(g):
    return g.reshape(CONV_NC, CONV_TC, D_MODEL)[:, :FF_SHARD].reshape(N_DEV, -1, D_MODEL)


def _asm_in(g):
    w = _cols_from_shards(g)
    return jnp.concatenate([w[:, 0:1024], w[:, 1344:D_IN], w[:, 1024:1344],
                            jnp.zeros((w.shape[0], D_INP - D_IN), w.dtype)], axis=1)


def _asm_uq(g):
    w = _cols_from_shards(g).reshape(-1, MLA_H, MLA_NOPE + MLA_ROPE)
    return jnp.pad(w, ((0, 0), (0, 0), (0, MLA_HW - MLA_NOPE - MLA_ROPE))).reshape(-1, MLA_H * MLA_HW)


_ASSEMBLE = dict(
    w_in=_asm_in,
    w_glu=lambda g: g.reshape(S5_W, S5_W),
    w_uq=_asm_uq,
    w_ukv=_cols_from_shards,
    w_out=lambda g: g.reshape(D_MODEL, D_MODEL),
    w_up=_asm_up,
    w_down=_asm_down,
)

_GRAD_SHARDS = dict(
    w_in=lambda g: _cols_to_shards(jnp.concatenate([g[:, 0:1024], g[:, 3072:D_IN], g[:, 1024:3072]], axis=1)),
    w_glu=lambda g: g.reshape(N_DEV, -1, S5_W),
    w_uq=lambda g: _cols_to_shards(
        g.reshape(-1, MLA_H, MLA_HW)[:, :, :MLA_NOPE + MLA_ROPE].reshape(-1, MLA_H * (MLA_NOPE + MLA_ROPE))),
    w_ukv=_cols_to_shards,
    w_out=lambda g: g.reshape(N_DEV, -1, D_MODEL),
    w_up=_grad_up,
    w_down=_grad_down,
)


_BIG = ("w_in", "w_glu", "w_uq", "w_ukv", "w_out", "w_up", "w_down")
_EARLY = ("w_in", "w_glu", "w_uq", "w_ukv", "ffn_conv_w")
_SMALL = ("s5_lambda_re", "s5_lambda_im", "s5_log_dt", "s5_b_re", "s5_b_im", "s5_c_re", "s5_c_im", "s5_d",
          "mla_q_norm", "mla_kv_norm", "hg_lb_logits", "hg_out_norm", "mix_pre_norm", "mix_post_norm",
          "ffn_pre_norm", "ffn_post_norm", "ffn_conv_w_full", "ffn_conv_b", "b_ada")
PACK_ROW = 1024


def _pack(parts):
    flat = jnp.concatenate([p.reshape(-1) for p in parts])
    n = flat.shape[0]
    pad = (-n) % (8 * PACK_ROW)
    return jnp.pad(flat, (0, pad)).reshape(-1, PACK_ROW)


def _unpack(packed, shapes):
    flat = packed.reshape(-1)
    out, pos = [], 0
    for s in shapes:
        n = int(np.prod(s))
        out.append(flat[pos:pos + n].reshape(s))
        pos += n
    return out


def _step(x, c, positions, loss_target, w, m, v):
    x = x[0]
    S = x.shape[0]
    L = w["w_in"].shape[0]
    D = D_MODEL
    me = 4 * lax.axis_index("x") + 2 * lax.axis_index("y") + lax.axis_index("c")

    (c_all,) = _exchange([c], False, "gather_c")
    c_all = c_all.reshape(N_DEV, D)
    (c_act,) = _rowwise(lambda a: (a * jax.nn.sigmoid(a),), [(c_all, D, 0)], [], [(D, F32)], "silu_c", tm=N_DEV)
    mod_part = jnp.stack([_mm(c_act, w["w_ada"][l], "nn", name="mm_ada") for l in range(L)])
    (mod_all,) = _exchange([mod_part], False, "gather_mod")
    mod_mine = lax.dynamic_index_in_dim(mod_all, me, axis=2, keepdims=False)
    mod_full = jnp.transpose(mod_mine, (1, 0, 2)).reshape(L, 6 * D) + w["b_ada"]
    mods = [[mod_full[l:l + 1, i * D:(i + 1) * D] for i in range(6)] for l in range(L)]

    cs, sn = _rope_tables(positions[0])
    lower, lower_vjp = jax.vjp(lambda lg: jnp.cumsum(jax.nn.softmax(lg, axis=0), axis=0)
                               - jax.nn.softmax(lg, axis=0)[0:1], w["hg_lb_logits"])
    conv_w_full = []

    def shard(n, l):
        return w[n][l] if n == "ffn_conv_w" else w[n][l].astype(BF16)

    def layer_params(l, early):
        Wl = {n: _ASSEMBLE[n](early[n]) for n in _EARLY[:-1]}
        cw_full = _cols_from_shards(early["ffn_conv_w"])
        s5_args = (w["s5_lambda_re"][l], w["s5_lambda_im"][l], w["s5_log_dt"][l], w["s5_b_re"][l], w["s5_b_im"][l],
                   w["s5_c_re"][l], w["s5_c_im"][l])
        (lbr, lbi, bd, cdt), prep_vjp = jax.vjp(_s5_prep, *s5_args)
        tab_fwd, tab_rev = _s5_tables(*s5_args[:3])
        Wl["bd"], Wl["cdt"] = bd.astype(BF16), cdt.astype(BF16)
        row = lambda a: a.reshape(1, -1)
        Pl = dict(
            n1=row(w["mix_pre_norm"][l]), n2=row(w["mix_post_norm"][l]), n3=row(w["ffn_pre_norm"][l]),
            n4=row(w["ffn_post_norm"][l]), s5_d=row(w["s5_d"][l]), q_norm=row(w["mla_q_norm"][l]),
            kv_norm=row(w["mla_kv_norm"][l]), lb=row(lower[l]), hg_on=row(w["hg_out_norm"][l]),
            conv_w=_pad_ff(cw_full), conv_b=_pad_ff(row(w["ffn_conv_b"][l])),
            tab_fwd=tab_fwd, tab_rev=tab_rev, cs=cs, sn=sn, lam_bar=(lbr, lbi))
        return Wl, Pl, prep_vjp, cw_full

    Ws, Ps, preps, Rs = [], [], [], []
    h = x
    half = w["w_up"].shape[1] // 2
    early = dict(zip(_EARLY, _exchange([shard(n, 0) for n in _EARLY], False, "gather_weights")))
    for l in range(L):
        Wl, Pl, prep_vjp, cw_full = layer_params(l, early)
        Ws.append(Wl)
        Ps.append(Pl)
        preps.append(prep_vjp)
        conv_w_full.append(cw_full)
        riders, stash, early = _Riders(), {}, {}

        def got_flash(got, Wl=Wl, stash=stash):
            Wl["w_out"] = _ASSEMBLE["w_out"](got[0])
            stash["up"] = got[1]

        def got_hg(got, Wl=Wl, stash=stash):
            Wl["w_up"] = _ASSEMBLE["w_up"](jnp.concatenate([stash["up"], got[0]], axis=1))

        def got_up(got, Wl=Wl):
            Wl["w_down"] = _ASSEMBLE["w_down"](got[0])

        riders.add("flash_fwd", lambda _, l=l: ([shard("w_out", l), shard("w_up", l)[:half]], False, [False, True]),
                   got_flash)
        riders.add("hgrn2_fwd", lambda _, l=l: ([shard("w_up", l)[half:]], False, [True]), got_hg)
        riders.add("mm_up", lambda _, l=l: ([shard("w_down", l)], False), got_up)
        if l + 1 < L:
            riders.add("conv_fwd", lambda _, l=l: ([shard(n, l + 1) for n in _EARLY[1:]], False),
                       lambda got, early=early: early.update(zip(_EARLY[1:], got)))
            riders.add("mm_down", lambda _, l=l: ([shard(_EARLY[0], l + 1)], False),
                       lambda got, early=early: early.update(zip(_EARLY[:1], got)))
        h, R = _layer_fwd(h, mods[l], Wl, Pl, riders)
        Rs.append(R)
    loss_local, dh = _loss_head(h, loss_target[0])
    loss = lax.psum(loss_local, ("x", "y", "c"))

    big_recv = {n: [None] * L for n in _BIG}
    small_g = {n: [None] * L for n in _SMALL if n != "hg_lb_logits"}
    dlower = [None] * L
    late = ("w_out", "w_glu", "w_uq", "w_ukv")
    pending_in = None
    for l in reversed(range(L)):
        riders = _Riders()

        def store(names, l=l):
            def done(got):
                for n, r in zip(names, got):
                    big_recv[n][l] = r
            return done

        def make_late(G, prev=pending_in):
            return [_GRAD_SHARDS[n](G[n]) for n in late] + ([prev[1]] if prev else []), True

        def done_late(got, l=l, prev=pending_in):
            store(late, l)(got)
            if prev:
                big_recv["w_in"][prev[0]] = got[len(late)]

        riders.add("flash_bwd_dq", lambda G: ([_GRAD_SHARDS["w_down"](G["w_down"])], True), store(("w_down",)))
        riders.add("flash_bwd_dkv", lambda G: ([_GRAD_SHARDS["w_up"](G["w_up"])], True, [True]), store(("w_up",)))
        riders.add("hgrn2_bwd", make_late, done_late)
        dh, dmod, G, sm = _layer_bwd(dh, mods[l], Ws[l], Ps[l], Rs[l], riders)
        Rs[l] = None
        pending_in = (l, _GRAD_SHARDS["w_in"](G["w_in"]))
        lbr, lbi = Ps[l]["lam_bar"]
        ar, ai = _ri_split(sm["acc"])
        dl = lax.complex(ar, ai) / lax.complex(lbr, -lbi)
        d_s5 = preps[l]((jnp.real(dl), jnp.imag(dl), sm["bd"], sm["cdt"]))
        for n, g in zip(("s5_lambda_re", "s5_lambda_im", "s5_log_dt", "s5_b_re", "s5_b_im", "s5_c_re", "s5_c_im"), d_s5):
            small_g[n][l] = g
        small_g["s5_d"][l] = sm["s5_d"][0]
        small_g["mla_q_norm"][l] = sm["q_norm"][0]
        small_g["mla_kv_norm"][l] = sm["kv_norm"][0]
        small_g["hg_out_norm"][l] = sm["hg_on"][0]
        small_g["mix_pre_norm"][l] = sm["n1"][0]
        small_g["mix_post_norm"][l] = sm["n2"][0]
        small_g["ffn_pre_norm"][l] = sm["n3"][0]
        small_g["ffn_post_norm"][l] = sm["n4"][0]
        small_g["ffn_conv_w_full"][l] = _unpad_ff(sm["conv_w"])
        small_g["ffn_conv_b"][l] = _unpad_ff(sm["conv_b"])[0]
        small_g["b_ada"][l] = dmod[0]
        dlower[l] = sm["lb"][0]
    (big_recv["w_in"][0],) = _exchange([pending_in[1]], True, "scatter_grads")
    small_g = {n: jnp.stack(gl) for n, gl in small_g.items()}
    (small_g["hg_lb_logits"],) = lower_vjp(jnp.stack(dlower))

    small_w = {n: w[n] for n in _SMALL if n != "ffn_conv_w_full"}
    small_w["ffn_conv_w_full"] = jnp.stack(conv_w_full)
    shapes = [small_w[n].shape for n in _SMALL]
    zeros_cw = jnp.zeros_like(small_w["ffn_conv_w_full"])
    pk_g = _pack([small_g[n] for n in _SMALL])
    pk_w = _pack([small_w[n] for n in _SMALL])
    pk_m = _pack([zeros_cw if n == "ffn_conv_w_full" else m[n] for n in _SMALL])
    pk_v = _pack([zeros_cw + 1.0 if n == "ffn_conv_w_full" else v[n] for n in _SMALL])
    (pk_all,) = _exchange([pk_g], False, "gather_small")
    sg, sd, sm_, sv = _adamw(pk_all[None], pk_w[None], pk_m[None], pk_v[None], name="adamw_small")
    small_out = {}
    for key, arr in (("g", sg), ("d", sd), ("m", sm_), ("v", sv)):
        small_out[key] = dict(zip(_SMALL, _unpack(arr[0], shapes)))

    n_cw = w["ffn_conv_w"].shape[-1]
    g_cw = lax.dynamic_slice_in_dim(small_out["g"]["ffn_conv_w_full"], me * n_cw, n_cw, axis=2)
    cw_out = _adamw(g_cw[:, None], w["ffn_conv_w"], m["ffn_conv_w"], v["ffn_conv_w"], name="adamw_conv_w")

    n_ada = w["w_ada"].shape[-1]
    flat_all = pk_all.reshape(N_DEV, -1)
    off = sum(int(np.prod(s)) for s in shapes[:-1])
    dmod_all = flat_all[:, off:off + L * 6 * D].reshape(N_DEV, L, 6 * D)
    dmod_cols = lax.dynamic_slice_in_dim(dmod_all, me * n_ada, n_ada, axis=2)
    g_ada = jnp.stack([_mm(c_act, dmod_cols[:, l], "tn", name="mm_ada_dw") for l in range(L)])
    ada_out = _adamw(g_ada[:, None], w["w_ada"], m["w_ada"], v["w_ada"], name="adamw_ada")

    big_out = {}
    for n in _BIG:
        recv = jnp.stack(big_recv[n])
        big_out[n] = _adamw(recv, w[n], m[n], v[n], name="adamw_" + n)
    big_out["w_ada"] = ada_out
    big_out["ffn_conv_w"] = cw_out
    return loss, dh[None], big_out, small_out


_WEIGHTS = ("w_in", "s5_lambda_re", "s5_lambda_im", "s5_log_dt", "s5_b_re", "s5_b_im", "s5_c_re", "s5_c_im", "s5_d",
            "s5_w_glu", "mla_q_norm", "mla_w_uq", "mla_kv_norm", "mla_w_ukv", "hg_lb_logits", "hg_out_norm", "w_out",
            "mix_pre_norm", "mix_post_norm", "ffn_pre_norm", "ffn_post_norm", "ffn_w_up", "ffn_conv_w", "ffn_conv_b",
            "ffn_w_down", "w_ada", "b_ada")
_ALIAS = {"s5_w_glu": "w_glu", "mla_w_uq": "w_uq", "mla_w_ukv": "w_ukv", "ffn_w_up": "w_up", "ffn_w_down": "w_down"}


def kernel(x, c, positions, w_in, s5_lambda_re, s5_lambda_im, s5_log_dt, s5_b_re, s5_b_im, s5_c_re, s5_c_im, s5_d, s5_w_glu, mla_q_norm, mla_w_uq, mla_kv_norm, mla_w_ukv, hg_lb_logits, hg_out_norm, w_out, mix_pre_norm, mix_post_norm, ffn_pre_norm, ffn_post_norm, ffn_w_up, ffn_conv_w, ffn_conv_b, ffn_w_down, w_ada, b_ada, loss_target, m_w_in, m_s5_lambda_re, m_s5_lambda_im, m_s5_log_dt, m_s5_b_re, m_s5_b_im, m_s5_c_re, m_s5_c_im, m_s5_d, m_s5_w_glu, m_mla_q_norm, m_mla_w_uq, m_mla_kv_norm, m_mla_w_ukv, m_hg_lb_logits, m_hg_out_norm, m_w_out, m_mix_pre_norm, m_mix_post_norm, m_ffn_pre_norm, m_ffn_post_norm, m_ffn_w_up, m_ffn_conv_w, m_ffn_conv_b, m_ffn_w_down, m_w_ada, m_b_ada, v_w_in, v_s5_lambda_re, v_s5_lambda_im, v_s5_log_dt, v_s5_b_re, v_s5_b_im, v_s5_c_re, v_s5_c_im, v_s5_d, v_s5_w_glu, v_mla_q_norm, v_mla_w_uq, v_mla_kv_norm, v_mla_w_ukv, v_hg_lb_logits, v_hg_out_norm, v_w_out, v_mix_pre_norm, v_mix_post_norm, v_ffn_pre_norm, v_ffn_post_norm, v_ffn_w_up, v_ffn_conv_w, v_ffn_conv_b, v_ffn_w_down, v_w_ada, v_b_ada):
    args = locals()
    key = lambda n: _ALIAS.get(n, n)
    w = {key(n): args[n] for n in _WEIGHTS}
    m = {key(n): args["m_" + n] for n in _WEIGHTS}
    v = {key(n): args["v_" + n] for n in _WEIGHTS}
    loss, grad_x, big, small = _step(x, c, positions, loss_target, w, m, v)

    def pick(n, idx):
        k = key(n)
        if k in big:
            return big[k][idx].reshape(w[k].shape)
        return small["gdmv"[idx]][k]

    outs = [loss, grad_x]
    for idx in range(4):
        outs += [pick(n, idx) for n in _WEIGHTS]
    return tuple(outs)
```

```python
import functools
import math

import numpy as np
import jax
import jax.numpy as jnp
from jax import lax
from jax.experimental import pallas as pl
from jax.experimental.pallas import tpu as pltpu

F32 = jnp.float32
BF16 = jnp.bfloat16
N_DEV = 8
V7X_VMEM_LIMIT = 56 * 1024 * 1024
MM_VMEM_BUDGET = 28 * 1024 * 1024
LANE = 128

D_MODEL = 2048
S5_W = 512
S5_G = 32
S5_C = 16
S5_P = 64
S5_N = S5_G * S5_P
S5_TL = 512
MLA_H = 8
MLA_NOPE = 128
MLA_ROPE = 64
MLA_V = 128
MLA_HW = 256
HG_H = 4
HG_D = 128
HG_CH = 16
D_FF = 5504
D_FFP = 5632
D_IN = 3392
D_INP = 3456
EPS = 1e-6
MASK_VALUE = -1e30
ROPE_THETA = 10000.0
ATT_SCALE = (MLA_NOPE + MLA_ROPE) ** -0.5

ADAM_LR = 0.001
ADAM_B1 = 0.9
ADAM_B2 = 0.999
ADAM_EPS = 1e-08
ADAM_WD = 0.01
ADAM_STEP = 10

_IN_PERM = np.concatenate([np.arange(0, 1024), np.arange(1344, 3392), np.arange(1024, 1344)])
_IN_INV = np.argsort(_IN_PERM)

_NN = (((1,), (0,)), ((), ()))
_NT = (((1,), (1,)), ((), ()))
_TN = (((0,), (0,)), ((), ()))


def _cp(*sem):
    return pltpu.CompilerParams(dimension_semantics=sem, vmem_limit_bytes=V7X_VMEM_LIMIT)


def _tile(n, cap, align=LANE):
    if n <= cap:
        return n
    t = (cap // align) * align
    while t >= align:
        if n % t == 0:
            return t
        t -= align
    raise ValueError(f"no tile for {n}")


def _bdot(a, b, dims):
    return lax.dot_general(a.astype(BF16), b.astype(BF16), dims, preferred_element_type=F32)


def _mm(a, b, mode, out_dtype=F32, a_col0=0, a_cols=None, name="mm", exch=None):
    if mode == "tn":
        K = a.shape[0]
        M = a_cols if a_cols is not None else a.shape[1]
        N = b.shape[1]
    else:
        M = a.shape[0]
        K = a_cols if a_cols is not None else a.shape[1]
        N = b.shape[0] if mode == "nt" else b.shape[1]
    tm = _tile(M, 1024, 8 if M < LANE else LANE)
    tn = _tile(N, 1024)
    kal = 8 if K < LANE else LANE
    tk = _tile(K, 2048, kal)
    osz = jnp.dtype(out_dtype).itemsize

    def vmem(tk_):
        acc = 0 if tk_ == K else tm * tn * 4
        return 2 * tk_ * (tm * a.dtype.itemsize + tn * b.dtype.itemsize) + 2 * tm * tn * osz + acc

    while vmem(tk) > MM_VMEM_BUDGET and tk > kal:
        tk = _tile(K, tk - kal, kal)
    nk = K // tk
    if mode == "tn":
        assert a_col0 % tm == 0
        a_spec = pl.BlockSpec((tk, tm), lambda i, j, k: (k, i + a_col0 // tm))
        b_spec = pl.BlockSpec((tk, tn), lambda i, j, k: (k, j))
        dims = _TN
    else:
        assert a_col0 % tk == 0
        a_spec = pl.BlockSpec((tm, tk), lambda i, j, k: (i, k + a_col0 // tk))
        if mode == "nn":
            b_spec = pl.BlockSpec((tk, tn), lambda i, j, k: (k, j))
            dims = _NN
        else:
            b_spec = pl.BlockSpec((tn, tk), lambda i, j, k: (j, k))
            dims = _NT

    if nk == 1:
        def body(a_ref, b_ref, o_ref):
            o_ref[...] = _bdot(a_ref[...], b_ref[...], dims).astype(o_ref.dtype)

        scratch = []
    else:
        def body(a_ref, b_ref, o_ref, acc_ref):
            k = pl.program_id(2)

            @pl.when(k == 0)
            def _():
                acc_ref[...] = jnp.zeros_like(acc_ref)

            acc_ref[...] += _bdot(a_ref[...], b_ref[...], dims)

            @pl.when(k == nk - 1)
            def _():
                o_ref[...] = acc_ref[...].astype(o_ref.dtype)

        scratch = [pltpu.VMEM((tm, tn), F32)]

    (out,), got = _call_hosting(
        body, name, (M // tm, N // tn, nk), [a_spec, b_spec], [pl.BlockSpec((tm, tn), lambda i, j, k: (i, j))],
        [jax.ShapeDtypeStruct((M, N), out_dtype)], [a, b], exch,
        scratch=scratch, sem=["parallel", "parallel", "arbitrary"])
    return out if exch is None else (out, got)


def _row_spec(tm, width, cb):
    return pl.BlockSpec((tm, width), lambda i: (i, cb))


def _rowwise(fn, rows, params, outs, name, tm=256):
    S = rows[0][0].shape[0]
    nr, npar = len(rows), len(params)

    def body(*refs):
        xs = [r[...].astype(F32) for r in refs[:nr]]
        ps = [p[...] for p in refs[nr:nr + npar]]
        res = fn(*xs, *ps)
        for o, r in zip(refs[nr + npar:], res):
            o[...] = r.astype(o.dtype)

    res = pl.pallas_call(
        body,
        name=name,
        grid=(S // tm,),
        in_specs=[_row_spec(tm, w, cb) for _, w, cb in rows]
        + [pl.BlockSpec(p.shape, lambda i: (0, 0)) for p in params],
        out_specs=[_row_spec(tm, w, 0) for w, _ in outs],
        out_shape=[jax.ShapeDtypeStruct((S, w), dt) for w, dt in outs],
        compiler_params=_cp("parallel"),
    )(*[r[0] for r in rows], *params)
    return list(res)


def _rowwise_vjp(fn, rows, params, cts, name, row_grads, add_rows=None, tm=256):
    S = rows[0][0].shape[0]
    add_rows = add_rows or {}
    nr, npar = len(rows), len(params)
    flat_cts = [c for group in cts for c in group]
    ncts = len(flat_cts)
    add_keys = sorted(add_rows)
    nadd = len(add_keys)
    grad_idx = [i for i in range(nr) if row_grads[i]]

    def body(*refs):
        i = pl.program_id(0)
        xs = [r[...].astype(F32) for r in refs[:nr]]
        ps = [p[...] for p in refs[nr:nr + npar]]
        ct_refs = refs[nr + npar:nr + npar + ncts]
        add_refs = refs[nr + npar + ncts:nr + npar + ncts + nadd]
        out_refs = refs[nr + npar + ncts + nadd:]
        ct_vals, pos = [], 0
        for group in cts:
            v = ct_refs[pos][...].astype(F32)
            for r in ct_refs[pos + 1:pos + len(group)]:
                v = v + r[...].astype(F32)
            pos += len(group)
            ct_vals.append(v)
        _, vjp = jax.vjp(lambda *a: tuple(fn(*a)), *xs, *ps)
        grads = vjp(tuple(ct_vals))
        for o, gi in zip(out_refs[:len(grad_idx)], grad_idx):
            g = grads[gi]
            if gi in add_rows:
                g = g + add_refs[add_keys.index(gi)][...].astype(F32)
            o[...] = g.astype(o.dtype)
        dprefs = out_refs[len(grad_idx):]

        @pl.when(i == 0)
        def _():
            for dp in dprefs:
                dp[...] = jnp.zeros_like(dp)

        for dp, g in zip(dprefs, grads[nr:]):
            dp[...] += g

    res = pl.pallas_call(
        body,
        name=name,
        grid=(S // tm,),
        in_specs=[_row_spec(tm, w, cb) for _, w, cb in rows]
        + [pl.BlockSpec(p.shape, lambda i: (0, 0)) for p in params]
        + [_row_spec(tm, w, cb) for _, w, cb in flat_cts]
        + [_row_spec(tm, add_rows[k][1], add_rows[k][2]) for k in add_keys],
        out_specs=[_row_spec(tm, rows[gi][1], 0) for gi in grad_idx]
        + [pl.BlockSpec(p.shape, lambda i: (0, 0)) for p in params],
        out_shape=[jax.ShapeDtypeStruct((S, rows[gi][1]), F32) for gi in grad_idx]
        + [jax.ShapeDtypeStruct(p.shape, F32) for p in params],
        compiler_params=_cp("arbitrary"),
    )(*[r[0] for r in rows], *params, *[c[0] for c in flat_cts], *[add_rows[k][0] for k in add_keys])
    res = list(res)
    return res[:len(grad_idx)], res[len(grad_idx):]


def _rms(x, gain):
    return x * lax.rsqrt(jnp.mean(x * x, axis=-1, keepdims=True) + EPS) * gain


def _f_pre(x, gain, sc, sh):
    return (_rms(x, gain) * (1.0 + sc) + sh,)


def _f_post(x, y, gain, g):
    return (x + g * _rms(y, gain),)


def _f_norm(x, gain):
    return (_rms(x, gain),)


def _f_s5a(yc, u, d):
    return (jax.nn.gelu(yc + d * u, approximate=True),)


def _f_s5b(g, z):
    return (g * jax.nn.sigmoid(z),)


def _loss_head(y, target, tm=256):
    S, D = y.shape

    def body(y_ref, t_ref, dy_ref, acc_ref):
        i = pl.program_id(0)
        e = y_ref[...] - t_ref[...]
        dy_ref[...] = e * (1.0 / D)

        @pl.when(i == 0)
        def _():
            acc_ref[...] = jnp.zeros_like(acc_ref)

        acc_ref[...] += jnp.sum(e * e, axis=0, keepdims=True)

    dy, acc = pl.pallas_call(
        body,
        name="loss_head",
        grid=(S // tm,),
        in_specs=[_row_spec(tm, D, 0), _row_spec(tm, D, 0)],
        out_specs=[_row_spec(tm, D, 0), pl.BlockSpec((1, D), lambda i: (0, 0))],
        out_shape=[jax.ShapeDtypeStruct((S, D), F32), jax.ShapeDtypeStruct((1, D), F32)],
        compiler_params=_cp("arbitrary"),
    )(y, target)
    return 0.5 * jnp.sum(acc) / D, dy


def _s5_tile_scan(xr, xi, tab_ref, reverse, row8):
    for k in (1, 2, 4):
        pr = tab_ref[pl.ds(k - 1, 1), 0:S5_TL] if not reverse else tab_ref[pl.ds(8 - k, 1), 0:S5_TL]
        pi = tab_ref[pl.ds(k - 1, 1), S5_TL:2 * S5_TL] if not reverse else tab_ref[pl.ds(8 - k, 1), S5_TL:2 * S5_TL]
        if not reverse:
            keep = row8 >= k
            sr = jnp.where(keep, pltpu.roll(xr, k, 0), 0.0)
            si = jnp.where(keep, pltpu.roll(xi, k, 0), 0.0)
        else:
            keep = row8 < 8 - k
            sr = jnp.where(keep, pltpu.roll(xr, 8 - k, 0), 0.0)
            si = jnp.where(keep, pltpu.roll(xi, 8 - k, 0), 0.0)
        xr, xi = xr + pr * sr - pi * si, xi + pr * si + pi * sr
    return xr, xi


def _s5_scan(bu, tab, reverse=False, h=None, bu_fwd=None, tr=512):
    S = bu.shape[0]
    tr = min(tr, S)
    nl = S5_N // S5_TL
    nrb = S // tr
    w = 2 * S5_TL
    nt = tr // 8
    rmap = (lambda j, i: (i, j)) if not reverse else (lambda j, i: (nrb - 1 - i, j))

    def body(*refs):
        if reverse:
            x_ref, tab_ref, h_ref, b_ref, o_ref, acc_ref, cr_ref, ci_ref = refs
        else:
            x_ref, tab_ref, o_ref, cr_ref, ci_ref = refs
        i = pl.program_id(1)
        row8 = lax.broadcasted_iota(jnp.int32, (8, S5_TL), 0)

        @pl.when(i == 0)
        def _():
            cr_ref[...] = jnp.zeros_like(cr_ref)
            ci_ref[...] = jnp.zeros_like(ci_ref)
            if reverse:
                acc_ref[...] = jnp.zeros_like(acc_ref)

        tr_all = tab_ref[:, 0:S5_TL]
        ti_all = tab_ref[:, S5_TL:w]

        def tile(t, carry):
            tt = (nt - 1 - t) if reverse else t
            r = pl.ds(pl.multiple_of(tt * 8, 8), 8)
            xr, xi = _s5_tile_scan(x_ref[r, 0:S5_TL], x_ref[r, S5_TL:w], tab_ref, reverse, row8)
            cr = jnp.broadcast_to(cr_ref[...], (8, S5_TL))
            ci = jnp.broadcast_to(ci_ref[...], (8, S5_TL))
            hr = xr + tr_all * cr - ti_all * ci
            hi = xi + tr_all * ci + ti_all * cr
            o_ref[r, 0:S5_TL] = hr
            o_ref[r, S5_TL:w] = hi
            edge = pl.ds(tt * 8, 1) if reverse else pl.ds(tt * 8 + 7, 1)
            cr_ref[...] = o_ref[edge, 0:S5_TL]
            ci_ref[...] = o_ref[edge, S5_TL:w]
            if reverse:
                dr = h_ref[r, 0:S5_TL] - b_ref[r, 0:S5_TL]
                di = h_ref[r, S5_TL:w] - b_ref[r, S5_TL:w]
                acc_ref[:, 0:S5_TL] += hr * dr + hi * di
                acc_ref[:, S5_TL:w] += hi * dr - hr * di
            return carry

        lax.fori_loop(0, nt, tile, 0)

    blk = pl.BlockSpec((tr, w), rmap)
    in_specs = [blk, pl.BlockSpec((8, w), lambda j, i: (0, j))]
    out_specs = [blk]
    out_shape = [jax.ShapeDtypeStruct((S, 2 * S5_N), F32)]
    args = [bu, tab]
    if reverse:
        in_specs += [blk, blk]
        args += [h, bu_fwd]
        out_specs.append(pl.BlockSpec((8, w), lambda j, i: (0, j)))
        out_shape.append(jax.ShapeDtypeStruct((8, 2 * S5_N), F32))
    res = pl.pallas_call(
        body,
        name="s5_scan_bwd" if reverse else "s5_scan_fwd",
        grid=(nl, nrb),
        in_specs=in_specs,
        out_specs=out_specs,
        out_shape=out_shape,
        scratch_shapes=[pltpu.VMEM((1, S5_TL), F32), pltpu.VMEM((1, S5_TL), F32)],
        compiler_params=_cp("parallel", "arbitrary"),
    )(*args)
    return res if reverse else res[0]


def _ri_cols(re, im):
    lead = re.shape[:-1]
    nl = S5_N // S5_TL
    z = jnp.stack([re.reshape(*lead, nl, S5_TL), im.reshape(*lead, nl, S5_TL)], axis=-2)
    return z.reshape(*lead, 2 * S5_N)


def _ri_split(z):
    lead = z.shape[:-1]
    nl = S5_N // S5_TL
    z = z.reshape(*lead, nl, 2, S5_TL)
    return z[..., 0, :].reshape(*lead, S5_N), z[..., 1, :].reshape(*lead, S5_N)


def _s5_prep(lre, lim, logdt, bre, bim, cre, cim):
    lam = lax.complex(lre, lim)
    dt = jnp.exp(logdt)[:, None]
    lam_bar = jnp.exp(lam * dt)
    b = lax.complex(bre, bim)
    b_bar = ((lam_bar - 1.0) / lam)[..., None] * b
    eye = jnp.eye(S5_G, dtype=F32)
    bd_re = jnp.einsum("gpc,gh->gchp", jnp.real(b_bar), eye).reshape(S5_W, S5_N)
    bd_im = jnp.einsum("gpc,gh->gchp", jnp.imag(b_bar), eye).reshape(S5_W, S5_N)
    bd = _ri_cols(bd_re, bd_im)
    cd_re = jnp.einsum("gcp,gh->gchp", cre, eye).reshape(S5_W, S5_N)
    cd_im = jnp.einsum("gcp,gh->gchp", -cim, eye).reshape(S5_W, S5_N)
    cdt = _ri_cols(cd_re, cd_im)

    def diag(m):
        return jnp.stack([m[j * S5_BW:(j + 1) * S5_BW, j * 2 * S5_TL:(j + 1) * 2 * S5_TL] for j in range(S5_NB)])

    return jnp.real(lam_bar).reshape(1, S5_N), jnp.imag(lam_bar).reshape(1, S5_N), diag(bd), diag(cdt)


S5_NB = S5_N // S5_TL
S5_BW = S5_W // S5_NB


def _mm_bd(a, b, mode, name):
    S = a.shape[0]
    wide = 2 * S5_TL
    t = min(S, 1024)
    wspec = pl.BlockSpec((1, S5_BW, wide), lambda i, j: (j, 0, 0))
    if mode == "grad":
        def body(a_ref, b_ref, o_ref):
            @pl.when(pl.program_id(1) == 0)
            def _():
                o_ref[...] = jnp.zeros_like(o_ref)

            o_ref[0] += _bdot(a_ref[...], b_ref[...], _TN)

        return pl.pallas_call(
            body, name=name, grid=(S5_NB, S // t),
            in_specs=[pl.BlockSpec((t, S5_BW), lambda j, k: (k, j)), pl.BlockSpec((t, wide), lambda j, k: (k, j))],
            out_specs=pl.BlockSpec((1, S5_BW, wide), lambda j, k: (j, 0, 0)),
            out_shape=jax.ShapeDtypeStruct((S5_NB, S5_BW, wide), F32),
            compiler_params=_cp("parallel", "arbitrary"))(a, b)
    win, wout, dims = (S5_BW, wide, _NN) if mode == "expand" else (wide, S5_BW, _NT)

    def body(a_ref, w_ref, o_ref):
        o_ref[...] = _bdot(a_ref[...], w_ref[0], dims)

    return pl.pallas_call(
        body, name=name, grid=(S // t, S5_NB),
        in_specs=[pl.BlockSpec((t, win), lambda i, j: (i, j)), wspec],
        out_specs=pl.BlockSpec((t, wout), lambda i, j: (i, j)),
        out_shape=jax.ShapeDtypeStruct((S, S5_NB * wout), F32),
        compiler_params=_cp("parallel", "parallel"))(a, b)


def _s5_tables(lre, lim, logdt):
    lam = lax.complex(lre, lim)
    dt = jnp.exp(logdt)[:, None]
    k = jnp.arange(1, 9, dtype=F32)[:, None, None]
    pw = jnp.exp((lam * dt)[None] * k).reshape(8, S5_N)
    fwd = _ri_cols(jnp.real(pw), jnp.imag(pw))
    rev = _ri_cols(jnp.real(pw)[::-1], -jnp.imag(pw)[::-1])
    return fwd, rev


def _rope_tables(positions):
    inv_freq = 1.0 / (ROPE_THETA ** (jnp.arange(0, MLA_ROPE, 2, dtype=F32) / MLA_ROPE))
    ang = positions.astype(F32)[:, None] * inv_freq
    cos, sin = jnp.cos(ang), jnp.sin(ang)
    z = jnp.zeros_like(cos)
    cs = jnp.concatenate([cos, cos, z, z], axis=-1)
    sn = jnp.concatenate([-sin, sin, z, z], axis=-1)
    return cs, sn


def _rope_fwd(qraw, kvraw, proj, cs, sn, tm=256):
    S = qraw.shape[0]
    HW = MLA_H * MLA_HW

    def rope(x, c, s):
        lane = lax.broadcasted_iota(jnp.int32, x.shape, 1)
        sw = jnp.where(lane < 32, pltpu.roll(x, 96, 1), jnp.where(lane < 64, pltpu.roll(x, 32, 1), 0.0))
        return x * c + sw * s

    def body(q_ref, kv_ref, kr_ref, cs_ref, sn_ref, qo_ref, ko_ref, vo_ref):
        c, s = cs_ref[...], sn_ref[...]
        kr = rope(kr_ref[...], c, s).astype(BF16)
        for h in range(MLA_H):
            o = h * MLA_HW
            qo_ref[:, o:o + 128] = q_ref[:, o:o + 128].astype(BF16)
            qo_ref[:, o + 128:o + 256] = rope(q_ref[:, o + 128:o + 256], c, s).astype(BF16)
            ko_ref[:, o:o + 128] = kv_ref[:, o:o + 128].astype(BF16)
            ko_ref[:, o + 128:o + 256] = kr
            vo_ref[:, h * 128:(h + 1) * 128] = kv_ref[:, o + 128:o + 256].astype(BF16)

    return pl.pallas_call(
        body,
        name="rope_fwd",
        grid=(S // tm,),
        in_specs=[_row_spec(tm, HW, 0), _row_spec(tm, HW, 0), _row_spec(tm, 128, (D_INP - 128) // 128),
                  _row_spec(tm, 128, 0), _row_spec(tm, 128, 0)],
        out_specs=[_row_spec(tm, HW, 0), _row_spec(tm, HW, 0), _row_spec(tm, MLA_H * MLA_V, 0)],
        out_shape=[jax.ShapeDtypeStruct((S, HW), BF16), jax.ShapeDtypeStruct((S, HW), BF16),
                   jax.ShapeDtypeStruct((S, MLA_H * MLA_V), BF16)],
        compiler_params=_cp("parallel"),
    )(qraw, kvraw, proj, cs, sn)


def _rope_bwd(dq, dk, dv, cs, sn, tm=256):
    S = dq.shape[0]
    HW = MLA_H * MLA_HW

    def rope_t(x, c, s):
        lane = lax.broadcasted_iota(jnp.int32, x.shape, 1)
        w = x * s
        sw = jnp.where(lane < 32, pltpu.roll(w, 96, 1), jnp.where(lane < 64, pltpu.roll(w, 32, 1), 0.0))
        return x * c + sw

    def body(dq_ref, dk_ref, dv_ref, cs_ref, sn_ref, qo_ref, kvo_ref, kro_ref):
        c, s = cs_ref[...], sn_ref[...]
        kr = jnp.zeros((tm, 128), F32)
        for h in range(MLA_H):
            o = h * MLA_HW
            qo_ref[:, o:o + 128] = dq_ref[:, o:o + 128]
            qo_ref[:, o + 128:o + 256] = rope_t(dq_ref[:, o + 128:o + 256], c, s)
            kvo_ref[:, o:o + 128] = dk_ref[:, o:o + 128]
            kvo_ref[:, o + 128:o + 256] = dv_ref[:, h * 128:(h + 1) * 128]
            kr = kr + dk_ref[:, o + 128:o + 256]
        kro_ref[...] = rope_t(kr, c, s)

    return pl.pallas_call(
        body,
        name="rope_bwd",
        grid=(S // tm,),
        in_specs=[_row_spec(tm, HW, 0), _row_spec(tm, HW, 0), _row_spec(tm, MLA_H * MLA_V, 0),
                  _row_spec(tm, 128, 0), _row_spec(tm, 128, 0)],
        out_specs=[_row_spec(tm, HW, 0), _row_spec(tm, HW, 0), _row_spec(tm, 128, 0)],
        out_shape=[jax.ShapeDtypeStruct((S, HW), F32), jax.ShapeDtypeStruct((S, HW), F32),
                   jax.ShapeDtypeStruct((S, 128), F32)],
        compiler_params=_cp("parallel"),
    )(dq, dk, dv, cs, sn)


ATT_T = 512


def _diag_mask(t):
    return lax.broadcasted_iota(jnp.int32, (t, t), 1) <= lax.broadcasted_iota(jnp.int32, (t, t), 0)


def _flash_fwd(q, k, v, exch=None):
    S = q.shape[0]
    t = min(ATT_T, S)
    nq = S // t

    def body(q_ref, k_ref, v_ref, o_ref, lse_ref):
        i = pl.program_id(1)
        qb = q_ref[...]

        def step(j, carry, masked):
            m, l, acc = carry
            r = pl.ds(pl.multiple_of(j * t, t), t)
            s = _bdot(qb, k_ref[r, :], _NT) * ATT_SCALE
            if masked:
                s = jnp.where(_diag_mask(t), s, MASK_VALUE)
            m_new = jnp.maximum(m, jnp.max(s, axis=-1, keepdims=True))
            alpha = jnp.exp(m - m_new)
            p = jnp.exp(s - m_new)
            l = alpha * l + jnp.sum(p, axis=-1, keepdims=True)
            acc = alpha * acc + _bdot(p, v_ref[r, :], _NN)
            return m_new, l, acc

        m0 = jnp.full((t, 1), MASK_VALUE, F32)
        init = (m0, jnp.zeros((t, 1), F32), jnp.zeros((t, MLA_V), F32))
        m, l, acc = step(i, lax.fori_loop(0, i, lambda j, c: step(j, c, False), init), True)
        o_ref[...] = acc / l
        lse_ref[...] = jnp.broadcast_to(m + jnp.log(l), (t, 128))

    return _call_hosting(
        body, "flash_fwd", (MLA_H, nq),
        [pl.BlockSpec((t, MLA_HW), lambda h, i: (i, h)),
         pl.BlockSpec((S, MLA_HW), lambda h, i: (0, h)),
         pl.BlockSpec((S, MLA_V), lambda h, i: (0, h))],
        [pl.BlockSpec((t, MLA_V), lambda h, i: (i, h)), pl.BlockSpec((t, 128), lambda h, i: (i, h))],
        [jax.ShapeDtypeStruct((S, MLA_H * MLA_V), F32), jax.ShapeDtypeStruct((S, MLA_H * 128), F32)],
        [q, k, v], exch)


def _flash_bwd_dq(q, k, v, o, lse, dcat, exch=None):
    S = q.shape[0]
    t = min(ATT_T, S)
    nq = S // t
    do_cb = S5_W // MLA_V

    def body(q_ref, k_ref, v_ref, o_ref, lse_ref, do_ref, dq_ref):
        i = pl.program_id(1)
        qb = q_ref[...]
        do = do_ref[...]
        delta = jnp.sum(do * o_ref[...], axis=-1, keepdims=True)
        lse1 = jnp.max(lse_ref[...], axis=-1, keepdims=True)
        dob = do.astype(BF16)

        def step(j, dq, masked):
            r = pl.ds(pl.multiple_of(j * t, t), t)
            kb = k_ref[r, :]
            s = _bdot(qb, kb, _NT) * ATT_SCALE
            p = jnp.exp(s - lse1)
            if masked:
                p = jnp.where(_diag_mask(t), p, 0.0)
            dp = _bdot(dob, v_ref[r, :], _NT)
            ds = p * (dp - delta) * ATT_SCALE
            return dq + _bdot(ds, kb, _NN)

        dq = lax.fori_loop(0, i, lambda j, c: step(j, c, False), jnp.zeros((t, MLA_HW), F32))
        dq_ref[...] = step(i, dq, True)

    (dq,), got = _call_hosting(
        body, "flash_bwd_dq", (MLA_H, nq),
        [pl.BlockSpec((t, MLA_HW), lambda h, i: (i, h)),
         pl.BlockSpec((S, MLA_HW), lambda h, i: (0, h)),
         pl.BlockSpec((S, MLA_V), lambda h, i: (0, h)),
         pl.BlockSpec((t, MLA_V), lambda h, i: (i, h)),
         pl.BlockSpec((t, 128), lambda h, i: (i, h)),
         pl.BlockSpec((t, MLA_V), lambda h, i: (i, do_cb + h))],
        [pl.BlockSpec((t, MLA_HW), lambda h, i: (i, h))], [jax.ShapeDtypeStruct((S, MLA_H * MLA_HW), F32)],
        [q, k, v, o, lse, dcat], exch)
    return dq, got


def _flash_bwd_dkv(q, k, v, o, lse, dcat, exch=None):
    S = q.shape[0]
    t = min(ATT_T, S)
    nq = S // t
    do_cb = S5_W // MLA_V

    def body(q_ref, k_ref, v_ref, o_ref, lse_ref, do_ref, dk_ref, dv_ref):
        j = pl.program_id(1)
        kb = k_ref[...]
        vb = v_ref[...]

        def step(i, carry, masked):
            dk, dv = carry
            r = pl.ds(pl.multiple_of(i * t, t), t)
            qb = q_ref[r, :]
            do = do_ref[r, :]
            delta = jnp.sum(do * o_ref[r, :], axis=-1, keepdims=True)
            lse1 = jnp.max(lse_ref[r, :], axis=-1, keepdims=True)
            s = _bdot(qb, kb, _NT) * ATT_SCALE
            p = jnp.exp(s - lse1)
            if masked:
                p = jnp.where(_diag_mask(t), p, 0.0)
            dob = do.astype(BF16)
            dv = dv + _bdot(p, dob, _TN)
            dp = _bdot(dob, vb, _NT)
            ds = p * (dp - delta) * ATT_SCALE
            dk = dk + _bdot(ds, qb, _TN)
            return dk, dv

        first = step(j, (jnp.zeros((t, MLA_HW), F32), jnp.zeros((t, MLA_V), F32)), True)
        dk, dv = lax.fori_loop(j + 1, nq, lambda i, c: step(i, c, False), first)
        dk_ref[...] = dk
        dv_ref[...] = dv

    return _call_hosting(
        body, "flash_bwd_dkv", (MLA_H, nq),
        [pl.BlockSpec((S, MLA_HW), lambda h, j: (0, h)),
         pl.BlockSpec((t, MLA_HW), lambda h, j: (j, h)),
         pl.BlockSpec((t, MLA_V), lambda h, j: (j, h)),
         pl.BlockSpec((S, MLA_V), lambda h, j: (0, h)),
         pl.BlockSpec((S, 128), lambda h, j: (0, h)),
         pl.BlockSpec((S, MLA_V), lambda h, j: (0, do_cb + h))],
        [pl.BlockSpec((t, MLA_HW), lambda h, j: (j, h)), pl.BlockSpec((t, MLA_V), lambda h, j: (j, h))],
        [jax.ShapeDtypeStruct((S, MLA_H * MLA_HW), F32), jax.ShapeDtypeStruct((S, MLA_H * MLA_V), F32)],
        [q, k, v, o, lse, dcat], exch)


def _split3(x):
    x1 = x.astype(BF16)
    r1 = x - x1.astype(F32)
    x2 = r1.astype(BF16)
    x3 = (r1 - x2.astype(F32)).astype(BF16)
    return x1, x2, x3


def _tri_matmul(x, upper):
    n = x.shape[0]
    r = lax.broadcasted_iota(jnp.int32, (n, n), 0)
    c = lax.broadcasted_iota(jnp.int32, (n, n), 1)
    tri = jnp.where((r <= c) if upper else (r >= c), 1.0, 0.0).astype(BF16)
    x1, x2, x3 = _split3(x)
    dot = lambda v: lax.dot_general(tri, v, _NN, preferred_element_type=F32)
    return dot(x1) + dot(x2) + dot(x3)


@jax.custom_vjp
def _cumsum_rows(x):
    return _tri_matmul(x, False)


def _cumsum_rows_fwd(x):
    return _tri_matmul(x, False), None


def _cumsum_rows_bwd(_, ct):
    return (_tri_matmul(ct, True),)


_cumsum_rows.defvjp(_cumsum_rows_fwd, _cumsum_rows_bwd)


def _hg_step(qin, fin, vin, gin, st, lb, on):
    n = qin.shape[0]
    sig = jax.nn.sigmoid(fin)
    g = jnp.log(lb + (1.0 - lb) * sig)
    k = (1.0 - lb) * jax.nn.sigmoid(-fin)
    q = qin * jax.nn.sigmoid(qin)
    b = _cumsum_rows(g)
    o = _bdot(q * jnp.exp(b), st, _NT)
    row = lax.broadcasted_iota(jnp.int32, (n, HG_D), 0)
    row1 = lax.broadcasted_iota(jnp.int32, (n, 1), 0)
    b_s = None
    for s in range(n):
        sel = row == s
        b_s = jnp.sum(jnp.where(sel, b, 0.0), axis=0, keepdims=True)
        k_s = jnp.sum(jnp.where(sel, k, 0.0), axis=0, keepdims=True)
        v_s = jnp.sum(jnp.where(sel, vin, 0.0), axis=0, keepdims=True)
        e = jnp.exp(jnp.minimum(b - b_s, 0.0))
        c = jnp.sum(q * e * k_s, axis=-1, keepdims=True)
        o = o + jnp.where(row1 >= s, c, 0.0) * v_s
    st_new = st * jnp.exp(b_s) + _bdot(vin, k * jnp.exp(b_s - b), _TN)
    y = _rms(o, on) * (gin * jax.nn.sigmoid(gin))
    return y, st_new


HG_W = HG_H * HG_D


def _hg_specs(tb, nb, reverse):
    rm = (lambda i: nb - 1 - i) if reverse else (lambda i: i)
    base = 1024 // HG_W
    return [pl.BlockSpec((tb, HG_W), lambda i, o=o: (rm(i), base + o)) for o in range(4)], rm


def _head(h):
    return slice(h * HG_D, (h + 1) * HG_D)


def _hg_fwd(proj, lb, on, exch=None, tb=256):
    S = proj.shape[0]
    nb = S // tb
    nc = tb // HG_CH
    in_specs, rm = _hg_specs(tb, nb, False)

    def body(q_ref, f_ref, v_ref, g_ref, lb_ref, on_ref, y_ref, sts_ref, st_ref):
        @pl.when(pl.program_id(0) == 0)
        def _():
            st_ref[...] = jnp.zeros_like(st_ref)

        def step(c, carry):
            r = pl.ds(pl.multiple_of(c * HG_CH, HG_CH), HG_CH)
            for h in range(HG_H):
                hs = _head(h)
                st = st_ref[h]
                sts_ref[h, c] = st
                y, st_new = _hg_step(q_ref[r, hs], f_ref[r, hs], v_ref[r, hs], g_ref[r, hs], st, lb_ref[:, hs],
                                     on_ref[...])
                y_ref[r, hs] = y
                st_ref[h] = st_new
            return carry

        lax.fori_loop(0, nc, step, 0)

    return _call_hosting(
        body, "hgrn2_fwd", (nb,),
        in_specs + [pl.BlockSpec((1, HG_W), lambda i: (0, 0)), pl.BlockSpec((1, HG_D), lambda i: (0, 0))],
        [pl.BlockSpec((tb, HG_W), lambda i: (i, 0)), pl.BlockSpec((HG_H, nc, HG_D, HG_D), lambda i: (0, i, 0, 0))],
        [jax.ShapeDtypeStruct((S, HG_W), F32), jax.ShapeDtypeStruct((HG_H, S // HG_CH, HG_D, HG_D), F32)],
        [proj, proj, proj, proj, lb, on], exch,
        scratch=[pltpu.VMEM((HG_H, HG_D, HG_D), F32)], sem=["arbitrary"])


def _hg_bwd(proj, sts, lb, on, dcat, exch=None, tb=256):
    S = proj.shape[0]
    nb = S // tb
    nc = tb // HG_CH
    in_specs, rm = _hg_specs(tb, nb, True)
    dy_cb = (S5_W + MLA_H * MLA_V) // HG_W

    def body(q_ref, f_ref, v_ref, g_ref, lb_ref, on_ref, sts_ref, dy_ref,
             dq_ref, df_ref, dv_ref, dg_ref, dlb_ref, don_ref, dst_ref):
        @pl.when(pl.program_id(0) == 0)
        def _():
            dst_ref[...] = jnp.zeros_like(dst_ref)
            dlb_ref[...] = jnp.zeros_like(dlb_ref)
            don_ref[...] = jnp.zeros_like(don_ref)

        def step(cc, carry):
            c = nc - 1 - cc
            r = pl.ds(pl.multiple_of(c * HG_CH, HG_CH), HG_CH)
            for h in range(HG_H):
                hs = _head(h)
                _, vjp = jax.vjp(_hg_step, q_ref[r, hs], f_ref[r, hs], v_ref[r, hs], g_ref[r, hs], sts_ref[h, c],
                                 lb_ref[:, hs], on_ref[...])
                dq, df, dv, dg, dst, dlb, don = vjp((dy_ref[r, hs], dst_ref[h]))
                dq_ref[r, hs] = dq
                df_ref[r, hs] = df
                dv_ref[r, hs] = dv
                dg_ref[r, hs] = dg
                dst_ref[h] = dst
                dlb_ref[:, hs] += dlb
                don_ref[:, hs] += don
            return carry

        lax.fori_loop(0, nc, step, 0)

    blk = pl.BlockSpec((tb, HG_W), lambda i: (rm(i), 0))
    par = pl.BlockSpec((1, HG_W), lambda i: (0, 0))
    return _call_hosting(
        body, "hgrn2_bwd", (nb,),
        in_specs + [par, pl.BlockSpec((1, HG_D), lambda i: (0, 0)),
                    pl.BlockSpec((HG_H, nc, HG_D, HG_D), lambda i: (0, rm(i), 0, 0)),
                    pl.BlockSpec((tb, HG_W), lambda i: (rm(i), dy_cb))],
        [blk, blk, blk, blk, par, par],
        [jax.ShapeDtypeStruct((S, HG_W), F32)] * 4 + [jax.ShapeDtypeStruct((1, HG_W), F32)] * 2,
        [proj, proj, proj, proj, lb, on, sts, dcat], exch,
        scratch=[pltpu.VMEM((HG_H, HG_D, HG_D), F32)], sem=["arbitrary"])


CONV_NC = 4
CONV_TC = D_FFP // CONV_NC


def _shift_down(cur, halo, k):
    tm = cur.shape[0]
    row = lax.broadcasted_iota(jnp.int32, cur.shape, 0)
    top = jnp.concatenate([pltpu.roll(halo, k, 0), jnp.zeros((tm - 8, cur.shape[1]), F32)], axis=0)
    return jnp.where(row < k, top, pltpu.roll(cur, k, 0))


def _conv3(cur, halo, w_ref, b_ref):
    return (b_ref[...] + _shift_down(cur, halo, 2) * w_ref[pl.ds(0, 1), :]
            + _shift_down(cur, halo, 1) * w_ref[pl.ds(1, 1), :] + cur * w_ref[pl.ds(2, 1), :])


def _conv_fwd(u0, cw, cb, exch=None, tm=256):
    S = u0.shape[0]
    nc = CONV_NC
    m8 = tm // 8
    prev = lambda i: jnp.maximum(i * m8 - 1, 0)

    def body(u_ref, p_ref, w_ref, b_ref, a_ref):
        on = (pl.program_id(1) > 0).astype(F32)
        u = _conv3(u_ref[...], p_ref[...] * on, w_ref, b_ref)
        a_ref[...] = (jax.nn.gelu(u[:, :CONV_TC], approximate=True) * u[:, CONV_TC:]).astype(a_ref.dtype)

    tc = CONV_TC
    (a,), got = _call_hosting(
        body, "conv_geglu_fwd", (nc, S // tm),
        [pl.BlockSpec((tm, 2 * tc), lambda j, i: (i, j)), pl.BlockSpec((8, 2 * tc), lambda j, i: (prev(i), j)),
         pl.BlockSpec((3, 2 * tc), lambda j, i: (0, j)), pl.BlockSpec((1, 2 * tc), lambda j, i: (0, j))],
        [pl.BlockSpec((tm, tc), lambda j, i: (i, j))], [jax.ShapeDtypeStruct((S, D_FFP), BF16)],
        [u0, u0, cw, cb], exch)
    return a, got


def _conv_bwd(da, u0, cw, cb, tm=256):
    S = u0.shape[0]
    nrb = S // tm
    m8 = tm // 8
    n = tm + 8
    tc = CONV_TC
    prev = lambda i: jnp.maximum(i * m8 - 1, 0)
    nxt = lambda i: jnp.minimum((i + 1) * m8, S // 8 - 1)

    def body(u_ref, p_ref, n_ref, w_ref, b_ref, da_ref, dan_ref, o_ref, dw_ref, db_ref):
        i = pl.program_id(1)
        cur = jnp.concatenate([u_ref[...], n_ref[...]], axis=0)
        halo = p_ref[...] * (i > 0).astype(F32)
        u = _conv3(cur, halo, w_ref, b_ref)
        dact = jnp.concatenate([da_ref[...], dan_ref[...] * (i < nrb - 1).astype(F32)], axis=0)
        _, vjp = jax.vjp(lambda g, v: jax.nn.gelu(g, approximate=True) * v, u[:, :tc], u[:, tc:])
        du = jnp.concatenate(vjp(dact), axis=1)
        du0 = (du * w_ref[pl.ds(2, 1), :] + pltpu.roll(du, n - 1, 0) * w_ref[pl.ds(1, 1), :]
               + pltpu.roll(du, n - 2, 0) * w_ref[pl.ds(0, 1), :])
        o_ref[...] = du0[:tm].astype(o_ref.dtype)

        @pl.when(i == 0)
        def _():
            dw_ref[...] = jnp.zeros_like(dw_ref)
            db_ref[...] = jnp.zeros_like(db_ref)

        own = jnp.where(lax.broadcasted_iota(jnp.int32, du.shape, 0) < tm, du, 0.0)
        dw_ref[pl.ds(0, 1), :] += jnp.sum(own * _shift_down(cur, halo, 2), axis=0, keepdims=True)
        dw_ref[pl.ds(1, 1), :] += jnp.sum(own * _shift_down(cur, halo, 1), axis=0, keepdims=True)
        dw_ref[pl.ds(2, 1), :] += jnp.sum(own * cur, axis=0, keepdims=True)
        db_ref[...] += jnp.sum(own, axis=0, keepdims=True)

    wide = lambda r, f: pl.BlockSpec((r, 2 * tc), f)
    return pl.pallas_call(
        body,
        name="conv_geglu_bwd",
        grid=(CONV_NC, nrb),
        in_specs=[wide(tm, lambda j, i: (i, j)), wide(8, lambda j, i: (prev(i), j)), wide(8, lambda j, i: (nxt(i), j)),
                  wide(3, lambda j, i: (0, j)), wide(1, lambda j, i: (0, j)),
                  pl.BlockSpec((tm, tc), lambda j, i: (i, j)), pl.BlockSpec((8, tc), lambda j, i: (nxt(i), j))],
        out_specs=[wide(tm, lambda j, i: (i, j)), wide(3, lambda j, i: (0, j)), wide(1, lambda j, i: (0, j))],
        out_shape=[jax.ShapeDtypeStruct((S, 2 * D_FFP), BF16), jax.ShapeDtypeStruct((3, 2 * D_FFP), F32),
                   jax.ShapeDtypeStruct((1, 2 * D_FFP), F32)],
        compiler_params=_cp("parallel", "arbitrary"),
    )(u0, u0, u0, cw, cb, da, da)


def _exchange(arrs, scatter, name, ff=None):
    n = len(arrs)

    def body(*refs):
        args = (refs[:n], refs[n:2 * n], *refs[2 * n:], scatter, ff)
        _exchange_start(*args)
        _exchange_wait(*args)

    hbm = pl.BlockSpec(memory_space=pltpu.HBM)
    out_shape, sems = _exchange_shapes(arrs, scatter)
    return pl.pallas_call(
        body,
        name=name,
        in_specs=[hbm] * n,
        out_specs=[hbm] * n,
        out_shape=out_shape,
        scratch_shapes=sems,
    )(*arrs)


def _ff_slot(d):
    return (d % 4) * 2 + d // 4


def _exchange_copies(ins, outs, send, recv, loc, scatter, ff=None):
    x, y, c = lax.axis_index("x"), lax.axis_index("y"), lax.axis_index("c")
    me = 4 * x + 2 * y + c
    sends, recvs, locs = [], [], []
    for a in range(len(ins)):
        slot = _ff_slot if ff and ff[a] else (lambda d: d)
        mine = me if scatter else slot(me)
        locs.append(pltpu.make_async_copy(ins[a].at[slot(me)] if scatter else ins[a], outs[a].at[mine], loc.at[a]))
        for k in range(1, N_DEV):
            px = 1 - x if k & 4 else x
            py = 1 - y if k & 2 else y
            pc = 1 - c if k & 1 else c
            peer = 4 * px + 2 * py + pc
            src = ins[a].at[slot(peer)] if scatter else ins[a]
            sems = dict(send_sem=send.at[a, k - 1], recv_sem=recv.at[a, k - 1], device_id=(px, py, pc),
                        device_id_type=pl.DeviceIdType.MESH)
            sends.append(pltpu.make_async_remote_copy(src_ref=src, dst_ref=outs[a].at[mine], **sems))
            theirs = peer if scatter else slot(peer)
            recvs.append(pltpu.make_async_remote_copy(src_ref=src, dst_ref=outs[a].at[theirs], **sems))
    return locs, sends, recvs


def _exchange_start(*refs):
    locs, sends, _ = _exchange_copies(*refs)
    for cp in locs + sends:
        cp.start()


def _exchange_wait(*refs):
    locs, sends, recvs = _exchange_copies(*refs)
    for cp in recvs:
        cp.wait_recv()
    for cp in sends:
        cp.wait_send()
    for cp in locs:
        cp.wait()


def _exchange_shapes(arrs, scatter):
    n = len(arrs)
    out_shape = [jax.ShapeDtypeStruct(a.shape if scatter else (N_DEV,) + a.shape, a.dtype) for a in arrs]
    sems = [pltpu.SemaphoreType.DMA((n, N_DEV - 1)), pltpu.SemaphoreType.DMA((n, N_DEV - 1)),
            pltpu.SemaphoreType.DMA((n,))]
    return out_shape, sems


def _call_hosting(body, name, grid, in_specs, out_specs, out_shape, args, exch, scratch=(), sem=None):
    scratch = list(scratch)
    if exch is None:
        res = pl.pallas_call(body, name=name, grid=grid, in_specs=in_specs, out_specs=out_specs, out_shape=out_shape,
                             scratch_shapes=scratch,
                             compiler_params=_cp(*(sem or ["parallel"] * len(grid))))(*args)
        return list(res), []
    arrs, scatter, ff = (*exch, None)[:3]
    n, n_in, n_out, n_scr = len(arrs), len(in_specs), len(out_specs), len(scratch)
    hbm = pl.BlockSpec(memory_space=pltpu.HBM)
    x_shape, sems = _exchange_shapes(arrs, scatter)

    def hosting_body(*refs):
        cin, xin = refs[:n_in], refs[n_in:n_in + n]
        cout, xout = refs[n_in + n:n_in + n + n_out], refs[n_in + n + n_out:n_in + 2 * n + n_out]
        cscr = refs[n_in + 2 * n + n_out:n_in + 2 * n + n_out + n_scr]
        xsem = refs[n_in + 2 * n + n_out + n_scr:]
        ids = [pl.program_id(d) for d in range(len(grid))]
        first = functools.reduce(jnp.logical_and, [i == 0 for i in ids])
        last = functools.reduce(jnp.logical_and, [i == g - 1 for i, g in zip(ids, grid)])

        @pl.when(first)
        def _():
            _exchange_start(xin, xout, *xsem, scatter, ff)

        body(*cin, *cout, *cscr)

        @pl.when(last)
        def _():
            _exchange_wait(xin, xout, *xsem, scatter, ff)

    res = pl.pallas_call(
        hosting_body, name=name + "_x", grid=grid, in_specs=in_specs + [hbm] * n, out_specs=out_specs + [hbm] * n,
        out_shape=out_shape + x_shape, scratch_shapes=scratch + sems,
        compiler_params=_cp(*["arbitrary"] * len(grid)))(*args, *arrs)
    return list(res[:n_out]), list(res[n_out:])


class _Riders:
    def __init__(self):
        self.make, self.done = {}, {}

    def add(self, host, make, done):
        self.make[host], self.done[host] = make, done

    def give(self, host, ctx=None):
        return self.make[host](ctx) if host in self.make else None

    def take(self, host, got):
        if host in self.done:
            self.done[host](got)


def _adamw(recv, w, m, v, name="adamw", exch=None):
    L, n, R, C = recv.shape
    fits = [t for t in range(8, R + 1, 8) if R % t == 0 and t * C * 4 <= (1 << 19)]
    tr = max(fits) if fits else R

    def body(r_ref, w_ref, m_ref, v_ref, g_ref, d_ref, mo_ref, vo_ref):
        g = r_ref[0, 0].astype(F32)
        for d in range(1, n):
            g = g + r_ref[0, d].astype(F32)
        mm = ADAM_B1 * m_ref[0] + (1.0 - ADAM_B1) * g
        vv = ADAM_B2 * v_ref[0] + (1.0 - ADAM_B2) * (g * g)
        m_hat = mm / (1.0 - ADAM_B1 ** ADAM_STEP)
        v_hat = vv / (1.0 - ADAM_B2 ** ADAM_STEP)
        g_ref[0] = g
        d_ref[0] = -ADAM_LR * (m_hat / (jnp.sqrt(v_hat) + ADAM_EPS) + ADAM_WD * w_ref[0])
        mo_ref[0] = mm
        vo_ref[0] = vv

    blk = pl.BlockSpec((1, tr, C), lambda l, i: (l, i, 0))
    outs, got = _call_hosting(
        body, name, (L, R // tr), [pl.BlockSpec((1, n, tr, C), lambda l, i: (l, 0, i, 0)), blk, blk, blk],
        [blk] * 4, [jax.ShapeDtypeStruct((L, R, C), F32)] * 4, [recv, w, m, v], exch)
    return outs if exch is None else (outs, got)


def _layer_fwd(x, mod, W, P, riders=None):
    riders = riders or _Riders()
    sh1, sc1, g1, sh2, sc2, g2 = mod
    D = D_MODEL
    R = {"x": x}
    (h1,) = _rowwise(_f_pre, [(x, D, 0)], [P["n1"], sc1, sh1], [(D, BF16)], "pre_norm")
    proj = _mm(h1, W["w_in"], "nn", name="mm_in")
    R["h1"], R["proj"] = h1, proj
    bu = _mm_bd(proj, W["bd"], "expand", "mm_s5_b")
    hs = _s5_scan(bu, P["tab_fwd"])
    yc = _mm_bd(hs, W["cdt"], "reduce", "mm_s5_c")
    (gg,) = _rowwise(_f_s5a, [(yc, S5_W, 0), (proj, S5_W, 0)], [P["s5_d"]], [(S5_W, F32)], "s5_gelu")
    z = _mm(gg, W["w_glu"], "nn", name="mm_glu")
    (ys5,) = _rowwise(_f_s5b, [(gg, S5_W, 0), (z, S5_W, 0)], [], [(S5_W, BF16)], "s5_glu")
    R.update(bu=bu, hs=hs, yc=yc, gg=gg, z=z)
    (qn,) = _rowwise(_f_norm, [(proj, 512, 1)], [P["q_norm"]], [(512, BF16)], "q_norm")
    (kvn,) = _rowwise(_f_norm, [(proj, 256, 12)], [P["kv_norm"]], [(256, BF16)], "kv_norm")
    qraw = _mm(qn, W["w_uq"], "nn", name="mm_uq")
    kvraw = _mm(kvn, W["w_ukv"], "nn", name="mm_ukv")
    q, k, v = _rope_fwd(qraw, kvraw, proj, P["cs"], P["sn"])
    (o, lse), got = _flash_fwd(q, k, v, riders.give("flash_fwd"))
    riders.take("flash_fwd", got)
    R.update(qn=qn, kvn=kvn, q=q, k=k, v=v, o=o, lse=lse)
    (yhg, sts), got = _hg_fwd(proj, P["lb"], P["hg_on"], riders.give("hgrn2_fwd"))
    riders.take("hgrn2_fwd", got)
    R["sts"] = sts
    cat = jnp.concatenate([ys5, o.astype(BF16), yhg.astype(BF16)], axis=-1)
    mixed = _mm(cat, W["w_out"], "nn", name="mm_out")
    (x2,) = _rowwise(_f_post, [(x, D, 0), (mixed, D, 0)], [P["n2"], g1], [(D, F32)], "post_norm")
    R.update(cat=cat, mixed=mixed, x2=x2)
    (h2,) = _rowwise(_f_pre, [(x2, D, 0)], [P["n3"], sc2, sh2], [(D, BF16)], "pre_norm")
    rider = riders.give("mm_up")
    u0 = _mm(h2, W["w_up"], "nn", name="mm_up", exch=rider)
    if rider is not None:
        u0, got = u0
        riders.take("mm_up", got)
    a, got = _conv_fwd(u0, P["conv_w"], P["conv_b"], riders.give("conv_fwd"))
    riders.take("conv_fwd", got)
    rider = riders.give("mm_down")
    y = _mm(a, W["w_down"], "nn", name="mm_down", exch=rider)
    if rider is not None:
        y, got = y
        riders.take("mm_down", got)
    (x3,) = _rowwise(_f_post, [(x2, D, 0), (y, D, 0)], [P["n4"], g2], [(D, F32)], "post_norm")
    R.update(h2=h2, u0=u0, a=a, y=y)
    return x3, R


def _layer_bwd(dx3, mod, W, P, R, riders=None):
    riders = riders or _Riders()
    sh1, sc1, g1, sh2, sc2, g2 = mod
    D = D_MODEL
    G = {}
    (dx2a, dy), (dn4, dg2) = _rowwise_vjp(_f_post, [(R["x2"], D, 0), (R["y"], D, 0)], [P["n4"], g2],
                                          [[(dx3, D, 0)]], "post_norm_bwd", [True, True])
    da = _mm(dy, W["w_down"], "nt", name="mm_down_dx")
    G["w_down"] = _mm(R["a"], dy, "tn", out_dtype=BF16, name="mm_down_dw")
    du0, dcw, dcb = _conv_bwd(da, R["u0"], P["conv_w"], P["conv_b"])
    dh2 = _mm(du0, W["w_up"], "nt", name="mm_up_dx")
    G["w_up"] = _mm(R["h2"], du0, "tn", out_dtype=BF16, name="mm_up_dw")
    (dx2,), (dn3, dsc2, dsh2) = _rowwise_vjp(_f_pre, [(R["x2"], D, 0)], [P["n3"], sc2, sh2], [[(dh2, D, 0)]],
                                             "pre_norm_bwd", [True], add_rows={0: (dx2a, D, 0)})
    (dxa, dmixed), (dn2, dg1) = _rowwise_vjp(_f_post, [(R["x"], D, 0), (R["mixed"], D, 0)], [P["n2"], g1],
                                             [[(dx2, D, 0)]], "post_norm_bwd", [True, True])
    dcat = _mm(dmixed, W["w_out"], "nt", name="mm_out_dx")
    G["w_out"] = _mm(R["cat"], dmixed, "tn", out_dtype=BF16, name="mm_out_dw")
    (dga, dz), _ = _rowwise_vjp(_f_s5b, [(R["gg"], S5_W, 0), (R["z"], S5_W, 0)], [], [[(dcat, S5_W, 0)]],
                                "s5_glu_bwd", [True, True])
    dgb = _mm(dz, W["w_glu"], "nt", name="mm_glu_dx")
    G["w_glu"] = _mm(R["gg"], dz, "tn", out_dtype=BF16, name="mm_glu_dw")
    (dyc, dua), (dd,) = _rowwise_vjp(_f_s5a, [(R["yc"], S5_W, 0), (R["proj"], S5_W, 0)], [P["s5_d"]],
                                     [[(dga, S5_W, 0), (dgb, S5_W, 0)]], "s5_gelu_bwd", [True, True])
    dhs = _mm_bd(dyc, W["cdt"], "expand", "mm_s5_c_dx")
    dcdt = _mm_bd(dyc, R["hs"], "grad", "mm_s5_c_dw")
    gs, acc = _s5_scan(dhs, P["tab_rev"], reverse=True, h=R["hs"], bu_fwd=R["bu"])
    dub = _mm_bd(gs, W["bd"], "reduce", "mm_s5_b_dx")
    dbd = _mm_bd(R["proj"], gs, "grad", "mm_s5_b_dw")
    dq, got = _flash_bwd_dq(R["q"], R["k"], R["v"], R["o"], R["lse"], dcat, riders.give("flash_bwd_dq", G))
    riders.take("flash_bwd_dq", got)
    (dk, dv), got = _flash_bwd_dkv(R["q"], R["k"], R["v"], R["o"], R["lse"], dcat, riders.give("flash_bwd_dkv", G))
    riders.take("flash_bwd_dkv", got)
    dqraw, dkvraw, dkr = _rope_bwd(dq, dk, dv, P["cs"], P["sn"])
    dqn = _mm(dqraw, W["w_uq"], "nt", name="mm_uq_dx")
    G["w_uq"] = _mm(R["qn"], dqraw, "tn", out_dtype=BF16, name="mm_uq_dw")
    dkvn = _mm(dkvraw, W["w_ukv"], "nt", name="mm_ukv_dx")
    G["w_ukv"] = _mm(R["kvn"], dkvraw, "tn", out_dtype=BF16, name="mm_ukv_dw")
    (dcq,), (dqnorm,) = _rowwise_vjp(_f_norm, [(R["proj"], 512, 1)], [P["q_norm"]], [[(dqn, 512, 0)]],
                                     "q_norm_bwd", [True])
    (dckv,), (dkvnorm,) = _rowwise_vjp(_f_norm, [(R["proj"], 256, 12)], [P["kv_norm"]], [[(dkvn, 256, 0)]],
                                       "kv_norm_bwd", [True])
    (dhq, dhf, dhi, dhg, dlb, don), got = _hg_bwd(R["proj"], R["sts"], P["lb"], P["hg_on"], dcat,
                                                  riders.give("hgrn2_bwd", G))
    riders.take("hgrn2_bwd", got)
    dproj = jnp.concatenate([dua + dub, dcq, dhq, dhf, dhi, dhg, dckv, dkr], axis=-1).astype(BF16)
    dh1 = _mm(dproj, W["w_in"], "nt", name="mm_in_dx")
    G["w_in"] = _mm(R["h1"], dproj, "tn", out_dtype=BF16, name="mm_in_dw")
    (dx,), (dn1, dsc1, dsh1) = _rowwise_vjp(_f_pre, [(R["x"], D, 0)], [P["n1"], sc1, sh1], [[(dh1, D, 0)]],
                                            "pre_norm_bwd", [True], add_rows={0: (dxa, D, 0)})
    dmod = jnp.concatenate([dsh1, dsc1, dg1, dsh2, dsc2, dg2], axis=-1)
    small = dict(n1=dn1, n2=dn2, n3=dn3, n4=dn4, s5_d=dd, q_norm=dqnorm, kv_norm=dkvnorm,
                 lb=dlb, hg_on=jnp.sum(don.reshape(HG_H, HG_D), axis=0, keepdims=True),
                 conv_w=dcw, conv_b=dcb, bd=dbd, cdt=dcdt, acc=jnp.sum(acc, axis=0, keepdims=True))
    return dx, dmod, G, small


def _cols_from_shards(g):
    return jnp.transpose(g, (1, 0, 2)).reshape(g.shape[1], -1)


def _cols_to_shards(w):
    K = w.shape[0]
    return jnp.transpose(w.reshape(K, N_DEV, -1), (1, 0, 2))


FF_SHARD = 2 * D_FF // N_DEV
FF_PAD = CONV_TC - FF_SHARD


def _pad_ff(w):
    lead = w.shape[:-1]
    w = jnp.swapaxes(w.reshape(*lead, 2, CONV_NC, FF_SHARD), -3, -2)
    return jnp.pad(w, [(0, 0)] * (w.ndim - 1) + [(0, FF_PAD)]).reshape(*lead, 2 * D_FFP)


def _unpad_ff(w):
    lead = w.shape[:-1]
    w = w.reshape(*lead, CONV_NC, 2, CONV_TC)[..., :FF_SHARD]
    return jnp.swapaxes(w, -3, -2).reshape(*lead, 2 * D_FF)


def _asm_up(g):
    return _cols_from_shards(jnp.pad(g, ((0, 0), (0, 0), (0, FF_PAD))))


def _grad_up(g):
    return jnp.transpose(g.reshape(g.shape[0], N_DEV, CONV_TC), (1, 0, 2))[..., :FF_SHARD]


def _asm_down(g):
    w = g.reshape(CONV_NC, FF_SHARD, D_MODEL)
    return jnp.pad(w, ((0, 0), (0, FF_PAD), (0, 0))).reshape(D_FFP, D_MODEL)


def _grad_down(g):
    return g.reshape(CONV_NC, CONV_TC, D_MODEL)[:, :FF_SHARD].reshape(N_DEV, -1, D_MODEL)


def _asm_in(g):
    w = _cols_from_shards(g)
    return jnp.concatenate([w[:, 0:1024], w[:, 1344:D_IN], w[:, 1024:1344],
                            jnp.zeros((w.shape[0], D_INP - D_IN), w.dtype)], axis=1)


def _asm_uq(g):
    w = _cols_from_shards(g).reshape(-1, MLA_H, MLA_NOPE + MLA_ROPE)
    return jnp.pad(w, ((0, 0), (0, 0), (0, MLA_HW - MLA_NOPE - MLA_ROPE))).reshape(-1, MLA_H * MLA_HW)


_ASSEMBLE = dict(
    w_in=_asm_in,
    w_glu=lambda g: g.reshape(S5_W, S5_W),
    w_uq=_asm_uq,
    w_ukv=_cols_from_shards,
    w_out=lambda g: g.reshape(D_MODEL, D_MODEL),
    w_up=_asm_up,
    w_down=_asm_down,
)

_GRAD_SHARDS = dict(
    w_in=lambda g: _cols_to_shards(jnp.concatenate([g[:, 0:1024], g[:, 3072:D_IN], g[:, 1024:3072]], axis=1)),
    w_glu=lambda g: g.reshape(N_DEV, -1, S5_W),
    w_uq=lambda g: _cols_to_shards(
        g.reshape(-1, MLA_H, MLA_HW)[:, :, :MLA_NOPE + MLA_ROPE].reshape(-1, MLA_H * (MLA_NOPE + MLA_ROPE))),
    w_ukv=_cols_to_shards,
    w_out=lambda g: g.reshape(N_DEV, -1, D_MODEL),
    w_up=_grad_up,
    w_down=_grad_down,
)


_BIG = ("w_in", "w_glu", "w_uq", "w_ukv", "w_out", "w_up", "w_down")
_EARLY = ("w_in", "w_glu", "w_uq", "w_ukv", "ffn_conv_w")
_SMALL = ("s5_lambda_re", "s5_lambda_im", "s5_log_dt", "s5_b_re", "s5_b_im", "s5_c_re", "s5_c_im", "s5_d",
          "mla_q_norm", "mla_kv_norm", "hg_lb_logits", "hg_out_norm", "mix_pre_norm", "mix_post_norm",
          "ffn_pre_norm", "ffn_post_norm", "ffn_conv_w_full", "ffn_conv_b", "b_ada")
PACK_ROW = 1024


def _pack(parts):
    flat = jnp.concatenate([p.reshape(-1) for p in parts])
    n = flat.shape[0]
    pad = (-n) % (8 * PACK_ROW)
    return jnp.pad(flat, (0, pad)).reshape(-1, PACK_ROW)


def _unpack(packed, shapes):
    flat = packed.reshape(-1)
    out, pos = [], 0
    for s in shapes:
        n = int(np.prod(s))
        out.append(flat[pos:pos + n].reshape(s))
        pos += n
    return out


def _step(x, c, positions, loss_target, w, m, v):
    x = x[0]
    S = x.shape[0]
    L = w["w_in"].shape[0]
    D = D_MODEL
    me = 4 * lax.axis_index("x") + 2 * lax.axis_index("y") + lax.axis_index("c")

    (c_all,) = _exchange([c], False, "gather_c")
    c_all = c_all.reshape(N_DEV, D)
    (c_act,) = _rowwise(lambda a: (a * jax.nn.sigmoid(a),), [(c_all, D, 0)], [], [(D, F32)], "silu_c", tm=N_DEV)
    mod_part = jnp.stack([_mm(c_act, w["w_ada"][l], "nn", name="mm_ada") for l in range(L)])
    (mod_all,) = _exchange([mod_part], False, "gather_mod")
    mod_mine = lax.dynamic_index_in_dim(mod_all, me, axis=2, keepdims=False)
    mod_full = jnp.transpose(mod_mine, (1, 0, 2)).reshape(L, 6 * D) + w["b_ada"]
    mods = [[mod_full[l:l + 1, i * D:(i + 1) * D] for i in range(6)] for l in range(L)]

    cs, sn = _rope_tables(positions[0])
    lower, lower_vjp = jax.vjp(lambda lg: jnp.cumsum(jax.nn.softmax(lg, axis=0), axis=0)
                               - jax.nn.softmax(lg, axis=0)[0:1], w["hg_lb_logits"])
    conv_w_full = []

    def shard(n, l):
        return w[n][l] if n == "ffn_conv_w" else w[n][l].astype(BF16)

    def layer_params(l, early):
        Wl = {n: _ASSEMBLE[n](early[n]) for n in _EARLY[:-1]}
        cw_full = _cols_from_shards(early["ffn_conv_w"])
        s5_args = (w["s5_lambda_re"][l], w["s5_lambda_im"][l], w["s5_log_dt"][l], w["s5_b_re"][l], w["s5_b_im"][l],
                   w["s5_c_re"][l], w["s5_c_im"][l])
        (lbr, lbi, bd, cdt), prep_vjp = jax.vjp(_s5_prep, *s5_args)
        tab_fwd, tab_rev = _s5_tables(*s5_args[:3])
        Wl["bd"], Wl["cdt"] = bd.astype(BF16), cdt.astype(BF16)
        row = lambda a: a.reshape(1, -1)
        Pl = dict(
            n1=row(w["mix_pre_norm"][l]), n2=row(w["mix_post_norm"][l]), n3=row(w["ffn_pre_norm"][l]),
            n4=row(w["ffn_post_norm"][l]), s5_d=row(w["s5_d"][l]), q_norm=row(w["mla_q_norm"][l]),
            kv_norm=row(w["mla_kv_norm"][l]), lb=row(lower[l]), hg_on=row(w["hg_out_norm"][l]),
            conv_w=_pad_ff(cw_full), conv_b=_pad_ff(row(w["ffn_conv_b"][l])),
            tab_fwd=tab_fwd, tab_rev=tab_rev, cs=cs, sn=sn, lam_bar=(lbr, lbi))
        return Wl, Pl, prep_vjp, cw_full

    Ws, Ps, preps, Rs = [], [], [], []
    h = x
    half = w["w_up"].shape[1] // 2
    early = dict(zip(_EARLY, _exchange([shard(n, 0) for n in _EARLY], False, "gather_weights")))
    for l in range(L):
        Wl, Pl, prep_vjp, cw_full = layer_params(l, early)
        Ws.append(Wl)
        Ps.append(Pl)
        preps.append(prep_vjp)
        conv_w_full.append(cw_full)
        riders, stash, early = _Riders(), {}, {}

        def got_flash(got, Wl=Wl, stash=stash):
            Wl["w_out"] = _ASSEMBLE["w_out"](got[0])
            stash["up"] = got[1]

        def got_hg(got, Wl=Wl, stash=stash):
            Wl["w_up"] = _ASSEMBLE["w_up"](jnp.concatenate([stash["up"], got[0]], axis=1))

        def got_up(got, Wl=Wl):
            Wl["w_down"] = _ASSEMBLE["w_down"](got[0])

        riders.add("flash_fwd", lambda _, l=l: ([shard("w_out", l), shard("w_up", l)[:half]], False, [False, True]),
                   got_flash)
        riders.add("hgrn2_fwd", lambda _, l=l: ([shard("w_up", l)[half:]], False, [True]), got_hg)
        riders.add("mm_up", lambda _, l=l: ([shard("w_down", l)], False), got_up)
        if l + 1 < L:
            riders.add("conv_fwd", lambda _, l=l: ([shard(n, l + 1) for n in _EARLY[1:]], False),
                       lambda got, early=early: early.update(zip(_EARLY[1:], got)))
            riders.add("mm_down", lambda _, l=l: ([shard(_EARLY[0], l + 1)], False),
                       lambda got, early=early: early.update(zip(_EARLY[:1], got)))
        h, R = _layer_fwd(h, mods[l], Wl, Pl, riders)
        Rs.append(R)
    loss_local, dh = _loss_head(h, loss_target[0])
    loss = lax.psum(loss_local, ("x", "y", "c"))

    big_recv = {n: [None] * L for n in _BIG}
    small_g = {n: [None] * L for n in _SMALL if n != "hg_lb_logits"}
    dlower = [None] * L
    late = ("w_out", "w_glu", "w_uq", "w_ukv")
    pending_in = None
    for l in reversed(range(L)):
        riders = _Riders()

        def store(names, l=l):
            def done(got):
                for n, r in zip(names, got):
                    big_recv[n][l] = r
            return done

        def make_late(G, prev=pending_in):
            return [_GRAD_SHARDS[n](G[n]) for n in late] + ([prev[1]] if prev else []), True

        def done_late(got, l=l, prev=pending_in):
            store(late, l)(got)
            if prev:
                big_recv["w_in"][prev[0]] = got[len(late)]

        riders.add("flash_bwd_dq", lambda G: ([_GRAD_SHARDS["w_down"](G["w_down"])], True), store(("w_down",)))
        riders.add("flash_bwd_dkv", lambda G: ([_GRAD_SHARDS["w_up"](G["w_up"])], True, [True]), store(("w_up",)))
        riders.add("hgrn2_bwd", make_late, done_late)
        dh, dmod, G, sm = _layer_bwd(dh, mods[l], Ws[l], Ps[l], Rs[l], riders)
        Rs[l] = None
        pending_in = (l, _GRAD_SHARDS["w_in"](G["w_in"]))
        lbr, lbi = Ps[l]["lam_bar"]
        ar, ai = _ri_split(sm["acc"])
        dl = lax.complex(ar, ai) / lax.complex(lbr, -lbi)
        d_s5 = preps[l]((jnp.real(dl), jnp.imag(dl), sm["bd"], sm["cdt"]))
        for n, g in zip(("s5_lambda_re", "s5_lambda_im", "s5_log_dt", "s5_b_re", "s5_b_im", "s5_c_re", "s5_c_im"), d_s5):
            small_g[n][l] = g
        small_g["s5_d"][l] = sm["s5_d"][0]
        small_g["mla_q_norm"][l] = sm["q_norm"][0]
        small_g["mla_kv_norm"][l] = sm["kv_norm"][0]
        small_g["hg_out_norm"][l] = sm["hg_on"][0]
        small_g["mix_pre_norm"][l] = sm["n1"][0]
        small_g["mix_post_norm"][l] = sm["n2"][0]
        small_g["ffn_pre_norm"][l] = sm["n3"][0]
        small_g["ffn_post_norm"][l] = sm["n4"][0]
        small_g["ffn_conv_w_full"][l] = _unpad_ff(sm["conv_w"])
        small_g["ffn_conv_b"][l] = _unpad_ff(sm["conv_b"])[0]
        small_g["b_ada"][l] = dmod[0]
        dlower[l] = sm["lb"][0]
    big_out = {}
    big_out["w_down"], (big_recv["w_in"][0],) = _adamw(jnp.stack(big_recv["w_down"]), w["w_down"], m["w_down"],
                                                       v["w_down"], name="adamw_w_down", exch=([pending_in[1]], True))
    small_g = {n: jnp.stack(gl) for n, gl in small_g.items()}
    (small_g["hg_lb_logits"],) = lower_vjp(jnp.stack(dlower))

    small_w = {n: w[n] for n in _SMALL if n != "ffn_conv_w_full"}
    small_w["ffn_conv_w_full"] = jnp.stack(conv_w_full)
    shapes = [small_w[n].shape for n in _SMALL]
    zeros_cw = jnp.zeros_like(small_w["ffn_conv_w_full"])
    pk_g = _pack([small_g[n] for n in _SMALL])
    pk_w = _pack([small_w[n] for n in _SMALL])
    pk_m = _pack([zeros_cw if n == "ffn_conv_w_full" else m[n] for n in _SMALL])
    pk_v = _pack([zeros_cw + 1.0 if n == "ffn_conv_w_full" else v[n] for n in _SMALL])
    big_out["w_up"], (pk_all,) = _adamw(jnp.stack(big_recv["w_up"]), w["w_up"], m["w_up"], v["w_up"],
                                        name="adamw_w_up", exch=([pk_g], False))
    sg, sd, sm_, sv = _adamw(pk_all[None], pk_w[None], pk_m[None], pk_v[None], name="adamw_small")
    small_out = {}
    for key, arr in (("g", sg), ("d", sd), ("m", sm_), ("v", sv)):
        small_out[key] = dict(zip(_SMALL, _unpack(arr[0], shapes)))

    n_cw = w["ffn_conv_w"].shape[-1]
    g_cw = lax.dynamic_slice_in_dim(small_out["g"]["ffn_conv_w_full"], me * n_cw, n_cw, axis=2)
    cw_out = _adamw(g_cw[:, None], w["ffn_conv_w"], m["ffn_conv_w"], v["ffn_conv_w"], name="adamw_conv_w")

    n_ada = w["w_ada"].shape[-1]
    flat_all = pk_all.reshape(N_DEV, -1)
    off = sum(int(np.prod(s)) for s in shapes[:-1])
    dmod_all = flat_all[:, off:off + L * 6 * D].reshape(N_DEV, L, 6 * D)
    dmod_cols = lax.dynamic_slice_in_dim(dmod_all, me * n_ada, n_ada, axis=2)
    g_ada = jnp.stack([_mm(c_act, dmod_cols[:, l], "tn", name="mm_ada_dw") for l in range(L)])
    ada_out = _adamw(g_ada[:, None], w["w_ada"], m["w_ada"], v["w_ada"], name="adamw_ada")

    for n in _BIG:
        if n not in big_out:
            recv = jnp.stack(big_recv[n])
            big_out[n] = _adamw(recv, w[n], m[n], v[n], name="adamw_" + n)
    big_out["w_ada"] = ada_out
    big_out["ffn_conv_w"] = cw_out
    return loss, dh[None], big_out, small_out


_WEIGHTS = ("w_in", "s5_lambda_re", "s5_lambda_im", "s5_log_dt", "s5_b_re", "s5_b_im", "s5_c_re", "s5_c_im", "s5_d",
            "s5_w_glu", "mla_q_norm", "mla_w_uq", "mla_kv_norm", "mla_w_ukv", "hg_lb_logits", "hg_out_norm", "w_out",
            "mix_pre_norm", "mix_post_norm", "ffn_pre_norm", "ffn_post_norm", "ffn_w_up", "ffn_conv_w", "ffn_conv_b",
            "ffn_w_down", "w_ada", "b_ada")
_ALIAS = {"s5_w_glu": "w_glu", "mla_w_uq": "w_uq", "mla_w_ukv": "w_ukv", "ffn_w_up": "w_up", "ffn_w_down": "w_down"}


def kernel(x, c, positions, w_in, s5_lambda_re, s5_lambda_im, s5_log_dt, s5_b_re, s5_b_im, s5_c_re, s5_c_im, s5_d, s5_w_glu, mla_q_norm, mla_w_uq, mla_kv_norm, mla_w_ukv, hg_lb_logits, hg_out_norm, w_out, mix_pre_norm, mix_post_norm, ffn_pre_norm, ffn_post_norm, ffn_w_up, ffn_conv_w, ffn_conv_b, ffn_w_down, w_ada, b_ada, loss_target, m_w_in, m_s5_lambda_re, m_s5_lambda_im, m_s5_log_dt, m_s5_b_re, m_s5_b_im, m_s5_c_re, m_s5_c_im, m_s5_d, m_s5_w_glu, m_mla_q_norm, m_mla_w_uq, m_mla_kv_norm, m_mla_w_ukv, m_hg_lb_logits, m_hg_out_norm, m_w_out, m_mix_pre_norm, m_mix_post_norm, m_ffn_pre_norm, m_ffn_post_norm, m_ffn_w_up, m_ffn_conv_w, m_ffn_conv_b, m_ffn_w_down, m_w_ada, m_b_ada, v_w_in, v_s5_lambda_re, v_s5_lambda_im, v_s5_log_dt, v_s5_b_re, v_s5_b_im, v_s5_c_re, v_s5_c_im, v_s5_d, v_s5_w_glu, v_mla_q_norm, v_mla_w_uq, v_mla_kv_norm, v_mla_w_ukv, v_hg_lb_logits, v_hg_out_norm, v_w_out, v_mix_pre_norm, v_mix_post_norm, v_ffn_pre_norm, v_ffn_post_norm, v_ffn_w_up, v_ffn_conv_w, v_ffn_conv_b, v_ffn_w_down, v_w_ada, v_b_ada):
    args = locals()
    key = lambda n: _ALIAS.get(n, n)
    w = {key(n): args[n] for n in _WEIGHTS}
    m = {key(n): args["m_" + n] for n in _WEIGHTS}
    v = {key(n): args["v_" + n] for n in _WEIGHTS}
    loss, grad_x, big, small = _step(x, c, positions, loss_target, w, m, v)

    def pick(n, idx):
        k = key(n)
        if k in big:
            return big[k][idx].reshape(w[k].shape)
        return small["gdmv"[idx]][k]

    outs = [loss, grad_x]
    for idx in range(4):
        outs += [pick(n, idx) for n in _WEIGHTS]
    return tuple(outs)
```
